```python
import math
import jax, jax.numpy as jnp
from jax import lax
import numpy as np

D_MODEL = 1024
BATCH = 4
SEQ = 4096
DEPTH = 1

EPS = 1e-6
ROPE_THETA = 10000.0
GLA_HEADS = 4
GLA_DK = 128
GLA_DV = 256
GLA_RANK = 16
GLA_TAU = 16.0
GLA_CHUNK = 64
GLA_QK = GLA_HEADS * GLA_DK
GLA_V = GLA_HEADS * GLA_DV
DIL_GROUPS = ((128, 1), (512, 4), (2048, 16))
DIL_HEADS = 4
DIL_HD = 128
DIL_BLOCK = 128
DIL_QK = len(DIL_GROUPS) * DIL_HEADS * DIL_HD
DIL_OUT = DIL_HEADS * DIL_HD
IN_SPLIT_SIZES = (GLA_QK, GLA_QK, GLA_V, GLA_V, GLA_RANK,
                  DIL_QK, DIL_QK, DIL_QK, DIL_OUT,
                  D_MODEL, D_MODEL)
IN_WIDTH = sum(IN_SPLIT_SIZES)

kernel_name = 'hybrid_gla_dilated_gated_merge'


def rms_norm(x, gain):
    xf = x.astype(jnp.float32)
    y = xf * lax.rsqrt(jnp.mean(xf * xf, axis=-1, keepdims=True) + EPS)
    return (y * gain.astype(jnp.float32)).astype(x.dtype)


def apply_rope(x, positions):
    half = x.shape[-1] // 2
    inv_freq = ROPE_THETA ** (-jnp.arange(half, dtype=jnp.float32) / half)
    ang = positions.astype(jnp.float32)[..., None] * inv_freq
    cos = jnp.cos(ang)[:, :, None, :]
    sin = jnp.sin(ang)[:, :, None, :]
    xf = x.astype(jnp.float32)
    x1, x2 = xf[..., :half], xf[..., half:]
    return jnp.concatenate([x1 * cos - x2 * sin, x2 * cos + x1 * sin], axis=-1).astype(x.dtype)


def gla_chunked(q, k, v, log_a):
    B, S, H, DK = q.shape
    DV = v.shape[-1]
    C = GLA_CHUNK
    N = S // C

    def chunks(t):
        return t.astype(jnp.float32).reshape(B, N, C, H, t.shape[-1]).transpose(0, 3, 1, 2, 4)

    qc = chunks(q) * (DK ** -0.5)
    kc, vc, gc = chunks(k), chunks(v), chunks(log_a)
    b = jnp.cumsum(gc, axis=3)
    b_last = b[:, :, :, -1:, :]
    q_t = qc * jnp.exp(b)
    k_t = kc * jnp.exp(-b)
    k_end = kc * jnp.exp(b_last - b)
    causal = jnp.tril(jnp.ones((C, C), dtype=bool))
    attn = jnp.where(causal, jnp.einsum('bhnid,bhnjd->bhnij', q_t, k_t), 0.0)
    o_intra = jnp.einsum('bhnij,bhnjv->bhniv', attn, vc)
    kv = jnp.einsum('bhnjd,bhnjv->bhndv', k_end, vc)
    decay = jnp.exp(b_last[:, :, :, 0, :])

    def step(state, inp):
        d, kvn = inp
        return d[..., None] * state + kvn, state

    s0 = jnp.zeros((B, H, DK, DV), jnp.float32)
    _, s_in = lax.scan(step, s0, (jnp.moveaxis(decay, 2, 0), jnp.moveaxis(kv, 2, 0)))
    o_inter = jnp.einsum('bhnid,nbhdv->bhniv', q_t, s_in)
    o = (o_intra + o_inter).transpose(0, 2, 3, 1, 4).reshape(B, S, H, DV)
    return o


def banded_window_attn(q, k, v, win, blk):
    N, L, H, D = q.shape
    nb = L // blk
    nprev = -(-win // blk)
    K = (nprev + 1) * blk
    pad = ((0, 0), (nprev * blk, 0), (0, 0), (0, 0))
    kb = jnp.pad(k, pad).reshape(N, nb + nprev, blk, H, D)
    vb = jnp.pad(v, pad).reshape(N, nb + nprev, blk, H, D)
    kw = jnp.concatenate([kb[:, j:j + nb] for j in range(nprev + 1)], axis=2)
    vw = jnp.concatenate([vb[:, j:j + nb] for j in range(nprev + 1)], axis=2)
    qb = q.reshape(N, nb, blk, H, D)
    s = jnp.einsum('nbqhd,nbkhd->nbhqk', qb, kw).astype(jnp.float32) * (D ** -0.5)
    qi = jnp.arange(blk)
    kj = jnp.arange(K)
    rel = qi[:, None] + nprev * blk - kj[None, :]
    kabs = jnp.arange(nb)[:, None] * blk - nprev * blk + kj[None, :]
    mask = ((rel >= 0) & (rel <= win))[None, :, :] & (kabs >= 0)[:, None, :]
    s = jnp.where(mask[None, :, None], s, -jnp.inf)
    m = jnp.max(s, axis=-1, keepdims=True)
    p = jnp.exp(s - m)
    l = jnp.sum(p, axis=-1, keepdims=True)
    o = jnp.einsum('nbhqk,nbkhd->nbqhd', p, vw.astype(jnp.float32))
    o = o / jnp.swapaxes(l, 2, 3)
    lse = jnp.swapaxes((m + jnp.log(l))[..., 0], 2, 3)
    return o.reshape(N, L, H, D), lse.reshape(N, L, H)


def dilated_group_attn(q, k, v, win, dil):
    B, S, H, D = q.shape
    L = S // dil
    Lp = -(-L // DIL_BLOCK) * DIL_BLOCK

    def to_sub(t):
        t = t.reshape(B, L, dil, H, D).transpose(0, 2, 1, 3, 4).reshape(B * dil, L, H, D)
        return jnp.pad(t, ((0, 0), (0, Lp - L), (0, 0), (0, 0)))

    o, lse = banded_window_attn(to_sub(q), to_sub(k), to_sub(v), win // dil, DIL_BLOCK)
    o = o[:, :L].reshape(B, dil, L, H, D).transpose(0, 2, 1, 3, 4).reshape(B, S, H, D)
    lse = lse[:, :L].reshape(B, dil, L, H).transpose(0, 2, 1, 3).reshape(B, S, H)
    return o, lse


def setup_inputs(seed: int = 0) -> dict:
    key = jax.random.key(seed)
    ks = jax.random.split(key, 12)
    f32 = jnp.float32
    x = jax.random.normal(ks[0], (BATCH, SEQ, D_MODEL), f32)
    positions = jnp.broadcast_to(jnp.arange(SEQ, dtype=jnp.int32), (BATCH, SEQ))
    norm_gain = 1.0 + 0.02 * jax.random.normal(ks[1], (DEPTH, D_MODEL), f32)
    w_in = jax.random.normal(ks[2], (DEPTH, D_MODEL, IN_WIDTH), f32) * D_MODEL ** -0.5
    gla_w_a2 = jax.random.normal(ks[3], (DEPTH, GLA_RANK, GLA_QK), f32) * GLA_RANK ** -0.5
    gla_b_a = 0.1 * jax.random.normal(ks[4], (DEPTH, GLA_QK), f32)
    gla_out_gain = 1.0 + 0.02 * jax.random.normal(ks[5], (DEPTH, GLA_DV), f32)
    dil_q_gain = 1.0 + 0.02 * jax.random.normal(ks[6], (DEPTH, DIL_HD), f32)
    dil_k_gain = 1.0 + 0.02 * jax.random.normal(ks[7], (DEPTH, DIL_HD), f32)
    w_gla_out = jax.random.normal(ks[8], (DEPTH, GLA_V, D_MODEL), f32) * GLA_V ** -0.5
    w_dil_out = jax.random.normal(ks[9], (DEPTH, DIL_OUT, D_MODEL), f32) * DIL_OUT ** -0.5
    w_o = jax.random.normal(ks[10], (DEPTH, D_MODEL, D_MODEL), f32) * D_MODEL ** -0.5
    return {'x': x, 'positions': positions, 'norm_gain': norm_gain, 'w_in': w_in,
            'gla_w_a2': gla_w_a2, 'gla_b_a': gla_b_a, 'gla_out_gain': gla_out_gain,
            'dil_q_gain': dil_q_gain, 'dil_k_gain': dil_k_gain,
            'w_gla_out': w_gla_out, 'w_dil_out': w_dil_out, 'w_o': w_o}


def reference(x, positions, norm_gain, w_in, gla_w_a2, gla_b_a, gla_out_gain,
              dil_q_gain, dil_k_gain, w_gla_out, w_dil_out, w_o):
    B, S, _ = x.shape
    offsets = np.cumsum(IN_SPLIT_SIZES)[:-1].tolist()
    n_groups = len(DIL_GROUPS)
    for layer in range(DEPTH):
        h = rms_norm(x, norm_gain[layer])
        proj = h @ w_in[layer]
        (q_a, k_a, v_a, r_a, a_lr, q_d, k_d, v_d, z_d, g_a, g_d) = jnp.split(proj, offsets, axis=-1)

        log_a = jax.nn.log_sigmoid((a_lr @ gla_w_a2[layer] + gla_b_a[layer]).astype(jnp.float32)) / GLA_TAU
        o_a = gla_chunked(q_a.reshape(B, S, GLA_HEADS, GLA_DK),
                          k_a.reshape(B, S, GLA_HEADS, GLA_DK),
                          v_a.reshape(B, S, GLA_HEADS, GLA_DV),
                          log_a.reshape(B, S, GLA_HEADS, GLA_DK))
        o_a = rms_norm(o_a, gla_out_gain[layer]).reshape(B, S, GLA_V)
        o_a = (o_a * jax.nn.silu(r_a.astype(jnp.float32))).astype(x.dtype)
        y_a = o_a @ w_gla_out[layer]

        n_heads_d = n_groups * DIL_HEADS
        qd = apply_rope(rms_norm(q_d.reshape(B, S, n_heads_d, DIL_HD), dil_q_gain[layer]), positions)
        kd = apply_rope(rms_norm(k_d.reshape(B, S, n_heads_d, DIL_HD), dil_k_gain[layer]), positions)
        vd = v_d.reshape(B, S, n_heads_d, DIL_HD)
        outs, lses = [], []
        for g, (win, dil) in enumerate(DIL_GROUPS):
            hs = slice(g * DIL_HEADS, (g + 1) * DIL_HEADS)
            o_g, lse_g = dilated_group_attn(qd[:, :, hs], kd[:, :, hs], vd[:, :, hs], win, dil)
            outs.append(o_g)
            lses.append(lse_g)
        wts = jax.nn.softmax(jnp.stack(lses, axis=0), axis=0)
        o_d = jnp.sum(wts[..., None] * jnp.stack(outs, axis=0), axis=0).reshape(B, S, DIL_OUT)
        o_d = (o_d * jax.nn.silu(z_d.astype(jnp.float32))).astype(x.dtype)
        y_d = o_d @ w_dil_out[layer]

        y = jax.nn.sigmoid(g_a) * y_a + jax.nn.sigmoid(g_d) * y_d
        x = x + (y @ w_o[layer]).astype(x.dtype)
    return x
```

```python
import functools

import jax
import jax.numpy as jnp
from jax import lax
from jax.experimental import pallas as pl
from jax.experimental.pallas import tpu as pltpu

D_MODEL = 1024
EPS = 1e-6
ROPE_THETA = 10000.0
GLA_HEADS = 4
GLA_DK = 128
GLA_DV = 256
GLA_RANK = 16
GLA_TAU = 16.0
GLA_QK = GLA_HEADS * GLA_DK
GLA_V = GLA_HEADS * GLA_DV
DIL_GROUPS = ((128, 1), (512, 4), (2048, 16))
DIL_HEADS = 4
DIL_HD = 128
DIL_QK = len(DIL_GROUPS) * DIL_HEADS * DIL_HD
DIL_OUT = DIL_HEADS * DIL_HD
IN_SPLIT_SIZES = (GLA_QK, GLA_QK, GLA_V, GLA_V, GLA_RANK,
                  DIL_QK, DIL_QK, DIL_QK, DIL_OUT, D_MODEL, D_MODEL)

LANES = 128
GLA_BLOCK = 128
GLA_MID = GLA_BLOCK // 2
ATT_BLOCK = 128
TOK_TILE = 1024
GLA_TOK_TILE = 512
MERGE_TOK_TILE = 512
COL_TILE = DIL_HEADS * DIL_HD
VMEM_LIMIT_BYTES = 48 * 1024 * 1024

F32 = jnp.float32
BF16 = jnp.bfloat16
NEG = -1e30

VG_VA, VG_RA, VG_GA, VG_GD, VG_VD, VG_ZD = 0, 2, 4, 6, 8, 11
VG_TILES = 12


def _params(*sem):
    return pltpu.CompilerParams(dimension_semantics=sem, vmem_limit_bytes=VMEM_LIMIT_BYTES)


def _dot(a, b):
    return jnp.dot(a, b, preferred_element_type=F32)


def _dot_nt(a, b):
    return lax.dot_general(a, b, (((1,), (1,)), ((), ())), preferred_element_type=F32)


def _dot_tn(a, b):
    return lax.dot_general(a, b, (((0,), (0,)), ((), ())), preferred_element_type=F32)


def _sigmoid(x):
    return 1.0 / (1.0 + jnp.exp(-x))


def _rmsnorm_kernel(x_ref, g_ref, h_ref):
    x = x_ref[...]
    ms = jnp.mean(x * x, axis=-1, keepdims=True)
    h_ref[...] = (x * lax.rsqrt(ms + EPS) * g_ref[...]).astype(h_ref.dtype)


def _rmsnorm(x2, gain):
    t, d = x2.shape
    return pl.pallas_call(
        _rmsnorm_kernel,
        grid=(t // TOK_TILE,),
        in_specs=[pl.BlockSpec((TOK_TILE, d), lambda i: (i, 0)),
                  pl.BlockSpec((1, d), lambda i: (0, 0))],
        out_specs=pl.BlockSpec((TOK_TILE, d), lambda i: (i, 0)),
        out_shape=jax.ShapeDtypeStruct((t, d), BF16),
        compiler_params=_params("parallel"),
        name="in_rmsnorm",
    )(x2, gain.reshape(1, d))


def _gla_proj_kernel(h_ref, wqk_ref, walr_ref, wa2_ref, ba_ref,
                     qin_ref, qmid_ref, kmid_ref, kend_ref, dec_ref):
    h = h_ref[...]
    qk = _dot(h, wqk_ref[...])
    alr = _dot(h, walr_ref[...]).astype(BF16)
    z = _dot(alr, wa2_ref[...]) + ba_ref[...]
    log_a = (jnp.minimum(z, 0.0) - jnp.log1p(jnp.exp(-jnp.abs(z)))) * (1.0 / GLA_TAU)
    row = lax.broadcasted_iota(jnp.int32, (GLA_BLOCK, GLA_BLOCK), 0)
    col = lax.broadcasted_iota(jnp.int32, (GLA_BLOCK, GLA_BLOCK), 1)
    tri = (col <= row).astype(BF16)
    for c in range(h.shape[0] // GLA_BLOCK):
        rows = slice(c * GLA_BLOCK, (c + 1) * GLA_BLOCK)
        la = log_a[rows]
        hi = la.astype(BF16)
        lo = (la - hi.astype(F32)).astype(BF16)
        b = _dot(tri, hi) + _dot(tri, lo)
        b_mid = b[GLA_MID - 1:GLA_MID]
        b_last = b[GLA_BLOCK - 1:GLA_BLOCK]
        q = qk[rows, :GLA_QK] * (GLA_DK ** -0.5)
        k = qk[rows, GLA_QK:]
        qin_ref[rows, :] = (q * jnp.exp(b)).astype(BF16)
        qmid_ref[rows, :] = (q * jnp.exp(b - b_mid)).astype(BF16)
        kmid_ref[rows, :] = (k * jnp.exp(b_mid - b)).astype(BF16)
        kend_ref[rows, :] = (k * jnp.exp(b_last - b)).astype(BF16)
        dec_ref[c] = jnp.broadcast_to(jnp.exp(b_last), (8, GLA_QK))


def _gla_proj(h, wqk, walr, wa2, ba):
    t, d = h.shape
    tm = GLA_TOK_TILE
    nblk = tm // GLA_BLOCK
    tok = lambda i: (i, 0)
    fixed = lambda i: (0, 0)
    qk_shape = jax.ShapeDtypeStruct((t, GLA_QK), BF16)
    return pl.pallas_call(
        _gla_proj_kernel,
        grid=(t // tm,),
        in_specs=[pl.BlockSpec((tm, d), tok),
                  pl.BlockSpec(wqk.shape, fixed),
                  pl.BlockSpec(walr.shape, fixed),
                  pl.BlockSpec(wa2.shape, fixed),
                  pl.BlockSpec(ba.shape, fixed)],
        out_specs=[pl.BlockSpec((tm, GLA_QK), tok)] * 4
        + [pl.BlockSpec((nblk, 8, GLA_QK), lambda i: (i, 0, 0))],
        out_shape=[qk_shape] * 4 + [jax.ShapeDtypeStruct((t // GLA_BLOCK, 8, GLA_QK), F32)],
        compiler_params=_params("parallel"),
        name="gla_proj",
    )(h, wqk, walr, wa2, ba)


def _dil_proj_kernel(h_ref, w_ref, g_ref, pos_ref, freq_ref, o_ref, cos_ref, sin_ref):
    @pl.when(pl.program_id(1) == 0)
    def _():
        ang = pos_ref[...] * freq_ref[...]
        lane = lax.broadcasted_iota(jnp.int32, ang.shape, 1)
        cos_ref[...] = jnp.cos(ang)
        sin_ref[...] = jnp.where(lane < DIL_HD // 2, -jnp.sin(ang), jnp.sin(ang))

    acc = _dot(h_ref[...], w_ref[...])
    cos = cos_ref[...]
    sin = sin_ref[...]
    for hh in range(DIL_HEADS):
        sl = slice(hh * DIL_HD, (hh + 1) * DIL_HD)
        xh = acc[:, sl]
        ms = jnp.mean(xh * xh, axis=-1, keepdims=True)
        y = xh * lax.rsqrt(ms + EPS) * g_ref[:, sl]
        y = y * cos + pltpu.roll(y, DIL_HD // 2, 1) * sin
        o_ref[:, sl] = y.astype(o_ref.dtype)


def _dil_proj(h, w, gains, pos, freq):
    t, d = h.shape
    n = w.shape[1]
    tm = TOK_TILE
    return pl.pallas_call(
        _dil_proj_kernel,
        grid=(t // tm, n // COL_TILE),
        in_specs=[pl.BlockSpec((tm, d), lambda i, j: (i, 0)),
                  pl.BlockSpec((d, COL_TILE), lambda i, j: (0, j)),
                  pl.BlockSpec((1, COL_TILE), lambda i, j: (0, j)),
                  pl.BlockSpec((tm, 1), lambda i, j: (i, 0)),
                  pl.BlockSpec((1, DIL_HD), lambda i, j: (0, 0))],
        out_specs=pl.BlockSpec((tm, COL_TILE), lambda i, j: (i, j)),
        out_shape=jax.ShapeDtypeStruct((t, n), BF16),
        scratch_shapes=[pltpu.VMEM((tm, DIL_HD), F32), pltpu.VMEM((tm, DIL_HD), F32)],
        compiler_params=_params("parallel", "arbitrary"),
        name="dil_proj",
    )(h, w, gains, pos, freq)


def _vg_proj_kernel(h_ref, w_ref, o_ref):
    acc = _dot(h_ref[...], w_ref[...])
    j = pl.program_id(1)
    is_silu = ((j >= VG_RA) & (j < VG_GA)) | (j >= VG_ZD)
    is_sig = (j >= VG_GA) & (j < VG_VD)

    @pl.when(is_silu)
    def _():
        o_ref[...] = (acc * _sigmoid(acc)).astype(o_ref.dtype)

    @pl.when(is_sig)
    def _():
        o_ref[...] = _sigmoid(acc).astype(o_ref.dtype)

    @pl.when(jnp.logical_not(is_silu | is_sig))
    def _():
        o_ref[...] = acc.astype(o_ref.dtype)


def _vg_proj(h, w):
    t, d = h.shape
    n = w.shape[1]
    tm = TOK_TILE
    return pl.pallas_call(
        _vg_proj_kernel,
        grid=(t // tm, n // COL_TILE),
        in_specs=[pl.BlockSpec((tm, d), lambda i, j: (i, 0)),
                  pl.BlockSpec((d, COL_TILE), lambda i, j: (0, j))],
        out_specs=pl.BlockSpec((tm, COL_TILE), lambda i, j: (i, j)),
        out_shape=jax.ShapeDtypeStruct((t, n), BF16),
        compiler_params=_params("parallel", "arbitrary"),
        name="vg_proj",
    )(h, w)


def _gla_kernel(qin_ref, qmid_ref, kmid_ref, kend_ref, dec_ref, v_ref, r_ref, gain_ref,
                o_ref, st_ref):
    @pl.when(pl.program_id(1) == 0)
    def _():
        st_ref[...] = jnp.zeros_like(st_ref)

    row = lax.broadcasted_iota(jnp.int32, (GLA_BLOCK, GLA_BLOCK), 0)
    col = lax.broadcasted_iota(jnp.int32, (GLA_BLOCK, GLA_BLOCK), 1)
    causal = col <= row
    for hh in range(GLA_HEADS):
        ks = slice(hh * GLA_DK, (hh + 1) * GLA_DK)
        vs = slice(hh * GLA_DV, (hh + 1) * GLA_DV)
        v = v_ref[:, vs]
        attn = _dot_nt(qmid_ref[:, ks], kmid_ref[:, ks])
        attn = jnp.where(causal, attn, 0.0).astype(BF16)
        st = st_ref[hh]
        o = _dot(attn, v) + _dot_nt(qin_ref[:, ks], st.astype(BF16))
        st_ref[hh] = st * dec_ref[0, 0:1, ks] + _dot_tn(v, kend_ref[:, ks])
        ms = jnp.mean(o * o, axis=-1, keepdims=True)
        o = o * lax.rsqrt(ms + EPS) * gain_ref[...] * r_ref[:, vs].astype(F32)
        o_ref[:, vs] = o.astype(o_ref.dtype)


def _gla(qin, qmid, kmid, kend, dec, vg, gain, batch):
    t = qin.shape[0]
    nblk = t // batch // GLA_BLOCK
    tok = lambda b, n: (b * nblk + n, 0)
    qk_spec = pl.BlockSpec((GLA_BLOCK, GLA_QK), tok)
    return pl.pallas_call(
        _gla_kernel,
        grid=(batch, nblk),
        in_specs=[qk_spec] * 4
        + [pl.BlockSpec((1, 8, GLA_QK), lambda b, n: (b * nblk + n, 0, 0)),
           pl.BlockSpec((GLA_BLOCK, GLA_V), lambda b, n: (b * nblk + n, VG_VA * COL_TILE // GLA_V)),
           pl.BlockSpec((GLA_BLOCK, GLA_V), lambda b, n: (b * nblk + n, VG_RA * COL_TILE // GLA_V)),
           pl.BlockSpec((1, GLA_DV), lambda b, n: (0, 0))],
        out_specs=pl.BlockSpec((GLA_BLOCK, GLA_V), tok),
        out_shape=jax.ShapeDtypeStruct((t, GLA_V), BF16),
        scratch_shapes=[pltpu.VMEM((GLA_HEADS, GLA_DV, GLA_DK), F32)],
        compiler_params=_params("parallel", "arbitrary"),
        name="gla_recurrence",
    )(qin, qmid, kmid, kend, dec, vg, vg, gain)


def _dil_attn_kernel(q_ref, kp_ref, kc_ref, vp_ref, vc_ref, o_ref, lse_ref):
    row = lax.broadcasted_iota(jnp.int32, (ATT_BLOCK, ATT_BLOCK), 0)
    col = lax.broadcasted_iota(jnp.int32, (ATT_BLOCK, ATT_BLOCK), 1)
    mask_c = col <= row
    mask_p = (col >= row) & (pl.program_id(2) > 0)
    lane = lax.broadcasted_iota(jnp.int32, (ATT_BLOCK, LANES), 1)
    lse_all = jnp.zeros((ATT_BLOCK, LANES), F32)
    for hh in range(DIL_HEADS):
        sl = slice(hh * DIL_HD, (hh + 1) * DIL_HD)
        q = q_ref[0, :, sl]
        s_c = jnp.where(mask_c, _dot_nt(q, kc_ref[0, :, sl]), NEG)
        s_p = jnp.where(mask_p, _dot_nt(q, kp_ref[0, :, sl]), NEG)
        m = jnp.maximum(jnp.max(s_c, axis=-1, keepdims=True), jnp.max(s_p, axis=-1, keepdims=True))
        p_c = jnp.exp(s_c - m)
        p_p = jnp.exp(s_p - m)
        l = jnp.sum(p_c, axis=-1, keepdims=True) + jnp.sum(p_p, axis=-1, keepdims=True)
        o = _dot(p_c.astype(BF16), vc_ref[0, :, sl]) + _dot(p_p.astype(BF16), vp_ref[0, :, sl])
        o_ref[0, :, sl] = (o / l).astype(o_ref.dtype)
        lse_all = jnp.where(lane == hh, m + jnp.log(l), lse_all)
    lse_ref[0] = lse_all


def _dil_attn(qk, vg, group, dil, batch):
    t = qk.shape[0]
    sub_len = t // batch // dil
    nblk = sub_len // ATT_BLOCK
    n_groups = len(DIL_GROUPS)
    qk_tiles = qk.shape[1] // COL_TILE
    qk3 = qk.reshape(batch, sub_len, dil * qk.shape[1])
    vg3 = vg.reshape(batch, sub_len, dil * vg.shape[1])
    blk = (1, ATT_BLOCK, COL_TILE)
    q_col = lambda r: r * qk_tiles + group
    k_col = lambda r: r * qk_tiles + n_groups + group
    v_col = lambda r: r * VG_TILES + VG_VD + group
    prev = lambda i: jnp.maximum(i - 1, 0)
    o, lse = pl.pallas_call(
        _dil_attn_kernel,
        grid=(batch, dil, nblk),
        in_specs=[pl.BlockSpec(blk, lambda b, r, i: (b, i, q_col(r))),
                  pl.BlockSpec(blk, lambda b, r, i: (b, prev(i), k_col(r))),
                  pl.BlockSpec(blk, lambda b, r, i: (b, i, k_col(r))),
                  pl.BlockSpec(blk, lambda b, r, i: (b, prev(i), v_col(r))),
                  pl.BlockSpec(blk, lambda b, r, i: (b, i, v_col(r)))],
        out_specs=[pl.BlockSpec(blk, lambda b, r, i: (b, i, r)),
                   pl.BlockSpec((1, ATT_BLOCK, LANES), lambda b, r, i: (b, i, r))],
        out_shape=[jax.ShapeDtypeStruct((batch, sub_len, dil * DIL_OUT), BF16),
                   jax.ShapeDtypeStruct((batch, sub_len, dil * LANES), F32)],
        compiler_params=_params("parallel", "parallel", "arbitrary"),
        name=f"dil_attn_d{dil}",
    )(qk3, qk3, qk3, vg3, vg3)
    return o.reshape(t, DIL_OUT), lse.reshape(t, LANES)


def _merge_kernel(x_ref, oa_ref, o0_ref, o1_ref, o2_ref, l0_ref, l1_ref, l2_ref,
                  sz_ref, sga_ref, sgd_ref, wga_ref, wdo_ref, wo_ref, out_ref):
    lses = [l0_ref[...], l1_ref[...], l2_ref[...]]
    outs = [o0_ref, o1_ref, o2_ref]
    m = jnp.maximum(jnp.maximum(lses[0], lses[1]), lses[2])
    es = [jnp.exp(l - m) for l in lses]
    den = es[0] + es[1] + es[2]
    ws = [e / den for e in es]
    heads = []
    for hh in range(DIL_HEADS):
        sl = slice(hh * DIL_HD, (hh + 1) * DIL_HD)
        acc = ws[0][:, hh:hh + 1] * outs[0][:, sl].astype(F32)
        for g in range(1, len(outs)):
            acc = acc + ws[g][:, hh:hh + 1] * outs[g][:, sl].astype(F32)
        heads.append(acc)
    o_d = (jnp.concatenate(heads, axis=-1) * sz_ref[...].astype(F32)).astype(BF16)
    y_a = _dot(oa_ref[...], wga_ref[...])
    y_d = _dot(o_d, wdo_ref[...])
    y = sga_ref[...].astype(F32) * y_a + sgd_ref[...].astype(F32) * y_d
    out_ref[...] = x_ref[...] + _dot(y.astype(BF16), wo_ref[...])


def _merge(x2, o_a, o_ds, lses, vg, wga, wdo, wo):
    t, d = x2.shape
    tm = MERGE_TOK_TILE
    tok = lambda i: (i, 0)
    fixed = lambda i: (0, 0)
    return pl.pallas_call(
        _merge_kernel,
        grid=(t // tm,),
        in_specs=[pl.BlockSpec((tm, d), tok),
                  pl.BlockSpec((tm, GLA_V), tok)]
        + [pl.BlockSpec((tm, DIL_OUT), tok)] * 3
        + [pl.BlockSpec((tm, LANES), tok)] * 3
        + [pl.BlockSpec((tm, DIL_OUT), lambda i: (i, VG_ZD)),
           pl.BlockSpec((tm, d), lambda i: (i, VG_GA * COL_TILE // D_MODEL)),
           pl.BlockSpec((tm, d), lambda i: (i, VG_GD * COL_TILE // D_MODEL)),
           pl.BlockSpec(wga.shape, fixed),
           pl.BlockSpec(wdo.shape, fixed),
           pl.BlockSpec(wo.shape, fixed)],
        out_specs=pl.BlockSpec((tm, d), tok),
        out_shape=jax.ShapeDtypeStruct((t, d), x2.dtype),
        compiler_params=_params("parallel"),
        name="merge_out",
    )(x2, o_a, *o_ds, *lses, vg, vg, vg, wga, wdo, wo)


def _split_w_in(w):
    offs = [0]
    for s in IN_SPLIT_SIZES:
        offs.append(offs[-1] + s)
    return [w[:, offs[i]:offs[i + 1]] for i in range(len(IN_SPLIT_SIZES))]


def kernel(x, positions, norm_gain, w_in, gla_w_a2, gla_b_a, gla_out_gain, dil_q_gain, dil_k_gain,
           w_gla_out, w_dil_out, w_o):
    batch, seq, d = x.shape
    t = batch * seq
    n_heads_d = len(DIL_GROUPS) * DIL_HEADS
    half = DIL_HD // 2
    inv_freq = ROPE_THETA ** (-jnp.arange(half, dtype=F32) / half)
    freq = jnp.concatenate([inv_freq, inv_freq]).reshape(1, DIL_HD)
    pos = positions.astype(F32).reshape(t, 1)
    x2 = x.reshape(t, d)
    for layer in range(norm_gain.shape[0]):
        (w_qa, w_ka, w_va, w_ra, w_alr, w_qd, w_kd, w_vd, w_zd, w_ga, w_gd) = _split_w_in(w_in[layer])
        wqk = jnp.concatenate([w_qa, w_ka], axis=1).astype(BF16)
        walr = jnp.pad(w_alr, ((0, 0), (0, LANES - GLA_RANK))).astype(BF16)
        wa2 = jnp.pad(gla_w_a2[layer], ((0, LANES - GLA_RANK), (0, 0))).astype(BF16)
        ba = gla_b_a[layer].reshape(1, GLA_QK)
        w_dqk = jnp.concatenate([w_qd, w_kd], axis=1).astype(BF16)
        g_dqk = jnp.concatenate([jnp.tile(dil_q_gain[layer] * (DIL_HD ** -0.5), n_heads_d),
                                 jnp.tile(dil_k_gain[layer], n_heads_d)]).reshape(1, 2 * DIL_QK)
        w_vg = jnp.concatenate([w_va, w_ra, w_ga, w_gd, w_vd, w_zd], axis=1).astype(BF16)

        h = _rmsnorm(x2, norm_gain[layer])
        qin, qmid, kmid, kend, dec = _gla_proj(h, wqk, walr, wa2, ba)
        dqk = _dil_proj(h, w_dqk, g_dqk, pos, freq)
        vg = _vg_proj(h, w_vg)
        o_a = _gla(qin, qmid, kmid, kend, dec, vg, gla_out_gain[layer].reshape(1, GLA_DV), batch)
        o_ds, lses = [], []
        for g, (win, dil) in enumerate(DIL_GROUPS):
            assert win // dil == ATT_BLOCK
            o_g, lse_g = _dil_attn(dqk, vg, g, dil, batch)
            o_ds.append(o_g)
            lses.append(lse_g)
        x2 = _merge(x2, o_a, o_ds, lses, vg,
                    w_gla_out[layer].astype(BF16), w_dil_out[layer].astype(BF16),
                    w_o[layer].astype(BF16))
    return x2.reshape(batch, seq, d)
```

```python
import jax
import jax.numpy as jnp
from jax import lax
from jax.experimental import pallas as pl
from jax.experimental.pallas import tpu as pltpu

D_MODEL = 1024
EPS = 1e-6
ROPE_THETA = 10000.0
GLA_HEADS = 4
GLA_DK = 128
GLA_DV = 256
GLA_RANK = 16
GLA_TAU = 16.0
GLA_QK = GLA_HEADS * GLA_DK
GLA_V = GLA_HEADS * GLA_DV
DIL_GROUPS = ((128, 1), (512, 4), (2048, 16))
DIL_HEADS = 4
DIL_HD = 128
DIL_QK = len(DIL_GROUPS) * DIL_HEADS * DIL_HD
DIL_OUT = DIL_HEADS * DIL_HD
IN_SPLIT_SIZES = (GLA_QK, GLA_QK, GLA_V, GLA_V, GLA_RANK,
                  DIL_QK, DIL_QK, DIL_QK, DIL_OUT, D_MODEL, D_MODEL)

LANES = 128
GLA_BLOCK = 128
GLA_MID = GLA_BLOCK // 2
ATT_BLOCK = 128
TOK_TILE = 1024
ROW_SUB = 256
GLA_TOK_TILE = 512
MERGE_TOK_TILE = 512
COL_TILE = DIL_HEADS * DIL_HD
VMEM_LIMIT_BYTES = 48 * 1024 * 1024

F32 = jnp.float32
BF16 = jnp.bfloat16
NEG = -1e30

VG_VA, VG_RA, VG_GA, VG_GD, VG_ZD, VG_TILES = 0, 2, 4, 6, 8, 9


def _params(*sem):
    return pltpu.CompilerParams(dimension_semantics=sem, vmem_limit_bytes=VMEM_LIMIT_BYTES)


def _dot(a, b):
    return jnp.dot(a, b, preferred_element_type=F32)


def _dot_nt(a, b):
    return lax.dot_general(a, b, (((1,), (1,)), ((), ())), preferred_element_type=F32)


def _dot_tn(a, b):
    return lax.dot_general(a, b, (((0,), (0,)), ((), ())), preferred_element_type=F32)


def _sigmoid(x):
    return 1.0 / (1.0 + jnp.exp(-x))


def _row_subs(n):
    return [slice(r * ROW_SUB, (r + 1) * ROW_SUB) for r in range(n // ROW_SUB)]


def _rmsnorm_kernel(x_ref, g_ref, *refs):
    h_refs, hs_ref = refs[:-1], refs[-1]
    x = x_ref[0]
    ms = jnp.mean(x * x, axis=-1, keepdims=True)
    hf = x * lax.rsqrt(ms + EPS) * g_ref[...]
    tm, d = x.shape
    for c in range(d // LANES):
        hs_ref[c] = hf[:, c * LANES:(c + 1) * LANES]
    for (_, dil), h_ref in zip(DIL_GROUPS, h_refs):
        if dil == 1:
            h_ref[0, 0] = hf.astype(h_ref.dtype)
            continue
        for r in range(dil):
            for c in range(d // LANES):
                h_ref[0, r, :, c * LANES:(c + 1) * LANES] = (
                    hs_ref[c, pl.ds(r, tm // dil, stride=dil), :].astype(h_ref.dtype))


def _rmsnorm(x, gain):
    batch, seq, d = x.shape
    tm = TOK_TILE
    return pl.pallas_call(
        _rmsnorm_kernel,
        grid=(batch, seq // tm),
        in_specs=[pl.BlockSpec((1, tm, d), lambda b, i: (b, i, 0)),
                  pl.BlockSpec((1, d), lambda b, i: (0, 0))],
        out_specs=[pl.BlockSpec((1, dil, tm // dil, d), lambda b, i: (b, 0, i, 0))
                   for _, dil in DIL_GROUPS],
        out_shape=[jax.ShapeDtypeStruct((batch, dil, seq // dil, d), BF16) for _, dil in DIL_GROUPS],
        scratch_shapes=[pltpu.VMEM((d // LANES, tm, LANES), F32)],
        compiler_params=_params("parallel", "parallel"),
        name="in_rmsnorm",
    )(x, gain.reshape(1, d))


def _gla_proj_kernel(h_ref, wqk_ref, walr_ref, wa2_ref, ba_ref,
                     qin_ref, qmid_ref, kmid_ref, kend_ref, dec_ref):
    row = lax.broadcasted_iota(jnp.int32, (GLA_BLOCK, GLA_BLOCK), 0)
    col = lax.broadcasted_iota(jnp.int32, (GLA_BLOCK, GLA_BLOCK), 1)
    tri = (col <= row).astype(BF16)
    for c in range(h_ref.shape[0] // GLA_BLOCK):
        rows = slice(c * GLA_BLOCK, (c + 1) * GLA_BLOCK)
        h = h_ref[rows, :]
        qk = _dot(h, wqk_ref[...])
        alr = _dot(h, walr_ref[...]).astype(BF16)
        z = _dot(alr, wa2_ref[...]) + ba_ref[...]
        la = (jnp.minimum(z, 0.0) - jnp.log1p(jnp.exp(-jnp.abs(z)))) * (1.0 / GLA_TAU)
        hi = la.astype(BF16)
        lo = (la - hi.astype(F32)).astype(BF16)
        b = _dot(tri, hi) + _dot(tri, lo)
        b_mid = b[GLA_MID - 1:GLA_MID]
        b_last = b[GLA_BLOCK - 1:GLA_BLOCK]
        q = qk[:, :GLA_QK] * (GLA_DK ** -0.5)
        k = qk[:, GLA_QK:]
        qin_ref[rows, :] = (q * jnp.exp(b)).astype(BF16)
        qmid_ref[rows, :] = (q * jnp.exp(b - b_mid)).astype(BF16)
        kmid_ref[rows, :] = (k * jnp.exp(b_mid - b)).astype(BF16)
        kend_ref[rows, :] = (k * jnp.exp(b_last - b)).astype(BF16)
        dec_ref[c] = jnp.broadcast_to(jnp.exp(b_last), (8, GLA_QK))


def _gla_proj(h, wqk, walr, wa2, ba):
    t, d = h.shape
    tm = GLA_TOK_TILE
    nblk = tm // GLA_BLOCK
    tok = lambda i: (i, 0)
    fixed = lambda i: (0, 0)
    qk_shape = jax.ShapeDtypeStruct((t, GLA_QK), BF16)
    return pl.pallas_call(
        _gla_proj_kernel,
        grid=(t // tm,),
        in_specs=[pl.BlockSpec((tm, d), tok),
                  pl.BlockSpec(wqk.shape, fixed),
                  pl.BlockSpec(walr.shape, fixed),
                  pl.BlockSpec(wa2.shape, fixed),
                  pl.BlockSpec(ba.shape, fixed)],
        out_specs=[pl.BlockSpec((tm, GLA_QK), tok)] * 4
        + [pl.BlockSpec((nblk, 8, GLA_QK), lambda i: (i, 0, 0))],
        out_shape=[qk_shape] * 4 + [jax.ShapeDtypeStruct((t // GLA_BLOCK, 8, GLA_QK), F32)],
        compiler_params=_params("parallel"),
        name="gla_proj",
    )(h, wqk, walr, wa2, ba)


def _dil_proj_kernel(h_ref, w_ref, g_ref, pos_ref, freq_ref, q_ref, k_ref, v_ref, cos_ref, sin_ref):
    ang = pos_ref[...] * freq_ref[...]
    lane = lax.broadcasted_iota(jnp.int32, ang.shape, 1)
    cos_ref[...] = jnp.cos(ang)
    sin_ref[...] = jnp.where(lane < DIL_HD // 2, -jnp.sin(ang), jnp.sin(ang))
    for kind, o_ref in enumerate((q_ref, k_ref, v_ref)):
        cols = slice(kind * COL_TILE, (kind + 1) * COL_TILE)
        for rows in _row_subs(h_ref.shape[0]):
            acc = _dot(h_ref[rows, :], w_ref[:, cols])
            if o_ref is v_ref:
                o_ref[rows, :] = acc.astype(o_ref.dtype)
                continue
            cos = cos_ref[rows, :]
            sin = sin_ref[rows, :]
            for hh in range(DIL_HEADS):
                sl = slice(hh * DIL_HD, (hh + 1) * DIL_HD)
                xh = acc[:, sl]
                ms = jnp.mean(xh * xh, axis=-1, keepdims=True)
                y = xh * lax.rsqrt(ms + EPS) * g_ref[kind:kind + 1, :]
                y = y * cos + pltpu.roll(y, DIL_HD // 2, 1) * sin
                o_ref[rows, sl] = y.astype(o_ref.dtype)


def _dil_proj(h, w, gains, pos, freq):
    t, d = h.shape
    tm = TOK_TILE
    tok = lambda i: (i, 0)
    fixed = lambda i: (0, 0)
    out = jax.ShapeDtypeStruct((t, COL_TILE), BF16)
    return pl.pallas_call(
        _dil_proj_kernel,
        grid=(t // tm,),
        in_specs=[pl.BlockSpec((tm, d), tok),
                  pl.BlockSpec(w.shape, fixed),
                  pl.BlockSpec(gains.shape, fixed),
                  pl.BlockSpec((tm, 1), tok),
                  pl.BlockSpec((1, DIL_HD), fixed)],
        out_specs=[pl.BlockSpec((tm, COL_TILE), tok)] * 3,
        out_shape=[out] * 3,
        scratch_shapes=[pltpu.VMEM((tm, DIL_HD), F32), pltpu.VMEM((tm, DIL_HD), F32)],
        compiler_params=_params("parallel"),
        name="dil_proj",
    )(h, w, gains, pos, freq)


def _vg_proj_kernel(h_ref, w_ref, o_ref):
    j = pl.program_id(1)
    is_silu = ((j >= VG_RA) & (j < VG_GA)) | (j >= VG_ZD)
    is_sig = (j >= VG_GA) & (j < VG_ZD)
    c_silu = jnp.where(is_silu, 1.0, 0.0).astype(F32)
    c_sig = jnp.where(is_sig, 1.0, 0.0).astype(F32)
    c_id = 1.0 - c_silu - c_sig
    for rows in _row_subs(h_ref.shape[0]):
        acc = _dot(h_ref[rows, :], w_ref[...])
        o_ref[rows, :] = (c_id * acc + _sigmoid(acc) * (c_sig + c_silu * acc)).astype(o_ref.dtype)


def _vg_proj(h, w):
    t, d = h.shape
    n = w.shape[1]
    tm = TOK_TILE
    return pl.pallas_call(
        _vg_proj_kernel,
        grid=(t // tm, n // COL_TILE),
        in_specs=[pl.BlockSpec((tm, d), lambda i, j: (i, 0)),
                  pl.BlockSpec((d, COL_TILE), lambda i, j: (0, j))],
        out_specs=pl.BlockSpec((tm, COL_TILE), lambda i, j: (i, j)),
        out_shape=jax.ShapeDtypeStruct((t, n), BF16),
        compiler_params=_params("parallel", "arbitrary"),
        name="vg_proj",
    )(h, w)


def _gla_kernel(qin_ref, qmid_ref, kmid_ref, kend_ref, dec_ref, v_ref, r_ref, gain_ref,
                o_ref, st_ref):
    @pl.when(pl.program_id(1) == 0)
    def _():
        st_ref[...] = jnp.zeros_like(st_ref)

    row = lax.broadcasted_iota(jnp.int32, (GLA_BLOCK, GLA_BLOCK), 0)
    col = lax.broadcasted_iota(jnp.int32, (GLA_BLOCK, GLA_BLOCK), 1)
    causal = col <= row
    for hh in range(GLA_HEADS):
        ks = slice(hh * GLA_DK, (hh + 1) * GLA_DK)
        vs = slice(hh * GLA_DV, (hh + 1) * GLA_DV)
        v = v_ref[:, vs]
        attn = _dot_nt(qmid_ref[:, ks], kmid_ref[:, ks])
        attn = jnp.where(causal, attn, 0.0).astype(BF16)
        st = st_ref[hh]
        o = _dot(attn, v) + _dot_nt(qin_ref[:, ks], st.astype(BF16))
        st_ref[hh] = st * dec_ref[0, 0:1, ks] + _dot_tn(v, kend_ref[:, ks])
        ms = jnp.mean(o * o, axis=-1, keepdims=True)
        o = o * lax.rsqrt(ms + EPS) * gain_ref[...] * r_ref[:, vs].astype(F32)
        o_ref[:, vs] = o.astype(o_ref.dtype)


def _gla(qin, qmid, kmid, kend, dec, vg, gain, batch):
    t = qin.shape[0]
    nblk = t // batch // GLA_BLOCK
    tok = lambda b, n: (b * nblk + n, 0)
    qk_spec = pl.BlockSpec((GLA_BLOCK, GLA_QK), tok)
    return pl.pallas_call(
        _gla_kernel,
        grid=(batch, nblk),
        in_specs=[qk_spec] * 4
        + [pl.BlockSpec((1, 8, GLA_QK), lambda b, n: (b * nblk + n, 0, 0)),
           pl.BlockSpec((GLA_BLOCK, GLA_V), lambda b, n: (b * nblk + n, VG_VA * COL_TILE // GLA_V)),
           pl.BlockSpec((GLA_BLOCK, GLA_V), lambda b, n: (b * nblk + n, VG_RA * COL_TILE // GLA_V)),
           pl.BlockSpec((1, GLA_DV), lambda b, n: (0, 0))],
        out_specs=pl.BlockSpec((GLA_BLOCK, GLA_V), tok),
        out_shape=jax.ShapeDtypeStruct((t, GLA_V), BF16),
        scratch_shapes=[pltpu.VMEM((GLA_HEADS, GLA_DV, GLA_DK), F32)],
        compiler_params=_params("parallel", "arbitrary"),
        name="gla_recurrence",
    )(qin, qmid, kmid, kend, dec, vg, vg, gain)


def _dil_attn_kernel(q_ref, kp_ref, kc_ref, vp_ref, vc_ref, o_ref, lse_ref):
    row = lax.broadcasted_iota(jnp.int32, (ATT_BLOCK, ATT_BLOCK), 0)
    col = lax.broadcasted_iota(jnp.int32, (ATT_BLOCK, ATT_BLOCK), 1)
    mask_c = col <= row
    mask_p = (col >= row) & (pl.program_id(1) > 0)
    lane = lax.broadcasted_iota(jnp.int32, (ATT_BLOCK, LANES), 1)
    lse_all = jnp.zeros((ATT_BLOCK, LANES), F32)
    for hh in range(DIL_HEADS):
        sl = slice(hh * DIL_HD, (hh + 1) * DIL_HD)
        q = q_ref[0, :, sl]
        s_c = jnp.where(mask_c, _dot_nt(q, kc_ref[0, :, sl]), NEG)
        s_p = jnp.where(mask_p, _dot_nt(q, kp_ref[0, :, sl]), NEG)
        m = jnp.maximum(jnp.max(s_c, axis=-1, keepdims=True), jnp.max(s_p, axis=-1, keepdims=True))
        p_c = jnp.exp(s_c - m)
        p_p = jnp.exp(s_p - m)
        l = jnp.sum(p_c, axis=-1, keepdims=True) + jnp.sum(p_p, axis=-1, keepdims=True)
        o = _dot(p_c.astype(BF16), vc_ref[0, :, sl]) + _dot(p_p.astype(BF16), vp_ref[0, :, sl])
        o_ref[0, :, sl] = (o / l).astype(o_ref.dtype)
        lse_all = jnp.where(lane == hh, m + jnp.log(l), lse_all)
    lse_ref[0] = lse_all


def _dil_attn(q, k, v, n_sub, dil):
    t = q.shape[0]
    sub_len = t // n_sub
    nblk = sub_len // ATT_BLOCK
    q3, k3, v3 = (a.reshape(n_sub, sub_len, COL_TILE) for a in (q, k, v))
    blk = (1, ATT_BLOCK, COL_TILE)
    cur = lambda s, i: (s, i, 0)
    prev = lambda s, i: (s, jnp.maximum(i - 1, 0), 0)
    o, lse = pl.pallas_call(
        _dil_attn_kernel,
        grid=(n_sub, nblk),
        in_specs=[pl.BlockSpec(blk, cur), pl.BlockSpec(blk, prev), pl.BlockSpec(blk, cur),
                  pl.BlockSpec(blk, prev), pl.BlockSpec(blk, cur)],
        out_specs=[pl.BlockSpec(blk, cur), pl.BlockSpec((1, ATT_BLOCK, LANES), cur)],
        out_shape=[jax.ShapeDtypeStruct((n_sub, sub_len, COL_TILE), BF16),
                   jax.ShapeDtypeStruct((n_sub, sub_len, LANES), F32)],
        compiler_params=_params("parallel", "arbitrary"),
        name=f"dil_attn_d{dil}",
    )(q3, k3, k3, v3, v3)
    return o, lse


def _merge_kernel(x_ref, oa_ref, o0_ref, o1_ref, o2_ref, l0_ref, l1_ref, l2_ref,
                  sz_ref, sga_ref, sgd_ref, wga_ref, wdo_ref, wo_ref, out_ref, og_ref, lg_ref):
    tm = x_ref.shape[1]
    for g, ((_, dil), o_ref, l_ref) in enumerate(zip(DIL_GROUPS, (o0_ref, o1_ref, o2_ref),
                                                     (l0_ref, l1_ref, l2_ref))):
        for r in range(dil):
            dst = pl.ds(r, tm // dil, stride=dil)
            lg_ref[g, dst, :] = l_ref[0, r]
            for hh in range(DIL_HEADS):
                og_ref[g, hh, dst, :] = o_ref[0, r, :, hh * DIL_HD:(hh + 1) * DIL_HD].astype(F32)
    n_groups = len(DIL_GROUPS)
    for rows in _row_subs(tm):
        lses = [lg_ref[g, rows, :] for g in range(n_groups)]
        m = jnp.maximum(jnp.maximum(lses[0], lses[1]), lses[2])
        es = [jnp.exp(l - m) for l in lses]
        den = es[0] + es[1] + es[2]
        ws = [e / den for e in es]
        heads = []
        for hh in range(DIL_HEADS):
            acc = ws[0][:, hh:hh + 1] * og_ref[0, hh, rows, :]
            for g in range(1, n_groups):
                acc = acc + ws[g][:, hh:hh + 1] * og_ref[g, hh, rows, :]
            heads.append(acc)
        o_d = (jnp.concatenate(heads, axis=-1) * sz_ref[rows, :].astype(F32)).astype(BF16)
        y_a = _dot(oa_ref[rows, :], wga_ref[...])
        y_d = _dot(o_d, wdo_ref[...])
        y = sga_ref[rows, :].astype(F32) * y_a + sgd_ref[rows, :].astype(F32) * y_d
        out_ref[0, rows, :] = x_ref[0, rows, :] + _dot(y.astype(BF16), wo_ref[...])


def _merge(x, o_a, o_ds, lses, vg, wga, wdo, wo):
    batch, seq, d = x.shape
    tm = MERGE_TOK_TILE
    nt = seq // tm
    tok = lambda b, i: (b * nt + i, 0)
    fixed = lambda b, i: (0, 0)
    dil_spec = lambda dil, w: pl.BlockSpec((1, dil, tm // dil, w), lambda b, i: (b, 0, i, 0))
    o_ds = [o.reshape(batch, dil, seq // dil, DIL_OUT) for o, (_, dil) in zip(o_ds, DIL_GROUPS)]
    lses = [l.reshape(batch, dil, seq // dil, LANES) for l, (_, dil) in zip(lses, DIL_GROUPS)]
    return pl.pallas_call(
        _merge_kernel,
        grid=(batch, nt),
        in_specs=[pl.BlockSpec((1, tm, d), lambda b, i: (b, i, 0)),
                  pl.BlockSpec((tm, GLA_V), tok)]
        + [dil_spec(dil, DIL_OUT) for _, dil in DIL_GROUPS]
        + [dil_spec(dil, LANES) for _, dil in DIL_GROUPS]
        + [pl.BlockSpec((tm, DIL_OUT), lambda b, i: (b * nt + i, VG_ZD)),
           pl.BlockSpec((tm, d), lambda b, i: (b * nt + i, VG_GA * COL_TILE // D_MODEL)),
           pl.BlockSpec((tm, d), lambda b, i: (b * nt + i, VG_GD * COL_TILE // D_MODEL)),
           pl.BlockSpec(wga.shape, fixed),
           pl.BlockSpec(wdo.shape, fixed),
           pl.BlockSpec(wo.shape, fixed)],
        out_specs=pl.BlockSpec((1, tm, d), lambda b, i: (b, i, 0)),
        out_shape=jax.ShapeDtypeStruct(x.shape, x.dtype),
        scratch_shapes=[pltpu.VMEM((len(DIL_GROUPS), DIL_HEADS, tm, DIL_HD), F32),
                        pltpu.VMEM((len(DIL_GROUPS), tm, LANES), F32)],
        compiler_params=_params("parallel", "parallel"),
        name="merge_out",
    )(x, o_a, *o_ds, *lses, vg, vg, vg, wga, wdo, wo)


def _split_w_in(w):
    offs = [0]
    for s in IN_SPLIT_SIZES:
        offs.append(offs[-1] + s)
    return [w[:, offs[i]:offs[i + 1]] for i in range(len(IN_SPLIT_SIZES))]


def _to_dilated(a, dil):
    batch, seq = a.shape[:2]
    return jnp.swapaxes(a.reshape(batch, seq // dil, dil, *a.shape[2:]), 1, 2)


def kernel(x, positions, norm_gain, w_in, gla_w_a2, gla_b_a, gla_out_gain, dil_q_gain, dil_k_gain,
           w_gla_out, w_dil_out, w_o):
    batch, seq, d = x.shape
    t = batch * seq
    half = DIL_HD // 2
    inv_freq = ROPE_THETA ** (-jnp.arange(half, dtype=F32) / half)
    freq = jnp.concatenate([inv_freq, inv_freq]).reshape(1, DIL_HD)
    pos = positions.astype(F32)
    for layer in range(norm_gain.shape[0]):
        (w_qa, w_ka, w_va, w_ra, w_alr, w_qd, w_kd, w_vd, w_zd, w_ga, w_gd) = _split_w_in(w_in[layer])
        wqk = jnp.concatenate([w_qa, w_ka], axis=1).astype(BF16)
        walr = jnp.pad(w_alr, ((0, 0), (0, LANES - GLA_RANK))).astype(BF16)
        wa2 = jnp.pad(gla_w_a2[layer], ((0, LANES - GLA_RANK), (0, 0))).astype(BF16)
        ba = gla_b_a[layer].reshape(1, GLA_QK)
        g_dqk = jnp.stack([dil_q_gain[layer] * (DIL_HD ** -0.5), dil_k_gain[layer]])
        w_vg = jnp.concatenate([w_va, w_ra, w_ga, w_gd, w_zd], axis=1).astype(BF16)

        hs = _rmsnorm(x, norm_gain[layer])
        h = hs[0].reshape(t, d)
        qin, qmid, kmid, kend, dec = _gla_proj(h, wqk, walr, wa2, ba)
        vg = _vg_proj(h, w_vg)
        o_a = _gla(qin, qmid, kmid, kend, dec, vg, gla_out_gain[layer].reshape(1, GLA_DV), batch)
        o_ds, lses = [], []
        for g, (win, dil) in enumerate(DIL_GROUPS):
            assert win // dil == ATT_BLOCK
            cols = slice(g * COL_TILE, (g + 1) * COL_TILE)
            w_g = jnp.concatenate([w_qd[:, cols], w_kd[:, cols], w_vd[:, cols]], axis=1).astype(BF16)
            q_g, k_g, v_g = _dil_proj(hs[g].reshape(t, d), w_g, g_dqk,
                                      _to_dilated(pos, dil).reshape(t, 1), freq)
            o_g, lse_g = _dil_attn(q_g, k_g, v_g, batch * dil, dil)
            o_ds.append(o_g)
            lses.append(lse_g)
        x = _merge(x, o_a, o_ds, lses, vg,
                   w_gla_out[layer].astype(BF16), w_dil_out[layer].astype(BF16),
                   w_o[layer].astype(BF16))
    return x
```

```python
import jax
import jax.numpy as jnp
from jax import lax
from jax.experimental import pallas as pl
from jax.experimental.pallas import tpu as pltpu

D_MODEL = 1024
EPS = 1e-6
ROPE_THETA = 10000.0
GLA_HEADS = 4
GLA_DK = 128
GLA_DV = 256
GLA_RANK = 16
GLA_TAU = 16.0
GLA_QK = GLA_HEADS * GLA_DK
GLA_V = GLA_HEADS * GLA_DV
DIL_GROUPS = ((128, 1), (512, 4), (2048, 16))
DIL_HEADS = 4
DIL_HD = 128
DIL_QK = len(DIL_GROUPS) * DIL_HEADS * DIL_HD
DIL_OUT = DIL_HEADS * DIL_HD
IN_SPLIT_SIZES = (GLA_QK, GLA_QK, GLA_V, GLA_V, GLA_RANK,
                  DIL_QK, DIL_QK, DIL_QK, DIL_OUT, D_MODEL, D_MODEL)

LANES = 128
GLA_BLOCK = 128
GLA_MID = GLA_BLOCK // 2
ATT_BLOCK = 128
ATT_SUPER = 512
TOK_TILE = 1024
ROW_SUB = 256
GLA_TOK_TILE = 512
MERGE_TOK_TILE = 512
COL_TILE = DIL_HEADS * DIL_HD
VMEM_LIMIT_BYTES = 48 * 1024 * 1024

F32 = jnp.float32
BF16 = jnp.bfloat16
NEG = -1e30

VG_VA, VG_RA, VG_GA, VG_GD, VG_ZD, VG_TILES = 0, 2, 4, 6, 8, 9


def _params(*sem):
    return pltpu.CompilerParams(dimension_semantics=sem, vmem_limit_bytes=VMEM_LIMIT_BYTES)


def _dot(a, b):
    return jnp.dot(a, b, preferred_element_type=F32)


def _dot_nt(a, b):
    return lax.dot_general(a, b, (((1,), (1,)), ((), ())), preferred_element_type=F32)


def _dot_tn(a, b):
    return lax.dot_general(a, b, (((0,), (0,)), ((), ())), preferred_element_type=F32)


def _sigmoid(x):
    return 1.0 / (1.0 + jnp.exp(-x))


def _row_subs(n):
    return [slice(r * ROW_SUB, (r + 1) * ROW_SUB) for r in range(n // ROW_SUB)]


def _rmsnorm_kernel(x_ref, g_ref, *refs):
    h_refs, hs_ref = refs[:-1], refs[-1]
    x = x_ref[0]
    ms = jnp.mean(x * x, axis=-1, keepdims=True)
    hf = x * lax.rsqrt(ms + EPS) * g_ref[...]
    tm, d = x.shape
    for c in range(d // LANES):
        hs_ref[c] = hf[:, c * LANES:(c + 1) * LANES]
    for (_, dil), h_ref in zip(DIL_GROUPS, h_refs):
        if dil == 1:
            h_ref[0, 0] = hf.astype(h_ref.dtype)
            continue
        for r in range(dil):
            for c in range(d // LANES):
                h_ref[0, r, :, c * LANES:(c + 1) * LANES] = (
                    hs_ref[c, pl.ds(r, tm // dil, stride=dil), :].astype(h_ref.dtype))


def _rmsnorm(x, gain):
    batch, seq, d = x.shape
    tm = TOK_TILE
    return pl.pallas_call(
        _rmsnorm_kernel,
        grid=(batch, seq // tm),
        in_specs=[pl.BlockSpec((1, tm, d), lambda b, i: (b, i, 0)),
                  pl.BlockSpec((1, d), lambda b, i: (0, 0))],
        out_specs=[pl.BlockSpec((1, dil, tm // dil, d), lambda b, i: (b, 0, i, 0))
                   for _, dil in DIL_GROUPS],
        out_shape=[jax.ShapeDtypeStruct((batch, dil, seq // dil, d), BF16) for _, dil in DIL_GROUPS],
        scratch_shapes=[pltpu.VMEM((d // LANES, tm, LANES), F32)],
        compiler_params=_params("parallel", "parallel"),
        name="in_rmsnorm",
    )(x, gain.reshape(1, d))


def _gla_proj_kernel(h_ref, wqk_ref, walr_ref, wa2_ref, ba_ref,
                     qin_ref, qmid_ref, kmid_ref, kend_ref, dec_ref):
    row = lax.broadcasted_iota(jnp.int32, (GLA_BLOCK, GLA_BLOCK), 0)
    col = lax.broadcasted_iota(jnp.int32, (GLA_BLOCK, GLA_BLOCK), 1)
    tri = (col <= row).astype(BF16)
    for c in range(h_ref.shape[0] // GLA_BLOCK):
        rows = slice(c * GLA_BLOCK, (c + 1) * GLA_BLOCK)
        h = h_ref[rows, :]
        qk = _dot(h, wqk_ref[...])
        alr = _dot(h, walr_ref[...]).astype(BF16)
        z = _dot(alr, wa2_ref[...]) + ba_ref[...]
        la = (jnp.minimum(z, 0.0) - jnp.log1p(jnp.exp(-jnp.abs(z)))) * (1.0 / GLA_TAU)
        hi = la.astype(BF16)
        lo = (la - hi.astype(F32)).astype(BF16)
        b = _dot(tri, hi) + _dot(tri, lo)
        b_mid = b[GLA_MID - 1:GLA_MID]
        b_last = b[GLA_BLOCK - 1:GLA_BLOCK]
        q = qk[:, :GLA_QK] * (GLA_DK ** -0.5)
        k = qk[:, GLA_QK:]
        qin_ref[rows, :] = (q * jnp.exp(b)).astype(BF16)
        qmid_ref[rows, :] = (q * jnp.exp(b - b_mid)).astype(BF16)
        kmid_ref[rows, :] = (k * jnp.exp(b_mid - b)).astype(BF16)
        kend_ref[rows, :] = (k * jnp.exp(b_last - b)).astype(BF16)
        dec_ref[c] = jnp.broadcast_to(jnp.exp(b_last), (8, GLA_QK))


def _gla_proj(h, wqk, walr, wa2, ba):
    t, d = h.shape
    tm = GLA_TOK_TILE
    nblk = tm // GLA_BLOCK
    tok = lambda i: (i, 0)
    fixed = lambda i: (0, 0)
    qk_shape = jax.ShapeDtypeStruct((t, GLA_QK), BF16)
    return pl.pallas_call(
        _gla_proj_kernel,
        grid=(t // tm,),
        in_specs=[pl.BlockSpec((tm, d), tok),
                  pl.BlockSpec(wqk.shape, fixed),
                  pl.BlockSpec(walr.shape, fixed),
                  pl.BlockSpec(wa2.shape, fixed),
                  pl.BlockSpec(ba.shape, fixed)],
        out_specs=[pl.BlockSpec((tm, GLA_QK), tok)] * 4
        + [pl.BlockSpec((nblk, 8, GLA_QK), lambda i: (i, 0, 0))],
        out_shape=[qk_shape] * 4 + [jax.ShapeDtypeStruct((t // GLA_BLOCK, 8, GLA_QK), F32)],
        compiler_params=_params("parallel"),
        name="gla_proj",
    )(h, wqk, walr, wa2, ba)


def _dil_proj_kernel(h_ref, w_ref, g_ref, pos_ref, freq_ref, q_ref, k_ref, v_ref, cos_ref, sin_ref):
    ang = pos_ref[...] * freq_ref[...]
    lane = lax.broadcasted_iota(jnp.int32, ang.shape, 1)
    cos_ref[...] = jnp.cos(ang)
    sin_ref[...] = jnp.where(lane < DIL_HD // 2, -jnp.sin(ang), jnp.sin(ang))
    for kind, o_ref in enumerate((q_ref, k_ref, v_ref)):
        cols = slice(kind * COL_TILE, (kind + 1) * COL_TILE)
        for rows in _row_subs(h_ref.shape[0]):
            acc = _dot(h_ref[rows, :], w_ref[:, cols])
            if o_ref is v_ref:
                o_ref[rows, :] = acc.astype(o_ref.dtype)
                continue
            cos = cos_ref[rows, :]
            sin = sin_ref[rows, :]
            for hh in range(DIL_HEADS):
                sl = slice(hh * DIL_HD, (hh + 1) * DIL_HD)
                xh = acc[:, sl]
                ms = jnp.mean(xh * xh, axis=-1, keepdims=True)
                y = xh * lax.rsqrt(ms + EPS) * g_ref[kind:kind + 1, :]
                y = y * cos + pltpu.roll(y, DIL_HD // 2, 1) * sin
                o_ref[rows, sl] = y.astype(o_ref.dtype)


def _dil_proj(h, w, gains, pos, freq):
    t, d = h.shape
    tm = TOK_TILE
    tok = lambda i: (i, 0)
    fixed = lambda i: (0, 0)
    out = jax.ShapeDtypeStruct((t, COL_TILE), BF16)
    return pl.pallas_call(
        _dil_proj_kernel,
        grid=(t // tm,),
        in_specs=[pl.BlockSpec((tm, d), tok),
                  pl.BlockSpec(w.shape, fixed),
                  pl.BlockSpec(gains.shape, fixed),
                  pl.BlockSpec((tm, 1), tok),
                  pl.BlockSpec((1, DIL_HD), fixed)],
        out_specs=[pl.BlockSpec((tm, COL_TILE), tok)] * 3,
        out_shape=[out] * 3,
        scratch_shapes=[pltpu.VMEM((tm, DIL_HD), F32), pltpu.VMEM((tm, DIL_HD), F32)],
        compiler_params=_params("parallel"),
        name="dil_proj",
    )(h, w, gains, pos, freq)


def _vg_proj_kernel(h_ref, w_ref, o_ref):
    j = pl.program_id(1)
    is_silu = ((j >= VG_RA) & (j < VG_GA)) | (j >= VG_ZD)
    is_sig = (j >= VG_GA) & (j < VG_ZD)
    c_silu = jnp.where(is_silu, 1.0, 0.0).astype(F32)
    c_sig = jnp.where(is_sig, 1.0, 0.0).astype(F32)
    c_id = 1.0 - c_silu - c_sig
    for rows in _row_subs(h_ref.shape[0]):
        acc = _dot(h_ref[rows, :], w_ref[...])
        o_ref[rows, :] = (c_id * acc + _sigmoid(acc) * (c_sig + c_silu * acc)).astype(o_ref.dtype)


def _vg_proj(h, w):
    t, d = h.shape
    n = w.shape[1]
    tm = TOK_TILE
    return pl.pallas_call(
        _vg_proj_kernel,
        grid=(t // tm, n // COL_TILE),
        in_specs=[pl.BlockSpec((tm, d), lambda i, j: (i, 0)),
                  pl.BlockSpec((d, COL_TILE), lambda i, j: (0, j))],
        out_specs=pl.BlockSpec((tm, COL_TILE), lambda i, j: (i, j)),
        out_shape=jax.ShapeDtypeStruct((t, n), BF16),
        compiler_params=_params("parallel", "arbitrary"),
        name="vg_proj",
    )(h, w)


def _gla_kernel(qin_ref, qmid_ref, kmid_ref, kend_ref, dec_ref, v_ref, r_ref, gain_ref,
                o_ref, st_ref):
    @pl.when(pl.program_id(0) == 0)
    def _():
        st_ref[...] = jnp.zeros_like(st_ref)

    row = lax.broadcasted_iota(jnp.int32, (GLA_BLOCK, GLA_BLOCK), 0)
    col = lax.broadcasted_iota(jnp.int32, (GLA_BLOCK, GLA_BLOCK), 1)
    causal = col <= row
    for b in range(qin_ref.shape[0]):
        for hh in range(GLA_HEADS):
            ks = slice(hh * GLA_DK, (hh + 1) * GLA_DK)
            vs = slice(hh * GLA_DV, (hh + 1) * GLA_DV)
            v = v_ref[b, :, vs]
            attn = _dot_nt(qmid_ref[b, :, ks], kmid_ref[b, :, ks])
            attn = jnp.where(causal, attn, 0.0).astype(BF16)
            st = st_ref[b, hh]
            o = _dot(attn, v) + _dot_nt(qin_ref[b, :, ks], st.astype(BF16))
            st_ref[b, hh] = st * dec_ref[b, 0, 0:1, ks] + _dot_tn(v, kend_ref[b, :, ks])
            ms = jnp.mean(o * o, axis=-1, keepdims=True)
            o = o * lax.rsqrt(ms + EPS) * gain_ref[...] * r_ref[b, :, vs].astype(F32)
            o_ref[b, :, vs] = o.astype(o_ref.dtype)


def _gla(qin, qmid, kmid, kend, dec, vg, gain, batch):
    t = qin.shape[0]
    seq = t // batch
    nblk = seq // GLA_BLOCK
    qin, qmid, kmid, kend = (a.reshape(batch, seq, GLA_QK) for a in (qin, qmid, kmid, kend))
    dec = dec.reshape(batch, nblk, 8, GLA_QK)
    vg = vg.reshape(batch, seq, vg.shape[1])
    qk_spec = pl.BlockSpec((batch, GLA_BLOCK, GLA_QK), lambda n: (0, n, 0))
    o = pl.pallas_call(
        _gla_kernel,
        grid=(nblk,),
        in_specs=[qk_spec] * 4
        + [pl.BlockSpec((batch, 1, 8, GLA_QK), lambda n: (0, n, 0, 0)),
           pl.BlockSpec((batch, GLA_BLOCK, GLA_V), lambda n: (0, n, VG_VA * COL_TILE // GLA_V)),
           pl.BlockSpec((batch, GLA_BLOCK, GLA_V), lambda n: (0, n, VG_RA * COL_TILE // GLA_V)),
           pl.BlockSpec((1, GLA_DV), lambda n: (0, 0))],
        out_specs=pl.BlockSpec((batch, GLA_BLOCK, GLA_V), lambda n: (0, n, 0)),
        out_shape=jax.ShapeDtypeStruct((batch, seq, GLA_V), BF16),
        scratch_shapes=[pltpu.VMEM((batch, GLA_HEADS, GLA_DV, GLA_DK), F32)],
        compiler_params=_params("arbitrary"),
        name="gla_recurrence",
    )(qin, qmid, kmid, kend, dec, vg, vg, gain)
    return o.reshape(t, GLA_V)


def _dil_attn_kernel(q_ref, kp_ref, kc_ref, vp_ref, vc_ref, o_ref, lse_ref):
    n_qblk = q_ref.shape[1] // ATT_BLOCK
    row = lax.broadcasted_iota(jnp.int32, (ATT_BLOCK, 2 * ATT_BLOCK), 0)
    col = lax.broadcasted_iota(jnp.int32, (ATT_BLOCK, 2 * ATT_BLOCK), 1)
    band = (col >= row) & (col <= row + ATT_BLOCK)
    band_first = band & ((col >= ATT_BLOCK) | (pl.program_id(1) > 0))
    ones = jnp.ones((2 * ATT_BLOCK, DIL_HD), BF16)
    lane = lax.broadcasted_iota(jnp.int32, (ATT_BLOCK, LANES), 1)
    for a in range(n_qblk):
        rows = slice(a * ATT_BLOCK, (a + 1) * ATT_BLOCK)
        lse_all = jnp.zeros((ATT_BLOCK, LANES), F32)
        for hh in range(DIL_HEADS):
            sl = slice(hh * DIL_HD, (hh + 1) * DIL_HD)
            if a == 0:
                k_win = jnp.concatenate([kp_ref[0, :, sl], kc_ref[0, :ATT_BLOCK, sl]], axis=0)
                v_win = jnp.concatenate([vp_ref[0, :, sl], vc_ref[0, :ATT_BLOCK, sl]], axis=0)
            else:
                win = slice((a - 1) * ATT_BLOCK, (a + 1) * ATT_BLOCK)
                k_win = kc_ref[0, win, sl]
                v_win = vc_ref[0, win, sl]
            s = _dot_nt(q_ref[0, rows, sl], k_win)
            s = jnp.where(band_first if a == 0 else band, s, NEG)
            m = jnp.max(s, axis=-1, keepdims=True)
            p = jnp.exp(s - m).astype(BF16)
            ol = _dot(p, jnp.concatenate([v_win, ones], axis=1))
            l = ol[:, DIL_HD:]
            o_ref[0, rows, sl] = (ol[:, :DIL_HD] / l).astype(o_ref.dtype)
            lse_all = jnp.where(lane == hh, m + jnp.log(l), lse_all)
        lse_ref[0, rows, :] = lse_all


def _dil_attn(q, k, v, n_sub, dil):
    t = q.shape[0]
    sub_len = t // n_sub
    qb = min(sub_len, ATT_SUPER)
    n_qblk = qb // ATT_BLOCK
    q3, k3, v3 = (a.reshape(n_sub, sub_len, COL_TILE) for a in (q, k, v))
    blk = (1, qb, COL_TILE)
    cur = lambda s, i: (s, i, 0)
    prev = lambda s, i: (s, jnp.maximum(i * n_qblk - 1, 0), 0)
    prev_blk = (1, ATT_BLOCK, COL_TILE)
    o, lse = pl.pallas_call(
        _dil_attn_kernel,
        grid=(n_sub, sub_len // qb),
        in_specs=[pl.BlockSpec(blk, cur), pl.BlockSpec(prev_blk, prev), pl.BlockSpec(blk, cur),
                  pl.BlockSpec(prev_blk, prev), pl.BlockSpec(blk, cur)],
        out_specs=[pl.BlockSpec(blk, cur), pl.BlockSpec((1, qb, LANES), cur)],
        out_shape=[jax.ShapeDtypeStruct((n_sub, sub_len, COL_TILE), BF16),
                   jax.ShapeDtypeStruct((n_sub, sub_len, LANES), F32)],
        compiler_params=_params("parallel", "arbitrary"),
        name=f"dil_attn_d{dil}",
    )(q3, k3, k3, v3, v3)
    return o, lse


def _merge_kernel(x_ref, oa_ref, o0_ref, o1_ref, o2_ref, l0_ref, l1_ref, l2_ref,
                  sz_ref, sga_ref, sgd_ref, wga_ref, wdo_ref, wo_ref, out_ref, og_ref, lg_ref):
    tm = x_ref.shape[1]
    for g, ((_, dil), o_ref, l_ref) in enumerate(zip(DIL_GROUPS, (o0_ref, o1_ref, o2_ref),
                                                     (l0_ref, l1_ref, l2_ref))):
        for r in range(dil):
            dst = pl.ds(r, tm // dil, stride=dil)
            lg_ref[g, dst, :] = l_ref[0, r]
            for hh in range(DIL_HEADS):
                og_ref[g, hh, dst, :] = o_ref[0, r, :, hh * DIL_HD:(hh + 1) * DIL_HD].astype(F32)
    n_groups = len(DIL_GROUPS)
    for rows in _row_subs(tm):
        lses = [lg_ref[g, rows, :] for g in range(n_groups)]
        m = jnp.maximum(jnp.maximum(lses[0], lses[1]), lses[2])
        es = [jnp.exp(l - m) for l in lses]
        den = es[0] + es[1] + es[2]
        ws = [e / den for e in es]
        heads = []
        for hh in range(DIL_HEADS):
            acc = ws[0][:, hh:hh + 1] * og_ref[0, hh, rows, :]
            for g in range(1, n_groups):
                acc = acc + ws[g][:, hh:hh + 1] * og_ref[g, hh, rows, :]
            heads.append(acc)
        o_d = (jnp.concatenate(heads, axis=-1) * sz_ref[rows, :].astype(F32)).astype(BF16)
        y_a = _dot(oa_ref[rows, :], wga_ref[...])
        y_d = _dot(o_d, wdo_ref[...])
        y = sga_ref[rows, :].astype(F32) * y_a + sgd_ref[rows, :].astype(F32) * y_d
        out_ref[0, rows, :] = x_ref[0, rows, :] + _dot(y.astype(BF16), wo_ref[...])


def _merge(x, o_a, o_ds, lses, vg, wga, wdo, wo):
    batch, seq, d = x.shape
    tm = MERGE_TOK_TILE
    nt = seq // tm
    tok = lambda b, i: (b * nt + i, 0)
    fixed = lambda b, i: (0, 0)
    dil_spec = lambda dil, w: pl.BlockSpec((1, dil, tm // dil, w), lambda b, i: (b, 0, i, 0))
    o_ds = [o.reshape(batch, dil, seq // dil, DIL_OUT) for o, (_, dil) in zip(o_ds, DIL_GROUPS)]
    lses = [l.reshape(batch, dil, seq // dil, LANES) for l, (_, dil) in zip(lses, DIL_GROUPS)]
    return pl.pallas_call(
        _merge_kernel,
        grid=(batch, nt),
        in_specs=[pl.BlockSpec((1, tm, d), lambda b, i: (b, i, 0)),
                  pl.BlockSpec((tm, GLA_V), tok)]
        + [dil_spec(dil, DIL_OUT) for _, dil in DIL_GROUPS]
        + [dil_spec(dil, LANES) for _, dil in DIL_GROUPS]
        + [pl.BlockSpec((tm, DIL_OUT), lambda b, i: (b * nt + i, VG_ZD)),
           pl.BlockSpec((tm, d), lambda b, i: (b * nt + i, VG_GA * COL_TILE // D_MODEL)),
           pl.BlockSpec((tm, d), lambda b, i: (b * nt + i, VG_GD * COL_TILE // D_MODEL)),
           pl.BlockSpec(wga.shape, fixed),
           pl.BlockSpec(wdo.shape, fixed),
           pl.BlockSpec(wo.shape, fixed)],
        out_specs=pl.BlockSpec((1, tm, d), lambda b, i: (b, i, 0)),
        out_shape=jax.ShapeDtypeStruct(x.shape, x.dtype),
        scratch_shapes=[pltpu.VMEM((len(DIL_GROUPS), DIL_HEADS, tm, DIL_HD), F32),
                        pltpu.VMEM((len(DIL_GROUPS), tm, LANES), F32)],
        compiler_params=_params("parallel", "parallel"),
        name="merge_out",
    )(x, o_a, *o_ds, *lses, vg, vg, vg, wga, wdo, wo)


def _split_w_in(w):
    offs = [0]
    for s in IN_SPLIT_SIZES:
        offs.append(offs[-1] + s)
    return [w[:, offs[i]:offs[i + 1]] for i in range(len(IN_SPLIT_SIZES))]


def _to_dilated(a, dil):
    batch, seq = a.shape[:2]
    return jnp.swapaxes(a.reshape(batch, seq // dil, dil, *a.shape[2:]), 1, 2)


def kernel(x, positions, norm_gain, w_in, gla_w_a2, gla_b_a, gla_out_gain, dil_q_gain, dil_k_gain,
           w_gla_out, w_dil_out, w_o):
    batch, seq, d = x.shape
    t = batch * seq
    half = DIL_HD // 2
    inv_freq = ROPE_THETA ** (-jnp.arange(half, dtype=F32) / half)
    freq = jnp.concatenate([inv_freq, inv_freq]).reshape(1, DIL_HD)
    pos = positions.astype(F32)
    for layer in range(norm_gain.shape[0]):
        (w_qa, w_ka, w_va, w_ra, w_alr, w_qd, w_kd, w_vd, w_zd, w_ga, w_gd) = _split_w_in(w_in[layer])
        wqk = jnp.concatenate([w_qa, w_ka], axis=1).astype(BF16)
        walr = jnp.pad(w_alr, ((0, 0), (0, LANES - GLA_RANK))).astype(BF16)
        wa2 = jnp.pad(gla_w_a2[layer], ((0, LANES - GLA_RANK), (0, 0))).astype(BF16)
        ba = gla_b_a[layer].reshape(1, GLA_QK)
        g_dqk = jnp.stack([dil_q_gain[layer] * (DIL_HD ** -0.5), dil_k_gain[layer]])
        w_vg = jnp.concatenate([w_va, w_ra, w_ga, w_gd, w_zd], axis=1).astype(BF16)

        hs = _rmsnorm(x, norm_gain[layer])
        h = hs[0].reshape(t, d)
        qin, qmid, kmid, kend, dec = _gla_proj(h, wqk, walr, wa2, ba)
        vg = _vg_proj(h, w_vg)
        o_a = _gla(qin, qmid, kmid, kend, dec, vg, gla_out_gain[layer].reshape(1, GLA_DV), batch)
        o_ds, lses = [], []
        for g, (win, dil) in enumerate(DIL_GROUPS):
            assert win // dil == ATT_BLOCK
            cols = slice(g * COL_TILE, (g + 1) * COL_TILE)
            w_g = jnp.concatenate([w_qd[:, cols], w_kd[:, cols], w_vd[:, cols]], axis=1).astype(BF16)
            q_g, k_g, v_g = _dil_proj(hs[g].reshape(t, d), w_g, g_dqk,
                                      _to_dilated(pos, dil).reshape(t, 1), freq)
            o_g, lse_g = _dil_attn(q_g, k_g, v_g, batch * dil, dil)
            o_ds.append(o_g)
            lses.append(lse_g)
        x = _merge(x, o_a, o_ds, lses, vg,
                   w_gla_out[layer].astype(BF16), w_dil_out[layer].astype(BF16),
                   w_o[layer].astype(BF16))
    return x
```

```python
import jax
import jax.numpy as jnp
from jax import lax
from jax.experimental import pallas as pl
from jax.experimental.pallas import tpu as pltpu

D_MODEL = 1024
EPS = 1e-6
ROPE_THETA = 10000.0
GLA_HEADS = 4
GLA_DK = 128
GLA_DV = 256
GLA_RANK = 16
GLA_TAU = 16.0
GLA_QK = GLA_HEADS * GLA_DK
GLA_V = GLA_HEADS * GLA_DV
DIL_GROUPS = ((128, 1), (512, 4), (2048, 16))
DIL_HEADS = 4
DIL_HD = 128
DIL_QK = len(DIL_GROUPS) * DIL_HEADS * DIL_HD
DIL_OUT = DIL_HEADS * DIL_HD
IN_SPLIT_SIZES = (GLA_QK, GLA_QK, GLA_V, GLA_V, GLA_RANK,
                  DIL_QK, DIL_QK, DIL_QK, DIL_OUT, D_MODEL, D_MODEL)

LANES = 128
GLA_BLOCK = 128
GLA_MID = GLA_BLOCK // 2
ATT_BLOCK = 128
ATT_SUPER = 512
TOK_TILE = 1024
ROW_SUB = 256
GLA_TOK_TILE = 512
MERGE_TOK_TILE = 512
COL_TILE = DIL_HEADS * DIL_HD
VMEM_LIMIT_BYTES = 48 * 1024 * 1024

F32 = jnp.float32
BF16 = jnp.bfloat16
NEG = -1e30

VG_VA, VG_RA, VG_GA, VG_GD, VG_ZD = 0, 2, 4, 6, 8
VG_ACTS = ("id", "id", "silu", "silu", "sigmoid", "sigmoid", "sigmoid", "sigmoid", "silu")


def _params(*sem):
    return pltpu.CompilerParams(dimension_semantics=sem, vmem_limit_bytes=VMEM_LIMIT_BYTES)


def _dot(a, b):
    return jnp.dot(a, b, preferred_element_type=F32)


def _dot_nt(a, b):
    return lax.dot_general(a, b, (((1,), (1,)), ((), ())), preferred_element_type=F32)


def _dot_tn(a, b):
    return lax.dot_general(a, b, (((0,), (0,)), ((), ())), preferred_element_type=F32)


def _sigmoid(x):
    return 0.5 * jnp.tanh(0.5 * x) + 0.5


def _row_subs(n):
    return [slice(r * ROW_SUB, (r + 1) * ROW_SUB) for r in range(n // ROW_SUB)]


def _rmsnorm_kernel(x_ref, g_ref, pos_ref, freq_ref, *refs):
    n = len(DIL_GROUPS)
    h_refs, cos_refs, sin_refs, hs_ref = refs[:n], refs[n:2 * n], refs[2 * n:3 * n], refs[3 * n]
    x = x_ref[0]
    ms = jnp.mean(x * x, axis=-1, keepdims=True)
    hf = x * lax.rsqrt(ms + EPS) * g_ref[...]
    tm, d = x.shape
    n_planes = d // LANES
    ang = pos_ref[0] * freq_ref[...]
    lane = lax.broadcasted_iota(jnp.int32, ang.shape, 1)
    cos = jnp.cos(ang)
    sin = jnp.where(lane < DIL_HD // 2, -jnp.sin(ang), jnp.sin(ang))
    for c in range(n_planes):
        hs_ref[c] = hf[:, c * LANES:(c + 1) * LANES]
    hs_ref[n_planes] = cos
    hs_ref[n_planes + 1] = sin
    for (_, dil), h_ref, cos_ref, sin_ref in zip(DIL_GROUPS, h_refs, cos_refs, sin_refs):
        if dil == 1:
            h_ref[0, 0] = hf.astype(h_ref.dtype)
            cos_ref[0, 0] = cos
            sin_ref[0, 0] = sin
            continue
        for r in range(dil):
            src = pl.ds(r, tm // dil, stride=dil)
            for c in range(n_planes):
                h_ref[0, r, :, c * LANES:(c + 1) * LANES] = hs_ref[c, src, :].astype(h_ref.dtype)
            cos_ref[0, r] = hs_ref[n_planes, src, :]
            sin_ref[0, r] = hs_ref[n_planes + 1, src, :]


def _rmsnorm(x, gain, pos, freq):
    batch, seq, d = x.shape
    tm = TOK_TILE
    dil_spec = lambda dil, w: pl.BlockSpec((1, dil, tm // dil, w), lambda b, i: (b, 0, i, 0))
    dil_shape = lambda dil, w, dt: jax.ShapeDtypeStruct((batch, dil, seq // dil, w), dt)
    dils = [dil for _, dil in DIL_GROUPS]
    outs = pl.pallas_call(
        _rmsnorm_kernel,
        grid=(batch, seq // tm),
        in_specs=[pl.BlockSpec((1, tm, d), lambda b, i: (b, i, 0)),
                  pl.BlockSpec((1, d), lambda b, i: (0, 0)),
                  pl.BlockSpec((1, tm, 1), lambda b, i: (b, i, 0)),
                  pl.BlockSpec((1, DIL_HD), lambda b, i: (0, 0))],
        out_specs=[dil_spec(dil, d) for dil in dils] + [dil_spec(dil, DIL_HD) for dil in dils] * 2,
        out_shape=[dil_shape(dil, d, BF16) for dil in dils]
        + [dil_shape(dil, DIL_HD, F32) for dil in dils] * 2,
        scratch_shapes=[pltpu.VMEM((d // LANES + 2, tm, LANES), F32)],
        compiler_params=_params("parallel", "parallel"),
        name="in_rmsnorm",
    )(x, gain.reshape(1, d), pos, freq)
    n = len(dils)
    return outs[:n], outs[n:2 * n], outs[2 * n:]


def _nat_proj_kernel(h_ref, wqk_ref, walr_ref, wa2_ref, ba_ref, wvg_ref,
                     qin_ref, qmid_ref, kmid_ref, kend_ref, dec_ref, vg_ref):
    row = lax.broadcasted_iota(jnp.int32, (GLA_BLOCK, GLA_BLOCK), 0)
    col = lax.broadcasted_iota(jnp.int32, (GLA_BLOCK, GLA_BLOCK), 1)
    tri = (col <= row).astype(BF16)
    for c in range(h_ref.shape[0] // GLA_BLOCK):
        rows = slice(c * GLA_BLOCK, (c + 1) * GLA_BLOCK)
        h = h_ref[rows, :]
        for j, act in enumerate(VG_ACTS):
            cols = slice(j * COL_TILE, (j + 1) * COL_TILE)
            acc = _dot(h, wvg_ref[:, cols])
            if act == "silu":
                acc = acc * _sigmoid(acc)
            elif act == "sigmoid":
                acc = _sigmoid(acc)
            vg_ref[rows, cols] = acc.astype(vg_ref.dtype)
        qk = _dot(h, wqk_ref[...])
        alr = _dot(h, walr_ref[...]).astype(BF16)
        z = _dot(alr, wa2_ref[...]) + ba_ref[...]
        la = (jnp.minimum(z, 0.0) - jnp.log1p(jnp.exp(-jnp.abs(z)))) * (1.0 / GLA_TAU)
        hi = la.astype(BF16)
        lo = (la - hi.astype(F32)).astype(BF16)
        b = _dot(tri, hi) + _dot(tri, lo)
        b_mid = b[GLA_MID - 1:GLA_MID]
        b_last = b[GLA_BLOCK - 1:GLA_BLOCK]
        q = qk[:, :GLA_QK] * (GLA_DK ** -0.5)
        k = qk[:, GLA_QK:]
        qin_ref[rows, :] = (q * jnp.exp(b)).astype(BF16)
        qmid_ref[rows, :] = (q * jnp.exp(b - b_mid)).astype(BF16)
        kmid_ref[rows, :] = (k * jnp.exp(b_mid - b)).astype(BF16)
        kend_ref[rows, :] = (k * jnp.exp(b_last - b)).astype(BF16)
        dec_ref[c] = jnp.broadcast_to(jnp.exp(b_last), (8, GLA_QK))


def _nat_proj(h, wqk, walr, wa2, ba, wvg):
    t, d = h.shape
    tm = GLA_TOK_TILE
    nblk = tm // GLA_BLOCK
    tok = lambda i: (i, 0)
    fixed = lambda i: (0, 0)
    qk_shape = jax.ShapeDtypeStruct((t, GLA_QK), BF16)
    return pl.pallas_call(
        _nat_proj_kernel,
        grid=(t // tm,),
        in_specs=[pl.BlockSpec((tm, d), tok),
                  pl.BlockSpec(wqk.shape, fixed),
                  pl.BlockSpec(walr.shape, fixed),
                  pl.BlockSpec(wa2.shape, fixed),
                  pl.BlockSpec(ba.shape, fixed),
                  pl.BlockSpec(wvg.shape, fixed)],
        out_specs=[pl.BlockSpec((tm, GLA_QK), tok)] * 4
        + [pl.BlockSpec((nblk, 8, GLA_QK), lambda i: (i, 0, 0)),
           pl.BlockSpec((tm, wvg.shape[1]), tok)],
        out_shape=[qk_shape] * 4 + [jax.ShapeDtypeStruct((t // GLA_BLOCK, 8, GLA_QK), F32),
                                    jax.ShapeDtypeStruct((t, wvg.shape[1]), BF16)],
        compiler_params=_params("parallel"),
        name="nat_proj",
    )(h, wqk, walr, wa2, ba, wvg)


def _dil_proj_kernel(h_ref, w_ref, g_ref, cos_ref, sin_ref, q_ref, k_ref, v_ref):
    for kind, o_ref in enumerate((q_ref, k_ref, v_ref)):
        cols = slice(kind * COL_TILE, (kind + 1) * COL_TILE)
        for rows in _row_subs(h_ref.shape[0]):
            acc = _dot(h_ref[rows, :], w_ref[:, cols])
            if o_ref is v_ref:
                o_ref[rows, :] = acc.astype(o_ref.dtype)
                continue
            cos = cos_ref[rows, :]
            sin = sin_ref[rows, :]
            for hh in range(DIL_HEADS):
                sl = slice(hh * DIL_HD, (hh + 1) * DIL_HD)
                xh = acc[:, sl]
                ms = jnp.mean(xh * xh, axis=-1, keepdims=True)
                y = xh * lax.rsqrt(ms + EPS) * g_ref[kind:kind + 1, :]
                y = y * cos + pltpu.roll(y, DIL_HD // 2, 1) * sin
                o_ref[rows, sl] = y.astype(o_ref.dtype)


def _dil_proj(h, w, gains, cos, sin):
    t, d = h.shape
    tm = TOK_TILE
    tok = lambda i: (i, 0)
    fixed = lambda i: (0, 0)
    out = jax.ShapeDtypeStruct((t, COL_TILE), BF16)
    return pl.pallas_call(
        _dil_proj_kernel,
        grid=(t // tm,),
        in_specs=[pl.BlockSpec((tm, d), tok),
                  pl.BlockSpec(w.shape, fixed),
                  pl.BlockSpec(gains.shape, fixed),
                  pl.BlockSpec((tm, DIL_HD), tok),
                  pl.BlockSpec((tm, DIL_HD), tok)],
        out_specs=[pl.BlockSpec((tm, COL_TILE), tok)] * 3,
        out_shape=[out] * 3,
        compiler_params=_params("parallel"),
        name="dil_proj",
    )(h, w, gains, cos, sin)


def _gla_kernel(qin_ref, qmid_ref, kmid_ref, kend_ref, dec_ref, v_ref, r_ref, gain_ref,
                o_ref, st_ref):
    @pl.when(pl.program_id(0) == 0)
    def _():
        st_ref[...] = jnp.zeros_like(st_ref)

    row = lax.broadcasted_iota(jnp.int32, (GLA_BLOCK, GLA_BLOCK), 0)
    col = lax.broadcasted_iota(jnp.int32, (GLA_BLOCK, GLA_BLOCK), 1)
    causal = col <= row
    for b in range(qin_ref.shape[0]):
        for hh in range(GLA_HEADS):
            ks = slice(hh * GLA_DK, (hh + 1) * GLA_DK)
            vs = slice(hh * GLA_DV, (hh + 1) * GLA_DV)
            v = v_ref[b, :, vs]
            attn = _dot_nt(qmid_ref[b, :, ks], kmid_ref[b, :, ks])
            attn = jnp.where(causal, attn, 0.0).astype(BF16)
            st = st_ref[b, hh]
            o = _dot(attn, v) + _dot_nt(qin_ref[b, :, ks], st.astype(BF16))
            st_ref[b, hh] = st * dec_ref[b, 0, 0:1, ks] + _dot_tn(v, kend_ref[b, :, ks])
            ms = jnp.mean(o * o, axis=-1, keepdims=True)
            o = o * lax.rsqrt(ms + EPS) * gain_ref[...] * r_ref[b, :, vs].astype(F32)
            o_ref[b, :, vs] = o.astype(o_ref.dtype)


def _gla(qin, qmid, kmid, kend, dec, vg, gain, batch):
    t = qin.shape[0]
    seq = t // batch
    nblk = seq // GLA_BLOCK
    qin, qmid, kmid, kend = (a.reshape(batch, seq, GLA_QK) for a in (qin, qmid, kmid, kend))
    dec = dec.reshape(batch, nblk, 8, GLA_QK)
    vg = vg.reshape(batch, seq, vg.shape[1])
    qk_spec = pl.BlockSpec((batch, GLA_BLOCK, GLA_QK), lambda n: (0, n, 0))
    o = pl.pallas_call(
        _gla_kernel,
        grid=(nblk,),
        in_specs=[qk_spec] * 4
        + [pl.BlockSpec((batch, 1, 8, GLA_QK), lambda n: (0, n, 0, 0)),
           pl.BlockSpec((batch, GLA_BLOCK, GLA_V), lambda n: (0, n, VG_VA * COL_TILE // GLA_V)),
           pl.BlockSpec((batch, GLA_BLOCK, GLA_V), lambda n: (0, n, VG_RA * COL_TILE // GLA_V)),
           pl.BlockSpec((1, GLA_DV), lambda n: (0, 0))],
        out_specs=pl.BlockSpec((batch, GLA_BLOCK, GLA_V), lambda n: (0, n, 0)),
        out_shape=jax.ShapeDtypeStruct((batch, seq, GLA_V), BF16),
        scratch_shapes=[pltpu.VMEM((batch, GLA_HEADS, GLA_DV, GLA_DK), F32)],
        compiler_params=_params("arbitrary"),
        name="gla_recurrence",
    )(qin, qmid, kmid, kend, dec, vg, vg, gain)
    return o.reshape(t, GLA_V)


def _dil_attn_kernel(q_ref, kp_ref, kc_ref, vp_ref, vc_ref, o_ref, lse_ref):
    n_qblk = q_ref.shape[1] // ATT_BLOCK
    row = lax.broadcasted_iota(jnp.int32, (ATT_BLOCK, 2 * ATT_BLOCK), 0)
    col = lax.broadcasted_iota(jnp.int32, (ATT_BLOCK, 2 * ATT_BLOCK), 1)
    band = (col >= row) & (col <= row + ATT_BLOCK)
    band_first = band & ((col >= ATT_BLOCK) | (pl.program_id(1) > 0))
    ones = jnp.ones((2 * ATT_BLOCK, DIL_HD), BF16)
    lane = lax.broadcasted_iota(jnp.int32, (ATT_BLOCK, LANES), 1)
    for a in range(n_qblk):
        rows = slice(a * ATT_BLOCK, (a + 1) * ATT_BLOCK)
        lse_all = jnp.zeros((ATT_BLOCK, LANES), F32)
        for hh in range(DIL_HEADS):
            sl = slice(hh * DIL_HD, (hh + 1) * DIL_HD)
            if a == 0:
                k_win = jnp.concatenate([kp_ref[0, :, sl], kc_ref[0, :ATT_BLOCK, sl]], axis=0)
                v_win = jnp.concatenate([vp_ref[0, :, sl], vc_ref[0, :ATT_BLOCK, sl]], axis=0)
            else:
                win = slice((a - 1) * ATT_BLOCK, (a + 1) * ATT_BLOCK)
                k_win = kc_ref[0, win, sl]
                v_win = vc_ref[0, win, sl]
            s = _dot_nt(q_ref[0, rows, sl], k_win)
            s = jnp.where(band_first if a == 0 else band, s, NEG)
            m = jnp.max(s, axis=-1, keepdims=True)
            p = jnp.exp(s - m).astype(BF16)
            ol = _dot(p, jnp.concatenate([v_win, ones], axis=1))
            l = ol[:, DIL_HD:]
            o_ref[0, rows, sl] = (ol[:, :DIL_HD] / l).astype(o_ref.dtype)
            lse_all = jnp.where(lane == hh, m + jnp.log(l), lse_all)
        lse_ref[0, rows, :] = lse_all


def _dil_attn(q, k, v, n_sub, dil):
    t = q.shape[0]
    sub_len = t // n_sub
    qb = min(sub_len, ATT_SUPER)
    n_qblk = qb // ATT_BLOCK
    q3, k3, v3 = (a.reshape(n_sub, sub_len, COL_TILE) for a in (q, k, v))
    blk = (1, qb, COL_TILE)
    cur = lambda s, i: (s, i, 0)
    prev = lambda s, i: (s, jnp.maximum(i * n_qblk - 1, 0), 0)
    prev_blk = (1, ATT_BLOCK, COL_TILE)
    o, lse = pl.pallas_call(
        _dil_attn_kernel,
        grid=(n_sub, sub_len // qb),
        in_specs=[pl.BlockSpec(blk, cur), pl.BlockSpec(prev_blk, prev), pl.BlockSpec(blk, cur),
                  pl.BlockSpec(prev_blk, prev), pl.BlockSpec(blk, cur)],
        out_specs=[pl.BlockSpec(blk, cur), pl.BlockSpec((1, qb, LANES), cur)],
        out_shape=[jax.ShapeDtypeStruct((n_sub, sub_len, COL_TILE), BF16),
                   jax.ShapeDtypeStruct((n_sub, sub_len, LANES), F32)],
        compiler_params=_params("parallel", "arbitrary"),
        name=f"dil_attn_d{dil}",
    )(q3, k3, k3, v3, v3)
    return o, lse


def _merge_kernel(x_ref, oa_ref, o0_ref, o1_ref, o2_ref, l0_ref, l1_ref, l2_ref,
                  sz_ref, sga_ref, sgd_ref, wga_ref, wdo_ref, wo_ref, out_ref, og_ref, lg_ref):
    tm = x_ref.shape[1]
    for g, ((_, dil), o_ref, l_ref) in enumerate(zip(DIL_GROUPS, (o0_ref, o1_ref, o2_ref),
                                                     (l0_ref, l1_ref, l2_ref))):
        for r in range(dil):
            dst = pl.ds(r, tm // dil, stride=dil)
            lg_ref[g, dst, :] = l_ref[0, r]
            for hh in range(DIL_HEADS):
                og_ref[g, hh, dst, :] = o_ref[0, r, :, hh * DIL_HD:(hh + 1) * DIL_HD].astype(F32)
    n_groups = len(DIL_GROUPS)
    for rows in _row_subs(tm):
        lses = [lg_ref[g, rows, :] for g in range(n_groups)]
        m = jnp.maximum(jnp.maximum(lses[0], lses[1]), lses[2])
        es = [jnp.exp(l - m) for l in lses]
        den = es[0] + es[1] + es[2]
        ws = [e / den for e in es]
        heads = []
        for hh in range(DIL_HEADS):
            acc = ws[0][:, hh:hh + 1] * og_ref[0, hh, rows, :]
            for g in range(1, n_groups):
                acc = acc + ws[g][:, hh:hh + 1] * og_ref[g, hh, rows, :]
            heads.append(acc)
        o_d = (jnp.concatenate(heads, axis=-1) * sz_ref[rows, :].astype(F32)).astype(BF16)
        y_a = _dot(oa_ref[rows, :], wga_ref[...])
        y_d = _dot(o_d, wdo_ref[...])
        y = sga_ref[rows, :].astype(F32) * y_a + sgd_ref[rows, :].astype(F32) * y_d
        out_ref[0, rows, :] = x_ref[0, rows, :] + _dot(y.astype(BF16), wo_ref[...])


def _merge(x, o_a, o_ds, lses, vg, wga, wdo, wo):
    batch, seq, d = x.shape
    tm = MERGE_TOK_TILE
    nt = seq // tm
    tok = lambda b, i: (b * nt + i, 0)
    fixed = lambda b, i: (0, 0)
    dil_spec = lambda dil, w: pl.BlockSpec((1, dil, tm // dil, w), lambda b, i: (b, 0, i, 0))
    o_ds = [o.reshape(batch, dil, seq // dil, DIL_OUT) for o, (_, dil) in zip(o_ds, DIL_GROUPS)]
    lses = [l.reshape(batch, dil, seq // dil, LANES) for l, (_, dil) in zip(lses, DIL_GROUPS)]
    return pl.pallas_call(
        _merge_kernel,
        grid=(batch, nt),
        in_specs=[pl.BlockSpec((1, tm, d), lambda b, i: (b, i, 0)),
                  pl.BlockSpec((tm, GLA_V), tok)]
        + [dil_spec(dil, DIL_OUT) for _, dil in DIL_GROUPS]
        + [dil_spec(dil, LANES) for _, dil in DIL_GROUPS]
        + [pl.BlockSpec((tm, DIL_OUT), lambda b, i: (b * nt + i, VG_ZD)),
           pl.BlockSpec((tm, d), lambda b, i: (b * nt + i, VG_GA * COL_TILE // D_MODEL)),
           pl.BlockSpec((tm, d), lambda b, i: (b * nt + i, VG_GD * COL_TILE // D_MODEL)),
           pl.BlockSpec(wga.shape, fixed),
           pl.BlockSpec(wdo.shape, fixed),
           pl.BlockSpec(wo.shape, fixed)],
        out_specs=pl.BlockSpec((1, tm, d), lambda b, i: (b, i, 0)),
        out_shape=jax.ShapeDtypeStruct(x.shape, x.dtype),
        scratch_shapes=[pltpu.VMEM((len(DIL_GROUPS), DIL_HEADS, tm, DIL_HD), F32),
                        pltpu.VMEM((len(DIL_GROUPS), tm, LANES), F32)],
        compiler_params=_params("parallel", "parallel"),
        name="merge_out",
    )(x, o_a, *o_ds, *lses, vg, vg, vg, wga, wdo, wo)


def _split_w_in(w):
    offs = [0]
    for s in IN_SPLIT_SIZES:
        offs.append(offs[-1] + s)
    return [w[:, offs[i]:offs[i + 1]] for i in range(len(IN_SPLIT_SIZES))]


def kernel(x, positions, norm_gain, w_in, gla_w_a2, gla_b_a, gla_out_gain, dil_q_gain, dil_k_gain,
           w_gla_out, w_dil_out, w_o):
    batch, seq, d = x.shape
    t = batch * seq
    half = DIL_HD // 2
    inv_freq = ROPE_THETA ** (-jnp.arange(half, dtype=F32) / half)
    freq = jnp.concatenate([inv_freq, inv_freq]).reshape(1, DIL_HD)
    pos = positions.astype(F32).reshape(batch, seq, 1)
    for layer in range(norm_gain.shape[0]):
        (w_qa, w_ka, w_va, w_ra, w_alr, w_qd, w_kd, w_vd, w_zd, w_ga, w_gd) = _split_w_in(w_in[layer])
        wqk = jnp.concatenate([w_qa, w_ka], axis=1).astype(BF16)
        walr = jnp.pad(w_alr, ((0, 0), (0, LANES - GLA_RANK))).astype(BF16)
        wa2 = jnp.pad(gla_w_a2[layer], ((0, LANES - GLA_RANK), (0, 0))).astype(BF16)
        ba = gla_b_a[layer].reshape(1, GLA_QK)
        g_dqk = jnp.stack([dil_q_gain[layer] * (DIL_HD ** -0.5), dil_k_gain[layer]])
        w_vg = jnp.concatenate([w_va, w_ra, w_ga, w_gd, w_zd], axis=1).astype(BF16)

        hs, coss, sins = _rmsnorm(x, norm_gain[layer], pos, freq)
        qin, qmid, kmid, kend, dec, vg = _nat_proj(hs[0].reshape(t, d), wqk, walr, wa2, ba, w_vg)
        o_a = _gla(qin, qmid, kmid, kend, dec, vg, gla_out_gain[layer].reshape(1, GLA_DV), batch)
        o_ds, lses = [], []
        for g, (win, dil) in enumerate(DIL_GROUPS):
            assert win // dil == ATT_BLOCK
            cols = slice(g * COL_TILE, (g + 1) * COL_TILE)
            w_g = jnp.concatenate([w_qd[:, cols], w_kd[:, cols], w_vd[:, cols]], axis=1).astype(BF16)
            q_g, k_g, v_g = _dil_proj(hs[g].reshape(t, d), w_g, g_dqk,
                                      coss[g].reshape(t, DIL_HD), sins[g].reshape(t, DIL_HD))
            o_g, lse_g = _dil_attn(q_g, k_g, v_g, batch * dil, dil)
            o_ds.append(o_g)
            lses.append(lse_g)
        x = _merge(x, o_a, o_ds, lses, vg,
                   w_gla_out[layer].astype(BF16), w_dil_out[layer].astype(BF16),
                   w_o[layer].astype(BF16))
    return x
```

```python
import jax
import jax.numpy as jnp
from jax import lax
from jax.experimental import pallas as pl
from jax.experimental.pallas import tpu as pltpu

D_MODEL = 1024
EPS = 1e-6
ROPE_THETA = 10000.0
GLA_HEADS = 4
GLA_DK = 128
GLA_DV = 256
GLA_RANK = 16
GLA_TAU = 16.0
GLA_QK = GLA_HEADS * GLA_DK
GLA_V = GLA_HEADS * GLA_DV
DIL_GROUPS = ((128, 1), (512, 4), (2048, 16))
DIL_HEADS = 4
DIL_HD = 128
DIL_QK = len(DIL_GROUPS) * DIL_HEADS * DIL_HD
DIL_OUT = DIL_HEADS * DIL_HD
IN_SPLIT_SIZES = (GLA_QK, GLA_QK, GLA_V, GLA_V, GLA_RANK,
                  DIL_QK, DIL_QK, DIL_QK, DIL_OUT, D_MODEL, D_MODEL)

LANES = 128
GLA_BLOCK = 128
GLA_MID = GLA_BLOCK // 2
ATT_BLOCK = 128
ATT_SUPER = 512
TOK_TILE = 1024
ROW_SUB = 256
GLA_TOK_TILE = 512
MERGE_TOK_TILE = 512
COL_TILE = DIL_HEADS * DIL_HD
VMEM_LIMIT_BYTES = 48 * 1024 * 1024

F32 = jnp.float32
BF16 = jnp.bfloat16
NEG = -1e30

VG_VA, VG_RA, VG_GA, VG_GD, VG_ZD = 0, 2, 4, 6, 8
VG_ACTS = ("id", "id", "silu", "silu", "sigmoid", "sigmoid", "sigmoid", "sigmoid", "silu")


def _params(*sem):
    return pltpu.CompilerParams(dimension_semantics=sem, vmem_limit_bytes=VMEM_LIMIT_BYTES)


def _dot(a, b):
    return jnp.dot(a, b, preferred_element_type=F32)


def _dot_nt(a, b):
    return lax.dot_general(a, b, (((1,), (1,)), ((), ())), preferred_element_type=F32)


def _dot_tn(a, b):
    return lax.dot_general(a, b, (((0,), (0,)), ((), ())), preferred_element_type=F32)


def _sigmoid(x):
    return 0.5 * jnp.tanh(0.5 * x) + 0.5


def _row_subs(n):
    return [slice(r * ROW_SUB, (r + 1) * ROW_SUB) for r in range(n // ROW_SUB)]


def _rmsnorm_kernel(x_ref, g_ref, pos_ref, freq_ref, *refs):
    n = len(DIL_GROUPS)
    h_refs, cos_refs, sin_refs, hs_ref = refs[:n], refs[n:2 * n], refs[2 * n:3 * n], refs[3 * n]
    x = x_ref[0]
    ms = jnp.mean(x * x, axis=-1, keepdims=True)
    hf = x * lax.rsqrt(ms + EPS) * g_ref[...]
    tm, d = x.shape
    n_planes = d // LANES
    ang = pos_ref[0] * freq_ref[...]
    lane = lax.broadcasted_iota(jnp.int32, ang.shape, 1)
    cos = jnp.cos(ang)
    sin = jnp.where(lane < DIL_HD // 2, -jnp.sin(ang), jnp.sin(ang))
    for c in range(n_planes):
        hs_ref[c] = hf[:, c * LANES:(c + 1) * LANES]
    hs_ref[n_planes] = cos
    hs_ref[n_planes + 1] = sin
    for (_, dil), h_ref, cos_ref, sin_ref in zip(DIL_GROUPS, h_refs, cos_refs, sin_refs):
        if dil == 1:
            h_ref[0, 0] = hf.astype(h_ref.dtype)
            cos_ref[0, 0] = cos
            sin_ref[0, 0] = sin
            continue
        for r in range(dil):
            src = pl.ds(r, tm // dil, stride=dil)
            for c in range(n_planes):
                h_ref[0, r, :, c * LANES:(c + 1) * LANES] = hs_ref[c, src, :].astype(h_ref.dtype)
            cos_ref[0, r] = hs_ref[n_planes, src, :]
            sin_ref[0, r] = hs_ref[n_planes + 1, src, :]


def _rmsnorm(x, gain, pos, freq):
    batch, seq, d = x.shape
    tm = TOK_TILE
    dil_spec = lambda dil, w: pl.BlockSpec((1, dil, tm // dil, w), lambda b, i: (b, 0, i, 0))
    dil_shape = lambda dil, w, dt: jax.ShapeDtypeStruct((batch, dil, seq // dil, w), dt)
    dils = [dil for _, dil in DIL_GROUPS]
    outs = pl.pallas_call(
        _rmsnorm_kernel,
        grid=(batch, seq // tm),
        in_specs=[pl.BlockSpec((1, tm, d), lambda b, i: (b, i, 0)),
                  pl.BlockSpec((1, d), lambda b, i: (0, 0)),
                  pl.BlockSpec((1, tm, 1), lambda b, i: (b, i, 0)),
                  pl.BlockSpec((1, DIL_HD), lambda b, i: (0, 0))],
        out_specs=[dil_spec(dil, d) for dil in dils] + [dil_spec(dil, DIL_HD) for dil in dils] * 2,
        out_shape=[dil_shape(dil, d, BF16) for dil in dils]
        + [dil_shape(dil, DIL_HD, F32) for dil in dils] * 2,
        scratch_shapes=[pltpu.VMEM((d // LANES + 2, tm, LANES), F32)],
        compiler_params=_params("parallel", "parallel"),
        name="in_rmsnorm",
    )(x, gain.reshape(1, d), pos, freq)
    n = len(dils)
    return outs[:n], outs[n:2 * n], outs[2 * n:]


def _nat_proj_kernel(h_ref, wqk_ref, walr_ref, wa2_ref, ba_ref, wvg_ref,
                     qin_ref, qmid_ref, kmid_ref, kend_ref, dec_ref, vg_ref):
    row = lax.broadcasted_iota(jnp.int32, (GLA_BLOCK, GLA_BLOCK), 0)
    col = lax.broadcasted_iota(jnp.int32, (GLA_BLOCK, GLA_BLOCK), 1)
    tri = (col <= row).astype(BF16)
    n_blk = h_ref.shape[0] // GLA_BLOCK
    n_vg = len(VG_ACTS)

    def vg_tiles(c, h, lo_j, hi_j):
        rows = slice(c * GLA_BLOCK, (c + 1) * GLA_BLOCK)
        for j in range(lo_j, hi_j):
            cols = slice(j * COL_TILE, (j + 1) * COL_TILE)
            acc = _dot(h, wvg_ref[:, cols])
            if VG_ACTS[j] == "silu":
                acc = acc * _sigmoid(acc)
            elif VG_ACTS[j] == "sigmoid":
                acc = _sigmoid(acc)
            vg_ref[rows, cols] = acc.astype(vg_ref.dtype)

    def decay_tail(c, qk, hi, lo):
        rows = slice(c * GLA_BLOCK, (c + 1) * GLA_BLOCK)
        b = _dot(tri, hi) + _dot(tri, lo)
        b_mid = b[GLA_MID - 1:GLA_MID]
        b_last = b[GLA_BLOCK - 1:GLA_BLOCK]
        q = qk[:, :GLA_QK] * (GLA_DK ** -0.5)
        k = qk[:, GLA_QK:]
        qin_ref[rows, :] = (q * jnp.exp(b)).astype(BF16)
        qmid_ref[rows, :] = (q * jnp.exp(b - b_mid)).astype(BF16)
        kmid_ref[rows, :] = (k * jnp.exp(b_mid - b)).astype(BF16)
        kend_ref[rows, :] = (k * jnp.exp(b_last - b)).astype(BF16)
        dec_ref[c] = jnp.broadcast_to(jnp.exp(b_last), (8, GLA_QK))

    pending = None
    for c in range(n_blk):
        h = h_ref[c * GLA_BLOCK:(c + 1) * GLA_BLOCK, :]
        alr = _dot(h, walr_ref[...]).astype(BF16)
        vg_tiles(c, h, 0, n_vg // 3)
        z = _dot(alr, wa2_ref[...]) + ba_ref[...]
        la = (jnp.minimum(z, 0.0) - jnp.log1p(jnp.exp(-jnp.abs(z)))) * (1.0 / GLA_TAU)
        hi = la.astype(BF16)
        lo = (la - hi.astype(F32)).astype(BF16)
        vg_tiles(c, h, n_vg // 3, 2 * n_vg // 3)
        if pending is not None:
            decay_tail(*pending)
        qk = _dot(h, wqk_ref[...])
        vg_tiles(c, h, 2 * n_vg // 3, n_vg)
        pending = (c, qk, hi, lo)
    decay_tail(*pending)


def _nat_proj(h, wqk, walr, wa2, ba, wvg):
    t, d = h.shape
    tm = GLA_TOK_TILE
    nblk = tm // GLA_BLOCK
    tok = lambda i: (i, 0)
    fixed = lambda i: (0, 0)
    qk_shape = jax.ShapeDtypeStruct((t, GLA_QK), BF16)
    return pl.pallas_call(
        _nat_proj_kernel,
        grid=(t // tm,),
        in_specs=[pl.BlockSpec((tm, d), tok),
                  pl.BlockSpec(wqk.shape, fixed),
                  pl.BlockSpec(walr.shape, fixed),
                  pl.BlockSpec(wa2.shape, fixed),
                  pl.BlockSpec(ba.shape, fixed),
                  pl.BlockSpec(wvg.shape, fixed)],
        out_specs=[pl.BlockSpec((tm, GLA_QK), tok)] * 4
        + [pl.BlockSpec((nblk, 8, GLA_QK), lambda i: (i, 0, 0)),
           pl.BlockSpec((tm, wvg.shape[1]), tok)],
        out_shape=[qk_shape] * 4 + [jax.ShapeDtypeStruct((t // GLA_BLOCK, 8, GLA_QK), F32),
                                    jax.ShapeDtypeStruct((t, wvg.shape[1]), BF16)],
        compiler_params=_params("parallel"),
        name="nat_proj",
    )(h, wqk, walr, wa2, ba, wvg)


def _dil_proj_kernel(h_ref, w_ref, g_ref, cos_ref, sin_ref, q_ref, k_ref, v_ref):
    def epilogue(kind, rows, acc):
        o_ref = (q_ref, k_ref, v_ref)[kind]
        if o_ref is v_ref:
            o_ref[rows, :] = acc.astype(o_ref.dtype)
            return
        cos = cos_ref[rows, :]
        sin = sin_ref[rows, :]
        for hh in range(DIL_HEADS):
            sl = slice(hh * DIL_HD, (hh + 1) * DIL_HD)
            xh = acc[:, sl]
            ms = jnp.mean(xh * xh, axis=-1, keepdims=True)
            y = xh * lax.rsqrt(ms + EPS) * g_ref[kind:kind + 1, :]
            y = y * cos + pltpu.roll(y, DIL_HD // 2, 1) * sin
            o_ref[rows, sl] = y.astype(o_ref.dtype)

    for kind in range(3):
        cols = slice(kind * COL_TILE, (kind + 1) * COL_TILE)
        for rows in _row_subs(h_ref.shape[0]):
            epilogue(kind, rows, _dot(h_ref[rows, :], w_ref[:, cols]))


def _dil_proj(h, w, gains, cos, sin):
    t, d = h.shape
    tm = TOK_TILE
    tok = lambda i: (i, 0)
    fixed = lambda i: (0, 0)
    out = jax.ShapeDtypeStruct((t, COL_TILE), BF16)
    return pl.pallas_call(
        _dil_proj_kernel,
        grid=(t // tm,),
        in_specs=[pl.BlockSpec((tm, d), tok),
                  pl.BlockSpec(w.shape, fixed),
                  pl.BlockSpec(gains.shape, fixed),
                  pl.BlockSpec((tm, DIL_HD), tok),
                  pl.BlockSpec((tm, DIL_HD), tok)],
        out_specs=[pl.BlockSpec((tm, COL_TILE), tok)] * 3,
        out_shape=[out] * 3,
        compiler_params=_params("parallel"),
        name="dil_proj",
    )(h, w, gains, cos, sin)


def _gla_kernel(qin_ref, qmid_ref, kmid_ref, kend_ref, dec_ref, v_ref, r_ref, gain_ref,
                o_ref, st_ref):
    @pl.when(pl.program_id(0) == 0)
    def _():
        st_ref[...] = jnp.zeros_like(st_ref)

    row = lax.broadcasted_iota(jnp.int32, (GLA_BLOCK, GLA_BLOCK), 0)
    col = lax.broadcasted_iota(jnp.int32, (GLA_BLOCK, GLA_BLOCK), 1)
    causal = col <= row
    for b in range(qin_ref.shape[0]):
        for hh in range(GLA_HEADS):
            ks = slice(hh * GLA_DK, (hh + 1) * GLA_DK)
            vs = slice(hh * GLA_DV, (hh + 1) * GLA_DV)
            v = v_ref[b, :, vs]
            attn = _dot_nt(qmid_ref[b, :, ks], kmid_ref[b, :, ks])
            attn = jnp.where(causal, attn, 0.0).astype(BF16)
            st = st_ref[b, hh]
            o = _dot(attn, v) + _dot_nt(qin_ref[b, :, ks], st.astype(BF16))
            st_ref[b, hh] = st * dec_ref[b, 0, 0:1, ks] + _dot_tn(v, kend_ref[b, :, ks])
            ms = jnp.mean(o * o, axis=-1, keepdims=True)
            o = o * lax.rsqrt(ms + EPS) * gain_ref[...] * r_ref[b, :, vs].astype(F32)
            o_ref[b, :, vs] = o.astype(o_ref.dtype)


def _gla(qin, qmid, kmid, kend, dec, vg, gain, batch):
    t = qin.shape[0]
    seq = t // batch
    nblk = seq // GLA_BLOCK
    qin, qmid, kmid, kend = (a.reshape(batch, seq, GLA_QK) for a in (qin, qmid, kmid, kend))
    dec = dec.reshape(batch, nblk, 8, GLA_QK)
    vg = vg.reshape(batch, seq, vg.shape[1])
    qk_spec = pl.BlockSpec((batch, GLA_BLOCK, GLA_QK), lambda n: (0, n, 0))
    o = pl.pallas_call(
        _gla_kernel,
        grid=(nblk,),
        in_specs=[qk_spec] * 4
        + [pl.BlockSpec((batch, 1, 8, GLA_QK), lambda n: (0, n, 0, 0)),
           pl.BlockSpec((batch, GLA_BLOCK, GLA_V), lambda n: (0, n, VG_VA * COL_TILE // GLA_V)),
           pl.BlockSpec((batch, GLA_BLOCK, GLA_V), lambda n: (0, n, VG_RA * COL_TILE // GLA_V)),
           pl.BlockSpec((1, GLA_DV), lambda n: (0, 0))],
        out_specs=pl.BlockSpec((batch, GLA_BLOCK, GLA_V), lambda n: (0, n, 0)),
        out_shape=jax.ShapeDtypeStruct((batch, seq, GLA_V), BF16),
        scratch_shapes=[pltpu.VMEM((batch, GLA_HEADS, GLA_DV, GLA_DK), F32)],
        compiler_params=_params("arbitrary"),
        name="gla_recurrence",
    )(qin, qmid, kmid, kend, dec, vg, vg, gain)
    return o.reshape(t, GLA_V)


def _dil_attn_kernel(q_ref, kp_ref, kc_ref, vp_ref, vc_ref, o_ref, lse_ref):
    n_qblk = q_ref.shape[1] // ATT_BLOCK
    row = lax.broadcasted_iota(jnp.int32, (ATT_BLOCK, 2 * ATT_BLOCK), 0)
    col = lax.broadcasted_iota(jnp.int32, (ATT_BLOCK, 2 * ATT_BLOCK), 1)
    band = (col >= row) & (col <= row + ATT_BLOCK)
    band_first = band & ((col >= ATT_BLOCK) | (pl.program_id(1) > 0))
    ones = jnp.ones((2 * ATT_BLOCK, DIL_HD), BF16)
    lane = lax.broadcasted_iota(jnp.int32, (ATT_BLOCK, LANES), 1)
    for a in range(n_qblk):
        rows = slice(a * ATT_BLOCK, (a + 1) * ATT_BLOCK)
        lse_all = jnp.zeros((ATT_BLOCK, LANES), F32)
        for hh in range(DIL_HEADS):
            sl = slice(hh * DIL_HD, (hh + 1) * DIL_HD)
            if a == 0:
                k_win = jnp.concatenate([kp_ref[0, :, sl], kc_ref[0, :ATT_BLOCK, sl]], axis=0)
                v_win = jnp.concatenate([vp_ref[0, :, sl], vc_ref[0, :ATT_BLOCK, sl]], axis=0)
            else:
                win = slice((a - 1) * ATT_BLOCK, (a + 1) * ATT_BLOCK)
                k_win = kc_ref[0, win, sl]
                v_win = vc_ref[0, win, sl]
            s = _dot_nt(q_ref[0, rows, sl], k_win)
            s = jnp.where(band_first if a == 0 else band, s, NEG)
            m = jnp.max(s, axis=-1, keepdims=True)
            p = jnp.exp(s - m).astype(BF16)
            ol = _dot(p, jnp.concatenate([v_win, ones], axis=1))
            l = ol[:, DIL_HD:]
            o_ref[0, rows, sl] = (ol[:, :DIL_HD] / l).astype(o_ref.dtype)
            lse_all = jnp.where(lane == hh, m + jnp.log(l), lse_all)
        lse_ref[0, rows, :] = lse_all


def _dil_attn(q, k, v, n_sub, dil):
    t = q.shape[0]
    sub_len = t // n_sub
    qb = min(sub_len, ATT_SUPER)
    n_qblk = qb // ATT_BLOCK
    q3, k3, v3 = (a.reshape(n_sub, sub_len, COL_TILE) for a in (q, k, v))
    blk = (1, qb, COL_TILE)
    cur = lambda s, i: (s, i, 0)
    prev = lambda s, i: (s, jnp.maximum(i * n_qblk - 1, 0), 0)
    prev_blk = (1, ATT_BLOCK, COL_TILE)
    o, lse = pl.pallas_call(
        _dil_attn_kernel,
        grid=(n_sub, sub_len // qb),
        in_specs=[pl.BlockSpec(blk, cur), pl.BlockSpec(prev_blk, prev), pl.BlockSpec(blk, cur),
                  pl.BlockSpec(prev_blk, prev), pl.BlockSpec(blk, cur)],
        out_specs=[pl.BlockSpec(blk, cur), pl.BlockSpec((1, qb, LANES), cur)],
        out_shape=[jax.ShapeDtypeStruct((n_sub, sub_len, COL_TILE), BF16),
                   jax.ShapeDtypeStruct((n_sub, sub_len, LANES), F32)],
        compiler_params=_params("parallel", "arbitrary"),
        name=f"dil_attn_d{dil}",
    )(q3, k3, k3, v3, v3)
    return o, lse


def _merge_kernel(x_ref, oa_ref, o0_ref, o1_ref, o2_ref, l0_ref, l1_ref, l2_ref,
                  sz_ref, sga_ref, sgd_ref, wga_ref, wdo_ref, wo_ref, out_ref, og_ref, lg_ref):
    tm = x_ref.shape[1]
    for g, ((_, dil), o_ref, l_ref) in enumerate(zip(DIL_GROUPS, (o0_ref, o1_ref, o2_ref),
                                                     (l0_ref, l1_ref, l2_ref))):
        for r in range(dil):
            dst = pl.ds(r, tm // dil, stride=dil)
            lg_ref[g, dst, :] = l_ref[0, r]
            for hh in range(DIL_HEADS):
                og_ref[g, hh, dst, :] = o_ref[0, r, :, hh * DIL_HD:(hh + 1) * DIL_HD].astype(F32)
    n_groups = len(DIL_GROUPS)

    def combine(rows):
        lses = [lg_ref[g, rows, :] for g in range(n_groups)]
        m = jnp.maximum(jnp.maximum(lses[0], lses[1]), lses[2])
        es = [jnp.exp(l - m) for l in lses]
        den = es[0] + es[1] + es[2]
        ws = [e / den for e in es]
        heads = []
        for hh in range(DIL_HEADS):
            acc = ws[0][:, hh:hh + 1] * og_ref[0, hh, rows, :]
            for g in range(1, n_groups):
                acc = acc + ws[g][:, hh:hh + 1] * og_ref[g, hh, rows, :]
            heads.append(acc)
        return (jnp.concatenate(heads, axis=-1) * sz_ref[rows, :].astype(F32)).astype(BF16)

    def branches(rows, o_d):
        y_a = _dot(oa_ref[rows, :], wga_ref[...])
        y_d = _dot(o_d, wdo_ref[...])
        return y_a, y_d

    def gate(rows, y_a, y_d):
        return (sga_ref[rows, :].astype(F32) * y_a + sgd_ref[rows, :].astype(F32) * y_d).astype(BF16)

    def project(rows, y):
        out_ref[0, rows, :] = x_ref[0, rows, :] + _dot(y, wo_ref[...])

    for rows in _row_subs(tm):
        project(rows, gate(rows, *branches(rows, combine(rows))))


def _merge(x, o_a, o_ds, lses, vg, wga, wdo, wo):
    batch, seq, d = x.shape
    tm = MERGE_TOK_TILE
    nt = seq // tm
    tok = lambda b, i: (b * nt + i, 0)
    fixed = lambda b, i: (0, 0)
    dil_spec = lambda dil, w: pl.BlockSpec((1, dil, tm // dil, w), lambda b, i: (b, 0, i, 0))
    o_ds = [o.reshape(batch, dil, seq // dil, DIL_OUT) for o, (_, dil) in zip(o_ds, DIL_GROUPS)]
    lses = [l.reshape(batch, dil, seq // dil, LANES) for l, (_, dil) in zip(lses, DIL_GROUPS)]
    return pl.pallas_call(
        _merge_kernel,
        grid=(batch, nt),
        in_specs=[pl.BlockSpec((1, tm, d), lambda b, i: (b, i, 0)),
                  pl.BlockSpec((tm, GLA_V), tok)]
        + [dil_spec(dil, DIL_OUT) for _, dil in DIL_GROUPS]
        + [dil_spec(dil, LANES) for _, dil in DIL_GROUPS]
        + [pl.BlockSpec((tm, DIL_OUT), lambda b, i: (b * nt + i, VG_ZD)),
           pl.BlockSpec((tm, d), lambda b, i: (b * nt + i, VG_GA * COL_TILE // D_MODEL)),
           pl.BlockSpec((tm, d), lambda b, i: (b * nt + i, VG_GD * COL_TILE // D_MODEL)),
           pl.BlockSpec(wga.shape, fixed),
           pl.BlockSpec(wdo.shape, fixed),
           pl.BlockSpec(wo.shape, fixed)],
        out_specs=pl.BlockSpec((1, tm, d), lambda b, i: (b, i, 0)),
        out_shape=jax.ShapeDtypeStruct(x.shape, x.dtype),
        scratch_shapes=[pltpu.VMEM((len(DIL_GROUPS), DIL_HEADS, tm, DIL_HD), F32),
                        pltpu.VMEM((len(DIL_GROUPS), tm, LANES), F32)],
        compiler_params=_params("parallel", "parallel"),
        name="merge_out",
    )(x, o_a, *o_ds, *lses, vg, vg, vg, wga, wdo, wo)


def _split_w_in(w):
    offs = [0]
    for s in IN_SPLIT_SIZES:
        offs.append(offs[-1] + s)
    return [w[:, offs[i]:offs[i + 1]] for i in range(len(IN_SPLIT_SIZES))]


def kernel(x, positions, norm_gain, w_in, gla_w_a2, gla_b_a, gla_out_gain, dil_q_gain, dil_k_gain,
           w_gla_out, w_dil_out, w_o):
    batch, seq, d = x.shape
    t = batch * seq
    half = DIL_HD // 2
    inv_freq = ROPE_THETA ** (-jnp.arange(half, dtype=F32) / half)
    freq = jnp.concatenate([inv_freq, inv_freq]).reshape(1, DIL_HD)
    pos = positions.astype(F32).reshape(batch, seq, 1)
    for layer in range(norm_gain.shape[0]):
        (w_qa, w_ka, w_va, w_ra, w_alr, w_qd, w_kd, w_vd, w_zd, w_ga, w_gd) = _split_w_in(w_in[layer])
        wqk = jnp.concatenate([w_qa, w_ka], axis=1).astype(BF16)
        walr = jnp.pad(w_alr, ((0, 0), (0, LANES - GLA_RANK))).astype(BF16)
        wa2 = jnp.pad(gla_w_a2[layer], ((0, LANES - GLA_RANK), (0, 0))).astype(BF16)
        ba = gla_b_a[layer].reshape(1, GLA_QK)
        g_dqk = jnp.stack([dil_q_gain[layer] * (DIL_HD ** -0.5), dil_k_gain[layer]])
        w_vg = jnp.concatenate([w_va, w_ra, w_ga, w_gd, w_zd], axis=1).astype(BF16)

        hs, coss, sins = _rmsnorm(x, norm_gain[layer], pos, freq)
        qin, qmid, kmid, kend, dec, vg = _nat_proj(hs[0].reshape(t, d), wqk, walr, wa2, ba, w_vg)
        o_a = _gla(qin, qmid, kmid, kend, dec, vg, gla_out_gain[layer].reshape(1, GLA_DV), batch)
        o_ds, lses = [], []
        for g, (win, dil) in enumerate(DIL_GROUPS):
            assert win // dil == ATT_BLOCK
            cols = slice(g * COL_TILE, (g + 1) * COL_TILE)
            w_g = jnp.concatenate([w_qd[:, cols], w_kd[:, cols], w_vd[:, cols]], axis=1).astype(BF16)
            q_g, k_g, v_g = _dil_proj(hs[g].reshape(t, d), w_g, g_dqk,
                                      coss[g].reshape(t, DIL_HD), sins[g].reshape(t, DIL_HD))
            o_g, lse_g = _dil_attn(q_g, k_g, v_g, batch * dil, dil)
            o_ds.append(o_g)
            lses.append(lse_g)
        x = _merge(x, o_a, o_ds, lses, vg,
                   w_gla_out[layer].astype(BF16), w_dil_out[layer].astype(BF16),
                   w_o[layer].astype(BF16))
    return x
```

```python
import functools

import jax
import jax.numpy as jnp
from jax import lax
from jax.experimental import pallas as pl
from jax.experimental.pallas import tpu as pltpu

D_MODEL = 1024
EPS = 1e-6
ROPE_THETA = 10000.0
GLA_HEADS = 4
GLA_DK = 128
GLA_DV = 256
GLA_RANK = 16
GLA_TAU = 16.0
GLA_QK = GLA_HEADS * GLA_DK
GLA_V = GLA_HEADS * GLA_DV
DIL_GROUPS = ((128, 1), (512, 4), (2048, 16))
DIL_HEADS = 4
DIL_HD = 128
DIL_QK = len(DIL_GROUPS) * DIL_HEADS * DIL_HD
DIL_OUT = DIL_HEADS * DIL_HD
IN_SPLIT_SIZES = (GLA_QK, GLA_QK, GLA_V, GLA_V, GLA_RANK,
                  DIL_QK, DIL_QK, DIL_QK, DIL_OUT, D_MODEL, D_MODEL)

LANES = 128
GLA_BLOCK = 128
GLA_MID = GLA_BLOCK // 2
ATT_BLOCK = 128
ATT_SUPER = 1024
NORM_TOK_TILE = 512
TOK_TILE = 1024
ROW_SUB = 256
GLA_TOK_TILE = 1024
MERGE_TOK_TILE = 512
COL_TILE = DIL_HEADS * DIL_HD
VMEM_LIMIT_BYTES = 48 * 1024 * 1024

F32 = jnp.float32
BF16 = jnp.bfloat16
NEG = -1e30

GATE_GA, GATE_GD, GATE_ZD = 0, 2, 4

(_QA, _KA, _VA, _RA, _ALR, _QD, _KD, _VD, _ZD, _GA, _GD) = (
    sum(IN_SPLIT_SIZES[:i]) for i in range(len(IN_SPLIT_SIZES)))


def _block_index(offset, width):
    assert offset % width == 0
    return offset // width


LO_QK, LO_V, LO_R = _block_index(_QA, 2 * GLA_QK), _block_index(_VA, GLA_V), _block_index(_RA, GLA_V)
HI_QD = _block_index(_QD - _QD, COL_TILE)
HI_ZD = _block_index(_ZD - _QD, DIL_OUT)
HI_GA, HI_GD = _block_index(_GA - _QD, D_MODEL), _block_index(_GD - _QD, D_MODEL)
assert _KA == _QA + GLA_QK and _KD - _QD == DIL_QK and _VD - _KD == DIL_QK


def _w_cols(width, index):
    return pl.BlockSpec((D_MODEL, width), lambda *_: (0, index))


def _params(*sem):
    return pltpu.CompilerParams(dimension_semantics=sem, vmem_limit_bytes=VMEM_LIMIT_BYTES)


def _dot(a, b):
    return jnp.dot(a, b, preferred_element_type=F32)


def _dot_nt(a, b):
    return lax.dot_general(a, b, (((1,), (1,)), ((), ())), preferred_element_type=F32)


def _dot_tn(a, b):
    return lax.dot_general(a, b, (((0,), (0,)), ((), ())), preferred_element_type=F32)


def _sigmoid(x):
    return 0.5 * jnp.tanh(0.5 * x) + 0.5


def _row_subs(n):
    return [slice(r * ROW_SUB, (r + 1) * ROW_SUB) for r in range(n // ROW_SUB)]


def _rmsnorm_kernel(x_ref, g_ref, pos_ref, freq_ref, *refs):
    n = len(DIL_GROUPS)
    h_refs, cos_refs, sin_refs, hs_ref = refs[:n], refs[n:2 * n], refs[2 * n:3 * n], refs[3 * n]
    x = x_ref[0]
    ms = jnp.mean(x * x, axis=-1, keepdims=True)
    hf = x * lax.rsqrt(ms + EPS) * g_ref[...]
    tm, d = x.shape
    n_planes = d // LANES
    ang = pos_ref[0] * freq_ref[...]
    lane = lax.broadcasted_iota(jnp.int32, ang.shape, 1)
    cos = jnp.cos(ang)
    sin = jnp.where(lane < DIL_HD // 2, -jnp.sin(ang), jnp.sin(ang))
    for c in range(n_planes):
        hs_ref[c] = hf[:, c * LANES:(c + 1) * LANES]
    hs_ref[n_planes] = cos
    hs_ref[n_planes + 1] = sin
    for (_, dil), h_ref, cos_ref, sin_ref in zip(DIL_GROUPS, h_refs, cos_refs, sin_refs):
        if dil == 1:
            h_ref[0, 0] = hf.astype(h_ref.dtype)
            cos_ref[0, 0] = cos
            sin_ref[0, 0] = sin
            continue
        for r in range(dil):
            src = pl.ds(r, tm // dil, stride=dil)
            for c in range(n_planes):
                h_ref[0, r, :, c * LANES:(c + 1) * LANES] = hs_ref[c, src, :].astype(h_ref.dtype)
            cos_ref[0, r] = hs_ref[n_planes, src, :]
            sin_ref[0, r] = hs_ref[n_planes + 1, src, :]


def _rmsnorm(x, gain, pos, freq):
    batch, seq, d = x.shape
    tm = NORM_TOK_TILE
    dil_spec = lambda dil, w: pl.BlockSpec((1, dil, tm // dil, w), lambda b, i: (b, 0, i, 0))
    dil_shape = lambda dil, w, dt: jax.ShapeDtypeStruct((batch, dil, seq // dil, w), dt)
    dils = [dil for _, dil in DIL_GROUPS]
    outs = pl.pallas_call(
        _rmsnorm_kernel,
        grid=(batch, seq // tm),
        in_specs=[pl.BlockSpec((1, tm, d), lambda b, i: (b, i, 0)),
                  pl.BlockSpec((1, d), lambda b, i: (0, 0)),
                  pl.BlockSpec((1, tm, 1), lambda b, i: (b, i, 0)),
                  pl.BlockSpec((1, DIL_HD), lambda b, i: (0, 0))],
        out_specs=[dil_spec(dil, d) for dil in dils] + [dil_spec(dil, DIL_HD) for dil in dils] * 2,
        out_shape=[dil_shape(dil, d, BF16) for dil in dils]
        + [dil_shape(dil, DIL_HD, F32) for dil in dils] * 2,
        scratch_shapes=[pltpu.VMEM((d // LANES + 2, tm, LANES), F32)],
        compiler_params=_params("parallel", "parallel"),
        name="in_rmsnorm",
    )(x, gain.reshape(1, d), pos, freq)
    n = len(dils)
    return outs[:n], outs[n:2 * n], outs[2 * n:]


def _nat_gla_kernel(h_ref, wqk_ref, wv_ref, wr_ref, wga_ref, wgd_ref, wzd_ref, walr_ref, wa2_ref,
                    ba_ref, gain_ref, oa_ref, gates_ref, st_ref, *, tiles_per_seq):
    @pl.when(pl.program_id(0) % tiles_per_seq == 0)
    def _():
        st_ref[...] = jnp.zeros_like(st_ref)

    row = lax.broadcasted_iota(jnp.int32, (GLA_BLOCK, GLA_BLOCK), 0)
    col = lax.broadcasted_iota(jnp.int32, (GLA_BLOCK, GLA_BLOCK), 1)
    causal = col <= row
    tri = causal.astype(BF16)
    n_blk = h_ref.shape[0] // GLA_BLOCK

    gate_plan = ((wga_ref, 0, "sigmoid"), (wga_ref, 1, "sigmoid"), (wgd_ref, 0, "sigmoid"),
                 (wgd_ref, 1, "sigmoid"), (wzd_ref, 0, "silu"))

    def gate_tiles(c, h, lo_j, hi_j):
        rows = slice(c * GLA_BLOCK, (c + 1) * GLA_BLOCK)
        for j in range(lo_j, hi_j):
            w_ref, wj, act = gate_plan[j]
            acc = _dot(h, w_ref[:, wj * COL_TILE:(wj + 1) * COL_TILE])
            sg = _sigmoid(acc)
            gates_ref[rows, j * COL_TILE:(j + 1) * COL_TILE] = (
                acc * sg if act == "silu" else sg).astype(gates_ref.dtype)

    def recurrence(c, qk, hi, lo, v, r):
        rows = slice(c * GLA_BLOCK, (c + 1) * GLA_BLOCK)
        b = _dot(tri, hi) + _dot(tri, lo)
        b_mid = b[GLA_MID - 1:GLA_MID]
        b_last = b[GLA_BLOCK - 1:GLA_BLOCK]
        q = qk[:, :GLA_QK] * (GLA_DK ** -0.5)
        k = qk[:, GLA_QK:]
        q_in = (q * jnp.exp(b)).astype(BF16)
        q_mid = (q * jnp.exp(b - b_mid)).astype(BF16)
        k_mid = (k * jnp.exp(b_mid - b)).astype(BF16)
        k_end = (k * jnp.exp(b_last - b)).astype(BF16)
        dec = jnp.exp(b_last)
        yield
        heads = [(slice(hh * GLA_DK, (hh + 1) * GLA_DK), slice(hh * GLA_DV, (hh + 1) * GLA_DV))
                 for hh in range(GLA_HEADS)]
        attn = [_dot_nt(q_mid[:, ks], k_mid[:, ks]) for ks, _ in heads]
        kv_t = [_dot_tn(v[:, vs], k_end[:, ks]) for ks, vs in heads]
        yield
        outs = []
        for hh, (ks, vs) in enumerate(heads):
            st = st_ref[hh]
            a = jnp.where(causal, attn[hh], 0.0).astype(BF16)
            outs.append(_dot(a, v[:, vs]) + _dot_nt(q_in[:, ks], st.astype(BF16)))
            st_ref[hh] = st * dec[:, ks] + kv_t[hh]
        yield
        for (_, vs), o in zip(heads, outs):
            ms = jnp.mean(o * o, axis=-1, keepdims=True)
            o = o * lax.rsqrt(ms + EPS) * gain_ref[...] * r[:, vs]
            oa_ref[rows, vs] = o.astype(oa_ref.dtype)
        yield

    def advance(gen):
        if gen is not None:
            next(gen)

    n_gate = len(gate_plan)
    gen = None
    for c in range(n_blk):
        h = h_ref[c * GLA_BLOCK:(c + 1) * GLA_BLOCK, :]
        alr = _dot(h, walr_ref[...]).astype(BF16)
        v = _dot(h, wv_ref[...]).astype(BF16)
        advance(gen)
        z = _dot(alr, wa2_ref[...]) + ba_ref[...]
        la = (jnp.minimum(z, 0.0) - jnp.log1p(jnp.exp(-jnp.abs(z)))) * (1.0 / GLA_TAU)
        hi = la.astype(BF16)
        lo = (la - hi.astype(F32)).astype(BF16)
        r = _dot(h, wr_ref[...])
        r = r * _sigmoid(r)
        advance(gen)
        gate_tiles(c, h, 0, 2)
        advance(gen)
        qk = _dot(h, wqk_ref[...])
        advance(gen)
        gate_tiles(c, h, 2, n_gate)
        gen = recurrence(c, qk, hi, lo, v, r)
    for _ in gen:
        pass


def _nat_gla(h, w_lo, w_hi, walr, wa2, ba, gain, seq):
    t, d = h.shape
    tm = GLA_TOK_TILE
    tok = lambda i: (i, 0)
    fixed = lambda i: (0, 0)
    small = (walr, wa2, ba, gain)
    n_gates = 2 * D_MODEL + DIL_OUT
    return pl.pallas_call(
        functools.partial(_nat_gla_kernel, tiles_per_seq=seq // tm),
        grid=(t // tm,),
        in_specs=[pl.BlockSpec((tm, d), tok),
                  _w_cols(2 * GLA_QK, LO_QK), _w_cols(GLA_V, LO_V), _w_cols(GLA_V, LO_R),
                  _w_cols(D_MODEL, HI_GA), _w_cols(D_MODEL, HI_GD), _w_cols(DIL_OUT, HI_ZD)]
        + [pl.BlockSpec(w.shape, fixed) for w in small],
        out_specs=[pl.BlockSpec((tm, GLA_V), tok), pl.BlockSpec((tm, n_gates), tok)],
        out_shape=[jax.ShapeDtypeStruct((t, GLA_V), BF16), jax.ShapeDtypeStruct((t, n_gates), BF16)],
        scratch_shapes=[pltpu.VMEM((GLA_HEADS, GLA_DV, GLA_DK), F32)],
        compiler_params=_params("arbitrary"),
        name="nat_gla",
    )(h, w_lo, w_lo, w_lo, w_hi, w_hi, w_hi, *small)


def _dil_proj_kernel(h_ref, wq_ref, wk_ref, wv_ref, g_ref, cos_ref, sin_ref, q_ref, k_ref, v_ref):
    def epilogue(kind, rows, acc):
        o_ref = (q_ref, k_ref, v_ref)[kind]
        if o_ref is v_ref:
            o_ref[rows, :] = acc.astype(o_ref.dtype)
            return
        cos = cos_ref[rows, :]
        sin = sin_ref[rows, :]
        for hh in range(DIL_HEADS):
            sl = slice(hh * DIL_HD, (hh + 1) * DIL_HD)
            xh = acc[:, sl]
            ms = jnp.mean(xh * xh, axis=-1, keepdims=True)
            y = xh * lax.rsqrt(ms + EPS) * g_ref[kind:kind + 1, :]
            y = y * cos + pltpu.roll(y, DIL_HD // 2, 1) * sin
            o_ref[rows, sl] = y.astype(o_ref.dtype)

    for kind, w_ref in enumerate((wq_ref, wk_ref, wv_ref)):
        for rows in _row_subs(h_ref.shape[0]):
            epilogue(kind, rows, _dot(h_ref[rows, :], w_ref[...]))


def _dil_proj(h, w_hi, group, gains, cos, sin):
    t, d = h.shape
    tm = TOK_TILE
    tok = lambda i: (i, 0)
    fixed = lambda i: (0, 0)
    out = jax.ShapeDtypeStruct((t, COL_TILE), BF16)
    n_groups = len(DIL_GROUPS)
    return pl.pallas_call(
        _dil_proj_kernel,
        grid=(t // tm,),
        in_specs=[pl.BlockSpec((tm, d), tok)]
        + [_w_cols(COL_TILE, HI_QD + kind * n_groups + group) for kind in range(3)]
        + [pl.BlockSpec(gains.shape, fixed),
                  pl.BlockSpec((tm, DIL_HD), tok),
                  pl.BlockSpec((tm, DIL_HD), tok)],
        out_specs=[pl.BlockSpec((tm, COL_TILE), tok)] * 3,
        out_shape=[out] * 3,
        compiler_params=_params("parallel"),
        name="dil_proj",
    )(h, w_hi, w_hi, w_hi, gains, cos, sin)


def _dil_attn_kernel(q_ref, kp_ref, kc_ref, vp_ref, vc_ref, o_ref, lse_ref):
    n_qblk = q_ref.shape[1] // ATT_BLOCK
    row = lax.broadcasted_iota(jnp.int32, (ATT_BLOCK, 2 * ATT_BLOCK), 0)
    col = lax.broadcasted_iota(jnp.int32, (ATT_BLOCK, 2 * ATT_BLOCK), 1)
    band = (col >= row) & (col <= row + ATT_BLOCK)
    band_first = band & ((col >= ATT_BLOCK) | (pl.program_id(1) > 0))
    ones = jnp.ones((2 * ATT_BLOCK, DIL_HD), BF16)
    lane = lax.broadcasted_iota(jnp.int32, (ATT_BLOCK, LANES), 1)
    for sub, a in [(sub, a) for sub in range(q_ref.shape[0]) for a in range(n_qblk)]:
        rows = slice(a * ATT_BLOCK, (a + 1) * ATT_BLOCK)
        lse_all = jnp.zeros((ATT_BLOCK, LANES), F32)
        for hh in range(DIL_HEADS):
            sl = slice(hh * DIL_HD, (hh + 1) * DIL_HD)
            if a == 0:
                k_win = jnp.concatenate([kp_ref[sub, :, sl], kc_ref[sub, :ATT_BLOCK, sl]], axis=0)
                v_win = jnp.concatenate([vp_ref[sub, :, sl], vc_ref[sub, :ATT_BLOCK, sl]], axis=0)
            else:
                win = slice((a - 1) * ATT_BLOCK, (a + 1) * ATT_BLOCK)
                k_win = kc_ref[sub, win, sl]
                v_win = vc_ref[sub, win, sl]
            s = _dot_nt(q_ref[sub, rows, sl], k_win)
            s = jnp.where(band_first if a == 0 else band, s, NEG)
            m = jnp.max(s, axis=-1, keepdims=True)
            p = jnp.exp(s - m).astype(BF16)
            ol = _dot(p, jnp.concatenate([v_win, ones], axis=1))
            l = ol[:, DIL_HD:]
            o_ref[sub, rows, sl] = (ol[:, :DIL_HD] / l).astype(o_ref.dtype)
            lse_all = jnp.where(lane == hh, m + jnp.log(l), lse_all)
        lse_ref[sub, rows, :] = lse_all


def _dil_attn(q, k, v, n_sub, dil):
    t = q.shape[0]
    sub_len = t // n_sub
    qb = min(sub_len, ATT_SUPER)
    ns = ATT_SUPER // qb
    n_qblk = qb // ATT_BLOCK
    q3, k3, v3 = (a.reshape(n_sub, sub_len, COL_TILE) for a in (q, k, v))
    blk = (ns, qb, COL_TILE)
    cur = lambda s, i: (s, i, 0)
    prev = lambda s, i: (s, jnp.maximum(i * n_qblk - 1, 0), 0)
    prev_blk = (ns, ATT_BLOCK, COL_TILE)
    o, lse = pl.pallas_call(
        _dil_attn_kernel,
        grid=(n_sub // ns, sub_len // qb),
        in_specs=[pl.BlockSpec(blk, cur), pl.BlockSpec(prev_blk, prev), pl.BlockSpec(blk, cur),
                  pl.BlockSpec(prev_blk, prev), pl.BlockSpec(blk, cur)],
        out_specs=[pl.BlockSpec(blk, cur), pl.BlockSpec((ns, qb, LANES), cur)],
        out_shape=[jax.ShapeDtypeStruct((n_sub, sub_len, COL_TILE), BF16),
                   jax.ShapeDtypeStruct((n_sub, sub_len, LANES), F32)],
        compiler_params=_params("parallel", "arbitrary"),
        name=f"dil_attn_d{dil}",
    )(q3, k3, k3, v3, v3)
    return o, lse


def _merge_kernel(x_ref, oa_ref, o0_ref, o1_ref, o2_ref, l0_ref, l1_ref, l2_ref,
                  sz_ref, sga_ref, sgd_ref, wga_ref, wdo_ref, wo_ref, out_ref, og_ref, lg_ref):
    tm = x_ref.shape[1]
    for g, ((_, dil), o_ref, l_ref) in enumerate(zip(DIL_GROUPS, (o0_ref, o1_ref, o2_ref),
                                                     (l0_ref, l1_ref, l2_ref))):
        for r in range(dil):
            dst = pl.ds(r, tm // dil, stride=dil)
            lg_ref[g, dst, :] = l_ref[0, r]
            for hh in range(DIL_HEADS):
                og_ref[g, hh, dst, :] = o_ref[0, r, :, hh * DIL_HD:(hh + 1) * DIL_HD].astype(F32)
    n_groups = len(DIL_GROUPS)

    def combine(rows):
        lses = [lg_ref[g, rows, :] for g in range(n_groups)]
        m = jnp.maximum(jnp.maximum(lses[0], lses[1]), lses[2])
        es = [jnp.exp(l - m) for l in lses]
        den = es[0] + es[1] + es[2]
        ws = [e / den for e in es]
        heads = []
        for hh in range(DIL_HEADS):
            acc = ws[0][:, hh:hh + 1] * og_ref[0, hh, rows, :]
            for g in range(1, n_groups):
                acc = acc + ws[g][:, hh:hh + 1] * og_ref[g, hh, rows, :]
            heads.append(acc)
        return (jnp.concatenate(heads, axis=-1) * sz_ref[rows, :].astype(F32)).astype(BF16)

    def branches(rows, o_d):
        y_a = _dot(oa_ref[rows, :], wga_ref[...])
        y_d = _dot(o_d, wdo_ref[...])
        return y_a, y_d

    def gate(rows, y_a, y_d):
        return (sga_ref[rows, :].astype(F32) * y_a + sgd_ref[rows, :].astype(F32) * y_d).astype(BF16)

    def project(rows, y):
        out_ref[0, rows, :] = x_ref[0, rows, :] + _dot(y, wo_ref[...])

    for rows in _row_subs(tm):
        project(rows, gate(rows, *branches(rows, combine(rows))))


def _merge(x, o_a, o_ds, lses, gates, wga, wdo, wo):
    batch, seq, d = x.shape
    tm = MERGE_TOK_TILE
    nt = seq // tm
    tok = lambda b, i: (b * nt + i, 0)
    fixed = lambda b, i: (0, 0)
    dil_spec = lambda dil, w: pl.BlockSpec((1, dil, tm // dil, w), lambda b, i: (b, 0, i, 0))
    o_ds = [o.reshape(batch, dil, seq // dil, DIL_OUT) for o, (_, dil) in zip(o_ds, DIL_GROUPS)]
    lses = [l.reshape(batch, dil, seq // dil, LANES) for l, (_, dil) in zip(lses, DIL_GROUPS)]
    return pl.pallas_call(
        _merge_kernel,
        grid=(batch, nt),
        in_specs=[pl.BlockSpec((1, tm, d), lambda b, i: (b, i, 0)),
                  pl.BlockSpec((tm, GLA_V), tok)]
        + [dil_spec(dil, DIL_OUT) for _, dil in DIL_GROUPS]
        + [dil_spec(dil, LANES) for _, dil in DIL_GROUPS]
        + [pl.BlockSpec((tm, DIL_OUT), lambda b, i: (b * nt + i, GATE_ZD)),
           pl.BlockSpec((tm, d), lambda b, i: (b * nt + i, GATE_GA * COL_TILE // D_MODEL)),
           pl.BlockSpec((tm, d), lambda b, i: (b * nt + i, GATE_GD * COL_TILE // D_MODEL)),
           pl.BlockSpec(wga.shape, fixed),
           pl.BlockSpec(wdo.shape, fixed),
           pl.BlockSpec(wo.shape, fixed)],
        out_specs=pl.BlockSpec((1, tm, d), lambda b, i: (b, i, 0)),
        out_shape=jax.ShapeDtypeStruct(x.shape, x.dtype),
        scratch_shapes=[pltpu.VMEM((len(DIL_GROUPS), DIL_HEADS, tm, DIL_HD), F32),
                        pltpu.VMEM((len(DIL_GROUPS), tm, LANES), F32)],
        compiler_params=_params("parallel", "parallel"),
        name="merge_out",
    )(x, o_a, *o_ds, *lses, gates, gates, gates, wga, wdo, wo)


def kernel(x, positions, norm_gain, w_in, gla_w_a2, gla_b_a, gla_out_gain, dil_q_gain, dil_k_gain,
           w_gla_out, w_dil_out, w_o):
    batch, seq, d = x.shape
    t = batch * seq
    half = DIL_HD // 2
    inv_freq = ROPE_THETA ** (-jnp.arange(half, dtype=F32) / half)
    freq = jnp.concatenate([inv_freq, inv_freq]).reshape(1, DIL_HD)
    pos = positions.astype(F32).reshape(batch, seq, 1)
    for layer in range(norm_gain.shape[0]):
        w = w_in[layer]
        w_lo = w[:, :_ALR].astype(BF16)
        w_hi = w[:, _QD:].astype(BF16)
        walr = jnp.pad(w[:, _ALR:_QD], ((0, 0), (0, LANES - GLA_RANK))).astype(BF16)
        wa2 = jnp.pad(gla_w_a2[layer], ((0, LANES - GLA_RANK), (0, 0))).astype(BF16)
        ba = gla_b_a[layer].reshape(1, GLA_QK)
        g_dqk = jnp.stack([dil_q_gain[layer] * (DIL_HD ** -0.5), dil_k_gain[layer]])

        hs, coss, sins = _rmsnorm(x, norm_gain[layer], pos, freq)
        o_a, gates = _nat_gla(hs[0].reshape(t, d), w_lo, w_hi, walr, wa2, ba,
                              gla_out_gain[layer].reshape(1, GLA_DV), seq)
        o_ds, lses = [], []
        for g, (win, dil) in enumerate(DIL_GROUPS):
            assert win // dil == ATT_BLOCK
            q_g, k_g, v_g = _dil_proj(hs[g].reshape(t, d), w_hi, g, g_dqk,
                                      coss[g].reshape(t, DIL_HD), sins[g].reshape(t, DIL_HD))
            o_g, lse_g = _dil_attn(q_g, k_g, v_g, batch * dil, dil)
            o_ds.append(o_g)
            lses.append(lse_g)
        x = _merge(x, o_a, o_ds, lses, gates,
                   w_gla_out[layer].astype(BF16), w_dil_out[layer].astype(BF16),
                   w_o[layer].astype(BF16))
    return x
```

```python
import functools

import jax
import jax.numpy as jnp
from jax import lax
from jax.experimental import pallas as pl
from jax.experimental.pallas import tpu as pltpu

D_MODEL = 1024
EPS = 1e-6
ROPE_THETA = 10000.0
GLA_HEADS = 4
GLA_DK = 128
GLA_DV = 256
GLA_RANK = 16
GLA_TAU = 16.0
GLA_QK = GLA_HEADS * GLA_DK
GLA_V = GLA_HEADS * GLA_DV
DIL_GROUPS = ((128, 1), (512, 4), (2048, 16))
DIL_HEADS = 4
DIL_HD = 128
DIL_QK = len(DIL_GROUPS) * DIL_HEADS * DIL_HD
DIL_OUT = DIL_HEADS * DIL_HD
IN_SPLIT_SIZES = (GLA_QK, GLA_QK, GLA_V, GLA_V, GLA_RANK,
                  DIL_QK, DIL_QK, DIL_QK, DIL_OUT, D_MODEL, D_MODEL)

LANES = 128
BF16_ROWS = 16
ROW_STRIDE = 4
GLA_BLOCK = 128
GLA_MID = GLA_BLOCK // 2
ATT_BLOCK = 128
ATT_SUPER = 1024
NORM_TOK_TILE = 512
TOK_TILE = 1024
ROW_SUB = 256
GLA_TOK_TILE = 512
MERGE_TOK_TILE = 512
COL_TILE = DIL_HEADS * DIL_HD
VMEM_LIMIT_BYTES = 48 * 1024 * 1024

F32 = jnp.float32
BF16 = jnp.bfloat16
NEG = -1e30

GATE_GA, GATE_GD, GATE_ZD = 0, 2, 4

(_QA, _KA, _VA, _RA, _ALR, _QD, _KD, _VD, _ZD, _GA, _GD) = (
    sum(IN_SPLIT_SIZES[:i]) for i in range(len(IN_SPLIT_SIZES)))


def _block_index(offset, width):
    assert offset % width == 0
    return offset // width


LO_QK, LO_V, LO_R = _block_index(_QA, 2 * GLA_QK), _block_index(_VA, GLA_V), _block_index(_RA, GLA_V)
HI_QD = _block_index(_QD - _QD, COL_TILE)
HI_ZD = _block_index(_ZD - _QD, DIL_OUT)
HI_GA, HI_GD = _block_index(_GA - _QD, D_MODEL), _block_index(_GD - _QD, D_MODEL)
assert _KA == _QA + GLA_QK and _KD - _QD == DIL_QK and _VD - _KD == DIL_QK


def _w_cols(width, index):
    return pl.BlockSpec((D_MODEL, width), lambda *_: (0, index))


def _params(*sem):
    return pltpu.CompilerParams(dimension_semantics=sem, vmem_limit_bytes=VMEM_LIMIT_BYTES)


def _dot(a, b):
    return jnp.dot(a, b, preferred_element_type=F32)


def _dot_nt(a, b):
    return lax.dot_general(a, b, (((1,), (1,)), ((), ())), preferred_element_type=F32)


def _dot_tn(a, b):
    return lax.dot_general(a, b, (((0,), (0,)), ((), ())), preferred_element_type=F32)


def _sigmoid(x):
    return 0.5 * jnp.tanh(0.5 * x) + 0.5


def _row_subs(n):
    return [slice(r * ROW_SUB, (r + 1) * ROW_SUB) for r in range(n // ROW_SUB)]


def _rmsnorm_kernel(x_ref, g_ref, pos_ref, freq_ref, h_ref, cos_ref, sin_ref):
    x = x_ref[...]
    ms = jnp.mean(x * x, axis=-1, keepdims=True)
    h_ref[...] = (x * lax.rsqrt(ms + EPS) * g_ref[...]).astype(h_ref.dtype)
    ang = pos_ref[...] * freq_ref[...]
    lane = lax.broadcasted_iota(jnp.int32, ang.shape, 1)
    cos_ref[...] = jnp.cos(ang)
    sin_ref[...] = jnp.where(lane < DIL_HD // 2, -jnp.sin(ang), jnp.sin(ang))


def _rmsnorm(x2, gain, pos, freq):
    t, d = x2.shape
    tm = NORM_TOK_TILE
    tok = lambda i: (i, 0)
    fixed = lambda i: (0, 0)
    table = jax.ShapeDtypeStruct((t, DIL_HD), F32)
    return pl.pallas_call(
        _rmsnorm_kernel,
        grid=(t // tm,),
        in_specs=[pl.BlockSpec((tm, d), tok), pl.BlockSpec((1, d), fixed),
                  pl.BlockSpec((tm, 1), tok), pl.BlockSpec((1, DIL_HD), fixed)],
        out_specs=[pl.BlockSpec((tm, d), tok), pl.BlockSpec((tm, DIL_HD), tok),
                   pl.BlockSpec((tm, DIL_HD), tok)],
        out_shape=[jax.ShapeDtypeStruct((t, d), BF16), table, table],
        compiler_params=_params("parallel"),
        name="in_rmsnorm",
    )(x2, gain.reshape(1, d), pos, freq)


def _nat_gla_kernel(h_ref, wqk_ref, wv_ref, wr_ref, wga_ref, wgd_ref, wzd_ref, walr_ref, wa2_ref,
                    ba_ref, gain_ref, oa_ref, gates_ref, st_ref, *, tiles_per_seq):
    @pl.when(pl.program_id(0) % tiles_per_seq == 0)
    def _():
        st_ref[...] = jnp.zeros_like(st_ref)

    row = lax.broadcasted_iota(jnp.int32, (GLA_BLOCK, GLA_BLOCK), 0)
    col = lax.broadcasted_iota(jnp.int32, (GLA_BLOCK, GLA_BLOCK), 1)
    causal = col <= row
    tri = causal.astype(BF16)
    n_blk = h_ref.shape[0] // GLA_BLOCK

    gate_plan = ((wga_ref, 0, "sigmoid"), (wga_ref, 1, "sigmoid"), (wgd_ref, 0, "sigmoid"),
                 (wgd_ref, 1, "sigmoid"), (wzd_ref, 0, "silu"))

    def gate_tiles(c, h, lo_j, hi_j):
        rows = slice(c * GLA_BLOCK, (c + 1) * GLA_BLOCK)
        for j in range(lo_j, hi_j):
            w_ref, wj, act = gate_plan[j]
            acc = _dot(h, w_ref[:, wj * COL_TILE:(wj + 1) * COL_TILE])
            sg = _sigmoid(acc)
            gates_ref[rows, j * COL_TILE:(j + 1) * COL_TILE] = (
                acc * sg if act == "silu" else sg).astype(gates_ref.dtype)

    def recurrence(c, qk, hi, lo, v, r):
        rows = slice(c * GLA_BLOCK, (c + 1) * GLA_BLOCK)
        b = _dot(tri, hi) + _dot(tri, lo)
        b_mid = b[GLA_MID - 1:GLA_MID]
        b_last = b[GLA_BLOCK - 1:GLA_BLOCK]
        q = qk[:, :GLA_QK] * (GLA_DK ** -0.5)
        k = qk[:, GLA_QK:]
        q_in = (q * jnp.exp(b)).astype(BF16)
        q_mid = (q * jnp.exp(b - b_mid)).astype(BF16)
        k_mid = (k * jnp.exp(b_mid - b)).astype(BF16)
        k_end = (k * jnp.exp(b_last - b)).astype(BF16)
        dec = jnp.exp(b_last)
        yield
        heads = [(slice(hh * GLA_DK, (hh + 1) * GLA_DK), slice(hh * GLA_DV, (hh + 1) * GLA_DV))
                 for hh in range(GLA_HEADS)]
        attn = [_dot_nt(q_mid[:, ks], k_mid[:, ks]) for ks, _ in heads]
        kv_t = [_dot_tn(v[:, vs], k_end[:, ks]) for ks, vs in heads]
        yield
        outs = []
        for hh, (ks, vs) in enumerate(heads):
            st = st_ref[hh]
            a = jnp.where(causal, attn[hh], 0.0).astype(BF16)
            outs.append(_dot(a, v[:, vs]) + _dot_nt(q_in[:, ks], st.astype(BF16)))
            st_ref[hh] = st * dec[:, ks] + kv_t[hh]
        yield
        for (_, vs), o in zip(heads, outs):
            ms = jnp.mean(o * o, axis=-1, keepdims=True)
            o = o * lax.rsqrt(ms + EPS) * gain_ref[...] * r[:, vs]
            oa_ref[rows, vs] = o.astype(oa_ref.dtype)
        yield

    def advance(gen):
        if gen is not None:
            next(gen)

    n_gate = len(gate_plan)
    gen = None
    for c in range(n_blk):
        h = h_ref[c * GLA_BLOCK:(c + 1) * GLA_BLOCK, :]
        alr = _dot(h, walr_ref[...]).astype(BF16)
        v = _dot(h, wv_ref[...]).astype(BF16)
        advance(gen)
        z = _dot(alr, wa2_ref[...]) + ba_ref[...]
        la = (jnp.minimum(z, 0.0) - jnp.log1p(jnp.exp(-jnp.abs(z)))) * (1.0 / GLA_TAU)
        hi = la.astype(BF16)
        lo = (la - hi.astype(F32)).astype(BF16)
        r = _dot(h, wr_ref[...])
        r = r * _sigmoid(r)
        advance(gen)
        gate_tiles(c, h, 0, 2)
        advance(gen)
        qk = _dot(h, wqk_ref[...])
        advance(gen)
        gate_tiles(c, h, 2, n_gate)
        gen = recurrence(c, qk, hi, lo, v, r)
    for _ in gen:
        pass


def _nat_gla(h, w_lo, w_hi, walr, wa2, ba, gain, seq):
    t, d = h.shape
    tm = GLA_TOK_TILE
    tok = lambda i: (i, 0)
    fixed = lambda i: (0, 0)
    small = (walr, wa2, ba, gain)
    n_gates = 2 * D_MODEL + DIL_OUT
    return pl.pallas_call(
        functools.partial(_nat_gla_kernel, tiles_per_seq=seq // tm),
        grid=(t // tm,),
        in_specs=[pl.BlockSpec((tm, d), tok),
                  _w_cols(2 * GLA_QK, LO_QK), _w_cols(GLA_V, LO_V), _w_cols(GLA_V, LO_R),
                  _w_cols(D_MODEL, HI_GA), _w_cols(D_MODEL, HI_GD), _w_cols(DIL_OUT, HI_ZD)]
        + [pl.BlockSpec(w.shape, fixed) for w in small],
        out_specs=[pl.BlockSpec((tm, GLA_V), tok), pl.BlockSpec((tm, n_gates), tok)],
        out_shape=[jax.ShapeDtypeStruct((t, GLA_V), BF16), jax.ShapeDtypeStruct((t, n_gates), BF16)],
        scratch_shapes=[pltpu.VMEM((GLA_HEADS, GLA_DV, GLA_DK), F32)],
        compiler_params=_params("arbitrary"),
        name="nat_gla",
    )(h, w_lo, w_lo, w_lo, w_hi, w_hi, w_hi, *small)


def _dil_proj_kernel(h_ref, wq_ref, wk_ref, wv_ref, g_ref, cos_ref, sin_ref, q_ref, k_ref, v_ref,
                     *scratch, dil):
    n_planes = COL_TILE // LANES
    tm = h_ref.shape[0]
    outs = (q_ref, k_ref, v_ref)

    def store(kind, rows, plane, y):
        if dil == 1:
            outs[kind][0, 0, rows, plane * LANES:(plane + 1) * LANES] = y.astype(q_ref.dtype)
        else:
            scratch[0][kind, plane, rows, :] = y

    def permute(kind):
        s0 = min(dil, ROW_STRIDE)
        s1 = dil // s0
        for plane in range(n_planes):
            cols = slice(plane * LANES, (plane + 1) * LANES)
            for r0 in range(s0):
                hop = scratch[0][kind, plane, pl.ds(r0, tm // s0, stride=s0), :]
                if s1 == 1:
                    outs[kind][0, r0, :, cols] = hop.astype(q_ref.dtype)
                    continue
                scratch[1][plane, r0] = hop
                for r1 in range(s1):
                    outs[kind][0, s0 * r1 + r0, :, cols] = (
                        scratch[1][plane, r0, pl.ds(r1, tm // dil, stride=s1), :].astype(q_ref.dtype))

    for kind, w_ref in enumerate((wq_ref, wk_ref, wv_ref)):
        for n, rows in enumerate(_row_subs(tm)):
            acc = _dot(h_ref[rows, :], w_ref[...])
            if dil > 1 and kind > 0 and n == 0:
                permute(kind - 1)
            for hh in range(n_planes):
                xh = acc[:, hh * DIL_HD:(hh + 1) * DIL_HD]
                if kind == 2:
                    store(kind, rows, hh, xh)
                    continue
                ms = jnp.mean(xh * xh, axis=-1, keepdims=True)
                y = xh * lax.rsqrt(ms + EPS) * g_ref[kind:kind + 1, :]
                y = y * cos_ref[rows, :] + pltpu.roll(y, DIL_HD // 2, 1) * sin_ref[rows, :]
                store(kind, rows, hh, y)
    if dil > 1:
        permute(2)


def _dil_proj(h, w_hi, group, dil, gains, cos, sin, batch):
    t, d = h.shape
    seq = t // batch
    tm = TOK_TILE
    nt = seq // tm
    tok = lambda b, i: (b * nt + i, 0)
    fixed = lambda b, i: (0, 0)
    out = jax.ShapeDtypeStruct((batch, dil, seq // dil, COL_TILE), BF16)
    n_groups = len(DIL_GROUPS)
    assert DIL_HD == LANES and tm % (dil * BF16_ROWS) == 0
    return pl.pallas_call(
        functools.partial(_dil_proj_kernel, dil=dil),
        grid=(batch, nt),
        in_specs=[pl.BlockSpec((tm, d), tok)]
        + [_w_cols(COL_TILE, HI_QD + kind * n_groups + group) for kind in range(3)]
        + [pl.BlockSpec(gains.shape, fixed),
           pl.BlockSpec((tm, DIL_HD), tok),
           pl.BlockSpec((tm, DIL_HD), tok)],
        out_specs=[pl.BlockSpec((1, dil, tm // dil, COL_TILE), lambda b, i: (b, 0, i, 0))] * 3,
        out_shape=[out] * 3,
        scratch_shapes=[] if dil == 1 else [
            pltpu.VMEM((3, COL_TILE // LANES, tm, LANES), F32),
            pltpu.VMEM((COL_TILE // LANES, ROW_STRIDE, tm // ROW_STRIDE, LANES), F32)],
        compiler_params=_params("parallel", "parallel"),
        name=f"dil_proj_d{dil}",
    )(h, w_hi, w_hi, w_hi, gains, cos, sin)


def _dil_attn_kernel(q_ref, kp_ref, kc_ref, vp_ref, vc_ref, o_ref, lse_ref):
    n_qblk = q_ref.shape[1] // ATT_BLOCK
    row = lax.broadcasted_iota(jnp.int32, (ATT_BLOCK, 2 * ATT_BLOCK), 0)
    col = lax.broadcasted_iota(jnp.int32, (ATT_BLOCK, 2 * ATT_BLOCK), 1)
    band = (col >= row) & (col <= row + ATT_BLOCK)
    band_first = band & ((col >= ATT_BLOCK) | (pl.program_id(1) > 0))
    ones = jnp.ones((2 * ATT_BLOCK, DIL_HD), BF16)
    lane = lax.broadcasted_iota(jnp.int32, (ATT_BLOCK, LANES), 1)
    for sub, a in [(sub, a) for sub in range(q_ref.shape[0]) for a in range(n_qblk)]:
        rows = slice(a * ATT_BLOCK, (a + 1) * ATT_BLOCK)
        lse_all = jnp.zeros((ATT_BLOCK, LANES), F32)
        for hh in range(DIL_HEADS):
            sl = slice(hh * DIL_HD, (hh + 1) * DIL_HD)
            if a == 0:
                k_win = jnp.concatenate([kp_ref[sub, :, sl], kc_ref[sub, :ATT_BLOCK, sl]], axis=0)
                v_win = jnp.concatenate([vp_ref[sub, :, sl], vc_ref[sub, :ATT_BLOCK, sl]], axis=0)
            else:
                win = slice((a - 1) * ATT_BLOCK, (a + 1) * ATT_BLOCK)
                k_win = kc_ref[sub, win, sl]
                v_win = vc_ref[sub, win, sl]
            s = _dot_nt(q_ref[sub, rows, sl], k_win)
            s = jnp.where(band_first if a == 0 else band, s, NEG)
            m = jnp.max(s, axis=-1, keepdims=True)
            p = jnp.exp(s - m).astype(BF16)
            ol = _dot(p, jnp.concatenate([v_win, ones], axis=1))
            l = ol[:, DIL_HD:]
            o_ref[sub, rows, sl] = (ol[:, :DIL_HD] / l).astype(o_ref.dtype)
            lse_all = jnp.where(lane == hh, m + jnp.log(l), lse_all)
        lse_ref[sub, rows, :] = lse_all


def _dil_attn(q, k, v, n_sub, dil):
    sub_len = q.size // COL_TILE // n_sub
    qb = min(sub_len, ATT_SUPER)
    ns = ATT_SUPER // qb
    n_qblk = qb // ATT_BLOCK
    q3, k3, v3 = (a.reshape(n_sub, sub_len, COL_TILE) for a in (q, k, v))
    blk = (ns, qb, COL_TILE)
    cur = lambda s, i: (s, i, 0)
    prev = lambda s, i: (s, jnp.maximum(i * n_qblk - 1, 0), 0)
    prev_blk = (ns, ATT_BLOCK, COL_TILE)
    o, lse = pl.pallas_call(
        _dil_attn_kernel,
        grid=(n_sub // ns, sub_len // qb),
        in_specs=[pl.BlockSpec(blk, cur), pl.BlockSpec(prev_blk, prev), pl.BlockSpec(blk, cur),
                  pl.BlockSpec(prev_blk, prev), pl.BlockSpec(blk, cur)],
        out_specs=[pl.BlockSpec(blk, cur), pl.BlockSpec((ns, qb, LANES), cur)],
        out_shape=[jax.ShapeDtypeStruct((n_sub, sub_len, COL_TILE), BF16),
                   jax.ShapeDtypeStruct((n_sub, sub_len, LANES), F32)],
        compiler_params=_params("parallel", "arbitrary"),
        name=f"dil_attn_d{dil}",
    )(q3, k3, k3, v3, v3)
    return o, lse


def _merge_kernel(x_ref, oa_ref, o0_ref, o1_ref, o2_ref, l0_ref, l1_ref, l2_ref,
                  sz_ref, sga_ref, sgd_ref, wga_ref, wdo_ref, wo_ref, out_ref, og_ref, lg_ref):
    tm = x_ref.shape[1]
    for g, ((_, dil), o_ref, l_ref) in enumerate(zip(DIL_GROUPS, (o0_ref, o1_ref, o2_ref),
                                                     (l0_ref, l1_ref, l2_ref))):
        for r in range(dil):
            dst = pl.ds(r, tm // dil, stride=dil)
            lg_ref[g, dst, :] = l_ref[0, r]
            for hh in range(DIL_HEADS):
                og_ref[g, hh, dst, :] = o_ref[0, r, :, hh * DIL_HD:(hh + 1) * DIL_HD].astype(F32)
    n_groups = len(DIL_GROUPS)

    def combine(rows):
        lses = [lg_ref[g, rows, :] for g in range(n_groups)]
        m = jnp.maximum(jnp.maximum(lses[0], lses[1]), lses[2])
        es = [jnp.exp(l - m) for l in lses]
        den = es[0] + es[1] + es[2]
        ws = [e / den for e in es]
        heads = []
        for hh in range(DIL_HEADS):
            acc = ws[0][:, hh:hh + 1] * og_ref[0, hh, rows, :]
            for g in range(1, n_groups):
                acc = acc + ws[g][:, hh:hh + 1] * og_ref[g, hh, rows, :]
            heads.append(acc)
        return (jnp.concatenate(heads, axis=-1) * sz_ref[rows, :].astype(F32)).astype(BF16)

    def branches(rows, o_d):
        y_a = _dot(oa_ref[rows, :], wga_ref[...])
        y_d = _dot(o_d, wdo_ref[...])
        return y_a, y_d

    def gate(rows, y_a, y_d):
        return (sga_ref[rows, :].astype(F32) * y_a + sgd_ref[rows, :].astype(F32) * y_d).astype(BF16)

    def project(rows, y):
        out_ref[0, rows, :] = x_ref[0, rows, :] + _dot(y, wo_ref[...])

    for rows in _row_subs(tm):
        project(rows, gate(rows, *branches(rows, combine(rows))))


def _merge(x, o_a, o_ds, lses, gates, wga, wdo, wo):
    batch, seq, d = x.shape
    tm = MERGE_TOK_TILE
    nt = seq // tm
    tok = lambda b, i: (b * nt + i, 0)
    fixed = lambda b, i: (0, 0)
    dil_spec = lambda dil, w: pl.BlockSpec((1, dil, tm // dil, w), lambda b, i: (b, 0, i, 0))
    o_ds = [o.reshape(batch, dil, seq // dil, DIL_OUT) for o, (_, dil) in zip(o_ds, DIL_GROUPS)]
    lses = [l.reshape(batch, dil, seq // dil, LANES) for l, (_, dil) in zip(lses, DIL_GROUPS)]
    return pl.pallas_call(
        _merge_kernel,
        grid=(batch, nt),
        in_specs=[pl.BlockSpec((1, tm, d), lambda b, i: (b, i, 0)),
                  pl.BlockSpec((tm, GLA_V), tok)]
        + [dil_spec(dil, DIL_OUT) for _, dil in DIL_GROUPS]
        + [dil_spec(dil, LANES) for _, dil in DIL_GROUPS]
        + [pl.BlockSpec((tm, DIL_OUT), lambda b, i: (b * nt + i, GATE_ZD)),
           pl.BlockSpec((tm, d), lambda b, i: (b * nt + i, GATE_GA * COL_TILE // D_MODEL)),
           pl.BlockSpec((tm, d), lambda b, i: (b * nt + i, GATE_GD * COL_TILE // D_MODEL)),
           pl.BlockSpec(wga.shape, fixed),
           pl.BlockSpec(wdo.shape, fixed),
           pl.BlockSpec(wo.shape, fixed)],
        out_specs=pl.BlockSpec((1, tm, d), lambda b, i: (b, i, 0)),
        out_shape=jax.ShapeDtypeStruct(x.shape, x.dtype),
        scratch_shapes=[pltpu.VMEM((len(DIL_GROUPS), DIL_HEADS, tm, DIL_HD), F32),
                        pltpu.VMEM((len(DIL_GROUPS), tm, LANES), F32)],
        compiler_params=_params("parallel", "parallel"),
        name="merge_out",
    )(x, o_a, *o_ds, *lses, gates, gates, gates, wga, wdo, wo)


def kernel(x, positions, norm_gain, w_in, gla_w_a2, gla_b_a, gla_out_gain, dil_q_gain, dil_k_gain,
           w_gla_out, w_dil_out, w_o):
    batch, seq, d = x.shape
    t = batch * seq
    half = DIL_HD // 2
    inv_freq = ROPE_THETA ** (-jnp.arange(half, dtype=F32) / half)
    freq = jnp.concatenate([inv_freq, inv_freq]).reshape(1, DIL_HD)
    pos = positions.astype(F32).reshape(t, 1)
    for layer in range(norm_gain.shape[0]):
        w = w_in[layer]
        w_lo = w[:, :_ALR].astype(BF16)
        w_hi = w[:, _QD:].astype(BF16)
        walr = jnp.pad(w[:, _ALR:_QD], ((0, 0), (0, LANES - GLA_RANK))).astype(BF16)
        wa2 = jnp.pad(gla_w_a2[layer], ((0, LANES - GLA_RANK), (0, 0))).astype(BF16)
        ba = gla_b_a[layer].reshape(1, GLA_QK)
        g_dqk = jnp.stack([dil_q_gain[layer] * (DIL_HD ** -0.5), dil_k_gain[layer]])

        h, cos, sin = _rmsnorm(x.reshape(t, d), norm_gain[layer], pos, freq)
        o_a, gates = _nat_gla(h, w_lo, w_hi, walr, wa2, ba,
                              gla_out_gain[layer].reshape(1, GLA_DV), seq)
        o_ds, lses = [], []
        for g, (win, dil) in enumerate(DIL_GROUPS):
            assert win // dil == ATT_BLOCK
            q_g, k_g, v_g = _dil_proj(h, w_hi, g, dil, g_dqk, cos, sin, batch)
            o_g, lse_g = _dil_attn(q_g, k_g, v_g, batch * dil, dil)
            o_ds.append(o_g)
            lses.append(lse_g)
        x = _merge(x, o_a, o_ds, lses, gates,
                   w_gla_out[layer].astype(BF16), w_dil_out[layer].astype(BF16),
                   w_o[layer].astype(BF16))
    return x
```

```python
import functools

import jax
import jax.numpy as jnp
from jax import lax
from jax.experimental import pallas as pl
from jax.experimental.pallas import tpu as pltpu

D_MODEL = 1024
EPS = 1e-6
ROPE_THETA = 10000.0
GLA_HEADS = 4
GLA_DK = 128
GLA_DV = 256
GLA_RANK = 16
GLA_TAU = 16.0
GLA_QK = GLA_HEADS * GLA_DK
GLA_V = GLA_HEADS * GLA_DV
DIL_GROUPS = ((128, 1), (512, 4), (2048, 16))
DIL_HEADS = 4
DIL_HD = 128
DIL_QK = len(DIL_GROUPS) * DIL_HEADS * DIL_HD
DIL_OUT = DIL_HEADS * DIL_HD
IN_SPLIT_SIZES = (GLA_QK, GLA_QK, GLA_V, GLA_V, GLA_RANK,
                  DIL_QK, DIL_QK, DIL_QK, DIL_OUT, D_MODEL, D_MODEL)

LANES = 128
BF16_ROWS = 16
ROW_STRIDE = 4
GLA_BLOCK = 128
GLA_MID = GLA_BLOCK // 2
ATT_BLOCK = 128
ATT_SUPER = 1024
TOK_TILE = 1024
ROW_SUB = 256
GLA_TOK_TILE = 512
MERGE_TOK_TILE = 512
COL_TILE = DIL_HEADS * DIL_HD
VMEM_LIMIT_BYTES = 48 * 1024 * 1024

F32 = jnp.float32
BF16 = jnp.bfloat16
NEG = -1e30

GATE_GA, GATE_GD, GATE_ZD = 0, 2, 4

(_QA, _KA, _VA, _RA, _ALR, _QD, _KD, _VD, _ZD, _GA, _GD) = (
    sum(IN_SPLIT_SIZES[:i]) for i in range(len(IN_SPLIT_SIZES)))


def _block_index(offset, width):
    assert offset % width == 0
    return offset // width


LO_QK, LO_V, LO_R = _block_index(_QA, 2 * GLA_QK), _block_index(_VA, GLA_V), _block_index(_RA, GLA_V)
HI_QD = _block_index(_QD - _QD, COL_TILE)
HI_ZD = _block_index(_ZD - _QD, DIL_OUT)
HI_GA, HI_GD = _block_index(_GA - _QD, D_MODEL), _block_index(_GD - _QD, D_MODEL)
HI_SHIFT = _QD % LANES
assert _KA == _QA + GLA_QK and _KD - _QD == DIL_QK and _VD - _KD == DIL_QK


def _w_cols(width, index):
    return pl.BlockSpec((D_MODEL, width), lambda *_: (0, index))


def _params(*sem):
    return pltpu.CompilerParams(dimension_semantics=sem, vmem_limit_bytes=VMEM_LIMIT_BYTES)


def _dot(a, b):
    return jnp.dot(a, b, preferred_element_type=F32)


def _dot_nt(a, b):
    return lax.dot_general(a, b, (((1,), (1,)), ((), ())), preferred_element_type=F32)


def _dot_tn(a, b):
    return lax.dot_general(a, b, (((0,), (0,)), ((), ())), preferred_element_type=F32)


def _sigmoid(x):
    return 0.5 * jnp.tanh(0.5 * x) + 0.5


def _row_subs(n):
    return [slice(r * ROW_SUB, (r + 1) * ROW_SUB) for r in range(n // ROW_SUB)]


def _w_hi_kernel(a_ref, b_ref, o_ref):
    n = COL_TILE // LANES
    tiles = [a_ref[:, j * LANES:(j + 1) * LANES] for j in range(n)] + [b_ref[...]]
    rolled = [pltpu.roll(tile, LANES - HI_SHIFT, 1) for tile in tiles]
    lane = lax.broadcasted_iota(jnp.int32, rolled[0].shape, 1)
    for j in range(n):
        o_ref[:, j * LANES:(j + 1) * LANES] = jnp.where(
            lane < LANES - HI_SHIFT, rolled[j], rolled[j + 1]).astype(o_ref.dtype)


def _w_hi(w):
    d, n = w.shape
    width = n - _QD
    base = _QD - HI_SHIFT
    a_blk = _block_index(base, COL_TILE)
    b_blk = _block_index(base + COL_TILE, LANES)
    return pl.pallas_call(
        _w_hi_kernel,
        grid=(_block_index(width, COL_TILE),),
        in_specs=[pl.BlockSpec((d, COL_TILE), lambda j: (0, a_blk + j)),
                  pl.BlockSpec((d, LANES), lambda j: (0, b_blk + j * (COL_TILE // LANES)))],
        out_specs=pl.BlockSpec((d, COL_TILE), lambda j: (0, j)),
        out_shape=jax.ShapeDtypeStruct((d, width), BF16),
        compiler_params=_params("parallel"),
        name="w_hi_align",
    )(w, w)


def _nat_gla_kernel(x_ref, ng_ref, pos_ref, freq_ref, wqk_ref, wv_ref, wr_ref, wga_ref, wgd_ref,
                    wzd_ref, walr_ref, wa2_ref, ba_ref, gain_ref,
                    oa_ref, gates_ref, h_ref, cos_ref, sin_ref, st_ref, *, tiles_per_seq):
    @pl.when(pl.program_id(0) % tiles_per_seq == 0)
    def _():
        st_ref[...] = jnp.zeros_like(st_ref)

    row = lax.broadcasted_iota(jnp.int32, (GLA_BLOCK, GLA_BLOCK), 0)
    col = lax.broadcasted_iota(jnp.int32, (GLA_BLOCK, GLA_BLOCK), 1)
    causal = col <= row
    tri = causal.astype(BF16)
    n_blk = x_ref.shape[0] // GLA_BLOCK
    lane = lax.broadcasted_iota(jnp.int32, (GLA_BLOCK, DIL_HD), 1)

    gate_plan = ((wga_ref, 0, "sigmoid"), (wga_ref, 1, "sigmoid"), (wgd_ref, 0, "sigmoid"),
                 (wgd_ref, 1, "sigmoid"), (wzd_ref, 0, "silu"))

    def gate_tiles(c, h, lo_j, hi_j):
        rows = slice(c * GLA_BLOCK, (c + 1) * GLA_BLOCK)
        for j in range(lo_j, hi_j):
            w_ref, wj, act = gate_plan[j]
            acc = _dot(h, w_ref[:, wj * COL_TILE:(wj + 1) * COL_TILE])
            sg = _sigmoid(acc)
            gates_ref[rows, j * COL_TILE:(j + 1) * COL_TILE] = (
                acc * sg if act == "silu" else sg).astype(gates_ref.dtype)

    def recurrence(c, qk, hi, lo, v, r):
        rows = slice(c * GLA_BLOCK, (c + 1) * GLA_BLOCK)
        b = _dot(tri, hi) + _dot(tri, lo)
        b_mid = b[GLA_MID - 1:GLA_MID]
        b_last = b[GLA_BLOCK - 1:GLA_BLOCK]
        q = qk[:, :GLA_QK] * (GLA_DK ** -0.5)
        k = qk[:, GLA_QK:]
        q_in = (q * jnp.exp(b)).astype(BF16)
        q_mid = (q * jnp.exp(b - b_mid)).astype(BF16)
        k_mid = (k * jnp.exp(b_mid - b)).astype(BF16)
        k_end = (k * jnp.exp(b_last - b)).astype(BF16)
        dec = jnp.exp(b_last)
        yield
        heads = [(slice(hh * GLA_DK, (hh + 1) * GLA_DK), slice(hh * GLA_DV, (hh + 1) * GLA_DV))
                 for hh in range(GLA_HEADS)]
        attn = [_dot_nt(q_mid[:, ks], k_mid[:, ks]) for ks, _ in heads]
        kv_t = [_dot_tn(v[:, vs], k_end[:, ks]) for ks, vs in heads]
        yield
        outs = []
        for hh, (ks, vs) in enumerate(heads):
            st = st_ref[hh]
            a = jnp.where(causal, attn[hh], 0.0).astype(BF16)
            outs.append(_dot(a, v[:, vs]) + _dot_nt(q_in[:, ks], st.astype(BF16)))
            st_ref[hh] = st * dec[:, ks] + kv_t[hh]
        yield
        for (_, vs), o in zip(heads, outs):
            ms = jnp.mean(o * o, axis=-1, keepdims=True)
            o = o * lax.rsqrt(ms + EPS) * gain_ref[...] * r[:, vs]
            oa_ref[rows, vs] = o.astype(oa_ref.dtype)
        yield

    def advance(gen):
        if gen is not None:
            next(gen)

    n_gate = len(gate_plan)
    gen = None
    for c in range(n_blk):
        rows = slice(c * GLA_BLOCK, (c + 1) * GLA_BLOCK)
        x = x_ref[rows, :]
        ms = jnp.mean(x * x, axis=-1, keepdims=True)
        h = (x * lax.rsqrt(ms + EPS) * ng_ref[...]).astype(BF16)
        h_ref[rows, :] = h
        alr = _dot(h, walr_ref[...]).astype(BF16)
        v = _dot(h, wv_ref[...]).astype(BF16)
        advance(gen)
        ang = pos_ref[rows, :] * freq_ref[...]
        cos_ref[rows, :] = jnp.cos(ang)
        sin_ref[rows, :] = jnp.where(lane < DIL_HD // 2, -jnp.sin(ang), jnp.sin(ang))
        z = _dot(alr, wa2_ref[...]) + ba_ref[...]
        la = (jnp.minimum(z, 0.0) - jnp.log1p(jnp.exp(-jnp.abs(z)))) * (1.0 / GLA_TAU)
        hi = la.astype(BF16)
        lo = (la - hi.astype(F32)).astype(BF16)
        r = _dot(h, wr_ref[...])
        r = r * _sigmoid(r)
        advance(gen)
        gate_tiles(c, h, 0, 2)
        advance(gen)
        qk = _dot(h, wqk_ref[...])
        advance(gen)
        gate_tiles(c, h, 2, n_gate)
        gen = recurrence(c, qk, hi, lo, v, r)
    for _ in gen:
        pass


def _nat_gla(x2, norm_gain, pos, freq, w_lo, w_hi, walr, wa2, ba, gain, seq):
    t, d = x2.shape
    tm = GLA_TOK_TILE
    tok = lambda i: (i, 0)
    fixed = lambda i: (0, 0)
    small = (walr, wa2, ba, gain)
    n_gates = 2 * D_MODEL + DIL_OUT
    table = jax.ShapeDtypeStruct((t, DIL_HD), F32)
    return pl.pallas_call(
        functools.partial(_nat_gla_kernel, tiles_per_seq=seq // tm),
        grid=(t // tm,),
        in_specs=[pl.BlockSpec((tm, d), tok), pl.BlockSpec((1, d), fixed),
                  pl.BlockSpec((tm, 1), tok), pl.BlockSpec((1, DIL_HD), fixed),
                  _w_cols(2 * GLA_QK, LO_QK), _w_cols(GLA_V, LO_V), _w_cols(GLA_V, LO_R),
                  _w_cols(D_MODEL, HI_GA), _w_cols(D_MODEL, HI_GD), _w_cols(DIL_OUT, HI_ZD)]
        + [pl.BlockSpec(w.shape, fixed) for w in small],
        out_specs=[pl.BlockSpec((tm, GLA_V), tok), pl.BlockSpec((tm, n_gates), tok),
                   pl.BlockSpec((tm, d), tok), pl.BlockSpec((tm, DIL_HD), tok),
                   pl.BlockSpec((tm, DIL_HD), tok)],
        out_shape=[jax.ShapeDtypeStruct((t, GLA_V), BF16), jax.ShapeDtypeStruct((t, n_gates), BF16),
                   jax.ShapeDtypeStruct((t, d), BF16), table, table],
        scratch_shapes=[pltpu.VMEM((GLA_HEADS, GLA_DV, GLA_DK), F32)],
        compiler_params=_params("arbitrary"),
        name="nat_gla",
    )(x2, norm_gain.reshape(1, d), pos, freq, w_lo, w_lo, w_lo, w_hi, w_hi, w_hi, *small)


def _dil_proj_kernel(h_ref, wq_ref, wk_ref, wv_ref, g_ref, cos_ref, sin_ref, q_ref, k_ref, v_ref,
                     *scratch, dil):
    n_planes = COL_TILE // LANES
    tm = h_ref.shape[0]
    outs = (q_ref, k_ref, v_ref)

    def store(kind, rows, plane, y):
        if dil == 1:
            outs[kind][0, 0, rows, plane * LANES:(plane + 1) * LANES] = y.astype(q_ref.dtype)
        else:
            scratch[0][kind, plane, rows, :] = y

    def permute(kind):
        s0 = min(dil, ROW_STRIDE)
        s1 = dil // s0
        for plane in range(n_planes):
            cols = slice(plane * LANES, (plane + 1) * LANES)
            for r0 in range(s0):
                hop = scratch[0][kind, plane, pl.ds(r0, tm // s0, stride=s0), :]
                if s1 == 1:
                    outs[kind][0, r0, :, cols] = hop.astype(q_ref.dtype)
                    continue
                scratch[1][plane, r0] = hop
                for r1 in range(s1):
                    outs[kind][0, s0 * r1 + r0, :, cols] = (
                        scratch[1][plane, r0, pl.ds(r1, tm // dil, stride=s1), :].astype(q_ref.dtype))

    for kind, w_ref in enumerate((wq_ref, wk_ref, wv_ref)):
        for n, rows in enumerate(_row_subs(tm)):
            acc = _dot(h_ref[rows, :], w_ref[...])
            if dil > 1 and kind > 0 and n == 0:
                permute(kind - 1)
            for hh in range(n_planes):
                xh = acc[:, hh * DIL_HD:(hh + 1) * DIL_HD]
                if kind == 2:
                    store(kind, rows, hh, xh)
                    continue
                ms = jnp.mean(xh * xh, axis=-1, keepdims=True)
                y = xh * lax.rsqrt(ms + EPS) * g_ref[kind:kind + 1, :]
                y = y * cos_ref[rows, :] + pltpu.roll(y, DIL_HD // 2, 1) * sin_ref[rows, :]
                store(kind, rows, hh, y)
    if dil > 1:
        permute(2)


def _dil_proj(h, w_hi, group, dil, gains, cos, sin, batch):
    t, d = h.shape
    seq = t // batch
    tm = TOK_TILE
    nt = seq // tm
    tok = lambda b, i: (b * nt + i, 0)
    fixed = lambda b, i: (0, 0)
    out = jax.ShapeDtypeStruct((batch, dil, seq // dil, COL_TILE), BF16)
    n_groups = len(DIL_GROUPS)
    assert DIL_HD == LANES and tm % (dil * BF16_ROWS) == 0
    return pl.pallas_call(
        functools.partial(_dil_proj_kernel, dil=dil),
        grid=(batch, nt),
        in_specs=[pl.BlockSpec((tm, d), tok)]
        + [_w_cols(COL_TILE, HI_QD + kind * n_groups + group) for kind in range(3)]
        + [pl.BlockSpec(gains.shape, fixed),
           pl.BlockSpec((tm, DIL_HD), tok),
           pl.BlockSpec((tm, DIL_HD), tok)],
        out_specs=[pl.BlockSpec((1, dil, tm // dil, COL_TILE), lambda b, i: (b, 0, i, 0))] * 3,
        out_shape=[out] * 3,
        scratch_shapes=[] if dil == 1 else [
            pltpu.VMEM((3, COL_TILE // LANES, tm, LANES), F32),
            pltpu.VMEM((COL_TILE // LANES, ROW_STRIDE, tm // ROW_STRIDE, LANES), F32)],
        compiler_params=_params("parallel", "parallel"),
        name=f"dil_proj_d{dil}",
    )(h, w_hi, w_hi, w_hi, gains, cos, sin)


def _dil_attn_kernel(q_ref, kp_ref, kc_ref, vp_ref, vc_ref, o_ref, lse_ref):
    n_qblk = q_ref.shape[1] // ATT_BLOCK
    row = lax.broadcasted_iota(jnp.int32, (ATT_BLOCK, 2 * ATT_BLOCK), 0)
    col = lax.broadcasted_iota(jnp.int32, (ATT_BLOCK, 2 * ATT_BLOCK), 1)
    band = (col >= row) & (col <= row + ATT_BLOCK)
    band_first = band & ((col >= ATT_BLOCK) | (pl.program_id(1) > 0))
    ones = jnp.ones((2 * ATT_BLOCK, DIL_HD), BF16)
    lane = lax.broadcasted_iota(jnp.int32, (ATT_BLOCK, LANES), 1)
    for sub, a in [(sub, a) for sub in range(q_ref.shape[0]) for a in range(n_qblk)]:
        rows = slice(a * ATT_BLOCK, (a + 1) * ATT_BLOCK)
        lse_all = jnp.zeros((ATT_BLOCK, LANES), F32)
        for hh in range(DIL_HEADS):
            sl = slice(hh * DIL_HD, (hh + 1) * DIL_HD)
            if a == 0:
                k_win = jnp.concatenate([kp_ref[sub, :, sl], kc_ref[sub, :ATT_BLOCK, sl]], axis=0)
                v_win = jnp.concatenate([vp_ref[sub, :, sl], vc_ref[sub, :ATT_BLOCK, sl]], axis=0)
            else:
                win = slice((a - 1) * ATT_BLOCK, (a + 1) * ATT_BLOCK)
                k_win = kc_ref[sub, win, sl]
                v_win = vc_ref[sub, win, sl]
            s = _dot_nt(q_ref[sub, rows, sl], k_win)
            s = jnp.where(band_first if a == 0 else band, s, NEG)
            m = jnp.max(s, axis=-1, keepdims=True)
            p = jnp.exp(s - m).astype(BF16)
            ol = _dot(p, jnp.concatenate([v_win, ones], axis=1))
            l = ol[:, DIL_HD:]
            o_ref[sub, rows, sl] = (ol[:, :DIL_HD] / l).astype(o_ref.dtype)
            lse_all = jnp.where(lane == hh, m + jnp.log(l), lse_all)
        lse_ref[sub, rows, :] = lse_all


def _dil_attn(q, k, v, n_sub, dil):
    sub_len = q.size // COL_TILE // n_sub
    qb = min(sub_len, ATT_SUPER)
    ns = ATT_SUPER // qb
    n_qblk = qb // ATT_BLOCK
    q3, k3, v3 = (a.reshape(n_sub, sub_len, COL_TILE) for a in (q, k, v))
    blk = (ns, qb, COL_TILE)
    cur = lambda s, i: (s, i, 0)
    prev = lambda s, i: (s, jnp.maximum(i * n_qblk - 1, 0), 0)
    prev_blk = (ns, ATT_BLOCK, COL_TILE)
    o, lse = pl.pallas_call(
        _dil_attn_kernel,
        grid=(n_sub // ns, sub_len // qb),
        in_specs=[pl.BlockSpec(blk, cur), pl.BlockSpec(prev_blk, prev), pl.BlockSpec(blk, cur),
                  pl.BlockSpec(prev_blk, prev), pl.BlockSpec(blk, cur)],
        out_specs=[pl.BlockSpec(blk, cur), pl.BlockSpec((ns, qb, LANES), cur)],
        out_shape=[jax.ShapeDtypeStruct((n_sub, sub_len, COL_TILE), BF16),
                   jax.ShapeDtypeStruct((n_sub, sub_len, LANES), F32)],
        compiler_params=_params("parallel", "arbitrary"),
        name=f"dil_attn_d{dil}",
    )(q3, k3, k3, v3, v3)
    return o, lse


def _merge_kernel(x_ref, oa_ref, o0_ref, o1_ref, o2_ref, l0_ref, l1_ref, l2_ref,
                  sz_ref, sga_ref, sgd_ref, wga_ref, wdo_ref, wo_ref, out_ref, og_ref, lg_ref):
    tm = x_ref.shape[1]
    for g, ((_, dil), o_ref, l_ref) in enumerate(zip(DIL_GROUPS, (o0_ref, o1_ref, o2_ref),
                                                     (l0_ref, l1_ref, l2_ref))):
        for r in range(dil):
            dst = pl.ds(r, tm // dil, stride=dil)
            lg_ref[g, dst, :] = l_ref[0, r]
            for hh in range(DIL_HEADS):
                og_ref[g, hh, dst, :] = o_ref[0, r, :, hh * DIL_HD:(hh + 1) * DIL_HD].astype(F32)
    n_groups = len(DIL_GROUPS)

    def combine(rows):
        lses = [lg_ref[g, rows, :] for g in range(n_groups)]
        m = jnp.maximum(jnp.maximum(lses[0], lses[1]), lses[2])
        es = [jnp.exp(l - m) for l in lses]
        den = es[0] + es[1] + es[2]
        ws = [e / den for e in es]
        heads = []
        for hh in range(DIL_HEADS):
            acc = ws[0][:, hh:hh + 1] * og_ref[0, hh, rows, :]
            for g in range(1, n_groups):
                acc = acc + ws[g][:, hh:hh + 1] * og_ref[g, hh, rows, :]
            heads.append(acc)
        return (jnp.concatenate(heads, axis=-1) * sz_ref[rows, :].astype(F32)).astype(BF16)

    def branches(rows, o_d):
        y_a = _dot(oa_ref[rows, :], wga_ref[...])
        y_d = _dot(o_d, wdo_ref[...])
        return y_a, y_d

    def gate(rows, y_a, y_d):
        return (sga_ref[rows, :].astype(F32) * y_a + sgd_ref[rows, :].astype(F32) * y_d).astype(BF16)

    def project(rows, y):
        out_ref[0, rows, :] = x_ref[0, rows, :] + _dot(y, wo_ref[...])

    for rows in _row_subs(tm):
        project(rows, gate(rows, *branches(rows, combine(rows))))


def _merge(x, o_a, o_ds, lses, gates, wga, wdo, wo):
    batch, seq, d = x.shape
    tm = MERGE_TOK_TILE
    nt = seq // tm
    tok = lambda b, i: (b * nt + i, 0)
    fixed = lambda b, i: (0, 0)
    dil_spec = lambda dil, w: pl.BlockSpec((1, dil, tm // dil, w), lambda b, i: (b, 0, i, 0))
    o_ds = [o.reshape(batch, dil, seq // dil, DIL_OUT) for o, (_, dil) in zip(o_ds, DIL_GROUPS)]
    lses = [l.reshape(batch, dil, seq // dil, LANES) for l, (_, dil) in zip(lses, DIL_GROUPS)]
    return pl.pallas_call(
        _merge_kernel,
        grid=(batch, nt),
        in_specs=[pl.BlockSpec((1, tm, d), lambda b, i: (b, i, 0)),
                  pl.BlockSpec((tm, GLA_V), tok)]
        + [dil_spec(dil, DIL_OUT) for _, dil in DIL_GROUPS]
        + [dil_spec(dil, LANES) for _, dil in DIL_GROUPS]
        + [pl.BlockSpec((tm, DIL_OUT), lambda b, i: (b * nt + i, GATE_ZD)),
           pl.BlockSpec((tm, d), lambda b, i: (b * nt + i, GATE_GA * COL_TILE // D_MODEL)),
           pl.BlockSpec((tm, d), lambda b, i: (b * nt + i, GATE_GD * COL_TILE // D_MODEL)),
           pl.BlockSpec(wga.shape, fixed),
           pl.BlockSpec(wdo.shape, fixed),
           pl.BlockSpec(wo.shape, fixed)],
        out_specs=pl.BlockSpec((1, tm, d), lambda b, i: (b, i, 0)),
        out_shape=jax.ShapeDtypeStruct(x.shape, x.dtype),
        scratch_shapes=[pltpu.VMEM((len(DIL_GROUPS), DIL_HEADS, tm, DIL_HD), F32),
                        pltpu.VMEM((len(DIL_GROUPS), tm, LANES), F32)],
        compiler_params=_params("parallel", "parallel"),
        name="merge_out",
    )(x, o_a, *o_ds, *lses, gates, gates, gates, wga, wdo, wo)


def kernel(x, positions, norm_gain, w_in, gla_w_a2, gla_b_a, gla_out_gain, dil_q_gain, dil_k_gain,
           w_gla_out, w_dil_out, w_o):
    batch, seq, d = x.shape
    t = batch * seq
    half = DIL_HD // 2
    inv_freq = ROPE_THETA ** (-jnp.arange(half, dtype=F32) / half)
    freq = jnp.concatenate([inv_freq, inv_freq]).reshape(1, DIL_HD)
    pos = positions.astype(F32).reshape(t, 1)
    for layer in range(norm_gain.shape[0]):
        w = w_in[layer]
        w_lo = w[:, :_ALR].astype(BF16)
        w_hi = _w_hi(w)
        walr = jnp.pad(w[:, _ALR:_QD], ((0, 0), (0, LANES - GLA_RANK))).astype(BF16)
        wa2 = jnp.pad(gla_w_a2[layer], ((0, LANES - GLA_RANK), (0, 0))).astype(BF16)
        ba = gla_b_a[layer].reshape(1, GLA_QK)
        g_dqk = jnp.stack([dil_q_gain[layer] * (DIL_HD ** -0.5), dil_k_gain[layer]])

        o_a, gates, h, cos, sin = _nat_gla(x.reshape(t, d), norm_gain[layer], pos, freq, w_lo, w_hi,
                                           walr, wa2, ba, gla_out_gain[layer].reshape(1, GLA_DV), seq)
        o_ds, lses = [], []
        for g, (win, dil) in enumerate(DIL_GROUPS):
            assert win // dil == ATT_BLOCK
            q_g, k_g, v_g = _dil_proj(h, w_hi, g, dil, g_dqk, cos, sin, batch)
            o_g, lse_g = _dil_attn(q_g, k_g, v_g, batch * dil, dil)
            o_ds.append(o_g)
            lses.append(lse_g)
        x = _merge(x, o_a, o_ds, lses, gates,
                   w_gla_out[layer].astype(BF16), w_dil_out[layer].astype(BF16),
                   w_o[layer].astype(BF16))
    return x
```

```python
import functools

import jax
import jax.numpy as jnp
from jax import lax
from jax.experimental import pallas as pl
from jax.experimental.pallas import tpu as pltpu

D_MODEL = 1024
EPS = 1e-6
ROPE_THETA = 10000.0
GLA_HEADS = 4
GLA_DK = 128
GLA_DV = 256
GLA_RANK = 16
GLA_TAU = 16.0
GLA_QK = GLA_HEADS * GLA_DK
GLA_V = GLA_HEADS * GLA_DV
DIL_GROUPS = ((128, 1), (512, 4), (2048, 16))
DIL_HEADS = 4
DIL_HD = 128
DIL_QK = len(DIL_GROUPS) * DIL_HEADS * DIL_HD
DIL_OUT = DIL_HEADS * DIL_HD
IN_SPLIT_SIZES = (GLA_QK, GLA_QK, GLA_V, GLA_V, GLA_RANK,
                  DIL_QK, DIL_QK, DIL_QK, DIL_OUT, D_MODEL, D_MODEL)

LANES = 128
BF16_ROWS = 16
ROW_STRIDE = 4
GLA_BLOCK = 128
GLA_MID = GLA_BLOCK // 2
ATT_BLOCK = 128
ATT_SUPER = 1024
TOK_TILE = 1024
ROW_SUB = 256
GLA_TOK_TILE = 512
MERGE_TOK_TILE = 512
COL_TILE = DIL_HEADS * DIL_HD
VMEM_LIMIT_BYTES = 48 * 1024 * 1024

F32 = jnp.float32
BF16 = jnp.bfloat16
NEG = -1e30

GATE_GA, GATE_GD, GATE_ZD = 0, 2, 4

(_QA, _KA, _VA, _RA, _ALR, _QD, _KD, _VD, _ZD, _GA, _GD) = (
    sum(IN_SPLIT_SIZES[:i]) for i in range(len(IN_SPLIT_SIZES)))


def _block_index(offset, width):
    assert offset % width == 0
    return offset // width


LO_QK, LO_V, LO_R = _block_index(_QA, 2 * GLA_QK), _block_index(_VA, GLA_V), _block_index(_RA, GLA_V)
LO_ALR = _block_index(_ALR, LANES)
HI_QD = _block_index(_QD - _QD, COL_TILE)
HI_ZD = _block_index(_ZD - _QD, DIL_OUT)
HI_GA, HI_GD = _block_index(_GA - _QD, D_MODEL), _block_index(_GD - _QD, D_MODEL)
assert _KA == _QA + GLA_QK and _KD - _QD == DIL_QK and _VD - _KD == DIL_QK


def _w_cols(width, index):
    return pl.BlockSpec((D_MODEL, width), lambda *_: (0, index))


def _params(*sem):
    return pltpu.CompilerParams(dimension_semantics=sem, vmem_limit_bytes=VMEM_LIMIT_BYTES)


def _dot(a, b):
    return jnp.dot(a, b, preferred_element_type=F32)


def _dot_nt(a, b):
    return lax.dot_general(a, b, (((1,), (1,)), ((), ())), preferred_element_type=F32)


def _dot_tn(a, b):
    return lax.dot_general(a, b, (((0,), (0,)), ((), ())), preferred_element_type=F32)


def _sigmoid(x):
    return 0.5 * jnp.tanh(0.5 * x) + 0.5


def _row_subs(n):
    return [slice(r * ROW_SUB, (r + 1) * ROW_SUB) for r in range(n // ROW_SUB)]


def _w_cast_kernel(wt_ref, o_ref):
    o_ref[...] = wt_ref[...].T.astype(o_ref.dtype)


def _w_cast(w_t, start, width, name):
    d = w_t.shape[1]
    assert start % BF16_ROWS == 0
    return pl.pallas_call(
        _w_cast_kernel,
        grid=(_block_index(width, COL_TILE),),
        in_specs=[pl.BlockSpec((pl.Element(COL_TILE), pl.Element(d)),
                               lambda j: (pl.multiple_of(start + j * COL_TILE, BF16_ROWS), 0))],
        out_specs=pl.BlockSpec((d, COL_TILE), lambda j: (0, j)),
        out_shape=jax.ShapeDtypeStruct((d, width), BF16),
        compiler_params=_params("parallel"),
        name=name,
    )(w_t)


def _nat_gla_kernel(x_ref, ng_ref, pos_ref, freq_ref, wqk_ref, wv_ref, wr_ref, wga_ref, wgd_ref,
                    wzd_ref, walr_ref, wa2_ref, ba_ref, gain_ref,
                    oa_ref, gates_ref, h_ref, cos_ref, sin_ref, st_ref, *, tiles_per_seq):
    @pl.when(pl.program_id(0) % tiles_per_seq == 0)
    def _():
        st_ref[...] = jnp.zeros_like(st_ref)

    row = lax.broadcasted_iota(jnp.int32, (GLA_BLOCK, GLA_BLOCK), 0)
    col = lax.broadcasted_iota(jnp.int32, (GLA_BLOCK, GLA_BLOCK), 1)
    causal = col <= row
    tri = causal.astype(BF16)
    n_blk = x_ref.shape[0] // GLA_BLOCK
    lane = lax.broadcasted_iota(jnp.int32, (GLA_BLOCK, DIL_HD), 1)

    gate_plan = ((wga_ref, 0, "sigmoid"), (wga_ref, 1, "sigmoid"), (wgd_ref, 0, "sigmoid"),
                 (wgd_ref, 1, "sigmoid"), (wzd_ref, 0, "silu"))

    def gate_tiles(c, h, lo_j, hi_j):
        rows = slice(c * GLA_BLOCK, (c + 1) * GLA_BLOCK)
        for j in range(lo_j, hi_j):
            w_ref, wj, act = gate_plan[j]
            acc = _dot(h, w_ref[:, wj * COL_TILE:(wj + 1) * COL_TILE])
            sg = _sigmoid(acc)
            gates_ref[rows, j * COL_TILE:(j + 1) * COL_TILE] = (
                acc * sg if act == "silu" else sg).astype(gates_ref.dtype)

    def recurrence(c, qk, hi, lo, v, r):
        rows = slice(c * GLA_BLOCK, (c + 1) * GLA_BLOCK)
        b = _dot(tri, hi) + _dot(tri, lo)
        b_mid = b[GLA_MID - 1:GLA_MID]
        b_last = b[GLA_BLOCK - 1:GLA_BLOCK]
        q = qk[:, :GLA_QK] * (GLA_DK ** -0.5)
        k = qk[:, GLA_QK:]
        q_in = (q * jnp.exp(b)).astype(BF16)
        q_mid = (q * jnp.exp(b - b_mid)).astype(BF16)
        k_mid = (k * jnp.exp(b_mid - b)).astype(BF16)
        k_end = (k * jnp.exp(b_last - b)).astype(BF16)
        dec = jnp.exp(b_last)
        yield
        heads = [(slice(hh * GLA_DK, (hh + 1) * GLA_DK), slice(hh * GLA_DV, (hh + 1) * GLA_DV))
                 for hh in range(GLA_HEADS)]
        attn = [_dot_nt(q_mid[:, ks], k_mid[:, ks]) for ks, _ in heads]
        kv_t = [_dot_tn(v[:, vs], k_end[:, ks]) for ks, vs in heads]
        yield
        outs = []
        for hh, (ks, vs) in enumerate(heads):
            st = st_ref[hh]
            a = jnp.where(causal, attn[hh], 0.0).astype(BF16)
            outs.append(_dot(a, v[:, vs]) + _dot_nt(q_in[:, ks], st.astype(BF16)))
            st_ref[hh] = st * dec[:, ks] + kv_t[hh]
        yield
        for (_, vs), o in zip(heads, outs):
            ms = jnp.mean(o * o, axis=-1, keepdims=True)
            o = o * lax.rsqrt(ms + EPS) * gain_ref[...] * r[:, vs]
            oa_ref[rows, vs] = o.astype(oa_ref.dtype)
        yield

    def advance(gen):
        if gen is not None:
            next(gen)

    n_gate = len(gate_plan)
    gen = None
    for c in range(n_blk):
        rows = slice(c * GLA_BLOCK, (c + 1) * GLA_BLOCK)
        x = x_ref[rows, :]
        ms = jnp.mean(x * x, axis=-1, keepdims=True)
        h = (x * lax.rsqrt(ms + EPS) * ng_ref[...]).astype(BF16)
        h_ref[rows, :] = h
        alr = _dot(h, walr_ref[...]).astype(BF16)
        v = _dot(h, wv_ref[...]).astype(BF16)
        advance(gen)
        ang = pos_ref[rows, :] * freq_ref[...]
        cos_ref[rows, :] = jnp.cos(ang)
        sin_ref[rows, :] = jnp.where(lane < DIL_HD // 2, -jnp.sin(ang), jnp.sin(ang))
        z = _dot(alr, wa2_ref[...]) + ba_ref[...]
        la = (jnp.minimum(z, 0.0) - jnp.log1p(jnp.exp(-jnp.abs(z)))) * (1.0 / GLA_TAU)
        hi = la.astype(BF16)
        lo = (la - hi.astype(F32)).astype(BF16)
        r = _dot(h, wr_ref[...])
        r = r * _sigmoid(r)
        advance(gen)
        gate_tiles(c, h, 0, 2)
        advance(gen)
        qk = _dot(h, wqk_ref[...])
        advance(gen)
        gate_tiles(c, h, 2, n_gate)
        gen = recurrence(c, qk, hi, lo, v, r)
    for _ in gen:
        pass


def _nat_gla(x2, norm_gain, pos, freq, w_lo, w_hi, wa2, ba, gain, seq):
    t, d = x2.shape
    tm = GLA_TOK_TILE
    tok = lambda i: (i, 0)
    fixed = lambda i: (0, 0)
    small = (wa2, ba, gain)
    n_gates = 2 * D_MODEL + DIL_OUT
    table = jax.ShapeDtypeStruct((t, DIL_HD), F32)
    return pl.pallas_call(
        functools.partial(_nat_gla_kernel, tiles_per_seq=seq // tm),
        grid=(t // tm,),
        in_specs=[pl.BlockSpec((tm, d), tok), pl.BlockSpec((1, d), fixed),
                  pl.BlockSpec((tm, 1), tok), pl.BlockSpec((1, DIL_HD), fixed),
                  _w_cols(2 * GLA_QK, LO_QK), _w_cols(GLA_V, LO_V), _w_cols(GLA_V, LO_R),
                  _w_cols(D_MODEL, HI_GA), _w_cols(D_MODEL, HI_GD), _w_cols(DIL_OUT, HI_ZD),
                  _w_cols(LANES, LO_ALR)]
        + [pl.BlockSpec(w.shape, fixed) for w in small],
        out_specs=[pl.BlockSpec((tm, GLA_V), tok), pl.BlockSpec((tm, n_gates), tok),
                   pl.BlockSpec((tm, d), tok), pl.BlockSpec((tm, DIL_HD), tok),
                   pl.BlockSpec((tm, DIL_HD), tok)],
        out_shape=[jax.ShapeDtypeStruct((t, GLA_V), BF16), jax.ShapeDtypeStruct((t, n_gates), BF16),
                   jax.ShapeDtypeStruct((t, d), BF16), table, table],
        scratch_shapes=[pltpu.VMEM((GLA_HEADS, GLA_DV, GLA_DK), F32)],
        compiler_params=_params("arbitrary"),
        name="nat_gla",
    )(x2, norm_gain.reshape(1, d), pos, freq, w_lo, w_lo, w_lo, w_hi, w_hi, w_hi, w_lo, *small)


def _dil_proj_kernel(h_ref, wq_ref, wk_ref, wv_ref, g_ref, cos_ref, sin_ref, q_ref, k_ref, v_ref,
                     *scratch, dil):
    n_planes = COL_TILE // LANES
    tm = h_ref.shape[0]
    outs = (q_ref, k_ref, v_ref)

    def store(kind, rows, plane, y):
        if dil == 1:
            outs[kind][0, 0, rows, plane * LANES:(plane + 1) * LANES] = y.astype(q_ref.dtype)
        else:
            scratch[0][kind, plane, rows, :] = y

    def permute(kind):
        s0 = min(dil, ROW_STRIDE)
        s1 = dil // s0
        for plane in range(n_planes):
            cols = slice(plane * LANES, (plane + 1) * LANES)
            for r0 in range(s0):
                hop = scratch[0][kind, plane, pl.ds(r0, tm // s0, stride=s0), :]
                if s1 == 1:
                    outs[kind][0, r0, :, cols] = hop.astype(q_ref.dtype)
                    continue
                scratch[1][plane, r0] = hop
                for r1 in range(s1):
                    outs[kind][0, s0 * r1 + r0, :, cols] = (
                        scratch[1][plane, r0, pl.ds(r1, tm // dil, stride=s1), :].astype(q_ref.dtype))

    for kind, w_ref in enumerate((wq_ref, wk_ref, wv_ref)):
        for n, rows in enumerate(_row_subs(tm)):
            acc = _dot(h_ref[rows, :], w_ref[...])
            if dil > 1 and kind > 0 and n == 0:
                permute(kind - 1)
            for hh in range(n_planes):
                xh = acc[:, hh * DIL_HD:(hh + 1) * DIL_HD]
                if kind == 2:
                    store(kind, rows, hh, xh)
                    continue
                ms = jnp.mean(xh * xh, axis=-1, keepdims=True)
                y = xh * lax.rsqrt(ms + EPS) * g_ref[kind:kind + 1, :]
                y = y * cos_ref[rows, :] + pltpu.roll(y, DIL_HD // 2, 1) * sin_ref[rows, :]
                store(kind, rows, hh, y)
    if dil > 1:
        permute(2)


def _dil_proj(h, w_hi, group, dil, gains, cos, sin, batch):
    t, d = h.shape
    seq = t // batch
    tm = TOK_TILE
    nt = seq // tm
    tok = lambda b, i: (b * nt + i, 0)
    fixed = lambda b, i: (0, 0)
    out = jax.ShapeDtypeStruct((batch, dil, seq // dil, COL_TILE), BF16)
    n_groups = len(DIL_GROUPS)
    assert DIL_HD == LANES and tm % (dil * BF16_ROWS) == 0
    return pl.pallas_call(
        functools.partial(_dil_proj_kernel, dil=dil),
        grid=(batch, nt),
        in_specs=[pl.BlockSpec((tm, d), tok)]
        + [_w_cols(COL_TILE, HI_QD + kind * n_groups + group) for kind in range(3)]
        + [pl.BlockSpec(gains.shape, fixed),
           pl.BlockSpec((tm, DIL_HD), tok),
           pl.BlockSpec((tm, DIL_HD), tok)],
        out_specs=[pl.BlockSpec((1, dil, tm // dil, COL_TILE), lambda b, i: (b, 0, i, 0))] * 3,
        out_shape=[out] * 3,
        scratch_shapes=[] if dil == 1 else [
            pltpu.VMEM((3, COL_TILE // LANES, tm, LANES), F32),
            pltpu.VMEM((COL_TILE // LANES, ROW_STRIDE, tm // ROW_STRIDE, LANES), F32)],
        compiler_params=_params("parallel", "parallel"),
        name=f"dil_proj_d{dil}",
    )(h, w_hi, w_hi, w_hi, gains, cos, sin)


def _dil_attn_kernel(q_ref, kp_ref, kc_ref, vp_ref, vc_ref, o_ref, lse_ref):
    n_qblk = q_ref.shape[1] // ATT_BLOCK
    row = lax.broadcasted_iota(jnp.int32, (ATT_BLOCK, 2 * ATT_BLOCK), 0)
    col = lax.broadcasted_iota(jnp.int32, (ATT_BLOCK, 2 * ATT_BLOCK), 1)
    band = (col >= row) & (col <= row + ATT_BLOCK)
    band_first = band & ((col >= ATT_BLOCK) | (pl.program_id(1) > 0))
    ones = jnp.ones((2 * ATT_BLOCK, DIL_HD), BF16)
    lane = lax.broadcasted_iota(jnp.int32, (ATT_BLOCK, LANES), 1)
    for sub, a in [(sub, a) for sub in range(q_ref.shape[0]) for a in range(n_qblk)]:
        rows = slice(a * ATT_BLOCK, (a + 1) * ATT_BLOCK)
        lse_all = jnp.zeros((ATT_BLOCK, LANES), F32)
        for hh in range(DIL_HEADS):
            sl = slice(hh * DIL_HD, (hh + 1) * DIL_HD)
            if a == 0:
                k_win = jnp.concatenate([kp_ref[sub, :, sl], kc_ref[sub, :ATT_BLOCK, sl]], axis=0)
                v_win = jnp.concatenate([vp_ref[sub, :, sl], vc_ref[sub, :ATT_BLOCK, sl]], axis=0)
            else:
                win = slice((a - 1) * ATT_BLOCK, (a + 1) * ATT_BLOCK)
                k_win = kc_ref[sub, win, sl]
                v_win = vc_ref[sub, win, sl]
            s = _dot_nt(q_ref[sub, rows, sl], k_win)
            s = jnp.where(band_first if a == 0 else band, s, NEG)
            m = jnp.max(s, axis=-1, keepdims=True)
            p = jnp.exp(s - m).astype(BF16)
            ol = _dot(p, jnp.concatenate([v_win, ones], axis=1))
            l = ol[:, DIL_HD:]
            o_ref[sub, rows, sl] = (ol[:, :DIL_HD] / l).astype(o_ref.dtype)
            lse_all = jnp.where(lane == hh, m + jnp.log(l), lse_all)
        lse_ref[sub, rows, :] = lse_all


def _dil_attn(q, k, v, n_sub, dil):
    sub_len = q.size // COL_TILE // n_sub
    qb = min(sub_len, ATT_SUPER)
    ns = ATT_SUPER // qb
    n_qblk = qb // ATT_BLOCK
    q3, k3, v3 = (a.reshape(n_sub, sub_len, COL_TILE) for a in (q, k, v))
    blk = (ns, qb, COL_TILE)
    cur = lambda s, i: (s, i, 0)
    prev = lambda s, i: (s, jnp.maximum(i * n_qblk - 1, 0), 0)
    prev_blk = (ns, ATT_BLOCK, COL_TILE)
    o, lse = pl.pallas_call(
        _dil_attn_kernel,
        grid=(n_sub // ns, sub_len // qb),
        in_specs=[pl.BlockSpec(blk, cur), pl.BlockSpec(prev_blk, prev), pl.BlockSpec(blk, cur),
                  pl.BlockSpec(prev_blk, prev), pl.BlockSpec(blk, cur)],
        out_specs=[pl.BlockSpec(blk, cur), pl.BlockSpec((ns, qb, LANES), cur)],
        out_shape=[jax.ShapeDtypeStruct((n_sub, sub_len, COL_TILE), BF16),
                   jax.ShapeDtypeStruct((n_sub, sub_len, LANES), F32)],
        compiler_params=_params("parallel", "arbitrary"),
        name=f"dil_attn_d{dil}",
    )(q3, k3, k3, v3, v3)
    return o, lse


def _merge_kernel(x_ref, oa_ref, o0_ref, o1_ref, o2_ref, l0_ref, l1_ref, l2_ref,
                  sz_ref, sga_ref, sgd_ref, wga_ref, wdo_ref, wo_ref, out_ref, og_ref, lg_ref):
    tm = x_ref.shape[1]
    for g, ((_, dil), o_ref, l_ref) in enumerate(zip(DIL_GROUPS, (o0_ref, o1_ref, o2_ref),
                                                     (l0_ref, l1_ref, l2_ref))):
        for r in range(dil):
            dst = pl.ds(r, tm // dil, stride=dil)
            lg_ref[g, dst, :] = l_ref[0, r]
            for hh in range(DIL_HEADS):
                og_ref[g, hh, dst, :] = o_ref[0, r, :, hh * DIL_HD:(hh + 1) * DIL_HD].astype(F32)
    n_groups = len(DIL_GROUPS)

    def combine(rows):
        lses = [lg_ref[g, rows, :] for g in range(n_groups)]
        m = jnp.maximum(jnp.maximum(lses[0], lses[1]), lses[2])
        es = [jnp.exp(l - m) for l in lses]
        den = es[0] + es[1] + es[2]
        ws = [e / den for e in es]
        heads = []
        for hh in range(DIL_HEADS):
            acc = ws[0][:, hh:hh + 1] * og_ref[0, hh, rows, :]
            for g in range(1, n_groups):
                acc = acc + ws[g][:, hh:hh + 1] * og_ref[g, hh, rows, :]
            heads.append(acc)
        return (jnp.concatenate(heads, axis=-1) * sz_ref[rows, :].astype(F32)).astype(BF16)

    def branches(rows, o_d):
        y_a = _dot(oa_ref[rows, :], wga_ref[...])
        y_d = _dot(o_d, wdo_ref[...])
        return y_a, y_d

    def gate(rows, y_a, y_d):
        return (sga_ref[rows, :].astype(F32) * y_a + sgd_ref[rows, :].astype(F32) * y_d).astype(BF16)

    def project(rows, y):
        out_ref[0, rows, :] = x_ref[0, rows, :] + _dot(y, wo_ref[...])

    for rows in _row_subs(tm):
        project(rows, gate(rows, *branches(rows, combine(rows))))


def _merge(x, o_a, o_ds, lses, gates, wga, wdo, wo):
    batch, seq, d = x.shape
    tm = MERGE_TOK_TILE
    nt = seq // tm
    tok = lambda b, i: (b * nt + i, 0)
    fixed = lambda b, i: (0, 0)
    dil_spec = lambda dil, w: pl.BlockSpec((1, dil, tm // dil, w), lambda b, i: (b, 0, i, 0))
    o_ds = [o.reshape(batch, dil, seq // dil, DIL_OUT) for o, (_, dil) in zip(o_ds, DIL_GROUPS)]
    lses = [l.reshape(batch, dil, seq // dil, LANES) for l, (_, dil) in zip(lses, DIL_GROUPS)]
    return pl.pallas_call(
        _merge_kernel,
        grid=(batch, nt),
        in_specs=[pl.BlockSpec((1, tm, d), lambda b, i: (b, i, 0)),
                  pl.BlockSpec((tm, GLA_V), tok)]
        + [dil_spec(dil, DIL_OUT) for _, dil in DIL_GROUPS]
        + [dil_spec(dil, LANES) for _, dil in DIL_GROUPS]
        + [pl.BlockSpec((tm, DIL_OUT), lambda b, i: (b * nt + i, GATE_ZD)),
           pl.BlockSpec((tm, d), lambda b, i: (b * nt + i, GATE_GA * COL_TILE // D_MODEL)),
           pl.BlockSpec((tm, d), lambda b, i: (b * nt + i, GATE_GD * COL_TILE // D_MODEL)),
           pl.BlockSpec(wga.shape, fixed),
           pl.BlockSpec(wdo.shape, fixed),
           pl.BlockSpec(wo.shape, fixed)],
        out_specs=pl.BlockSpec((1, tm, d), lambda b, i: (b, i, 0)),
        out_shape=jax.ShapeDtypeStruct(x.shape, x.dtype),
        scratch_shapes=[pltpu.VMEM((len(DIL_GROUPS), DIL_HEADS, tm, DIL_HD), F32),
                        pltpu.VMEM((len(DIL_GROUPS), tm, LANES), F32)],
        compiler_params=_params("parallel", "parallel"),
        name="merge_out",
    )(x, o_a, *o_ds, *lses, gates, gates, gates, wga, wdo, wo)


def kernel(x, positions, norm_gain, w_in, gla_w_a2, gla_b_a, gla_out_gain, dil_q_gain, dil_k_gain,
           w_gla_out, w_dil_out, w_o):
    batch, seq, d = x.shape
    t = batch * seq
    half = DIL_HD // 2
    inv_freq = ROPE_THETA ** (-jnp.arange(half, dtype=F32) / half)
    freq = jnp.concatenate([inv_freq, inv_freq]).reshape(1, DIL_HD)
    pos = positions.astype(F32).reshape(t, 1)
    for layer in range(norm_gain.shape[0]):
        w_t = jnp.swapaxes(w_in[layer], 0, 1)
        w_lo = _w_cast(w_t, _QA, _ALR - _QA + COL_TILE, "w_lo_cast")
        w_hi = _w_cast(w_t, _QD, w_t.shape[0] - _QD, "w_hi_cast")
        wa2 = jnp.pad(gla_w_a2[layer], ((0, LANES - GLA_RANK), (0, 0))).astype(BF16)
        ba = gla_b_a[layer].reshape(1, GLA_QK)
        g_dqk = jnp.stack([dil_q_gain[layer] * (DIL_HD ** -0.5), dil_k_gain[layer]])

        o_a, gates, h, cos, sin = _nat_gla(x.reshape(t, d), norm_gain[layer], pos, freq, w_lo, w_hi,
                                           wa2, ba, gla_out_gain[layer].reshape(1, GLA_DV), seq)
        o_ds, lses = [], []
        for g, (win, dil) in enumerate(DIL_GROUPS):
            assert win // dil == ATT_BLOCK
            q_g, k_g, v_g = _dil_proj(h, w_hi, g, dil, g_dqk, cos, sin, batch)
            o_g, lse_g = _dil_attn(q_g, k_g, v_g, batch * dil, dil)
            o_ds.append(o_g)
            lses.append(lse_g)
        x = _merge(x, o_a, o_ds, lses, gates,
                   w_gla_out[layer].astype(BF16), w_dil_out[layer].astype(BF16),
                   w_o[layer].astype(BF16))
    return x
```

```python
import functools

import jax
import jax.numpy as jnp
from jax import lax
from jax.experimental import pallas as pl
from jax.experimental.pallas import tpu as pltpu

D_MODEL = 1024
EPS = 1e-6
ROPE_THETA = 10000.0
GLA_HEADS = 4
GLA_DK = 128
GLA_DV = 256
GLA_RANK = 16
GLA_TAU = 16.0
GLA_QK = GLA_HEADS * GLA_DK
GLA_V = GLA_HEADS * GLA_DV
DIL_GROUPS = ((128, 1), (512, 4), (2048, 16))
DIL_HEADS = 4
DIL_HD = 128
DIL_QK = len(DIL_GROUPS) * DIL_HEADS * DIL_HD
DIL_OUT = DIL_HEADS * DIL_HD
IN_SPLIT_SIZES = (GLA_QK, GLA_QK, GLA_V, GLA_V, GLA_RANK,
                  DIL_QK, DIL_QK, DIL_QK, DIL_OUT, D_MODEL, D_MODEL)

LANES = 128
BF16_ROWS = 16
ROW_STRIDE = 4
GLA_BLOCK = 128
GLA_MID = GLA_BLOCK // 2
ATT_BLOCK = 128
ATT_SUPER = 1024
TOK_TILE = 1024
ROW_SUB = 256
DIL_ROW_SUB = 256
GLA_TOK_TILE = 1024
MERGE_TOK_TILE = 512
COL_TILE = DIL_HEADS * DIL_HD
VMEM_LIMIT_BYTES = 56 * 1024 * 1024

F32 = jnp.float32
BF16 = jnp.bfloat16
NEG = -1e30

GATE_GA, GATE_GD, GATE_ZD = 0, 2, 4

(_QA, _KA, _VA, _RA, _ALR, _QD, _KD, _VD, _ZD, _GA, _GD) = (
    sum(IN_SPLIT_SIZES[:i]) for i in range(len(IN_SPLIT_SIZES)))


def _block_index(offset, width):
    assert offset % width == 0
    return offset // width


W_BLOCK = 1024
LO_PAD = -(-(_ALR + LANES) // W_BLOCK) * W_BLOCK
LO_QK, LO_V, LO_R = _block_index(_QA, 2 * GLA_QK), _block_index(_VA, GLA_V), _block_index(_RA, GLA_V)
LO_ALR = _block_index(_ALR, LANES)
HI_QD = _block_index(LO_PAD, COL_TILE)
HI_ZD = _block_index(LO_PAD + _ZD - _QD, DIL_OUT)
HI_GA, HI_GD = _block_index(LO_PAD + _GA - _QD, D_MODEL), _block_index(LO_PAD + _GD - _QD, D_MODEL)
assert _KA == _QA + GLA_QK and _KD - _QD == DIL_QK and _VD - _KD == DIL_QK


def _w_cols(width, index):
    return pl.BlockSpec((D_MODEL, width), lambda *_: (0, index))


def _params(*sem):
    return pltpu.CompilerParams(dimension_semantics=sem, vmem_limit_bytes=VMEM_LIMIT_BYTES)


def _dot(a, b):
    return jnp.dot(a, b, preferred_element_type=F32)


def _dot_nt(a, b):
    return lax.dot_general(a, b, (((1,), (1,)), ((), ())), preferred_element_type=F32)


def _dot_tn(a, b):
    return lax.dot_general(a, b, (((0,), (0,)), ((), ())), preferred_element_type=F32)


def _sigmoid(x):
    return 0.5 * jnp.tanh(0.5 * x) + 0.5


def _row_subs(n, sub=ROW_SUB):
    return [slice(r * sub, (r + 1) * sub) for r in range(n // sub)]


def _w_cast_kernel(wt_ref, o_ref):
    o_ref[...] = wt_ref[...].T.astype(o_ref.dtype)


def _w_cast(w_t):
    n, d = w_t.shape
    n_lo = _block_index(LO_PAD, W_BLOCK)
    n_hi = _block_index(n - _QD, W_BLOCK)
    assert _QD % BF16_ROWS == 0 and LO_PAD <= n

    def rows(j):
        start = jnp.where(j < n_lo, j * W_BLOCK, _QD + (j - n_lo) * W_BLOCK)
        return pl.multiple_of(start, BF16_ROWS), 0

    return pl.pallas_call(
        _w_cast_kernel,
        grid=(n_lo + n_hi,),
        in_specs=[pl.BlockSpec((pl.Element(W_BLOCK), pl.Element(d)), rows)],
        out_specs=pl.BlockSpec((d, W_BLOCK), lambda j: (0, j)),
        out_shape=jax.ShapeDtypeStruct((d, (n_lo + n_hi) * W_BLOCK), BF16),
        compiler_params=_params("parallel"),
        name="w_cast",
    )(w_t)


def _nat_gla_kernel(x_ref, ng_ref, pos_ref, freq_ref, wqk_ref, wv_ref, wr_ref, wga_ref, wgd_ref,
                    wzd_ref, walr_ref, wa2_ref, ba_ref, gain_ref,
                    oa_ref, gates_ref, h_ref, cos_ref, sin_ref, st_ref, *, tiles_per_seq):
    @pl.when(pl.program_id(0) % tiles_per_seq == 0)
    def _():
        st_ref[...] = jnp.zeros_like(st_ref)

    row = lax.broadcasted_iota(jnp.int32, (GLA_BLOCK, GLA_BLOCK), 0)
    col = lax.broadcasted_iota(jnp.int32, (GLA_BLOCK, GLA_BLOCK), 1)
    causal = col <= row
    tri = causal.astype(BF16)
    n_blk = x_ref.shape[0] // GLA_BLOCK
    lane = lax.broadcasted_iota(jnp.int32, (GLA_BLOCK, DIL_HD), 1)

    gate_plan = ((wga_ref, 0, "sigmoid"), (wga_ref, 1, "sigmoid"), (wgd_ref, 0, "sigmoid"),
                 (wgd_ref, 1, "sigmoid"), (wzd_ref, 0, "silu"))

    def gate_tiles(c, h, lo_j, hi_j):
        rows = slice(c * GLA_BLOCK, (c + 1) * GLA_BLOCK)
        for j in range(lo_j, hi_j):
            w_ref, wj, act = gate_plan[j]
            acc = _dot(h, w_ref[:, wj * COL_TILE:(wj + 1) * COL_TILE])
            sg = _sigmoid(acc)
            gates_ref[rows, j * COL_TILE:(j + 1) * COL_TILE] = (
                acc * sg if act == "silu" else sg).astype(gates_ref.dtype)

    def recurrence(c, qk, hi, lo, v, r):
        rows = slice(c * GLA_BLOCK, (c + 1) * GLA_BLOCK)
        b = _dot(tri, hi) + _dot(tri, lo)
        b_mid = b[GLA_MID - 1:GLA_MID]
        b_last = b[GLA_BLOCK - 1:GLA_BLOCK]
        q = qk[:, :GLA_QK] * (GLA_DK ** -0.5)
        k = qk[:, GLA_QK:]
        q_in = (q * jnp.exp(b)).astype(BF16)
        q_mid = (q * jnp.exp(b - b_mid)).astype(BF16)
        k_mid = (k * jnp.exp(b_mid - b)).astype(BF16)
        k_end = (k * jnp.exp(b_last - b)).astype(BF16)
        dec = jnp.exp(b_last)
        yield
        heads = [(slice(hh * GLA_DK, (hh + 1) * GLA_DK), slice(hh * GLA_DV, (hh + 1) * GLA_DV))
                 for hh in range(GLA_HEADS)]
        attn = [_dot_nt(q_mid[:, ks], k_mid[:, ks]) for ks, _ in heads]
        kv_t = [_dot_tn(v[:, vs], k_end[:, ks]) for ks, vs in heads]
        yield
        outs = []
        for hh, (ks, vs) in enumerate(heads):
            st = st_ref[hh]
            a = jnp.where(causal, attn[hh], 0.0).astype(BF16)
            outs.append(_dot(a, v[:, vs]) + _dot_nt(q_in[:, ks], st.astype(BF16)))
            st_ref[hh] = st * dec[:, ks] + kv_t[hh]
        yield
        for (_, vs), o in zip(heads, outs):
            ms = jnp.mean(o * o, axis=-1, keepdims=True)
            o = o * lax.rsqrt(ms + EPS) * gain_ref[...] * r[:, vs]
            oa_ref[rows, vs] = o.astype(oa_ref.dtype)
        yield

    def advance(gen):
        if gen is not None:
            next(gen)

    n_gate = len(gate_plan)
    gen = None
    for c in range(n_blk):
        rows = slice(c * GLA_BLOCK, (c + 1) * GLA_BLOCK)
        x = x_ref[rows, :]
        ms = jnp.mean(x * x, axis=-1, keepdims=True)
        h = (x * lax.rsqrt(ms + EPS) * ng_ref[...]).astype(BF16)
        h_ref[rows, :] = h
        alr = _dot(h, walr_ref[...]).astype(BF16)
        v = _dot(h, wv_ref[...]).astype(BF16)
        advance(gen)
        ang = pos_ref[rows, :] * freq_ref[...]
        cos_ref[rows, :] = jnp.cos(ang)
        sin_ref[rows, :] = jnp.where(lane < DIL_HD // 2, -jnp.sin(ang), jnp.sin(ang))
        z = _dot(alr, wa2_ref[...]) + ba_ref[...]
        la = (jnp.minimum(z, 0.0) - jnp.log1p(jnp.exp(-jnp.abs(z)))) * (1.0 / GLA_TAU)
        hi = la.astype(BF16)
        lo = (la - hi.astype(F32)).astype(BF16)
        r = _dot(h, wr_ref[...])
        r = r * _sigmoid(r)
        advance(gen)
        gate_tiles(c, h, 0, 2)
        advance(gen)
        qk = _dot(h, wqk_ref[...])
        advance(gen)
        gate_tiles(c, h, 2, n_gate)
        gen = recurrence(c, qk, hi, lo, v, r)
    for _ in gen:
        pass


def _nat_gla(x2, norm_gain, pos, freq, w, wa2, ba, gain, seq):
    t, d = x2.shape
    tm = GLA_TOK_TILE
    tok = lambda i: (i, 0)
    fixed = lambda i: (0, 0)
    small = (wa2, ba, gain)
    n_gates = 2 * D_MODEL + DIL_OUT
    table = jax.ShapeDtypeStruct((t, DIL_HD), F32)
    return pl.pallas_call(
        functools.partial(_nat_gla_kernel, tiles_per_seq=seq // tm),
        grid=(t // tm,),
        in_specs=[pl.BlockSpec((tm, d), tok), pl.BlockSpec((1, d), fixed),
                  pl.BlockSpec((tm, 1), tok), pl.BlockSpec((1, DIL_HD), fixed),
                  _w_cols(2 * GLA_QK, LO_QK), _w_cols(GLA_V, LO_V), _w_cols(GLA_V, LO_R),
                  _w_cols(D_MODEL, HI_GA), _w_cols(D_MODEL, HI_GD), _w_cols(DIL_OUT, HI_ZD),
                  _w_cols(LANES, LO_ALR)]
        + [pl.BlockSpec(w.shape, fixed) for w in small],
        out_specs=[pl.BlockSpec((tm, GLA_V), tok), pl.BlockSpec((tm, n_gates), tok),
                   pl.BlockSpec((tm, d), tok), pl.BlockSpec((tm, DIL_HD), tok),
                   pl.BlockSpec((tm, DIL_HD), tok)],
        out_shape=[jax.ShapeDtypeStruct((t, GLA_V), BF16), jax.ShapeDtypeStruct((t, n_gates), BF16),
                   jax.ShapeDtypeStruct((t, d), BF16), table, table],
        scratch_shapes=[pltpu.VMEM((GLA_HEADS, GLA_DV, GLA_DK), F32)],
        compiler_params=_params("arbitrary"),
        name="nat_gla",
    )(x2, norm_gain.reshape(1, d), pos, freq, *([w] * 7), *small)


def _dil_proj_kernel(h_ref, wq_ref, wk_ref, wv_ref, g_ref, cos_ref, sin_ref, q_ref, k_ref, v_ref,
                     *scratch, dil):
    n_planes = COL_TILE // LANES
    tm = h_ref.shape[0]
    outs = (q_ref, k_ref, v_ref)

    def store(kind, rows, plane, y):
        if dil == 1:
            outs[kind][0, 0, rows, plane * LANES:(plane + 1) * LANES] = y.astype(q_ref.dtype)
        else:
            scratch[0][kind, plane, rows, :] = y

    def permute(kind):
        s0 = min(dil, ROW_STRIDE)
        s1 = dil // s0
        for plane in range(n_planes):
            cols = slice(plane * LANES, (plane + 1) * LANES)
            for r0 in range(s0):
                hop = scratch[0][kind, plane, pl.ds(r0, tm // s0, stride=s0), :]
                if s1 == 1:
                    outs[kind][0, r0, :, cols] = hop.astype(q_ref.dtype)
                    continue
                scratch[1][plane, r0] = hop
                for r1 in range(s1):
                    outs[kind][0, s0 * r1 + r0, :, cols] = (
                        scratch[1][plane, r0, pl.ds(r1, tm // dil, stride=s1), :].astype(q_ref.dtype))

    for kind, w_ref in enumerate((wq_ref, wk_ref, wv_ref)):
        for n, rows in enumerate(_row_subs(tm, DIL_ROW_SUB)):
            acc = _dot(h_ref[rows, :], w_ref[...])
            if dil > 1 and kind > 0 and n == 0:
                permute(kind - 1)
            for hh in range(n_planes):
                xh = acc[:, hh * DIL_HD:(hh + 1) * DIL_HD]
                if kind == 2:
                    store(kind, rows, hh, xh)
                    continue
                ms = jnp.mean(xh * xh, axis=-1, keepdims=True)
                y = xh * lax.rsqrt(ms + EPS) * g_ref[kind:kind + 1, :]
                y = y * cos_ref[rows, :] + pltpu.roll(y, DIL_HD // 2, 1) * sin_ref[rows, :]
                store(kind, rows, hh, y)
    if dil > 1:
        permute(2)


def _dil_proj(h, w, group, dil, gains, cos, sin, batch):
    t, d = h.shape
    seq = t // batch
    tm = TOK_TILE
    nt = seq // tm
    tok = lambda b, i: (b * nt + i, 0)
    fixed = lambda b, i: (0, 0)
    out = jax.ShapeDtypeStruct((batch, dil, seq // dil, COL_TILE), BF16)
    n_groups = len(DIL_GROUPS)
    assert DIL_HD == LANES and tm % (dil * BF16_ROWS) == 0
    return pl.pallas_call(
        functools.partial(_dil_proj_kernel, dil=dil),
        grid=(batch, nt),
        in_specs=[pl.BlockSpec((tm, d), tok)]
        + [_w_cols(COL_TILE, HI_QD + kind * n_groups + group) for kind in range(3)]
        + [pl.BlockSpec(gains.shape, fixed),
           pl.BlockSpec((tm, DIL_HD), tok),
           pl.BlockSpec((tm, DIL_HD), tok)],
        out_specs=[pl.BlockSpec((1, dil, tm // dil, COL_TILE), lambda b, i: (b, 0, i, 0))] * 3,
        out_shape=[out] * 3,
        scratch_shapes=[] if dil == 1 else [
            pltpu.VMEM((3, COL_TILE // LANES, tm, LANES), F32),
            pltpu.VMEM((COL_TILE // LANES, ROW_STRIDE, tm // ROW_STRIDE, LANES), F32)],
        compiler_params=_params("parallel", "parallel"),
        name=f"dil_proj_d{dil}",
    )(h, w, w, w, gains, cos, sin)


def _dil_attn_kernel(q_ref, kp_ref, kc_ref, vp_ref, vc_ref, o_ref, lse_ref):
    n_qblk = q_ref.shape[1] // ATT_BLOCK
    row = lax.broadcasted_iota(jnp.int32, (ATT_BLOCK, 2 * ATT_BLOCK), 0)
    col = lax.broadcasted_iota(jnp.int32, (ATT_BLOCK, 2 * ATT_BLOCK), 1)
    band = (col >= row) & (col <= row + ATT_BLOCK)
    band_first = band & ((col >= ATT_BLOCK) | (pl.program_id(1) > 0))
    ones = jnp.ones((2 * ATT_BLOCK, DIL_HD), BF16)
    lane = lax.broadcasted_iota(jnp.int32, (ATT_BLOCK, LANES), 1)
    for sub, a in [(sub, a) for sub in range(q_ref.shape[0]) for a in range(n_qblk)]:
        rows = slice(a * ATT_BLOCK, (a + 1) * ATT_BLOCK)
        lse_all = jnp.zeros((ATT_BLOCK, LANES), F32)
        for hh in range(DIL_HEADS):
            sl = slice(hh * DIL_HD, (hh + 1) * DIL_HD)
            if a == 0:
                k_win = jnp.concatenate([kp_ref[sub, :, sl], kc_ref[sub, :ATT_BLOCK, sl]], axis=0)
                v_win = jnp.concatenate([vp_ref[sub, :, sl], vc_ref[sub, :ATT_BLOCK, sl]], axis=0)
            else:
                win = slice((a - 1) * ATT_BLOCK, (a + 1) * ATT_BLOCK)
                k_win = kc_ref[sub, win, sl]
                v_win = vc_ref[sub, win, sl]
            s = _dot_nt(q_ref[sub, rows, sl], k_win)
            s = jnp.where(band_first if a == 0 else band, s, NEG)
            m = jnp.max(s, axis=-1, keepdims=True)
            p = jnp.exp(s - m).astype(BF16)
            ol = _dot(p, jnp.concatenate([v_win, ones], axis=1))
            l = ol[:, DIL_HD:]
            o_ref[sub, rows, sl] = (ol[:, :DIL_HD] / l).astype(o_ref.dtype)
            lse_all = jnp.where(lane == hh, m + jnp.log(l), lse_all)
        lse_ref[sub, rows, :] = lse_all


def _dil_attn(q, k, v, n_sub, dil):
    sub_len = q.size // COL_TILE // n_sub
    qb = min(sub_len, ATT_SUPER)
    ns = ATT_SUPER // qb
    n_qblk = qb // ATT_BLOCK
    q3, k3, v3 = (a.reshape(n_sub, sub_len, COL_TILE) for a in (q, k, v))
    blk = (ns, qb, COL_TILE)
    cur = lambda s, i: (s, i, 0)
    prev = lambda s, i: (s, jnp.maximum(i * n_qblk - 1, 0), 0)
    prev_blk = (ns, ATT_BLOCK, COL_TILE)
    o, lse = pl.pallas_call(
        _dil_attn_kernel,
        grid=(n_sub // ns, sub_len // qb),
        in_specs=[pl.BlockSpec(blk, cur), pl.BlockSpec(prev_blk, prev), pl.BlockSpec(blk, cur),
                  pl.BlockSpec(prev_blk, prev), pl.BlockSpec(blk, cur)],
        out_specs=[pl.BlockSpec(blk, cur), pl.BlockSpec((ns, qb, LANES), cur)],
        out_shape=[jax.ShapeDtypeStruct((n_sub, sub_len, COL_TILE), BF16),
                   jax.ShapeDtypeStruct((n_sub, sub_len, LANES), F32)],
        compiler_params=_params("parallel", "arbitrary"),
        name=f"dil_attn_d{dil}",
    )(q3, k3, k3, v3, v3)
    return o, lse


def _merge_kernel(x_ref, oa_ref, o0_ref, o1_ref, o2_ref, l0_ref, l1_ref, l2_ref,
                  sz_ref, sga_ref, sgd_ref, wga_ref, wdo_ref, wo_ref, out_ref, og_ref, lg_ref):
    tm = x_ref.shape[1]
    for g, ((_, dil), o_ref, l_ref) in enumerate(zip(DIL_GROUPS, (o0_ref, o1_ref, o2_ref),
                                                     (l0_ref, l1_ref, l2_ref))):
        for r in range(dil):
            dst = pl.ds(r, tm // dil, stride=dil)
            lg_ref[g, dst, :] = l_ref[0, r]
            for hh in range(DIL_HEADS):
                og_ref[g, hh, dst, :] = o_ref[0, r, :, hh * DIL_HD:(hh + 1) * DIL_HD].astype(F32)
    n_groups = len(DIL_GROUPS)

    def combine(rows):
        lses = [lg_ref[g, rows, :] for g in range(n_groups)]
        m = jnp.maximum(jnp.maximum(lses[0], lses[1]), lses[2])
        es = [jnp.exp(l - m) for l in lses]
        den = es[0] + es[1] + es[2]
        ws = [e / den for e in es]
        heads = []
        for hh in range(DIL_HEADS):
            acc = ws[0][:, hh:hh + 1] * og_ref[0, hh, rows, :]
            for g in range(1, n_groups):
                acc = acc + ws[g][:, hh:hh + 1] * og_ref[g, hh, rows, :]
            heads.append(acc)
        return (jnp.concatenate(heads, axis=-1) * sz_ref[rows, :].astype(F32)).astype(BF16)

    def branches(rows, o_d):
        y_a = _dot(oa_ref[rows, :], wga_ref[...])
        y_d = _dot(o_d, wdo_ref[...])
        return y_a, y_d

    def gate(rows, y_a, y_d):
        return (sga_ref[rows, :].astype(F32) * y_a + sgd_ref[rows, :].astype(F32) * y_d).astype(BF16)

    def project(rows, y):
        out_ref[0, rows, :] = x_ref[0, rows, :] + _dot(y, wo_ref[...])

    for rows in _row_subs(tm):
        project(rows, gate(rows, *branches(rows, combine(rows))))


def _merge(x, o_a, o_ds, lses, gates, wga, wdo, wo):
    batch, seq, d = x.shape
    tm = MERGE_TOK_TILE
    nt = seq // tm
    tok = lambda b, i: (b * nt + i, 0)
    fixed = lambda b, i: (0, 0)
    dil_spec = lambda dil, w: pl.BlockSpec((1, dil, tm // dil, w), lambda b, i: (b, 0, i, 0))
    o_ds = [o.reshape(batch, dil, seq // dil, DIL_OUT) for o, (_, dil) in zip(o_ds, DIL_GROUPS)]
    lses = [l.reshape(batch, dil, seq // dil, LANES) for l, (_, dil) in zip(lses, DIL_GROUPS)]
    return pl.pallas_call(
        _merge_kernel,
        grid=(batch, nt),
        in_specs=[pl.BlockSpec((1, tm, d), lambda b, i: (b, i, 0)),
                  pl.BlockSpec((tm, GLA_V), tok)]
        + [dil_spec(dil, DIL_OUT) for _, dil in DIL_GROUPS]
        + [dil_spec(dil, LANES) for _, dil in DIL_GROUPS]
        + [pl.BlockSpec((tm, DIL_OUT), lambda b, i: (b * nt + i, GATE_ZD)),
           pl.BlockSpec((tm, d), lambda b, i: (b * nt + i, GATE_GA * COL_TILE // D_MODEL)),
           pl.BlockSpec((tm, d), lambda b, i: (b * nt + i, GATE_GD * COL_TILE // D_MODEL)),
           pl.BlockSpec(wga.shape, fixed),
           pl.BlockSpec(wdo.shape, fixed),
           pl.BlockSpec(wo.shape, fixed)],
        out_specs=pl.BlockSpec((1, tm, d), lambda b, i: (b, i, 0)),
        out_shape=jax.ShapeDtypeStruct(x.shape, x.dtype),
        scratch_shapes=[pltpu.VMEM((len(DIL_GROUPS), DIL_HEADS, tm, DIL_HD), F32),
                        pltpu.VMEM((len(DIL_GROUPS), tm, LANES), F32)],
        compiler_params=_params("parallel", "parallel"),
        name="merge_out",
    )(x, o_a, *o_ds, *lses, gates, gates, gates, wga, wdo, wo)


def kernel(x, positions, norm_gain, w_in, gla_w_a2, gla_b_a, gla_out_gain, dil_q_gain, dil_k_gain,
           w_gla_out, w_dil_out, w_o):
    batch, seq, d = x.shape
    t = batch * seq
    half = DIL_HD // 2
    inv_freq = ROPE_THETA ** (-jnp.arange(half, dtype=F32) / half)
    freq = jnp.concatenate([inv_freq, inv_freq]).reshape(1, DIL_HD)
    pos = positions.astype(F32).reshape(t, 1)
    for layer in range(norm_gain.shape[0]):
        w = _w_cast(jnp.swapaxes(w_in[layer], 0, 1))
        wa2 = jnp.pad(gla_w_a2[layer], ((0, LANES - GLA_RANK), (0, 0))).astype(BF16)
        ba = gla_b_a[layer].reshape(1, GLA_QK)
        g_dqk = jnp.stack([dil_q_gain[layer] * (DIL_HD ** -0.5), dil_k_gain[layer]])

        o_a, gates, h, cos, sin = _nat_gla(x.reshape(t, d), norm_gain[layer], pos, freq, w,
                                           wa2, ba, gla_out_gain[layer].reshape(1, GLA_DV), seq)
        o_ds, lses = [], []
        for g, (win, dil) in enumerate(DIL_GROUPS):
            assert win // dil == ATT_BLOCK
            q_g, k_g, v_g = _dil_proj(h, w, g, dil, g_dqk, cos, sin, batch)
            o_g, lse_g = _dil_attn(q_g, k_g, v_g, batch * dil, dil)
            o_ds.append(o_g)
            lses.append(lse_g)
        x = _merge(x, o_a, o_ds, lses, gates,
                   w_gla_out[layer].astype(BF16), w_dil_out[layer].astype(BF16),
                   w_o[layer].astype(BF16))
    return x
```

```python
import functools

import jax
import jax.numpy as jnp
from jax import lax
from jax.experimental import pallas as pl
from jax.experimental.pallas import tpu as pltpu

D_MODEL = 1024
EPS = 1e-6
ROPE_THETA = 10000.0
GLA_HEADS = 4
GLA_DK = 128
GLA_DV = 256
GLA_RANK = 16
GLA_TAU = 16.0
GLA_QK = GLA_HEADS * GLA_DK
GLA_V = GLA_HEADS * GLA_DV
DIL_GROUPS = ((128, 1), (512, 4), (2048, 16))
DIL_HEADS = 4
DIL_HD = 128
DIL_QK = len(DIL_GROUPS) * DIL_HEADS * DIL_HD
DIL_OUT = DIL_HEADS * DIL_HD
IN_SPLIT_SIZES = (GLA_QK, GLA_QK, GLA_V, GLA_V, GLA_RANK,
                  DIL_QK, DIL_QK, DIL_QK, DIL_OUT, D_MODEL, D_MODEL)

LANES = 128
BF16_ROWS = 16
ROW_STRIDE = 4
GLA_BLOCK = 128
GLA_MID = GLA_BLOCK // 2
ATT_BLOCK = 128
ATT_SUPER = 1024
TOK_TILE = 1024
ROW_SUB = 256
DIL_ROW_SUB = 256
PERMUTE_ROWS = 512
GLA_TOK_TILE = 512
MERGE_TOK_TILE = 512
COL_TILE = DIL_HEADS * DIL_HD
VMEM_LIMIT_BYTES = 56 * 1024 * 1024

F32 = jnp.float32
BF16 = jnp.bfloat16
NEG = -1e30

GATE_GA, GATE_GD, GATE_ZD = 0, 2, 4

(_QA, _KA, _VA, _RA, _ALR, _QD, _KD, _VD, _ZD, _GA, _GD) = (
    sum(IN_SPLIT_SIZES[:i]) for i in range(len(IN_SPLIT_SIZES)))


def _block_index(offset, width):
    assert offset % width == 0
    return offset // width


W_BLOCK = 1024
LO_PAD = -(-(_ALR + LANES) // W_BLOCK) * W_BLOCK
LO_QK, LO_V, LO_R = _block_index(_QA, 2 * GLA_QK), _block_index(_VA, GLA_V), _block_index(_RA, GLA_V)
LO_ALR = _block_index(_ALR, LANES)
HI_QD = _block_index(LO_PAD, COL_TILE)
HI_ZD = _block_index(LO_PAD + _ZD - _QD, DIL_OUT)
HI_GA, HI_GD = _block_index(LO_PAD + _GA - _QD, D_MODEL), _block_index(LO_PAD + _GD - _QD, D_MODEL)
assert _KA == _QA + GLA_QK and _KD - _QD == DIL_QK and _VD - _KD == DIL_QK


def _w_cols(width, index):
    return pl.BlockSpec((D_MODEL, width), lambda *_: (0, index))


def _params(*sem):
    return pltpu.CompilerParams(dimension_semantics=sem, vmem_limit_bytes=VMEM_LIMIT_BYTES)


def _dot(a, b):
    return jnp.dot(a, b, preferred_element_type=F32)


def _dot_nt(a, b):
    return lax.dot_general(a, b, (((1,), (1,)), ((), ())), preferred_element_type=F32)


def _dot_tn(a, b):
    return lax.dot_general(a, b, (((0,), (0,)), ((), ())), preferred_element_type=F32)


def _sigmoid(x):
    return 0.5 * jnp.tanh(0.5 * x) + 0.5


def _row_subs(n, sub=ROW_SUB):
    return [slice(r * sub, (r + 1) * sub) for r in range(n // sub)]


def _w_cast_kernel(wt_ref, o_ref):
    o_ref[...] = wt_ref[...].T.astype(o_ref.dtype)


def _w_cast(w_t):
    n, d = w_t.shape
    n_lo = _block_index(LO_PAD, W_BLOCK)
    n_hi = _block_index(n - _QD, W_BLOCK)
    assert _QD % BF16_ROWS == 0 and LO_PAD <= n

    def rows(j):
        start = jnp.where(j < n_lo, j * W_BLOCK, _QD + (j - n_lo) * W_BLOCK)
        return pl.multiple_of(start, BF16_ROWS), 0

    return pl.pallas_call(
        _w_cast_kernel,
        grid=(n_lo + n_hi,),
        in_specs=[pl.BlockSpec((pl.Element(W_BLOCK), pl.Element(d)), rows)],
        out_specs=pl.BlockSpec((d, W_BLOCK), lambda j: (0, j)),
        out_shape=jax.ShapeDtypeStruct((d, (n_lo + n_hi) * W_BLOCK), BF16),
        compiler_params=_params("parallel"),
        name="w_cast",
    )(w_t)


def _nat_gla_kernel(x_ref, ng_ref, pos_ref, freq_ref, wqk_ref, wv_ref, wr_ref, wga_ref, wgd_ref,
                    wzd_ref, walr_ref, wa2_ref, ba_ref, gain_ref,
                    oa_ref, gates_ref, h_ref, cos_ref, sin_ref, st_ref, *, tiles_per_seq):
    @pl.when(pl.program_id(0) % tiles_per_seq == 0)
    def _():
        st_ref[...] = jnp.zeros_like(st_ref)

    row = lax.broadcasted_iota(jnp.int32, (GLA_BLOCK, GLA_BLOCK), 0)
    col = lax.broadcasted_iota(jnp.int32, (GLA_BLOCK, GLA_BLOCK), 1)
    causal = col <= row
    tri = causal.astype(BF16)
    n_blk = x_ref.shape[0] // GLA_BLOCK
    lane = lax.broadcasted_iota(jnp.int32, (GLA_BLOCK, DIL_HD), 1)

    gate_plan = ((wga_ref, 0, "sigmoid"), (wga_ref, 1, "sigmoid"), (wgd_ref, 0, "sigmoid"),
                 (wgd_ref, 1, "sigmoid"), (wzd_ref, 0, "silu"))

    def gate_tiles(c, h, lo_j, hi_j):
        rows = slice(c * GLA_BLOCK, (c + 1) * GLA_BLOCK)
        for j in range(lo_j, hi_j):
            w_ref, wj, act = gate_plan[j]
            acc = _dot(h, w_ref[:, wj * COL_TILE:(wj + 1) * COL_TILE])
            sg = _sigmoid(acc)
            gates_ref[rows, j * COL_TILE:(j + 1) * COL_TILE] = (
                acc * sg if act == "silu" else sg).astype(gates_ref.dtype)

    def recurrence(c, qk, hi, lo, v, r):
        rows = slice(c * GLA_BLOCK, (c + 1) * GLA_BLOCK)
        b = _dot(tri, hi) + _dot(tri, lo)
        b_mid = b[GLA_MID - 1:GLA_MID]
        b_last = b[GLA_BLOCK - 1:GLA_BLOCK]
        q = qk[:, :GLA_QK] * (GLA_DK ** -0.5)
        k = qk[:, GLA_QK:]
        q_in = (q * jnp.exp(b)).astype(BF16)
        q_mid = (q * jnp.exp(b - b_mid)).astype(BF16)
        k_mid = (k * jnp.exp(b_mid - b)).astype(BF16)
        k_end = (k * jnp.exp(b_last - b)).astype(BF16)
        dec = jnp.exp(b_last)
        yield
        heads = [(slice(hh * GLA_DK, (hh + 1) * GLA_DK), slice(hh * GLA_DV, (hh + 1) * GLA_DV))
                 for hh in range(GLA_HEADS)]
        attn = [_dot_nt(q_mid[:, ks], k_mid[:, ks]) for ks, _ in heads]
        kv_t = [_dot_tn(v[:, vs], k_end[:, ks]) for ks, vs in heads]
        yield
        outs = []
        for hh, (ks, vs) in enumerate(heads):
            st = st_ref[hh]
            a = jnp.where(causal, attn[hh], 0.0).astype(BF16)
            outs.append(_dot(a, v[:, vs]) + _dot_nt(q_in[:, ks], st.astype(BF16)))
            st_ref[hh] = st * dec[:, ks] + kv_t[hh]
        yield
        for (_, vs), o in zip(heads, outs):
            ms = jnp.mean(o * o, axis=-1, keepdims=True)
            o = o * lax.rsqrt(ms + EPS) * gain_ref[...] * r[:, vs]
            oa_ref[rows, vs] = o.astype(oa_ref.dtype)
        yield

    def advance(gen):
        if gen is not None:
            next(gen)

    n_gate = len(gate_plan)
    gen = None
    for c in range(n_blk):
        rows = slice(c * GLA_BLOCK, (c + 1) * GLA_BLOCK)
        x = x_ref[rows, :]
        ms = jnp.mean(x * x, axis=-1, keepdims=True)
        h = (x * lax.rsqrt(ms + EPS) * ng_ref[...]).astype(BF16)
        h_ref[rows, :] = h
        alr = _dot(h, walr_ref[...]).astype(BF16)
        v = _dot(h, wv_ref[...]).astype(BF16)
        advance(gen)
        ang = pos_ref[rows, :] * freq_ref[...]
        cos_ref[rows, :] = jnp.cos(ang)
        sin_ref[rows, :] = jnp.where(lane < DIL_HD // 2, -jnp.sin(ang), jnp.sin(ang))
        z = _dot(alr, wa2_ref[...]) + ba_ref[...]
        la = (jnp.minimum(z, 0.0) - jnp.log1p(jnp.exp(-jnp.abs(z)))) * (1.0 / GLA_TAU)
        hi = la.astype(BF16)
        lo = (la - hi.astype(F32)).astype(BF16)
        r = _dot(h, wr_ref[...])
        r = r * _sigmoid(r)
        advance(gen)
        gate_tiles(c, h, 0, 2)
        advance(gen)
        qk = _dot(h, wqk_ref[...])
        advance(gen)
        gate_tiles(c, h, 2, n_gate)
        gen = recurrence(c, qk, hi, lo, v, r)
    for _ in gen:
        pass


def _nat_gla(x2, norm_gain, pos, freq, w, wa2, ba, gain, seq):
    t, d = x2.shape
    tm = GLA_TOK_TILE
    tok = lambda i: (i, 0)
    fixed = lambda i: (0, 0)
    small = (wa2, ba, gain)
    n_gates = 2 * D_MODEL + DIL_OUT
    table = jax.ShapeDtypeStruct((t, DIL_HD), F32)
    return pl.pallas_call(
        functools.partial(_nat_gla_kernel, tiles_per_seq=seq // tm),
        grid=(t // tm,),
        in_specs=[pl.BlockSpec((tm, d), tok), pl.BlockSpec((1, d), fixed),
                  pl.BlockSpec((tm, 1), tok), pl.BlockSpec((1, DIL_HD), fixed),
                  _w_cols(2 * GLA_QK, LO_QK), _w_cols(GLA_V, LO_V), _w_cols(GLA_V, LO_R),
                  _w_cols(D_MODEL, HI_GA), _w_cols(D_MODEL, HI_GD), _w_cols(DIL_OUT, HI_ZD),
                  _w_cols(LANES, LO_ALR)]
        + [pl.BlockSpec(w.shape, fixed) for w in small],
        out_specs=[pl.BlockSpec((tm, GLA_V), tok), pl.BlockSpec((tm, n_gates), tok),
                   pl.BlockSpec((tm, d), tok), pl.BlockSpec((tm, DIL_HD), tok),
                   pl.BlockSpec((tm, DIL_HD), tok)],
        out_shape=[jax.ShapeDtypeStruct((t, GLA_V), BF16), jax.ShapeDtypeStruct((t, n_gates), BF16),
                   jax.ShapeDtypeStruct((t, d), BF16), table, table],
        scratch_shapes=[pltpu.VMEM((GLA_HEADS, GLA_DV, GLA_DK), F32)],
        compiler_params=_params("arbitrary"),
        name="nat_gla",
    )(x2, norm_gain.reshape(1, d), pos, freq, *([w] * 7), *small)


def _dil_proj_kernel(h_ref, wq_ref, wk_ref, wv_ref, g_ref, cos_ref, sin_ref, q_ref, k_ref, v_ref,
                     *scratch, dil):
    n_planes = COL_TILE // LANES
    tm = h_ref.shape[0]
    outs = (q_ref, k_ref, v_ref)

    def store(kind, rows, plane, y):
        if dil == 1:
            outs[kind][0, 0, rows, plane * LANES:(plane + 1) * LANES] = y.astype(q_ref.dtype)
        else:
            scratch[0][kind, plane, rows, :] = y

    def permute(kind, part):
        s0 = min(dil, ROW_STRIDE)
        s1 = dil // s0
        t0 = part * PERMUTE_ROWS
        mid = slice(t0 // s0, (t0 + PERMUTE_ROWS) // s0)
        dst = slice(t0 // dil, (t0 + PERMUTE_ROWS) // dil)
        for plane in range(n_planes):
            cols = slice(plane * LANES, (plane + 1) * LANES)
            for r0 in range(s0):
                hop = scratch[0][kind, plane, pl.ds(t0 + r0, PERMUTE_ROWS // s0, stride=s0), :]
                if s1 == 1:
                    outs[kind][0, r0, dst, cols] = hop.astype(q_ref.dtype)
                    continue
                scratch[1][kind, plane, r0, mid, :] = hop
                for r1 in range(s1):
                    outs[kind][0, s0 * r1 + r0, dst, cols] = scratch[1][
                        kind, plane, r0, pl.ds(mid.start + r1, PERMUTE_ROWS // dil, stride=s1), :
                    ].astype(q_ref.dtype)

    def epilogue(rows, accs):
        for kind, acc in enumerate(accs):
            for hh in range(n_planes):
                xh = acc[:, hh * DIL_HD:(hh + 1) * DIL_HD]
                if kind == 2:
                    store(kind, rows, hh, xh)
                    continue
                ms = jnp.mean(xh * xh, axis=-1, keepdims=True)
                y = xh * lax.rsqrt(ms + EPS) * g_ref[kind:kind + 1, :]
                y = y * cos_ref[rows, :] + pltpu.roll(y, DIL_HD // 2, 1) * sin_ref[rows, :]
                store(kind, rows, hh, y)

    def finish(rows, accs):
        epilogue(rows, accs)
        if dil > 1 and rows.stop % PERMUTE_ROWS == 0:
            for kind in range(3):
                permute(kind, rows.stop // PERMUTE_ROWS - 1)

    pending = None
    for rows in _row_subs(tm, DIL_ROW_SUB):
        h = h_ref[rows, :]
        accs = [_dot(h, w_ref[...]) for w_ref in (wq_ref, wk_ref, wv_ref)]
        if pending is not None:
            finish(*pending)
        pending = (rows, accs)
    finish(*pending)


def _dil_proj(h, w, group, dil, gains, cos, sin, batch):
    t, d = h.shape
    seq = t // batch
    tm = TOK_TILE
    nt = seq // tm
    tok = lambda b, i: (b * nt + i, 0)
    fixed = lambda b, i: (0, 0)
    out = jax.ShapeDtypeStruct((batch, dil, seq // dil, COL_TILE), BF16)
    n_groups = len(DIL_GROUPS)
    assert DIL_HD == LANES and PERMUTE_ROWS % (dil * BF16_ROWS) == 0 and tm % PERMUTE_ROWS == 0
    return pl.pallas_call(
        functools.partial(_dil_proj_kernel, dil=dil),
        grid=(batch, nt),
        in_specs=[pl.BlockSpec((tm, d), tok)]
        + [_w_cols(COL_TILE, HI_QD + kind * n_groups + group) for kind in range(3)]
        + [pl.BlockSpec(gains.shape, fixed),
           pl.BlockSpec((tm, DIL_HD), tok),
           pl.BlockSpec((tm, DIL_HD), tok)],
        out_specs=[pl.BlockSpec((1, dil, tm // dil, COL_TILE), lambda b, i: (b, 0, i, 0))] * 3,
        out_shape=[out] * 3,
        scratch_shapes=[] if dil == 1 else [
            pltpu.VMEM((3, COL_TILE // LANES, tm, LANES), F32),
            pltpu.VMEM((3, COL_TILE // LANES, ROW_STRIDE, tm // ROW_STRIDE, LANES), F32)],
        compiler_params=_params("parallel", "parallel"),
        name=f"dil_proj_d{dil}",
    )(h, w, w, w, gains, cos, sin)


def _dil_attn_kernel(q_ref, kp_ref, kc_ref, vp_ref, vc_ref, o_ref, lse_ref):
    n_qblk = q_ref.shape[1] // ATT_BLOCK
    row = lax.broadcasted_iota(jnp.int32, (ATT_BLOCK, 2 * ATT_BLOCK), 0)
    col = lax.broadcasted_iota(jnp.int32, (ATT_BLOCK, 2 * ATT_BLOCK), 1)
    band = (col >= row) & (col <= row + ATT_BLOCK)
    band_first = band & ((col >= ATT_BLOCK) | (pl.program_id(1) > 0))
    ones = jnp.ones((2 * ATT_BLOCK, DIL_HD), BF16)
    lane = lax.broadcasted_iota(jnp.int32, (ATT_BLOCK, LANES), 1)
    for sub, a in [(sub, a) for sub in range(q_ref.shape[0]) for a in range(n_qblk)]:
        rows = slice(a * ATT_BLOCK, (a + 1) * ATT_BLOCK)
        lse_all = jnp.zeros((ATT_BLOCK, LANES), F32)
        for hh in range(DIL_HEADS):
            sl = slice(hh * DIL_HD, (hh + 1) * DIL_HD)
            if a == 0:
                k_win = jnp.concatenate([kp_ref[sub, :, sl], kc_ref[sub, :ATT_BLOCK, sl]], axis=0)
                v_win = jnp.concatenate([vp_ref[sub, :, sl], vc_ref[sub, :ATT_BLOCK, sl]], axis=0)
            else:
                win = slice((a - 1) * ATT_BLOCK, (a + 1) * ATT_BLOCK)
                k_win = kc_ref[sub, win, sl]
                v_win = vc_ref[sub, win, sl]
            s = _dot_nt(q_ref[sub, rows, sl], k_win)
            s = jnp.where(band_first if a == 0 else band, s, NEG)
            m = jnp.max(s, axis=-1, keepdims=True)
            p = jnp.exp(s - m).astype(BF16)
            ol = _dot(p, jnp.concatenate([v_win, ones], axis=1))
            l = ol[:, DIL_HD:]
            o_ref[sub, rows, sl] = (ol[:, :DIL_HD] / l).astype(o_ref.dtype)
            lse_all = jnp.where(lane == hh, m + jnp.log(l), lse_all)
        lse_ref[sub, rows, :] = lse_all


def _dil_attn(q, k, v, n_sub, dil):
    sub_len = q.size // COL_TILE // n_sub
    qb = min(sub_len, ATT_SUPER)
    ns = ATT_SUPER // qb
    n_qblk = qb // ATT_BLOCK
    q3, k3, v3 = (a.reshape(n_sub, sub_len, COL_TILE) for a in (q, k, v))
    blk = (ns, qb, COL_TILE)
    cur = lambda s, i: (s, i, 0)
    prev = lambda s, i: (s, jnp.maximum(i * n_qblk - 1, 0), 0)
    prev_blk = (ns, ATT_BLOCK, COL_TILE)
    o, lse = pl.pallas_call(
        _dil_attn_kernel,
        grid=(n_sub // ns, sub_len // qb),
        in_specs=[pl.BlockSpec(blk, cur), pl.BlockSpec(prev_blk, prev), pl.BlockSpec(blk, cur),
                  pl.BlockSpec(prev_blk, prev), pl.BlockSpec(blk, cur)],
        out_specs=[pl.BlockSpec(blk, cur), pl.BlockSpec((ns, qb, LANES), cur)],
        out_shape=[jax.ShapeDtypeStruct((n_sub, sub_len, COL_TILE), BF16),
                   jax.ShapeDtypeStruct((n_sub, sub_len, LANES), F32)],
        compiler_params=_params("parallel", "arbitrary"),
        name=f"dil_attn_d{dil}",
    )(q3, k3, k3, v3, v3)
    return o, lse


def _merge_kernel(x_ref, oa_ref, o0_ref, o1_ref, o2_ref, l0_ref, l1_ref, l2_ref,
                  sz_ref, sga_ref, sgd_ref, wga_ref, wdo_ref, wo_ref, out_ref, og_ref, lg_ref):
    tm = x_ref.shape[1]
    for g, ((_, dil), o_ref, l_ref) in enumerate(zip(DIL_GROUPS, (o0_ref, o1_ref, o2_ref),
                                                     (l0_ref, l1_ref, l2_ref))):
        for r in range(dil):
            dst = pl.ds(r, tm // dil, stride=dil)
            lg_ref[g, dst, :] = l_ref[0, r]
            for hh in range(DIL_HEADS):
                og_ref[g, hh, dst, :] = o_ref[0, r, :, hh * DIL_HD:(hh + 1) * DIL_HD].astype(F32)
    n_groups = len(DIL_GROUPS)

    def combine(rows):
        lses = [lg_ref[g, rows, :] for g in range(n_groups)]
        m = jnp.maximum(jnp.maximum(lses[0], lses[1]), lses[2])
        es = [jnp.exp(l - m) for l in lses]
        den = es[0] + es[1] + es[2]
        ws = [e / den for e in es]
        heads = []
        for hh in range(DIL_HEADS):
            acc = ws[0][:, hh:hh + 1] * og_ref[0, hh, rows, :]
            for g in range(1, n_groups):
                acc = acc + ws[g][:, hh:hh + 1] * og_ref[g, hh, rows, :]
            heads.append(acc)
        return (jnp.concatenate(heads, axis=-1) * sz_ref[rows, :].astype(F32)).astype(BF16)

    def branches(rows, o_d):
        y_a = _dot(oa_ref[rows, :], wga_ref[...])
        y_d = _dot(o_d, wdo_ref[...])
        return y_a, y_d

    def gate(rows, y_a, y_d):
        return (sga_ref[rows, :].astype(F32) * y_a + sgd_ref[rows, :].astype(F32) * y_d).astype(BF16)

    def project(rows, y):
        out_ref[0, rows, :] = x_ref[0, rows, :] + _dot(y, wo_ref[...])

    for rows in _row_subs(tm):
        project(rows, gate(rows, *branches(rows, combine(rows))))


def _merge(x, o_a, o_ds, lses, gates, wga, wdo, wo):
    batch, seq, d = x.shape
    tm = MERGE_TOK_TILE
    nt = seq // tm
    tok = lambda b, i: (b * nt + i, 0)
    fixed = lambda b, i: (0, 0)
    dil_spec = lambda dil, w: pl.BlockSpec((1, dil, tm // dil, w), lambda b, i: (b, 0, i, 0))
    o_ds = [o.reshape(batch, dil, seq // dil, DIL_OUT) for o, (_, dil) in zip(o_ds, DIL_GROUPS)]
    lses = [l.reshape(batch, dil, seq // dil, LANES) for l, (_, dil) in zip(lses, DIL_GROUPS)]
    return pl.pallas_call(
        _merge_kernel,
        grid=(batch, nt),
        in_specs=[pl.BlockSpec((1, tm, d), lambda b, i: (b, i, 0)),
                  pl.BlockSpec((tm, GLA_V), tok)]
        + [dil_spec(dil, DIL_OUT) for _, dil in DIL_GROUPS]
        + [dil_spec(dil, LANES) for _, dil in DIL_GROUPS]
        + [pl.BlockSpec((tm, DIL_OUT), lambda b, i: (b * nt + i, GATE_ZD)),
           pl.BlockSpec((tm, d), lambda b, i: (b * nt + i, GATE_GA * COL_TILE // D_MODEL)),
           pl.BlockSpec((tm, d), lambda b, i: (b * nt + i, GATE_GD * COL_TILE // D_MODEL)),
           pl.BlockSpec(wga.shape, fixed),
           pl.BlockSpec(wdo.shape, fixed),
           pl.BlockSpec(wo.shape, fixed)],
        out_specs=pl.BlockSpec((1, tm, d), lambda b, i: (b, i, 0)),
        out_shape=jax.ShapeDtypeStruct(x.shape, x.dtype),
        scratch_shapes=[pltpu.VMEM((len(DIL_GROUPS), DIL_HEADS, tm, DIL_HD), F32),
                        pltpu.VMEM((len(DIL_GROUPS), tm, LANES), F32)],
        compiler_params=_params("parallel", "parallel"),
        name="merge_out",
    )(x, o_a, *o_ds, *lses, gates, gates, gates, wga, wdo, wo)


def kernel(x, positions, norm_gain, w_in, gla_w_a2, gla_b_a, gla_out_gain, dil_q_gain, dil_k_gain,
           w_gla_out, w_dil_out, w_o):
    batch, seq, d = x.shape
    t = batch * seq
    half = DIL_HD // 2
    inv_freq = ROPE_THETA ** (-jnp.arange(half, dtype=F32) / half)
    freq = jnp.concatenate([inv_freq, inv_freq]).reshape(1, DIL_HD)
    pos = positions.astype(F32).reshape(t, 1)
    for layer in range(norm_gain.shape[0]):
        w = _w_cast(jnp.swapaxes(w_in[layer], 0, 1))
        wa2 = jnp.pad(gla_w_a2[layer], ((0, LANES - GLA_RANK), (0, 0))).astype(BF16)
        ba = gla_b_a[layer].reshape(1, GLA_QK)
        g_dqk = jnp.stack([dil_q_gain[layer] * (DIL_HD ** -0.5), dil_k_gain[layer]])

        o_a, gates, h, cos, sin = _nat_gla(x.reshape(t, d), norm_gain[layer], pos, freq, w,
                                           wa2, ba, gla_out_gain[layer].reshape(1, GLA_DV), seq)
        o_ds, lses = [], []
        for g, (win, dil) in enumerate(DIL_GROUPS):
            assert win // dil == ATT_BLOCK
            q_g, k_g, v_g = _dil_proj(h, w, g, dil, g_dqk, cos, sin, batch)
            o_g, lse_g = _dil_attn(q_g, k_g, v_g, batch * dil, dil)
            o_ds.append(o_g)
            lses.append(lse_g)
        x = _merge(x, o_a, o_ds, lses, gates,
                   w_gla_out[layer].astype(BF16), w_dil_out[layer].astype(BF16),
                   w_o[layer].astype(BF16))
    return x
```

```python
import functools

import jax
import jax.numpy as jnp
from jax import lax
from jax.experimental import pallas as pl
from jax.experimental.pallas import tpu as pltpu

D_MODEL = 1024
EPS = 1e-6
ROPE_THETA = 10000.0
GLA_HEADS = 4
GLA_DK = 128
GLA_DV = 256
GLA_RANK = 16
GLA_TAU = 16.0
GLA_QK = GLA_HEADS * GLA_DK
GLA_V = GLA_HEADS * GLA_DV
DIL_GROUPS = ((128, 1), (512, 4), (2048, 16))
DIL_HEADS = 4
DIL_HD = 128
DIL_QK = len(DIL_GROUPS) * DIL_HEADS * DIL_HD
DIL_OUT = DIL_HEADS * DIL_HD
IN_SPLIT_SIZES = (GLA_QK, GLA_QK, GLA_V, GLA_V, GLA_RANK,
                  DIL_QK, DIL_QK, DIL_QK, DIL_OUT, D_MODEL, D_MODEL)

LANES = 128
BF16_ROWS = 16
ROW_STRIDE = 4
GLA_BLOCK = 128
GLA_MID = GLA_BLOCK // 2
ATT_BLOCK = 128
ATT_SUPER = 1024
TOK_TILE = 1024
ROW_SUB = 256
DIL_ROW_SUB = 256
PERMUTE_ROWS = 512
GLA_TOK_TILE = 512
MERGE_TOK_TILE = 512
COL_TILE = DIL_HEADS * DIL_HD
VMEM_LIMIT_BYTES = 56 * 1024 * 1024

F32 = jnp.float32
BF16 = jnp.bfloat16
NEG = -1e30

GATE_GA, GATE_GD, GATE_ZD = 0, 2, 4

(_QA, _KA, _VA, _RA, _ALR, _QD, _KD, _VD, _ZD, _GA, _GD) = (
    sum(IN_SPLIT_SIZES[:i]) for i in range(len(IN_SPLIT_SIZES)))


def _block_index(offset, width):
    assert offset % width == 0
    return offset // width


W_BLOCK = 1024
LO_PAD = -(-(_ALR + LANES) // W_BLOCK) * W_BLOCK
LO_QK, LO_V, LO_R = _block_index(_QA, 2 * GLA_QK), _block_index(_VA, GLA_V), _block_index(_RA, GLA_V)
LO_ALR = _block_index(_ALR, LANES)
HI_QD = _block_index(LO_PAD, COL_TILE)
HI_ZD = _block_index(LO_PAD + _ZD - _QD, DIL_OUT)
HI_GA, HI_GD = _block_index(LO_PAD + _GA - _QD, D_MODEL), _block_index(LO_PAD + _GD - _QD, D_MODEL)
assert _KA == _QA + GLA_QK and _KD - _QD == DIL_QK and _VD - _KD == DIL_QK


def _w_cols(width, index):
    return pl.BlockSpec((D_MODEL, width), lambda *_: (0, index))


def _params(*sem):
    return pltpu.CompilerParams(dimension_semantics=sem, vmem_limit_bytes=VMEM_LIMIT_BYTES)


def _dot(a, b):
    return jnp.dot(a, b, preferred_element_type=F32)


def _dot_nt(a, b):
    return lax.dot_general(a, b, (((1,), (1,)), ((), ())), preferred_element_type=F32)


def _dot_tn(a, b):
    return lax.dot_general(a, b, (((0,), (0,)), ((), ())), preferred_element_type=F32)


def _sigmoid_of_twice(half_x):
    return 0.5 * jnp.tanh(half_x) + 0.5


def _silu_of_twice(half_x):
    return half_x * jnp.tanh(half_x) + half_x


def _row_subs(n, sub=ROW_SUB):
    return [slice(r * sub, (r + 1) * sub) for r in range(n // sub)]


def _w_cast_kernel(wt_ref, o_ref, *, n_lo):
    j = pl.program_id(0)
    start = jnp.where(j < n_lo, j * W_BLOCK, _QD + (j - n_lo) * W_BLOCK)
    col = start + lax.broadcasted_iota(jnp.int32, (W_BLOCK, 1), 0)
    is_gate = ((col >= _RA) & (col < _ALR)) | (col >= _ZD)
    o_ref[...] = (wt_ref[...] * jnp.where(is_gate, 0.5, 1.0)).T.astype(o_ref.dtype)


def _w_cast(w_t):
    n, d = w_t.shape
    n_lo = _block_index(LO_PAD, W_BLOCK)
    n_hi = _block_index(n - _QD, W_BLOCK)
    assert _QD % BF16_ROWS == 0 and LO_PAD <= n

    def rows(j):
        start = jnp.where(j < n_lo, j * W_BLOCK, _QD + (j - n_lo) * W_BLOCK)
        return pl.multiple_of(start, BF16_ROWS), 0

    return pl.pallas_call(
        functools.partial(_w_cast_kernel, n_lo=n_lo),
        grid=(n_lo + n_hi,),
        in_specs=[pl.BlockSpec((pl.Element(W_BLOCK), pl.Element(d)), rows)],
        out_specs=pl.BlockSpec((d, W_BLOCK), lambda j: (0, j)),
        out_shape=jax.ShapeDtypeStruct((d, (n_lo + n_hi) * W_BLOCK), BF16),
        compiler_params=_params("parallel"),
        name="w_cast",
    )(w_t)


def _nat_gla_kernel(x_ref, ng_ref, pos_ref, freq_ref, wqk_ref, wv_ref, wr_ref, wga_ref, wgd_ref,
                    wzd_ref, walr_ref, wa2_ref, ba_ref, gain_ref,
                    oa_ref, gates_ref, h_ref, cos_ref, sin_ref, st_ref, *, tiles_per_seq):
    @pl.when(pl.program_id(0) % tiles_per_seq == 0)
    def _():
        st_ref[...] = jnp.zeros_like(st_ref)

    row = lax.broadcasted_iota(jnp.int32, (GLA_BLOCK, GLA_BLOCK), 0)
    col = lax.broadcasted_iota(jnp.int32, (GLA_BLOCK, GLA_BLOCK), 1)
    causal = col <= row
    tri = causal.astype(BF16)
    n_blk = x_ref.shape[0] // GLA_BLOCK
    low = lax.broadcasted_iota(jnp.int32, (GLA_BLOCK // 2, DIL_HD), 1) < DIL_HD // 2

    gate_plan = ((wga_ref, 0, "sigmoid"), (wga_ref, 1, "sigmoid"), (wgd_ref, 0, "sigmoid"),
                 (wgd_ref, 1, "sigmoid"), (wzd_ref, 0, "silu"))

    def gate_tiles(c, h, lo_j, hi_j):
        rows = slice(c * GLA_BLOCK, (c + 1) * GLA_BLOCK)
        for j in range(lo_j, hi_j):
            w_ref, wj, act = gate_plan[j]
            acc = _dot(h, w_ref[:, wj * COL_TILE:(wj + 1) * COL_TILE])
            act_fn = _silu_of_twice if act == "silu" else _sigmoid_of_twice
            gates_ref[rows, j * COL_TILE:(j + 1) * COL_TILE] = act_fn(acc).astype(gates_ref.dtype)

    def recurrence(c, qk, hi, lo, v, r):
        rows = slice(c * GLA_BLOCK, (c + 1) * GLA_BLOCK)
        b = _dot(tri, hi) + _dot(tri, lo)
        b_mid = b[GLA_MID - 1:GLA_MID]
        b_last = b[GLA_BLOCK - 1:GLA_BLOCK]
        q = qk[:, :GLA_QK] * (GLA_DK ** -0.5)
        k = qk[:, GLA_QK:]
        q_in = (q * jnp.exp(b)).astype(BF16)
        q_mid = (q * jnp.exp(b - b_mid)).astype(BF16)
        k_mid = (k * jnp.exp(b_mid - b)).astype(BF16)
        k_end = (k * jnp.exp(b_last - b)).astype(BF16)
        dec = jnp.exp(b_last)
        yield
        heads = [(slice(hh * GLA_DK, (hh + 1) * GLA_DK), slice(hh * GLA_DV, (hh + 1) * GLA_DV))
                 for hh in range(GLA_HEADS)]
        attn = [_dot_nt(q_mid[:, ks], k_mid[:, ks]) for ks, _ in heads]
        kv_t = [_dot_tn(v[:, vs], k_end[:, ks]) for ks, vs in heads]
        yield
        outs = []
        for hh, (ks, vs) in enumerate(heads):
            st = st_ref[hh]
            a = jnp.where(causal, attn[hh], 0.0).astype(BF16)
            outs.append(_dot(a, v[:, vs]) + _dot_nt(q_in[:, ks], st.astype(BF16)))
            st_ref[hh] = st * dec[:, ks] + kv_t[hh]
        yield
        for (_, vs), o in zip(heads, outs):
            ms = jnp.mean(o * o, axis=-1, keepdims=True)
            o = o * lax.rsqrt(ms + EPS) * gain_ref[...] * r[:, vs]
            oa_ref[rows, vs] = o.astype(oa_ref.dtype)
        yield

    def advance(gen):
        if gen is not None:
            next(gen)

    n_gate = len(gate_plan)
    gen = None
    for c in range(n_blk):
        rows = slice(c * GLA_BLOCK, (c + 1) * GLA_BLOCK)
        x = x_ref[rows, :]
        ms = jnp.mean(x * x, axis=-1, keepdims=True)
        h = (x * lax.rsqrt(ms + EPS) * ng_ref[...]).astype(BF16)
        h_ref[rows, :] = h
        alr = _dot(h, walr_ref[...]).astype(BF16)
        v = _dot(h, wv_ref[...]).astype(BF16)
        advance(gen)
        half = GLA_BLOCK // 2
        top = slice(c * GLA_BLOCK, c * GLA_BLOCK + half)
        bot = slice(c * GLA_BLOCK + half, (c + 1) * GLA_BLOCK)
        ang = jnp.where(low, pos_ref[top, :], pos_ref[bot, :]) * freq_ref[...]
        cos, sin = jnp.cos(ang), jnp.sin(ang)
        cos_x, sin_x = pltpu.roll(cos, DIL_HD // 2, 1), pltpu.roll(sin, DIL_HD // 2, 1)
        cos_ref[top, :] = jnp.where(low, cos, cos_x)
        cos_ref[bot, :] = jnp.where(low, cos_x, cos)
        sin_ref[top, :] = jnp.where(low, -sin, sin_x)
        sin_ref[bot, :] = jnp.where(low, -sin_x, sin)
        z = _dot(alr, wa2_ref[...]) + ba_ref[...]
        la = (jnp.minimum(z, 0.0) - jnp.log1p(jnp.exp(-jnp.abs(z)))) * (1.0 / GLA_TAU)
        hi = la.astype(BF16)
        lo = (la - hi.astype(F32)).astype(BF16)
        r = _silu_of_twice(_dot(h, wr_ref[...]))
        advance(gen)
        gate_tiles(c, h, 0, 2)
        advance(gen)
        qk = _dot(h, wqk_ref[...])
        advance(gen)
        gate_tiles(c, h, 2, n_gate)
        gen = recurrence(c, qk, hi, lo, v, r)
    for _ in gen:
        pass


def _nat_gla(x2, norm_gain, pos, freq, w, wa2, ba, gain, seq):
    t, d = x2.shape
    tm = GLA_TOK_TILE
    tok = lambda i: (i, 0)
    fixed = lambda i: (0, 0)
    small = (wa2, ba, gain)
    n_gates = 2 * D_MODEL + DIL_OUT
    table = jax.ShapeDtypeStruct((t, DIL_HD), F32)
    return pl.pallas_call(
        functools.partial(_nat_gla_kernel, tiles_per_seq=seq // tm),
        grid=(t // tm,),
        in_specs=[pl.BlockSpec((tm, d), tok), pl.BlockSpec((1, d), fixed),
                  pl.BlockSpec((tm, 1), tok), pl.BlockSpec((1, DIL_HD), fixed),
                  _w_cols(2 * GLA_QK, LO_QK), _w_cols(GLA_V, LO_V), _w_cols(GLA_V, LO_R),
                  _w_cols(D_MODEL, HI_GA), _w_cols(D_MODEL, HI_GD), _w_cols(DIL_OUT, HI_ZD),
                  _w_cols(LANES, LO_ALR)]
        + [pl.BlockSpec(w.shape, fixed) for w in small],
        out_specs=[pl.BlockSpec((tm, GLA_V), tok), pl.BlockSpec((tm, n_gates), tok),
                   pl.BlockSpec((tm, d), tok), pl.BlockSpec((tm, DIL_HD), tok),
                   pl.BlockSpec((tm, DIL_HD), tok)],
        out_shape=[jax.ShapeDtypeStruct((t, GLA_V), BF16), jax.ShapeDtypeStruct((t, n_gates), BF16),
                   jax.ShapeDtypeStruct((t, d), BF16), table, table],
        scratch_shapes=[pltpu.VMEM((GLA_HEADS, GLA_DV, GLA_DK), F32)],
        compiler_params=_params("arbitrary"),
        name="nat_gla",
    )(x2, norm_gain.reshape(1, d), pos, freq, *([w] * 7), *small)


def _dil_proj_kernel(h_ref, wq_ref, wk_ref, wv_ref, g_ref, cos_ref, sin_ref, q_ref, k_ref, v_ref,
                     *scratch, dil):
    n_planes = COL_TILE // LANES
    tm = h_ref.shape[0]
    outs = (q_ref, k_ref, v_ref)

    def store(kind, rows, plane, y):
        if dil == 1:
            outs[kind][0, 0, rows, plane * LANES:(plane + 1) * LANES] = y.astype(q_ref.dtype)
        else:
            scratch[0][kind, plane, rows, :] = y

    def permute(kind, part):
        s0 = min(dil, ROW_STRIDE)
        s1 = dil // s0
        t0 = part * PERMUTE_ROWS
        mid = slice(t0 // s0, (t0 + PERMUTE_ROWS) // s0)
        dst = slice(t0 // dil, (t0 + PERMUTE_ROWS) // dil)
        for plane in range(n_planes):
            cols = slice(plane * LANES, (plane + 1) * LANES)
            for r0 in range(s0):
                hop = scratch[0][kind, plane, pl.ds(t0 + r0, PERMUTE_ROWS // s0, stride=s0), :]
                if s1 == 1:
                    outs[kind][0, r0, dst, cols] = hop.astype(q_ref.dtype)
                    continue
                scratch[1][kind, plane, r0, mid, :] = hop
                for r1 in range(s1):
                    outs[kind][0, s0 * r1 + r0, dst, cols] = scratch[1][
                        kind, plane, r0, pl.ds(mid.start + r1, PERMUTE_ROWS // dil, stride=s1), :
                    ].astype(q_ref.dtype)

    def epilogue(rows, accs):
        for kind, acc in enumerate(accs):
            for hh in range(n_planes):
                xh = acc[:, hh * DIL_HD:(hh + 1) * DIL_HD]
                if kind == 2:
                    store(kind, rows, hh, xh)
                    continue
                ms = jnp.mean(xh * xh, axis=-1, keepdims=True)
                y = xh * lax.rsqrt(ms + EPS) * g_ref[kind:kind + 1, :]
                y = y * cos_ref[rows, :] + pltpu.roll(y, DIL_HD // 2, 1) * sin_ref[rows, :]
                store(kind, rows, hh, y)

    def finish(rows, accs):
        epilogue(rows, accs)
        if dil > 1 and rows.stop % PERMUTE_ROWS == 0:
            for kind in range(3):
                permute(kind, rows.stop // PERMUTE_ROWS - 1)

    pending = None
    for rows in _row_subs(tm, DIL_ROW_SUB):
        h = h_ref[rows, :]
        accs = [_dot(h, w_ref[...]) for w_ref in (wq_ref, wk_ref, wv_ref)]
        if pending is not None:
            finish(*pending)
        pending = (rows, accs)
    finish(*pending)


def _dil_proj(h, w, group, dil, gains, cos, sin, batch):
    t, d = h.shape
    seq = t // batch
    tm = TOK_TILE
    nt = seq // tm
    tok = lambda b, i: (b * nt + i, 0)
    fixed = lambda b, i: (0, 0)
    out = jax.ShapeDtypeStruct((batch, dil, seq // dil, COL_TILE), BF16)
    n_groups = len(DIL_GROUPS)
    assert DIL_HD == LANES and PERMUTE_ROWS % (dil * BF16_ROWS) == 0 and tm % PERMUTE_ROWS == 0
    return pl.pallas_call(
        functools.partial(_dil_proj_kernel, dil=dil),
        grid=(batch, nt),
        in_specs=[pl.BlockSpec((tm, d), tok)]
        + [_w_cols(COL_TILE, HI_QD + kind * n_groups + group) for kind in range(3)]
        + [pl.BlockSpec(gains.shape, fixed),
           pl.BlockSpec((tm, DIL_HD), tok),
           pl.BlockSpec((tm, DIL_HD), tok)],
        out_specs=[pl.BlockSpec((1, dil, tm // dil, COL_TILE), lambda b, i: (b, 0, i, 0))] * 3,
        out_shape=[out] * 3,
        scratch_shapes=[] if dil == 1 else [
            pltpu.VMEM((3, COL_TILE // LANES, tm, LANES), F32),
            pltpu.VMEM((3, COL_TILE // LANES, ROW_STRIDE, tm // ROW_STRIDE, LANES), F32)],
        compiler_params=_params("parallel", "parallel"),
        name=f"dil_proj_d{dil}",
    )(h, w, w, w, gains, cos, sin)


def _dil_attn_kernel(q_ref, kp_ref, kc_ref, vp_ref, vc_ref, o_ref, lse_ref):
    n_qblk = q_ref.shape[1] // ATT_BLOCK
    row = lax.broadcasted_iota(jnp.int32, (ATT_BLOCK, 2 * ATT_BLOCK), 0)
    col = lax.broadcasted_iota(jnp.int32, (ATT_BLOCK, 2 * ATT_BLOCK), 1)
    band = (col >= row) & (col <= row + ATT_BLOCK)
    band_first = band & ((col >= ATT_BLOCK) | (pl.program_id(1) > 0))
    ones = jnp.ones((2 * ATT_BLOCK, DIL_HD), BF16)
    lane = lax.broadcasted_iota(jnp.int32, (ATT_BLOCK, LANES), 1)
    for sub, a in [(sub, a) for sub in range(q_ref.shape[0]) for a in range(n_qblk)]:
        rows = slice(a * ATT_BLOCK, (a + 1) * ATT_BLOCK)
        lse_all = jnp.zeros((ATT_BLOCK, LANES), F32)
        for hh in range(DIL_HEADS):
            sl = slice(hh * DIL_HD, (hh + 1) * DIL_HD)
            if a == 0:
                k_win = jnp.concatenate([kp_ref[sub, :, sl], kc_ref[sub, :ATT_BLOCK, sl]], axis=0)
                v_win = jnp.concatenate([vp_ref[sub, :, sl], vc_ref[sub, :ATT_BLOCK, sl]], axis=0)
            else:
                win = slice((a - 1) * ATT_BLOCK, (a + 1) * ATT_BLOCK)
                k_win = kc_ref[sub, win, sl]
                v_win = vc_ref[sub, win, sl]
            s = _dot_nt(q_ref[sub, rows, sl], k_win)
            s = jnp.where(band_first if a == 0 else band, s, NEG)
            m = jnp.max(s, axis=-1, keepdims=True)
            p = jnp.exp(s - m).astype(BF16)
            ol = _dot(p, jnp.concatenate([v_win, ones], axis=1))
            l = ol[:, DIL_HD:]
            o_ref[sub, rows, sl] = (ol[:, :DIL_HD] / l).astype(o_ref.dtype)
            lse_all = jnp.where(lane == hh, m + jnp.log(l), lse_all)
        lse_ref[sub, rows, :] = lse_all


def _dil_attn(q, k, v, n_sub, dil):
    sub_len = q.size // COL_TILE // n_sub
    qb = min(sub_len, ATT_SUPER)
    ns = ATT_SUPER // qb
    n_qblk = qb // ATT_BLOCK
    q3, k3, v3 = (a.reshape(n_sub, sub_len, COL_TILE) for a in (q, k, v))
    blk = (ns, qb, COL_TILE)
    cur = lambda s, i: (s, i, 0)
    prev = lambda s, i: (s, jnp.maximum(i * n_qblk - 1, 0), 0)
    prev_blk = (ns, ATT_BLOCK, COL_TILE)
    o, lse = pl.pallas_call(
        _dil_attn_kernel,
        grid=(n_sub // ns, sub_len // qb),
        in_specs=[pl.BlockSpec(blk, cur), pl.BlockSpec(prev_blk, prev), pl.BlockSpec(blk, cur),
                  pl.BlockSpec(prev_blk, prev), pl.BlockSpec(blk, cur)],
        out_specs=[pl.BlockSpec(blk, cur), pl.BlockSpec((ns, qb, LANES), cur)],
        out_shape=[jax.ShapeDtypeStruct((n_sub, sub_len, COL_TILE), BF16),
                   jax.ShapeDtypeStruct((n_sub, sub_len, LANES), F32)],
        compiler_params=_params("parallel", "arbitrary"),
        name=f"dil_attn_d{dil}",
    )(q3, k3, k3, v3, v3)
    return o, lse


def _merge_kernel(x_ref, oa_ref, o0_ref, o1_ref, o2_ref, l0_ref, l1_ref, l2_ref,
                  sz_ref, sga_ref, sgd_ref, wga_ref, wdo_ref, wo_ref, out_ref, og_ref, lg_ref):
    tm = x_ref.shape[1]
    for g, ((_, dil), o_ref, l_ref) in enumerate(zip(DIL_GROUPS, (o0_ref, o1_ref, o2_ref),
                                                     (l0_ref, l1_ref, l2_ref))):
        for r in range(dil):
            dst = pl.ds(r, tm // dil, stride=dil)
            lg_ref[g, dst, :] = l_ref[0, r]
            for hh in range(DIL_HEADS):
                og_ref[g, hh, dst, :] = o_ref[0, r, :, hh * DIL_HD:(hh + 1) * DIL_HD].astype(F32)
    n_groups = len(DIL_GROUPS)

    def combine(rows):
        lses = [lg_ref[g, rows, :] for g in range(n_groups)]
        m = jnp.maximum(jnp.maximum(lses[0], lses[1]), lses[2])
        es = [jnp.exp(l - m) for l in lses]
        den = es[0] + es[1] + es[2]
        ws = [e / den for e in es]
        heads = []
        for hh in range(DIL_HEADS):
            acc = ws[0][:, hh:hh + 1] * og_ref[0, hh, rows, :]
            for g in range(1, n_groups):
                acc = acc + ws[g][:, hh:hh + 1] * og_ref[g, hh, rows, :]
            heads.append(acc)
        return (jnp.concatenate(heads, axis=-1) * sz_ref[rows, :].astype(F32)).astype(BF16)

    def branches(rows, o_d):
        y_a = _dot(oa_ref[rows, :], wga_ref[...])
        y_d = _dot(o_d, wdo_ref[...])
        return y_a, y_d

    def gate(rows, y_a, y_d):
        return (sga_ref[rows, :].astype(F32) * y_a + sgd_ref[rows, :].astype(F32) * y_d).astype(BF16)

    def project(rows, y):
        out_ref[0, rows, :] = x_ref[0, rows, :] + _dot(y, wo_ref[...])

    for rows in _row_subs(tm):
        project(rows, gate(rows, *branches(rows, combine(rows))))


def _merge(x, o_a, o_ds, lses, gates, wga, wdo, wo):
    batch, seq, d = x.shape
    tm = MERGE_TOK_TILE
    nt = seq // tm
    tok = lambda b, i: (b * nt + i, 0)
    fixed = lambda b, i: (0, 0)
    dil_spec = lambda dil, w: pl.BlockSpec((1, dil, tm // dil, w), lambda b, i: (b, 0, i, 0))
    o_ds = [o.reshape(batch, dil, seq // dil, DIL_OUT) for o, (_, dil) in zip(o_ds, DIL_GROUPS)]
    lses = [l.reshape(batch, dil, seq // dil, LANES) for l, (_, dil) in zip(lses, DIL_GROUPS)]
    return pl.pallas_call(
        _merge_kernel,
        grid=(batch, nt),
        in_specs=[pl.BlockSpec((1, tm, d), lambda b, i: (b, i, 0)),
                  pl.BlockSpec((tm, GLA_V), tok)]
        + [dil_spec(dil, DIL_OUT) for _, dil in DIL_GROUPS]
        + [dil_spec(dil, LANES) for _, dil in DIL_GROUPS]
        + [pl.BlockSpec((tm, DIL_OUT), lambda b, i: (b * nt + i, GATE_ZD)),
           pl.BlockSpec((tm, d), lambda b, i: (b * nt + i, GATE_GA * COL_TILE // D_MODEL)),
           pl.BlockSpec((tm, d), lambda b, i: (b * nt + i, GATE_GD * COL_TILE // D_MODEL)),
           pl.BlockSpec(wga.shape, fixed),
           pl.BlockSpec(wdo.shape, fixed),
           pl.BlockSpec(wo.shape, fixed)],
        out_specs=pl.BlockSpec((1, tm, d), lambda b, i: (b, i, 0)),
        out_shape=jax.ShapeDtypeStruct(x.shape, x.dtype),
        scratch_shapes=[pltpu.VMEM((len(DIL_GROUPS), DIL_HEADS, tm, DIL_HD), F32),
                        pltpu.VMEM((len(DIL_GROUPS), tm, LANES), F32)],
        compiler_params=_params("parallel", "parallel"),
        name="merge_out",
    )(x, o_a, *o_ds, *lses, gates, gates, gates, wga, wdo, wo)


def kernel(x, positions, norm_gain, w_in, gla_w_a2, gla_b_a, gla_out_gain, dil_q_gain, dil_k_gain,
           w_gla_out, w_dil_out, w_o):
    batch, seq, d = x.shape
    t = batch * seq
    half = DIL_HD // 2
    inv_freq = ROPE_THETA ** (-jnp.arange(half, dtype=F32) / half)
    freq = jnp.concatenate([inv_freq, inv_freq]).reshape(1, DIL_HD)
    pos = positions.astype(F32).reshape(t, 1)
    for layer in range(norm_gain.shape[0]):
        w = _w_cast(jnp.swapaxes(w_in[layer], 0, 1))
        wa2 = jnp.pad(gla_w_a2[layer], ((0, LANES - GLA_RANK), (0, 0))).astype(BF16)
        ba = gla_b_a[layer].reshape(1, GLA_QK)
        g_dqk = jnp.stack([dil_q_gain[layer] * (DIL_HD ** -0.5), dil_k_gain[layer]])

        o_a, gates, h, cos, sin = _nat_gla(x.reshape(t, d), norm_gain[layer], pos, freq, w,
                                           wa2, ba, gla_out_gain[layer].reshape(1, GLA_DV), seq)
        o_ds, lses = [], []
        for g, (win, dil) in enumerate(DIL_GROUPS):
            assert win // dil == ATT_BLOCK
            q_g, k_g, v_g = _dil_proj(h, w, g, dil, g_dqk, cos, sin, batch)
            o_g, lse_g = _dil_attn(q_g, k_g, v_g, batch * dil, dil)
            o_ds.append(o_g)
            lses.append(lse_g)
        x = _merge(x, o_a, o_ds, lses, gates,
                   w_gla_out[layer].astype(BF16), w_dil_out[layer].astype(BF16),
                   w_o[layer].astype(BF16))
    return x
```

```python
import functools

import jax
import jax.numpy as jnp
from jax import lax
from jax.experimental import pallas as pl
from jax.experimental.pallas import tpu as pltpu

D_MODEL = 1024
EPS = 1e-6
ROPE_THETA = 10000.0
GLA_HEADS = 4
GLA_DK = 128
GLA_DV = 256
GLA_RANK = 16
GLA_TAU = 16.0
GLA_QK = GLA_HEADS * GLA_DK
GLA_V = GLA_HEADS * GLA_DV
DIL_GROUPS = ((128, 1), (512, 4), (2048, 16))
DIL_HEADS = 4
DIL_HD = 128
DIL_QK = len(DIL_GROUPS) * DIL_HEADS * DIL_HD
DIL_OUT = DIL_HEADS * DIL_HD
IN_SPLIT_SIZES = (GLA_QK, GLA_QK, GLA_V, GLA_V, GLA_RANK,
                  DIL_QK, DIL_QK, DIL_QK, DIL_OUT, D_MODEL, D_MODEL)

LANES = 128
BF16_ROWS = 16
ROW_STRIDE = 4
GLA_BLOCK = 128
GLA_MID = GLA_BLOCK // 2
ATT_BLOCK = 128
ATT_SUPER = 1024
TOK_TILE = 1024
ROW_SUB = 256
DIL_ROW_SUB = 256
PERMUTE_ROWS = 512
GLA_TOK_TILE = 512
MERGE_TOK_TILE = 512
COL_TILE = DIL_HEADS * DIL_HD
VMEM_LIMIT_BYTES = 56 * 1024 * 1024

F32 = jnp.float32
BF16 = jnp.bfloat16
NEG = -1e30
LOG2_E = 1.4426950408889634

GATE_GA, GATE_GD, GATE_ZD = 0, 2, 4

(_QA, _KA, _VA, _RA, _ALR, _QD, _KD, _VD, _ZD, _GA, _GD) = (
    sum(IN_SPLIT_SIZES[:i]) for i in range(len(IN_SPLIT_SIZES)))


def _block_index(offset, width):
    assert offset % width == 0
    return offset // width


W_BLOCK = 1024
LO_PAD = -(-(_ALR + LANES) // W_BLOCK) * W_BLOCK
LO_QK, LO_V, LO_R = _block_index(_QA, 2 * GLA_QK), _block_index(_VA, GLA_V), _block_index(_RA, GLA_V)
LO_ALR = _block_index(_ALR, LANES)
HI_QD = _block_index(LO_PAD, COL_TILE)
HI_ZD = _block_index(LO_PAD + _ZD - _QD, DIL_OUT)
HI_GA, HI_GD = _block_index(LO_PAD + _GA - _QD, D_MODEL), _block_index(LO_PAD + _GD - _QD, D_MODEL)
assert _KA == _QA + GLA_QK and _KD - _QD == DIL_QK and _VD - _KD == DIL_QK


def _w_cols(width, index):
    return pl.BlockSpec((D_MODEL, width), lambda *_: (0, index))


def _params(*sem):
    return pltpu.CompilerParams(dimension_semantics=sem, vmem_limit_bytes=VMEM_LIMIT_BYTES)


def _dot(a, b):
    return jnp.dot(a, b, preferred_element_type=F32)


def _dot_nt(a, b):
    return lax.dot_general(a, b, (((1,), (1,)), ((), ())), preferred_element_type=F32)


def _dot_tn(a, b):
    return lax.dot_general(a, b, (((0,), (0,)), ((), ())), preferred_element_type=F32)


def _sigmoid_of_twice(half_x):
    return 0.5 * jnp.tanh(half_x) + 0.5


def _silu_of_twice(half_x):
    return half_x * jnp.tanh(half_x) + half_x


def _row_subs(n, sub=ROW_SUB):
    return [slice(r * sub, (r + 1) * sub) for r in range(n // sub)]


def _w_cast_kernel(wt_ref, o_ref, *, n_lo):
    j = pl.program_id(0)
    start = jnp.where(j < n_lo, j * W_BLOCK, _QD + (j - n_lo) * W_BLOCK)
    col = start + lax.broadcasted_iota(jnp.int32, (W_BLOCK, 1), 0)
    is_gate = ((col >= _RA) & (col < _ALR)) | (col >= _ZD)
    o_ref[...] = (wt_ref[...] * jnp.where(is_gate, 0.5, 1.0)).T.astype(o_ref.dtype)


def _w_cast(w_t):
    n, d = w_t.shape
    n_lo = _block_index(LO_PAD, W_BLOCK)
    n_hi = _block_index(n - _QD, W_BLOCK)
    assert _QD % BF16_ROWS == 0 and LO_PAD <= n

    def rows(j):
        start = jnp.where(j < n_lo, j * W_BLOCK, _QD + (j - n_lo) * W_BLOCK)
        return pl.multiple_of(start, BF16_ROWS), 0

    return pl.pallas_call(
        functools.partial(_w_cast_kernel, n_lo=n_lo),
        grid=(n_lo + n_hi,),
        in_specs=[pl.BlockSpec((pl.Element(W_BLOCK), pl.Element(d)), rows)],
        out_specs=pl.BlockSpec((d, W_BLOCK), lambda j: (0, j)),
        out_shape=jax.ShapeDtypeStruct((d, (n_lo + n_hi) * W_BLOCK), BF16),
        compiler_params=_params("parallel"),
        name="w_cast",
    )(w_t)


def _nat_gla_kernel(x_ref, ng_ref, pos_ref, freq_ref, wqk_ref, wv_ref, wr_ref, wga_ref, wgd_ref,
                    wzd_ref, walr_ref, wa2_ref, ba_ref, gain_ref,
                    oa_ref, gates_ref, h_ref, cos_ref, sin_ref, st_ref, *, tiles_per_seq):
    @pl.when(pl.program_id(0) % tiles_per_seq == 0)
    def _():
        st_ref[...] = jnp.zeros_like(st_ref)

    row = lax.broadcasted_iota(jnp.int32, (GLA_BLOCK, GLA_BLOCK), 0)
    col = lax.broadcasted_iota(jnp.int32, (GLA_BLOCK, GLA_BLOCK), 1)
    causal = col <= row
    tri = causal.astype(BF16)
    n_blk = x_ref.shape[0] // GLA_BLOCK
    low = lax.broadcasted_iota(jnp.int32, (GLA_BLOCK // 2, DIL_HD), 1) < DIL_HD // 2

    gate_plan = ((wga_ref, 0, "sigmoid"), (wga_ref, 1, "sigmoid"), (wgd_ref, 0, "sigmoid"),
                 (wgd_ref, 1, "sigmoid"), (wzd_ref, 0, "silu"))

    def gate_tiles(c, h, lo_j, hi_j):
        rows = slice(c * GLA_BLOCK, (c + 1) * GLA_BLOCK)
        for j in range(lo_j, hi_j):
            w_ref, wj, act = gate_plan[j]
            acc = _dot(h, w_ref[:, wj * COL_TILE:(wj + 1) * COL_TILE])
            act_fn = _silu_of_twice if act == "silu" else _sigmoid_of_twice
            gates_ref[rows, j * COL_TILE:(j + 1) * COL_TILE] = act_fn(acc).astype(gates_ref.dtype)

    def recurrence(c, qk, hi, lo, v, r):
        rows = slice(c * GLA_BLOCK, (c + 1) * GLA_BLOCK)
        b = _dot(tri, hi) + _dot(tri, lo)
        b_mid = b[GLA_MID - 1:GLA_MID]
        b_last = b[GLA_BLOCK - 1:GLA_BLOCK]
        q = qk[:, :GLA_QK] * (GLA_DK ** -0.5)
        k = qk[:, GLA_QK:]
        q_in = (q * jnp.exp(b)).astype(BF16)
        q_mid = (q * jnp.exp(b - b_mid)).astype(BF16)
        k_mid = (k * jnp.exp(b_mid - b)).astype(BF16)
        k_end = (k * jnp.exp(b_last - b)).astype(BF16)
        dec = jnp.exp(b_last)
        yield
        heads = [(slice(hh * GLA_DK, (hh + 1) * GLA_DK), slice(hh * GLA_DV, (hh + 1) * GLA_DV))
                 for hh in range(GLA_HEADS)]
        attn = [_dot_nt(q_mid[:, ks], k_mid[:, ks]) for ks, _ in heads]
        kv_t = [_dot_tn(v[:, vs], k_end[:, ks]) for ks, vs in heads]
        yield
        outs = []
        for hh, (ks, vs) in enumerate(heads):
            st = st_ref[hh]
            a = jnp.where(causal, attn[hh], 0.0).astype(BF16)
            outs.append(_dot(a, v[:, vs]) + _dot_nt(q_in[:, ks], st.astype(BF16)))
            st_ref[hh] = st * dec[:, ks] + kv_t[hh]
        yield
        for (_, vs), o in zip(heads, outs):
            ms = jnp.mean(o * o, axis=-1, keepdims=True)
            o = o * lax.rsqrt(ms + EPS) * gain_ref[...] * r[:, vs]
            oa_ref[rows, vs] = o.astype(oa_ref.dtype)
        yield

    def advance(gen):
        if gen is not None:
            next(gen)

    n_gate = len(gate_plan)
    gen = None
    for c in range(n_blk):
        rows = slice(c * GLA_BLOCK, (c + 1) * GLA_BLOCK)
        x = x_ref[rows, :]
        ms = jnp.mean(x * x, axis=-1, keepdims=True)
        h = (x * lax.rsqrt(ms + EPS) * ng_ref[...]).astype(BF16)
        h_ref[rows, :] = h
        alr = _dot(h, walr_ref[...]).astype(BF16)
        v = _dot(h, wv_ref[...]).astype(BF16)
        advance(gen)
        half = GLA_BLOCK // 2
        top = slice(c * GLA_BLOCK, c * GLA_BLOCK + half)
        bot = slice(c * GLA_BLOCK + half, (c + 1) * GLA_BLOCK)
        ang = jnp.where(low, pos_ref[top, :], pos_ref[bot, :]) * freq_ref[...]
        cos, sin = jnp.cos(ang), jnp.sin(ang)
        cos_x, sin_x = pltpu.roll(cos, DIL_HD // 2, 1), pltpu.roll(sin, DIL_HD // 2, 1)
        cos_ref[top, :] = jnp.where(low, cos, cos_x)
        cos_ref[bot, :] = jnp.where(low, cos_x, cos)
        sin_ref[top, :] = jnp.where(low, -sin, sin_x)
        sin_ref[bot, :] = jnp.where(low, -sin_x, sin)
        z = _dot(alr, wa2_ref[...]) + ba_ref[...]
        la = (jnp.minimum(z, 0.0) - jnp.log1p(jnp.exp(-jnp.abs(z)))) * (1.0 / GLA_TAU)
        hi = la.astype(BF16)
        lo = (la - hi.astype(F32)).astype(BF16)
        r = _silu_of_twice(_dot(h, wr_ref[...]))
        advance(gen)
        gate_tiles(c, h, 0, 2)
        advance(gen)
        qk = _dot(h, wqk_ref[...])
        advance(gen)
        gate_tiles(c, h, 2, n_gate)
        gen = recurrence(c, qk, hi, lo, v, r)
    for _ in gen:
        pass


def _nat_gla(x2, norm_gain, pos, freq, w, wa2, ba, gain, seq):
    t, d = x2.shape
    tm = GLA_TOK_TILE
    tok = lambda i: (i, 0)
    fixed = lambda i: (0, 0)
    small = (wa2, ba, gain)
    n_gates = 2 * D_MODEL + DIL_OUT
    table = jax.ShapeDtypeStruct((t, DIL_HD), F32)
    return pl.pallas_call(
        functools.partial(_nat_gla_kernel, tiles_per_seq=seq // tm),
        grid=(t // tm,),
        in_specs=[pl.BlockSpec((tm, d), tok), pl.BlockSpec((1, d), fixed),
                  pl.BlockSpec((tm, 1), tok), pl.BlockSpec((1, DIL_HD), fixed),
                  _w_cols(2 * GLA_QK, LO_QK), _w_cols(GLA_V, LO_V), _w_cols(GLA_V, LO_R),
                  _w_cols(D_MODEL, HI_GA), _w_cols(D_MODEL, HI_GD), _w_cols(DIL_OUT, HI_ZD),
                  _w_cols(LANES, LO_ALR)]
        + [pl.BlockSpec(w.shape, fixed) for w in small],
        out_specs=[pl.BlockSpec((tm, GLA_V), tok), pl.BlockSpec((tm, n_gates), tok),
                   pl.BlockSpec((tm, d), tok), pl.BlockSpec((tm, DIL_HD), tok),
                   pl.BlockSpec((tm, DIL_HD), tok)],
        out_shape=[jax.ShapeDtypeStruct((t, GLA_V), BF16), jax.ShapeDtypeStruct((t, n_gates), BF16),
                   jax.ShapeDtypeStruct((t, d), BF16), table, table],
        scratch_shapes=[pltpu.VMEM((GLA_HEADS, GLA_DV, GLA_DK), F32)],
        compiler_params=_params("arbitrary"),
        name="nat_gla",
    )(x2, norm_gain.reshape(1, d), pos, freq, *([w] * 7), *small)


def _dil_proj_kernel(h_ref, wq_ref, wk_ref, wv_ref, g_ref, cos_ref, sin_ref, q_ref, k_ref, v_ref,
                     *scratch, dil):
    n_planes = COL_TILE // LANES
    tm = h_ref.shape[0]
    outs = (q_ref, k_ref, v_ref)

    def store(kind, rows, plane, y):
        if dil == 1:
            outs[kind][0, 0, rows, plane * LANES:(plane + 1) * LANES] = y.astype(q_ref.dtype)
        else:
            scratch[0][kind, plane, rows, :] = y

    def permute(kind, part):
        s0 = min(dil, ROW_STRIDE)
        s1 = dil // s0
        t0 = part * PERMUTE_ROWS
        mid = slice(t0 // s0, (t0 + PERMUTE_ROWS) // s0)
        dst = slice(t0 // dil, (t0 + PERMUTE_ROWS) // dil)
        for plane in range(n_planes):
            cols = slice(plane * LANES, (plane + 1) * LANES)
            for r0 in range(s0):
                hop = scratch[0][kind, plane, pl.ds(t0 + r0, PERMUTE_ROWS // s0, stride=s0), :]
                if s1 == 1:
                    outs[kind][0, r0, dst, cols] = hop.astype(q_ref.dtype)
                    continue
                scratch[1][kind, plane, r0, mid, :] = hop
                for r1 in range(s1):
                    outs[kind][0, s0 * r1 + r0, dst, cols] = scratch[1][
                        kind, plane, r0, pl.ds(mid.start + r1, PERMUTE_ROWS // dil, stride=s1), :
                    ].astype(q_ref.dtype)

    def epilogue(rows, accs):
        for kind, acc in enumerate(accs):
            for hh in range(n_planes):
                xh = acc[:, hh * DIL_HD:(hh + 1) * DIL_HD]
                if kind == 2:
                    store(kind, rows, hh, xh)
                    continue
                ms = jnp.mean(xh * xh, axis=-1, keepdims=True)
                y = xh * lax.rsqrt(ms + EPS) * g_ref[kind:kind + 1, :]
                y = y * cos_ref[rows, :] + pltpu.roll(y, DIL_HD // 2, 1) * sin_ref[rows, :]
                store(kind, rows, hh, y)

    def finish(rows, accs):
        epilogue(rows, accs)
        if dil > 1 and rows.stop % PERMUTE_ROWS == 0:
            for kind in range(3):
                permute(kind, rows.stop // PERMUTE_ROWS - 1)

    pending = None
    for rows in _row_subs(tm, DIL_ROW_SUB):
        h = h_ref[rows, :]
        accs = [_dot(h, w_ref[...]) for w_ref in (wq_ref, wk_ref, wv_ref)]
        if pending is not None:
            finish(*pending)
        pending = (rows, accs)
    finish(*pending)


def _dil_proj(h, w, group, dil, gains, cos, sin, batch):
    t, d = h.shape
    seq = t // batch
    tm = TOK_TILE
    nt = seq // tm
    tok = lambda b, i: (b * nt + i, 0)
    fixed = lambda b, i: (0, 0)
    out = jax.ShapeDtypeStruct((batch, dil, seq // dil, COL_TILE), BF16)
    n_groups = len(DIL_GROUPS)
    assert DIL_HD == LANES and PERMUTE_ROWS % (dil * BF16_ROWS) == 0 and tm % PERMUTE_ROWS == 0
    return pl.pallas_call(
        functools.partial(_dil_proj_kernel, dil=dil),
        grid=(batch, nt),
        in_specs=[pl.BlockSpec((tm, d), tok)]
        + [_w_cols(COL_TILE, HI_QD + kind * n_groups + group) for kind in range(3)]
        + [pl.BlockSpec(gains.shape, fixed),
           pl.BlockSpec((tm, DIL_HD), tok),
           pl.BlockSpec((tm, DIL_HD), tok)],
        out_specs=[pl.BlockSpec((1, dil, tm // dil, COL_TILE), lambda b, i: (b, 0, i, 0))] * 3,
        out_shape=[out] * 3,
        scratch_shapes=[] if dil == 1 else [
            pltpu.VMEM((3, COL_TILE // LANES, tm, LANES), F32),
            pltpu.VMEM((3, COL_TILE // LANES, ROW_STRIDE, tm // ROW_STRIDE, LANES), F32)],
        compiler_params=_params("parallel", "parallel"),
        name=f"dil_proj_d{dil}",
    )(h, w, w, w, gains, cos, sin)


def _dil_attn_kernel(q_ref, kp_ref, kc_ref, vp_ref, vc_ref, o_ref, lse_ref):
    n_qblk = q_ref.shape[1] // ATT_BLOCK
    row = lax.broadcasted_iota(jnp.int32, (ATT_BLOCK, 2 * ATT_BLOCK), 0)
    col = lax.broadcasted_iota(jnp.int32, (ATT_BLOCK, 2 * ATT_BLOCK), 1)
    band = (col >= row) & (col <= row + ATT_BLOCK)
    band_first = band & ((col >= ATT_BLOCK) | (pl.program_id(1) > 0))
    ones = jnp.ones((2 * ATT_BLOCK, DIL_HD), BF16)
    lane = lax.broadcasted_iota(jnp.int32, (ATT_BLOCK, LANES), 1)
    for sub, a in [(sub, a) for sub in range(q_ref.shape[0]) for a in range(n_qblk)]:
        rows = slice(a * ATT_BLOCK, (a + 1) * ATT_BLOCK)
        lse_all = jnp.zeros((ATT_BLOCK, LANES), F32)
        for hh in range(DIL_HEADS):
            sl = slice(hh * DIL_HD, (hh + 1) * DIL_HD)
            if a == 0:
                k_win = jnp.concatenate([kp_ref[sub, :, sl], kc_ref[sub, :ATT_BLOCK, sl]], axis=0)
                v_win = jnp.concatenate([vp_ref[sub, :, sl], vc_ref[sub, :ATT_BLOCK, sl]], axis=0)
            else:
                win = slice((a - 1) * ATT_BLOCK, (a + 1) * ATT_BLOCK)
                k_win = kc_ref[sub, win, sl]
                v_win = vc_ref[sub, win, sl]
            s = _dot_nt(q_ref[sub, rows, sl], k_win)
            s = jnp.where(band_first if a == 0 else band, s, NEG)
            m = jnp.max(s, axis=-1, keepdims=True)
            p = jnp.exp2(s - m).astype(BF16)
            ol = _dot(p, jnp.concatenate([v_win, ones], axis=1))
            l = ol[:, DIL_HD:]
            o_ref[sub, rows, sl] = (ol[:, :DIL_HD] / l).astype(o_ref.dtype)
            lse_all = jnp.where(lane == hh, m + jnp.log(l) * LOG2_E, lse_all)
        lse_ref[sub, rows, :] = lse_all


def _dil_attn(q, k, v, n_sub, dil):
    sub_len = q.size // COL_TILE // n_sub
    qb = min(sub_len, ATT_SUPER)
    ns = ATT_SUPER // qb
    n_qblk = qb // ATT_BLOCK
    q3, k3, v3 = (a.reshape(n_sub, sub_len, COL_TILE) for a in (q, k, v))
    blk = (ns, qb, COL_TILE)
    cur = lambda s, i: (s, i, 0)
    prev = lambda s, i: (s, jnp.maximum(i * n_qblk - 1, 0), 0)
    prev_blk = (ns, ATT_BLOCK, COL_TILE)
    o, lse = pl.pallas_call(
        _dil_attn_kernel,
        grid=(n_sub // ns, sub_len // qb),
        in_specs=[pl.BlockSpec(blk, cur), pl.BlockSpec(prev_blk, prev), pl.BlockSpec(blk, cur),
                  pl.BlockSpec(prev_blk, prev), pl.BlockSpec(blk, cur)],
        out_specs=[pl.BlockSpec(blk, cur), pl.BlockSpec((ns, qb, LANES), cur)],
        out_shape=[jax.ShapeDtypeStruct((n_sub, sub_len, COL_TILE), BF16),
                   jax.ShapeDtypeStruct((n_sub, sub_len, LANES), F32)],
        compiler_params=_params("parallel", "arbitrary"),
        name=f"dil_attn_d{dil}",
    )(q3, k3, k3, v3, v3)
    return o, lse


def _merge_kernel(x_ref, oa_ref, o0_ref, o1_ref, o2_ref, l0_ref, l1_ref, l2_ref,
                  sz_ref, sga_ref, sgd_ref, wga_ref, wdo_ref, wo_ref, out_ref, og_ref, lg_ref, hop_ref):
    tm = x_ref.shape[1]
    for g, ((_, dil), o_ref, l_ref) in enumerate(zip(DIL_GROUPS, (o0_ref, o1_ref, o2_ref),
                                                     (l0_ref, l1_ref, l2_ref))):
        s0 = min(dil, ROW_STRIDE)
        s1 = dil // s0
        planes = [(lg_ref.at[g], lambda r: l_ref[0, r])] + [
            (og_ref.at[g, hh], lambda r, hh=hh: o_ref[0, r, :, hh * DIL_HD:(hh + 1) * DIL_HD].astype(F32))
            for hh in range(DIL_HEADS)]
        for p, (dst_ref, src) in enumerate(planes):
            for r0 in range(s0):
                if s1 == 1:
                    dst_ref[pl.ds(r0, tm // s0, stride=s0), :] = src(r0)
                    continue
                for r1 in range(s1):
                    hop_ref[p, pl.ds(r1, tm // dil, stride=s1), :] = src(s0 * r1 + r0)
                dst_ref[pl.ds(r0, tm // s0, stride=s0), :] = hop_ref[p, 0:tm // s0, :]
    n_groups = len(DIL_GROUPS)

    def combine(rows):
        lses = [lg_ref[g, rows, :] for g in range(n_groups)]
        m = jnp.maximum(jnp.maximum(lses[0], lses[1]), lses[2])
        es = [jnp.exp2(l - m) for l in lses]
        den = es[0] + es[1] + es[2]
        ws = [e / den for e in es]
        heads = []
        for hh in range(DIL_HEADS):
            acc = ws[0][:, hh:hh + 1] * og_ref[0, hh, rows, :]
            for g in range(1, n_groups):
                acc = acc + ws[g][:, hh:hh + 1] * og_ref[g, hh, rows, :]
            heads.append(acc)
        return (jnp.concatenate(heads, axis=-1) * sz_ref[rows, :].astype(F32)).astype(BF16)

    def branches(rows, o_d):
        y_a = _dot(oa_ref[rows, :], wga_ref[...])
        y_d = _dot(o_d, wdo_ref[...])
        return y_a, y_d

    def gate(rows, y_a, y_d):
        return (sga_ref[rows, :].astype(F32) * y_a + sgd_ref[rows, :].astype(F32) * y_d).astype(BF16)

    def project(rows, y):
        out_ref[0, rows, :] = x_ref[0, rows, :] + _dot(y, wo_ref[...])

    for rows in _row_subs(tm):
        project(rows, gate(rows, *branches(rows, combine(rows))))


def _merge(x, o_a, o_ds, lses, gates, wga, wdo, wo):
    batch, seq, d = x.shape
    tm = MERGE_TOK_TILE
    nt = seq // tm
    tok = lambda b, i: (b * nt + i, 0)
    fixed = lambda b, i: (0, 0)
    dil_spec = lambda dil, w: pl.BlockSpec((1, dil, tm // dil, w), lambda b, i: (b, 0, i, 0))
    o_ds = [o.reshape(batch, dil, seq // dil, DIL_OUT) for o, (_, dil) in zip(o_ds, DIL_GROUPS)]
    lses = [l.reshape(batch, dil, seq // dil, LANES) for l, (_, dil) in zip(lses, DIL_GROUPS)]
    return pl.pallas_call(
        _merge_kernel,
        grid=(batch, nt),
        in_specs=[pl.BlockSpec((1, tm, d), lambda b, i: (b, i, 0)),
                  pl.BlockSpec((tm, GLA_V), tok)]
        + [dil_spec(dil, DIL_OUT) for _, dil in DIL_GROUPS]
        + [dil_spec(dil, LANES) for _, dil in DIL_GROUPS]
        + [pl.BlockSpec((tm, DIL_OUT), lambda b, i: (b * nt + i, GATE_ZD)),
           pl.BlockSpec((tm, d), lambda b, i: (b * nt + i, GATE_GA * COL_TILE // D_MODEL)),
           pl.BlockSpec((tm, d), lambda b, i: (b * nt + i, GATE_GD * COL_TILE // D_MODEL)),
           pl.BlockSpec(wga.shape, fixed),
           pl.BlockSpec(wdo.shape, fixed),
           pl.BlockSpec(wo.shape, fixed)],
        out_specs=pl.BlockSpec((1, tm, d), lambda b, i: (b, i, 0)),
        out_shape=jax.ShapeDtypeStruct(x.shape, x.dtype),
        scratch_shapes=[pltpu.VMEM((len(DIL_GROUPS), DIL_HEADS, tm, DIL_HD), F32),
                        pltpu.VMEM((len(DIL_GROUPS), tm, LANES), F32),
                        pltpu.VMEM((DIL_HEADS + 1, tm // ROW_STRIDE, LANES), F32)],
        compiler_params=_params("parallel", "parallel"),
        name="merge_out",
    )(x, o_a, *o_ds, *lses, gates, gates, gates, wga, wdo, wo)


def kernel(x, positions, norm_gain, w_in, gla_w_a2, gla_b_a, gla_out_gain, dil_q_gain, dil_k_gain,
           w_gla_out, w_dil_out, w_o):
    batch, seq, d = x.shape
    t = batch * seq
    half = DIL_HD // 2
    inv_freq = ROPE_THETA ** (-jnp.arange(half, dtype=F32) / half)
    freq = jnp.concatenate([inv_freq, inv_freq]).reshape(1, DIL_HD)
    pos = positions.astype(F32).reshape(t, 1)
    for layer in range(norm_gain.shape[0]):
        w = _w_cast(jnp.swapaxes(w_in[layer], 0, 1))
        wa2 = jnp.pad(gla_w_a2[layer], ((0, LANES - GLA_RANK), (0, 0))).astype(BF16)
        ba = gla_b_a[layer].reshape(1, GLA_QK)
        g_dqk = jnp.stack([dil_q_gain[layer] * (DIL_HD ** -0.5 * LOG2_E), dil_k_gain[layer]])

        o_a, gates, h, cos, sin = _nat_gla(x.reshape(t, d), norm_gain[layer], pos, freq, w,
                                           wa2, ba, gla_out_gain[layer].reshape(1, GLA_DV), seq)
        o_ds, lses = [], []
        for g, (win, dil) in enumerate(DIL_GROUPS):
            assert win // dil == ATT_BLOCK
            q_g, k_g, v_g = _dil_proj(h, w, g, dil, g_dqk, cos, sin, batch)
            o_g, lse_g = _dil_attn(q_g, k_g, v_g, batch * dil, dil)
            o_ds.append(o_g)
            lses.append(lse_g)
        x = _merge(x, o_a, o_ds, lses, gates,
                   w_gla_out[layer].astype(BF16), w_dil_out[layer].astype(BF16),
                   w_o[layer].astype(BF16))
    return x
```

```python
import functools
import itertools

import jax
import jax.numpy as jnp
from jax import lax
from jax.experimental import pallas as pl
from jax.experimental.pallas import tpu as pltpu

D_MODEL = 1024
EPS = 1e-6
ROPE_THETA = 10000.0
GLA_HEADS = 4
GLA_DK = 128
GLA_DV = 256
GLA_RANK = 16
GLA_TAU = 16.0
GLA_QK = GLA_HEADS * GLA_DK
GLA_V = GLA_HEADS * GLA_DV
DIL_GROUPS = ((128, 1), (512, 4), (2048, 16))
DIL_HEADS = 4
DIL_HD = 128
DIL_QK = len(DIL_GROUPS) * DIL_HEADS * DIL_HD
DIL_OUT = DIL_HEADS * DIL_HD
IN_SPLIT_SIZES = (GLA_QK, GLA_QK, GLA_V, GLA_V, GLA_RANK,
                  DIL_QK, DIL_QK, DIL_QK, DIL_OUT, D_MODEL, D_MODEL)

LANES = 128
BF16_ROWS = 16
ROW_STRIDE = 4
GLA_BLOCK = 128
GLA_MID = GLA_BLOCK // 2
ATT_BLOCK = 128
ATT_SUPER = 1024
TOK_TILE = 1024
ROW_SUB = 256
DIL_ROW_SUB = 256
PERMUTE_ROWS = 512
GLA_TOK_TILE = 512
GLA_WIDE = 128
MERGE_TOK_TILE = 512
COL_TILE = DIL_HEADS * DIL_HD
VMEM_LIMIT_BYTES = 56 * 1024 * 1024

F32 = jnp.float32
BF16 = jnp.bfloat16
NEG = -1e30
LOG2_E = 1.4426950408889634

GATE_GA, GATE_GD, GATE_ZD = 0, 2, 4

(_QA, _KA, _VA, _RA, _ALR, _QD, _KD, _VD, _ZD, _GA, _GD) = (
    sum(IN_SPLIT_SIZES[:i]) for i in range(len(IN_SPLIT_SIZES)))


def _block_index(offset, width):
    assert offset % width == 0
    return offset // width


W_BLOCK = 1024
LO_PAD = -(-(_ALR + LANES) // W_BLOCK) * W_BLOCK
LO_QK, LO_V, LO_R = _block_index(_QA, 2 * GLA_QK), _block_index(_VA, GLA_V), _block_index(_RA, GLA_V)
LO_ALR = _block_index(_ALR, LANES)
HI_QD = _block_index(LO_PAD, COL_TILE)
HI_ZD = _block_index(LO_PAD + _ZD - _QD, DIL_OUT)
HI_GA, HI_GD = _block_index(LO_PAD + _GA - _QD, D_MODEL), _block_index(LO_PAD + _GD - _QD, D_MODEL)
assert _KA == _QA + GLA_QK and _KD - _QD == DIL_QK and _VD - _KD == DIL_QK


def _w_cols(width, index):
    return pl.BlockSpec((D_MODEL, width), lambda *_: (0, index))


def _params(*sem):
    return pltpu.CompilerParams(dimension_semantics=sem, vmem_limit_bytes=VMEM_LIMIT_BYTES)


def _dot(a, b):
    return jnp.dot(a, b, preferred_element_type=F32)


def _dot_nt(a, b):
    return lax.dot_general(a, b, (((1,), (1,)), ((), ())), preferred_element_type=F32)


def _dot_tn(a, b):
    return lax.dot_general(a, b, (((0,), (0,)), ((), ())), preferred_element_type=F32)


def _sigmoid_of_twice(half_x):
    return 0.5 * jnp.tanh(half_x) + 0.5


def _silu_of_twice(half_x):
    return half_x * jnp.tanh(half_x) + half_x


def _row_subs(n, sub=ROW_SUB):
    return [slice(r * sub, (r + 1) * sub) for r in range(n // sub)]


def _w_cast_kernel(wt_ref, o_ref, *, n_lo):
    j = pl.program_id(0)
    start = jnp.where(j < n_lo, j * W_BLOCK, _QD + (j - n_lo) * W_BLOCK)
    col = start + lax.broadcasted_iota(jnp.int32, (W_BLOCK, 1), 0)
    is_gate = ((col >= _RA) & (col < _ALR)) | (col >= _ZD)
    o_ref[...] = (wt_ref[...] * jnp.where(is_gate, 0.5, 1.0)).T.astype(o_ref.dtype)


def _w_cast(w_t):
    n, d = w_t.shape
    n_lo = _block_index(LO_PAD, W_BLOCK)
    n_hi = _block_index(n - _QD, W_BLOCK)
    assert _QD % BF16_ROWS == 0 and LO_PAD <= n

    def rows(j):
        start = jnp.where(j < n_lo, j * W_BLOCK, _QD + (j - n_lo) * W_BLOCK)
        return pl.multiple_of(start, BF16_ROWS), 0

    return pl.pallas_call(
        functools.partial(_w_cast_kernel, n_lo=n_lo),
        grid=(n_lo + n_hi,),
        in_specs=[pl.BlockSpec((pl.Element(W_BLOCK), pl.Element(d)), rows)],
        out_specs=pl.BlockSpec((d, W_BLOCK), lambda j: (0, j)),
        out_shape=jax.ShapeDtypeStruct((d, (n_lo + n_hi) * W_BLOCK), BF16),
        compiler_params=_params("parallel"),
        name="w_cast",
    )(w_t)


def _nat_gla_kernel(x_ref, ng_ref, pos_ref, freq_ref, wqk_ref, wv_ref, wr_ref, wga_ref, wgd_ref,
                    wzd_ref, walr_ref, wa2_ref, ba_ref, gain_ref,
                    oa_ref, gates_ref, h_ref, cos_ref, sin_ref, st_ref, *, tiles_per_seq):
    @pl.when(pl.program_id(0) % tiles_per_seq == 0)
    def _():
        st_ref[...] = jnp.zeros_like(st_ref)

    row = lax.broadcasted_iota(jnp.int32, (GLA_BLOCK, GLA_BLOCK), 0)
    col = lax.broadcasted_iota(jnp.int32, (GLA_BLOCK, GLA_BLOCK), 1)
    causal = col <= row
    tri = causal.astype(BF16)
    n_blk = x_ref.shape[0] // GLA_BLOCK
    low = lax.broadcasted_iota(jnp.int32, (GLA_BLOCK // 2, DIL_HD), 1) < DIL_HD // 2

    gate_plan = ((wga_ref, 0, "sigmoid"), (wga_ref, 1, "sigmoid"), (wgd_ref, 0, "sigmoid"),
                 (wgd_ref, 1, "sigmoid"), (wzd_ref, 0, "silu"))

    def gate_tiles(rows, h, lo_j, hi_j):
        for j in range(lo_j, hi_j):
            w_ref, wj, act = gate_plan[j]
            acc = _dot(h, w_ref[:, wj * COL_TILE:(wj + 1) * COL_TILE])
            act_fn = _silu_of_twice if act == "silu" else _sigmoid_of_twice
            gates_ref[rows, j * COL_TILE:(j + 1) * COL_TILE] = act_fn(acc).astype(gates_ref.dtype)

    def recurrence(c, qk, hi, lo, v, r):
        rows = slice(c * GLA_BLOCK, (c + 1) * GLA_BLOCK)
        b = _dot(tri, hi) + _dot(tri, lo)
        b_mid = b[GLA_MID - 1:GLA_MID]
        b_last = b[GLA_BLOCK - 1:GLA_BLOCK]
        q = qk[:, :GLA_QK] * (GLA_DK ** -0.5)
        k = qk[:, GLA_QK:]
        q_in = (q * jnp.exp(b)).astype(BF16)
        q_mid = (q * jnp.exp(b - b_mid)).astype(BF16)
        k_mid = (k * jnp.exp(b_mid - b)).astype(BF16)
        k_end = (k * jnp.exp(b_last - b)).astype(BF16)
        dec = jnp.exp(b_last)
        yield
        heads = [(slice(hh * GLA_DK, (hh + 1) * GLA_DK), slice(hh * GLA_DV, (hh + 1) * GLA_DV))
                 for hh in range(GLA_HEADS)]
        attn = [_dot_nt(q_mid[:, ks], k_mid[:, ks]) for ks, _ in heads]
        kv_t = [_dot_tn(v[:, vs], k_end[:, ks]) for ks, vs in heads]
        yield
        outs = []
        for hh, (ks, vs) in enumerate(heads):
            st = st_ref[hh]
            a = jnp.where(causal, attn[hh], 0.0).astype(BF16)
            outs.append(_dot(a, v[:, vs]) + _dot_nt(q_in[:, ks], st.astype(BF16)))
            st_ref[hh] = st * dec[:, ks] + kv_t[hh]
        yield
        for (_, vs), o in zip(heads, outs):
            ms = jnp.mean(o * o, axis=-1, keepdims=True)
            o = o * lax.rsqrt(ms + EPS) * gain_ref[...] * r[:, vs]
            oa_ref[rows, vs] = o.astype(oa_ref.dtype)
        yield

    def advance(gen):
        if gen is not None:
            for _ in range(per):
                next(gen)

    def rope_tables(c):
        half = GLA_BLOCK // 2
        top = slice(c * GLA_BLOCK, c * GLA_BLOCK + half)
        bot = slice(c * GLA_BLOCK + half, (c + 1) * GLA_BLOCK)
        ang = jnp.where(low, pos_ref[top, :], pos_ref[bot, :]) * freq_ref[...]
        cos, sin = jnp.cos(ang), jnp.sin(ang)
        cos_x, sin_x = pltpu.roll(cos, DIL_HD // 2, 1), pltpu.roll(sin, DIL_HD // 2, 1)
        cos_ref[top, :] = jnp.where(low, cos, cos_x)
        cos_ref[bot, :] = jnp.where(low, cos_x, cos)
        sin_ref[top, :] = jnp.where(low, -sin, sin_x)
        sin_ref[bot, :] = jnp.where(low, -sin_x, sin)

    per = GLA_WIDE // GLA_BLOCK
    gen = late_gates = None
    for wb in range(x_ref.shape[0] // GLA_WIDE):
        rows = slice(wb * GLA_WIDE, (wb + 1) * GLA_WIDE)
        x = x_ref[rows, :]
        ms = jnp.mean(x * x, axis=-1, keepdims=True)
        h = (x * lax.rsqrt(ms + EPS) * ng_ref[...]).astype(BF16)
        h_ref[rows, :] = h
        alr = _dot(h, walr_ref[...]).astype(BF16)
        v = _dot(h, wv_ref[...]).astype(BF16)
        advance(gen)
        for c in range(wb * per, (wb + 1) * per):
            rope_tables(c)
        z = _dot(alr, wa2_ref[...]) + ba_ref[...]
        la = (jnp.minimum(z, 0.0) - jnp.log1p(jnp.exp(-jnp.abs(z)))) * (1.0 / GLA_TAU)
        hi = la.astype(BF16)
        lo = (la - hi.astype(F32)).astype(BF16)
        r = _silu_of_twice(_dot(h, wr_ref[...]))
        advance(gen)
        if late_gates is not None:
            gate_tiles(*late_gates, 0, 2)
        advance(gen)
        qk = _dot(h, wqk_ref[...])
        advance(gen)
        if late_gates is not None:
            gate_tiles(*late_gates, 2, len(gate_plan))
        late_gates = (rows, h)
        blocks = [slice(i * GLA_BLOCK, (i + 1) * GLA_BLOCK) for i in range(per)]
        gen = itertools.chain(*[recurrence(wb * per + i, qk[blk], hi[blk], lo[blk], v[blk], r[blk])
                                for i, blk in enumerate(blocks)])
    advance(gen)
    gate_tiles(*late_gates, 0, 2)
    advance(gen)
    gate_tiles(*late_gates, 2, len(gate_plan))
    for _ in gen:
        pass


def _nat_gla(x2, norm_gain, pos, freq, w, wa2, ba, gain, seq):
    t, d = x2.shape
    tm = GLA_TOK_TILE
    tok = lambda i: (i, 0)
    fixed = lambda i: (0, 0)
    small = (wa2, ba, gain)
    n_gates = 2 * D_MODEL + DIL_OUT
    table = jax.ShapeDtypeStruct((t, DIL_HD), F32)
    return pl.pallas_call(
        functools.partial(_nat_gla_kernel, tiles_per_seq=seq // tm),
        grid=(t // tm,),
        in_specs=[pl.BlockSpec((tm, d), tok), pl.BlockSpec((1, d), fixed),
                  pl.BlockSpec((tm, 1), tok), pl.BlockSpec((1, DIL_HD), fixed),
                  _w_cols(2 * GLA_QK, LO_QK), _w_cols(GLA_V, LO_V), _w_cols(GLA_V, LO_R),
                  _w_cols(D_MODEL, HI_GA), _w_cols(D_MODEL, HI_GD), _w_cols(DIL_OUT, HI_ZD),
                  _w_cols(LANES, LO_ALR)]
        + [pl.BlockSpec(w.shape, fixed) for w in small],
        out_specs=[pl.BlockSpec((tm, GLA_V), tok), pl.BlockSpec((tm, n_gates), tok),
                   pl.BlockSpec((tm, d), tok), pl.BlockSpec((tm, DIL_HD), tok),
                   pl.BlockSpec((tm, DIL_HD), tok)],
        out_shape=[jax.ShapeDtypeStruct((t, GLA_V), BF16), jax.ShapeDtypeStruct((t, n_gates), BF16),
                   jax.ShapeDtypeStruct((t, d), BF16), table, table],
        scratch_shapes=[pltpu.VMEM((GLA_HEADS, GLA_DV, GLA_DK), F32)],
        compiler_params=_params("arbitrary"),
        name="nat_gla",
    )(x2, norm_gain.reshape(1, d), pos, freq, *([w] * 7), *small)


def _dil_proj_kernel(h_ref, wq_ref, wk_ref, wv_ref, g_ref, cos_ref, sin_ref, q_ref, k_ref, v_ref,
                     *scratch, dil):
    n_planes = COL_TILE // LANES
    tm = h_ref.shape[0]
    outs = (q_ref, k_ref, v_ref)

    def store(kind, rows, plane, y):
        if dil == 1:
            outs[kind][0, 0, rows, plane * LANES:(plane + 1) * LANES] = y.astype(q_ref.dtype)
        else:
            scratch[0][kind, plane, rows, :] = y

    def permute(kind, part):
        s0 = min(dil, ROW_STRIDE)
        s1 = dil // s0
        t0 = part * PERMUTE_ROWS
        mid = slice(t0 // s0, (t0 + PERMUTE_ROWS) // s0)
        dst = slice(t0 // dil, (t0 + PERMUTE_ROWS) // dil)
        for plane in range(n_planes):
            cols = slice(plane * LANES, (plane + 1) * LANES)
            for r0 in range(s0):
                hop = scratch[0][kind, plane, pl.ds(t0 + r0, PERMUTE_ROWS // s0, stride=s0), :]
                if s1 == 1:
                    outs[kind][0, r0, dst, cols] = hop.astype(q_ref.dtype)
                    continue
                scratch[1][kind, plane, r0, mid, :] = hop
                for r1 in range(s1):
                    outs[kind][0, s0 * r1 + r0, dst, cols] = scratch[1][
                        kind, plane, r0, pl.ds(mid.start + r1, PERMUTE_ROWS // dil, stride=s1), :
                    ].astype(q_ref.dtype)

    def epilogue(rows, accs):
        for kind, acc in enumerate(accs):
            for hh in range(n_planes):
                xh = acc[:, hh * DIL_HD:(hh + 1) * DIL_HD]
                if kind == 2:
                    store(kind, rows, hh, xh)
                    continue
                ms = jnp.mean(xh * xh, axis=-1, keepdims=True)
                y = xh * lax.rsqrt(ms + EPS) * g_ref[kind:kind + 1, :]
                y = y * cos_ref[rows, :] + pltpu.roll(y, DIL_HD // 2, 1) * sin_ref[rows, :]
                store(kind, rows, hh, y)

    def finish(rows, accs):
        epilogue(rows, accs)
        if dil > 1 and rows.stop % PERMUTE_ROWS == 0:
            for kind in range(3):
                permute(kind, rows.stop // PERMUTE_ROWS - 1)

    pending = None
    for rows in _row_subs(tm, DIL_ROW_SUB):
        h = h_ref[rows, :]
        accs = [_dot(h, w_ref[...]) for w_ref in (wq_ref, wk_ref, wv_ref)]
        if pending is not None:
            finish(*pending)
        pending = (rows, accs)
    finish(*pending)


def _dil_proj(h, w, group, dil, gains, cos, sin, batch):
    t, d = h.shape
    seq = t // batch
    tm = TOK_TILE
    nt = seq // tm
    tok = lambda b, i: (b * nt + i, 0)
    fixed = lambda b, i: (0, 0)
    out = jax.ShapeDtypeStruct((batch, dil, seq // dil, COL_TILE), BF16)
    n_groups = len(DIL_GROUPS)
    assert DIL_HD == LANES and PERMUTE_ROWS % (dil * BF16_ROWS) == 0 and tm % PERMUTE_ROWS == 0
    return pl.pallas_call(
        functools.partial(_dil_proj_kernel, dil=dil),
        grid=(batch, nt),
        in_specs=[pl.BlockSpec((tm, d), tok)]
        + [_w_cols(COL_TILE, HI_QD + kind * n_groups + group) for kind in range(3)]
        + [pl.BlockSpec(gains.shape, fixed),
           pl.BlockSpec((tm, DIL_HD), tok),
           pl.BlockSpec((tm, DIL_HD), tok)],
        out_specs=[pl.BlockSpec((1, dil, tm // dil, COL_TILE), lambda b, i: (b, 0, i, 0))] * 3,
        out_shape=[out] * 3,
        scratch_shapes=[] if dil == 1 else [
            pltpu.VMEM((3, COL_TILE // LANES, tm, LANES), F32),
            pltpu.VMEM((3, COL_TILE // LANES, ROW_STRIDE, tm // ROW_STRIDE, LANES), F32)],
        compiler_params=_params("parallel", "parallel"),
        name=f"dil_proj_d{dil}",
    )(h, w, w, w, gains, cos, sin)


def _dil_attn_kernel(q_ref, kp_ref, kc_ref, vp_ref, vc_ref, o_ref, lse_ref):
    n_qblk = q_ref.shape[1] // ATT_BLOCK
    row = lax.broadcasted_iota(jnp.int32, (ATT_BLOCK, 2 * ATT_BLOCK), 0)
    col = lax.broadcasted_iota(jnp.int32, (ATT_BLOCK, 2 * ATT_BLOCK), 1)
    band = (col >= row) & (col <= row + ATT_BLOCK)
    band_first = band & ((col >= ATT_BLOCK) | (pl.program_id(1) > 0))
    ones = jnp.ones((2 * ATT_BLOCK, DIL_HD), BF16)
    lane = lax.broadcasted_iota(jnp.int32, (ATT_BLOCK, LANES), 1)
    for sub, a in [(sub, a) for sub in range(q_ref.shape[0]) for a in range(n_qblk)]:
        rows = slice(a * ATT_BLOCK, (a + 1) * ATT_BLOCK)
        lse_all = jnp.zeros((ATT_BLOCK, LANES), F32)
        for hh in range(DIL_HEADS):
            sl = slice(hh * DIL_HD, (hh + 1) * DIL_HD)
            if a == 0:
                k_win = jnp.concatenate([kp_ref[sub, :, sl], kc_ref[sub, :ATT_BLOCK, sl]], axis=0)
                v_win = jnp.concatenate([vp_ref[sub, :, sl], vc_ref[sub, :ATT_BLOCK, sl]], axis=0)
            else:
                win = slice((a - 1) * ATT_BLOCK, (a + 1) * ATT_BLOCK)
                k_win = kc_ref[sub, win, sl]
                v_win = vc_ref[sub, win, sl]
            s = _dot_nt(q_ref[sub, rows, sl], k_win)
            s = jnp.where(band_first if a == 0 else band, s, NEG)
            m = jnp.max(s, axis=-1, keepdims=True)
            p = jnp.exp2(s - m).astype(BF16)
            ol = _dot(p, jnp.concatenate([v_win, ones], axis=1))
            l = ol[:, DIL_HD:]
            o_ref[sub, rows, sl] = (ol[:, :DIL_HD] / l).astype(o_ref.dtype)
            lse_all = jnp.where(lane == hh, m + jnp.log(l) * LOG2_E, lse_all)
        lse_ref[sub, rows, :] = lse_all


def _dil_attn(q, k, v, n_sub, dil):
    sub_len = q.size // COL_TILE // n_sub
    qb = min(sub_len, ATT_SUPER)
    ns = ATT_SUPER // qb
    n_qblk = qb // ATT_BLOCK
    q3, k3, v3 = (a.reshape(n_sub, sub_len, COL_TILE) for a in (q, k, v))
    blk = (ns, qb, COL_TILE)
    cur = lambda s, i: (s, i, 0)
    prev = lambda s, i: (s, jnp.maximum(i * n_qblk - 1, 0), 0)
    prev_blk = (ns, ATT_BLOCK, COL_TILE)
    o, lse = pl.pallas_call(
        _dil_attn_kernel,
        grid=(n_sub // ns, sub_len // qb),
        in_specs=[pl.BlockSpec(blk, cur), pl.BlockSpec(prev_blk, prev), pl.BlockSpec(blk, cur),
                  pl.BlockSpec(prev_blk, prev), pl.BlockSpec(blk, cur)],
        out_specs=[pl.BlockSpec(blk, cur), pl.BlockSpec((ns, qb, LANES), cur)],
        out_shape=[jax.ShapeDtypeStruct((n_sub, sub_len, COL_TILE), BF16),
                   jax.ShapeDtypeStruct((n_sub, sub_len, LANES), F32)],
        compiler_params=_params("parallel", "arbitrary"),
        name=f"dil_attn_d{dil}",
    )(q3, k3, k3, v3, v3)
    return o, lse


def _merge_kernel(x_ref, oa_ref, o0_ref, o1_ref, o2_ref, l0_ref, l1_ref, l2_ref,
                  sz_ref, sga_ref, sgd_ref, wga_ref, wdo_ref, wo_ref, out_ref, og_ref, lg_ref, hop_ref):
    tm = x_ref.shape[1]
    for g, ((_, dil), o_ref, l_ref) in enumerate(zip(DIL_GROUPS, (o0_ref, o1_ref, o2_ref),
                                                     (l0_ref, l1_ref, l2_ref))):
        s0 = min(dil, ROW_STRIDE)
        s1 = dil // s0
        planes = [(lg_ref.at[g], lambda r: l_ref[0, r])] + [
            (og_ref.at[g, hh], lambda r, hh=hh: o_ref[0, r, :, hh * DIL_HD:(hh + 1) * DIL_HD].astype(F32))
            for hh in range(DIL_HEADS)]
        for p, (dst_ref, src) in enumerate(planes):
            for r0 in range(s0):
                if s1 == 1:
                    dst_ref[pl.ds(r0, tm // s0, stride=s0), :] = src(r0)
                    continue
                for r1 in range(s1):
                    hop_ref[p, pl.ds(r1, tm // dil, stride=s1), :] = src(s0 * r1 + r0)
                dst_ref[pl.ds(r0, tm // s0, stride=s0), :] = hop_ref[p, 0:tm // s0, :]
    n_groups = len(DIL_GROUPS)

    def combine(rows):
        lses = [lg_ref[g, rows, :] for g in range(n_groups)]
        m = jnp.maximum(jnp.maximum(lses[0], lses[1]), lses[2])
        es = [jnp.exp2(l - m) for l in lses]
        den = es[0] + es[1] + es[2]
        ws = [e / den for e in es]
        heads = []
        for hh in range(DIL_HEADS):
            acc = ws[0][:, hh:hh + 1] * og_ref[0, hh, rows, :]
            for g in range(1, n_groups):
                acc = acc + ws[g][:, hh:hh + 1] * og_ref[g, hh, rows, :]
            heads.append(acc)
        return (jnp.concatenate(heads, axis=-1) * sz_ref[rows, :].astype(F32)).astype(BF16)

    def branches(rows, o_d):
        y_a = _dot(oa_ref[rows, :], wga_ref[...])
        y_d = _dot(o_d, wdo_ref[...])
        return y_a, y_d

    def gate(rows, y_a, y_d):
        return (sga_ref[rows, :].astype(F32) * y_a + sgd_ref[rows, :].astype(F32) * y_d).astype(BF16)

    def project(rows, y):
        out_ref[0, rows, :] = x_ref[0, rows, :] + _dot(y, wo_ref[...])

    for rows in _row_subs(tm):
        project(rows, gate(rows, *branches(rows, combine(rows))))


def _merge(x, o_a, o_ds, lses, gates, wga, wdo, wo):
    batch, seq, d = x.shape
    tm = MERGE_TOK_TILE
    nt = seq // tm
    tok = lambda b, i: (b * nt + i, 0)
    fixed = lambda b, i: (0, 0)
    dil_spec = lambda dil, w: pl.BlockSpec((1, dil, tm // dil, w), lambda b, i: (b, 0, i, 0))
    o_ds = [o.reshape(batch, dil, seq // dil, DIL_OUT) for o, (_, dil) in zip(o_ds, DIL_GROUPS)]
    lses = [l.reshape(batch, dil, seq // dil, LANES) for l, (_, dil) in zip(lses, DIL_GROUPS)]
    return pl.pallas_call(
        _merge_kernel,
        grid=(batch, nt),
        in_specs=[pl.BlockSpec((1, tm, d), lambda b, i: (b, i, 0)),
                  pl.BlockSpec((tm, GLA_V), tok)]
        + [dil_spec(dil, DIL_OUT) for _, dil in DIL_GROUPS]
        + [dil_spec(dil, LANES) for _, dil in DIL_GROUPS]
        + [pl.BlockSpec((tm, DIL_OUT), lambda b, i: (b * nt + i, GATE_ZD)),
           pl.BlockSpec((tm, d), lambda b, i: (b * nt + i, GATE_GA * COL_TILE // D_MODEL)),
           pl.BlockSpec((tm, d), lambda b, i: (b * nt + i, GATE_GD * COL_TILE // D_MODEL)),
           pl.BlockSpec(wga.shape, fixed),
           pl.BlockSpec(wdo.shape, fixed),
           pl.BlockSpec(wo.shape, fixed)],
        out_specs=pl.BlockSpec((1, tm, d), lambda b, i: (b, i, 0)),
        out_shape=jax.ShapeDtypeStruct(x.shape, x.dtype),
        scratch_shapes=[pltpu.VMEM((len(DIL_GROUPS), DIL_HEADS, tm, DIL_HD), F32),
                        pltpu.VMEM((len(DIL_GROUPS), tm, LANES), F32),
                        pltpu.VMEM((DIL_HEADS + 1, tm // ROW_STRIDE, LANES), F32)],
        compiler_params=_params("parallel", "parallel"),
        name="merge_out",
    )(x, o_a, *o_ds, *lses, gates, gates, gates, wga, wdo, wo)


def kernel(x, positions, norm_gain, w_in, gla_w_a2, gla_b_a, gla_out_gain, dil_q_gain, dil_k_gain,
           w_gla_out, w_dil_out, w_o):
    batch, seq, d = x.shape
    t = batch * seq
    half = DIL_HD // 2
    inv_freq = ROPE_THETA ** (-jnp.arange(half, dtype=F32) / half)
    freq = jnp.concatenate([inv_freq, inv_freq]).reshape(1, DIL_HD)
    pos = positions.astype(F32).reshape(t, 1)
    for layer in range(norm_gain.shape[0]):
        w = _w_cast(jnp.swapaxes(w_in[layer], 0, 1))
        wa2 = jnp.pad(gla_w_a2[layer], ((0, LANES - GLA_RANK), (0, 0))).astype(BF16)
        ba = gla_b_a[layer].reshape(1, GLA_QK)
        g_dqk = jnp.stack([dil_q_gain[layer] * (DIL_HD ** -0.5 * LOG2_E), dil_k_gain[layer]])

        o_a, gates, h, cos, sin = _nat_gla(x.reshape(t, d), norm_gain[layer], pos, freq, w,
                                           wa2, ba, gla_out_gain[layer].reshape(1, GLA_DV), seq)
        o_ds, lses = [], []
        for g, (win, dil) in enumerate(DIL_GROUPS):
            assert win // dil == ATT_BLOCK
            q_g, k_g, v_g = _dil_proj(h, w, g, dil, g_dqk, cos, sin, batch)
            o_g, lse_g = _dil_attn(q_g, k_g, v_g, batch * dil, dil)
            o_ds.append(o_g)
            lses.append(lse_g)
        x = _merge(x, o_a, o_ds, lses, gates,
                   w_gla_out[layer].astype(BF16), w_dil_out[layer].astype(BF16),
                   w_o[layer].astype(BF16))
    return x
```

```python
import functools
import itertools

import jax
import jax.numpy as jnp
from jax import lax
from jax.experimental import pallas as pl
from jax.experimental.pallas import tpu as pltpu

D_MODEL = 1024
EPS = 1e-6
ROPE_THETA = 10000.0
GLA_HEADS = 4
GLA_DK = 128
GLA_DV = 256
GLA_RANK = 16
GLA_TAU = 16.0
GLA_QK = GLA_HEADS * GLA_DK
GLA_V = GLA_HEADS * GLA_DV
DIL_GROUPS = ((128, 1), (512, 4), (2048, 16))
DIL_HEADS = 4
DIL_HD = 128
DIL_QK = len(DIL_GROUPS) * DIL_HEADS * DIL_HD
DIL_OUT = DIL_HEADS * DIL_HD
IN_SPLIT_SIZES = (GLA_QK, GLA_QK, GLA_V, GLA_V, GLA_RANK,
                  DIL_QK, DIL_QK, DIL_QK, DIL_OUT, D_MODEL, D_MODEL)

LANES = 128
BF16_ROWS = 16
ROW_STRIDE = 4
GLA_BLOCK = 128
GLA_MID = GLA_BLOCK // 2
ATT_BLOCK = 128
ATT_SUPER = 2048
TOK_TILE = 1024
ROW_SUB = 256
DIL_ROW_SUB = 256
PERMUTE_ROWS = 512
GLA_TOK_TILE = 512
GLA_WIDE = 128
MERGE_TOK_TILE = 512
COL_TILE = DIL_HEADS * DIL_HD
VMEM_LIMIT_BYTES = 56 * 1024 * 1024

F32 = jnp.float32
BF16 = jnp.bfloat16
NEG = -1e30
LOG2_E = 1.4426950408889634

GATE_GA, GATE_GD, GATE_ZD = 0, 2, 4

(_QA, _KA, _VA, _RA, _ALR, _QD, _KD, _VD, _ZD, _GA, _GD) = (
    sum(IN_SPLIT_SIZES[:i]) for i in range(len(IN_SPLIT_SIZES)))


def _block_index(offset, width):
    assert offset % width == 0
    return offset // width


W_BLOCK = 1024
LO_PAD = -(-(_ALR + LANES) // W_BLOCK) * W_BLOCK
LO_QK, LO_V, LO_R = _block_index(_QA, 2 * GLA_QK), _block_index(_VA, GLA_V), _block_index(_RA, GLA_V)
LO_ALR = _block_index(_ALR, LANES)
HI_QD = _block_index(LO_PAD, COL_TILE)
HI_ZD = _block_index(LO_PAD + _ZD - _QD, DIL_OUT)
HI_GA, HI_GD = _block_index(LO_PAD + _GA - _QD, D_MODEL), _block_index(LO_PAD + _GD - _QD, D_MODEL)
assert _KA == _QA + GLA_QK and _KD - _QD == DIL_QK and _VD - _KD == DIL_QK


def _w_cols(width, index):
    return pl.BlockSpec((D_MODEL, width), lambda *_: (0, index))


def _params(*sem):
    return pltpu.CompilerParams(dimension_semantics=sem, vmem_limit_bytes=VMEM_LIMIT_BYTES)


def _dot(a, b):
    return jnp.dot(a, b, preferred_element_type=F32)


def _dot_nt(a, b):
    return lax.dot_general(a, b, (((1,), (1,)), ((), ())), preferred_element_type=F32)


def _dot_tn(a, b):
    return lax.dot_general(a, b, (((0,), (0,)), ((), ())), preferred_element_type=F32)


def _sigmoid_of_twice(half_x):
    return 0.5 * jnp.tanh(half_x) + 0.5


def _silu_of_twice(half_x):
    return half_x * jnp.tanh(half_x) + half_x


def _row_subs(n, sub=ROW_SUB):
    return [slice(r * sub, (r + 1) * sub) for r in range(n // sub)]


def _w_cast_kernel(wt_ref, o_ref, *, n_lo):
    j = pl.program_id(0)
    start = jnp.where(j < n_lo, j * W_BLOCK, _QD + (j - n_lo) * W_BLOCK)
    col = start + lax.broadcasted_iota(jnp.int32, (W_BLOCK, 1), 0)
    is_gate = ((col >= _RA) & (col < _ALR)) | (col >= _ZD)
    o_ref[...] = (wt_ref[...] * jnp.where(is_gate, 0.5, 1.0)).T.astype(o_ref.dtype)


def _w_cast(w_t):
    n, d = w_t.shape
    n_lo = _block_index(LO_PAD, W_BLOCK)
    n_hi = _block_index(n - _QD, W_BLOCK)
    assert _QD % BF16_ROWS == 0 and LO_PAD <= n

    def rows(j):
        start = jnp.where(j < n_lo, j * W_BLOCK, _QD + (j - n_lo) * W_BLOCK)
        return pl.multiple_of(start, BF16_ROWS), 0

    return pl.pallas_call(
        functools.partial(_w_cast_kernel, n_lo=n_lo),
        grid=(n_lo + n_hi,),
        in_specs=[pl.BlockSpec((pl.Element(W_BLOCK), pl.Element(d)), rows)],
        out_specs=pl.BlockSpec((d, W_BLOCK), lambda j: (0, j)),
        out_shape=jax.ShapeDtypeStruct((d, (n_lo + n_hi) * W_BLOCK), BF16),
        compiler_params=_params("parallel"),
        name="w_cast",
    )(w_t)


def _nat_gla_kernel(x_ref, ng_ref, pos_ref, freq_ref, wqk_ref, wv_ref, wr_ref, wga_ref, wgd_ref,
                    wzd_ref, walr_ref, wa2_ref, ba_ref, gain_ref,
                    oa_ref, gates_ref, h_ref, cos_ref, sin_ref, st_ref, *, tiles_per_seq):
    @pl.when(pl.program_id(0) % tiles_per_seq == 0)
    def _():
        st_ref[...] = jnp.zeros_like(st_ref)

    row = lax.broadcasted_iota(jnp.int32, (GLA_BLOCK, GLA_BLOCK), 0)
    col = lax.broadcasted_iota(jnp.int32, (GLA_BLOCK, GLA_BLOCK), 1)
    causal = col <= row
    tri = causal.astype(BF16)
    low = lax.broadcasted_iota(jnp.int32, (GLA_BLOCK // 2, DIL_HD), 1) < DIL_HD // 2

    gate_plan = ((wga_ref, 0, "sigmoid"), (wga_ref, 1, "sigmoid"), (wgd_ref, 0, "sigmoid"),
                 (wgd_ref, 1, "sigmoid"), (wzd_ref, 0, "silu"))

    def gate_tiles(rows, h, lo_j, hi_j):
        for j in range(lo_j, hi_j):
            w_ref, wj, act = gate_plan[j]
            acc = _dot(h, w_ref[:, wj * COL_TILE:(wj + 1) * COL_TILE])
            act_fn = _silu_of_twice if act == "silu" else _sigmoid_of_twice
            gates_ref[rows, j * COL_TILE:(j + 1) * COL_TILE] = act_fn(acc).astype(gates_ref.dtype)

    def recurrence(c, qk, hi, lo, v, r):
        rows = slice(c * GLA_BLOCK, (c + 1) * GLA_BLOCK)
        b = _dot(tri, hi) + _dot(tri, lo)
        b_mid = b[GLA_MID - 1:GLA_MID]
        b_last = b[GLA_BLOCK - 1:GLA_BLOCK]
        q = qk[:, :GLA_QK] * (GLA_DK ** -0.5)
        k = qk[:, GLA_QK:]
        q_in = (q * jnp.exp(b)).astype(BF16)
        q_mid = (q * jnp.exp(b - b_mid)).astype(BF16)
        k_mid = (k * jnp.exp(b_mid - b)).astype(BF16)
        k_end = (k * jnp.exp(b_last - b)).astype(BF16)
        dec = jnp.exp(b_last)
        yield
        heads = [(slice(hh * GLA_DK, (hh + 1) * GLA_DK), slice(hh * GLA_DV, (hh + 1) * GLA_DV))
                 for hh in range(GLA_HEADS)]
        attn = [_dot_nt(q_mid[:, ks], k_mid[:, ks]) for ks, _ in heads]
        kv_t = [_dot_tn(v[:, vs], k_end[:, ks]) for ks, vs in heads]
        yield
        outs = []
        for hh, (ks, vs) in enumerate(heads):
            st = st_ref[hh]
            a = jnp.where(causal, attn[hh], 0.0).astype(BF16)
            outs.append(_dot(a, v[:, vs]) + _dot_nt(q_in[:, ks], st.astype(BF16)))
            st_ref[hh] = st * dec[:, ks] + kv_t[hh]
        yield
        for (_, vs), o in zip(heads, outs):
            ms = jnp.mean(o * o, axis=-1, keepdims=True)
            o = o * lax.rsqrt(ms + EPS) * gain_ref[...] * r[:, vs]
            oa_ref[rows, vs] = o.astype(oa_ref.dtype)
        yield

    def advance(gen):
        if gen is not None:
            for _ in range(per):
                next(gen)

    def rope_tables(c):
        half = GLA_BLOCK // 2
        top = slice(c * GLA_BLOCK, c * GLA_BLOCK + half)
        bot = slice(c * GLA_BLOCK + half, (c + 1) * GLA_BLOCK)
        ang = jnp.where(low, pos_ref[top, :], pos_ref[bot, :]) * freq_ref[...]
        cos, sin = jnp.cos(ang), jnp.sin(ang)
        cos_x, sin_x = pltpu.roll(cos, DIL_HD // 2, 1), pltpu.roll(sin, DIL_HD // 2, 1)
        cos_ref[top, :] = jnp.where(low, cos, cos_x)
        cos_ref[bot, :] = jnp.where(low, cos_x, cos)
        sin_ref[top, :] = jnp.where(low, -sin, sin_x)
        sin_ref[bot, :] = jnp.where(low, -sin_x, sin)

    per = GLA_WIDE // GLA_BLOCK
    gen = late_gates = None
    for wb in range(x_ref.shape[0] // GLA_WIDE):
        rows = slice(wb * GLA_WIDE, (wb + 1) * GLA_WIDE)
        x = x_ref[rows, :]
        ms = jnp.mean(x * x, axis=-1, keepdims=True)
        h = (x * lax.rsqrt(ms + EPS) * ng_ref[...]).astype(BF16)
        h_ref[rows, :] = h
        alr = _dot(h, walr_ref[...]).astype(BF16)
        v = _dot(h, wv_ref[...]).astype(BF16)
        advance(gen)
        for c in range(wb * per, (wb + 1) * per):
            rope_tables(c)
        z = _dot(alr, wa2_ref[...]) + ba_ref[...]
        la = (jnp.minimum(z, 0.0) - jnp.log1p(jnp.exp(-jnp.abs(z)))) * (1.0 / GLA_TAU)
        hi = la.astype(BF16)
        lo = (la - hi.astype(F32)).astype(BF16)
        r = _silu_of_twice(_dot(h, wr_ref[...]))
        advance(gen)
        if late_gates is not None:
            gate_tiles(*late_gates, 0, 2)
        advance(gen)
        qk = _dot(h, wqk_ref[...])
        advance(gen)
        if late_gates is not None:
            gate_tiles(*late_gates, 2, len(gate_plan))
        late_gates = (rows, h)
        blocks = [slice(i * GLA_BLOCK, (i + 1) * GLA_BLOCK) for i in range(per)]
        gen = itertools.chain(*[recurrence(wb * per + i, qk[blk], hi[blk], lo[blk], v[blk], r[blk])
                                for i, blk in enumerate(blocks)])
    advance(gen)
    gate_tiles(*late_gates, 0, 2)
    advance(gen)
    gate_tiles(*late_gates, 2, len(gate_plan))
    for _ in gen:
        pass


def _nat_gla(x2, norm_gain, pos, freq, w, wa2, ba, gain, seq):
    t, d = x2.shape
    tm = GLA_TOK_TILE
    tok = lambda i: (i, 0)
    fixed = lambda i: (0, 0)
    small = (wa2, ba, gain)
    n_gates = 2 * D_MODEL + DIL_OUT
    table = jax.ShapeDtypeStruct((t, DIL_HD), F32)
    return pl.pallas_call(
        functools.partial(_nat_gla_kernel, tiles_per_seq=seq // tm),
        grid=(t // tm,),
        in_specs=[pl.BlockSpec((tm, d), tok), pl.BlockSpec((1, d), fixed),
                  pl.BlockSpec((tm, 1), tok), pl.BlockSpec((1, DIL_HD), fixed),
                  _w_cols(2 * GLA_QK, LO_QK), _w_cols(GLA_V, LO_V), _w_cols(GLA_V, LO_R),
                  _w_cols(D_MODEL, HI_GA), _w_cols(D_MODEL, HI_GD), _w_cols(DIL_OUT, HI_ZD),
                  _w_cols(LANES, LO_ALR)]
        + [pl.BlockSpec(w.shape, fixed) for w in small],
        out_specs=[pl.BlockSpec((tm, GLA_V), tok), pl.BlockSpec((tm, n_gates), tok),
                   pl.BlockSpec((tm, d), tok), pl.BlockSpec((tm, DIL_HD), tok),
                   pl.BlockSpec((tm, DIL_HD), tok)],
        out_shape=[jax.ShapeDtypeStruct((t, GLA_V), BF16), jax.ShapeDtypeStruct((t, n_gates), BF16),
                   jax.ShapeDtypeStruct((t, d), BF16), table, table],
        scratch_shapes=[pltpu.VMEM((GLA_HEADS, GLA_DV, GLA_DK), F32)],
        compiler_params=_params("arbitrary"),
        name="nat_gla",
    )(x2, norm_gain.reshape(1, d), pos, freq, *([w] * 7), *small)


def _dil_proj_kernel(h_ref, wq_ref, wk_ref, wv_ref, g_ref, cos_ref, sin_ref, q_ref, k_ref, v_ref,
                     *scratch, dil):
    n_planes = COL_TILE // LANES
    tm = h_ref.shape[0]
    outs = (q_ref, k_ref, v_ref)

    def store(kind, rows, plane, y):
        if dil == 1:
            outs[kind][0, 0, rows, plane * LANES:(plane + 1) * LANES] = y.astype(q_ref.dtype)
        else:
            scratch[0][kind, plane, rows, :] = y

    def permute(kind, part):
        s0 = min(dil, ROW_STRIDE)
        s1 = dil // s0
        t0 = part * PERMUTE_ROWS
        mid = slice(t0 // s0, (t0 + PERMUTE_ROWS) // s0)
        dst = slice(t0 // dil, (t0 + PERMUTE_ROWS) // dil)
        for plane in range(n_planes):
            cols = slice(plane * LANES, (plane + 1) * LANES)
            for r0 in range(s0):
                hop = scratch[0][kind, plane, pl.ds(t0 + r0, PERMUTE_ROWS // s0, stride=s0), :]
                if s1 == 1:
                    outs[kind][0, r0, dst, cols] = hop.astype(q_ref.dtype)
                    continue
                scratch[1][kind, plane, r0, mid, :] = hop
                for r1 in range(s1):
                    outs[kind][0, s0 * r1 + r0, dst, cols] = scratch[1][
                        kind, plane, r0, pl.ds(mid.start + r1, PERMUTE_ROWS // dil, stride=s1), :
                    ].astype(q_ref.dtype)

    def epilogue(rows, accs):
        for kind, acc in enumerate(accs):
            for hh in range(n_planes):
                xh = acc[:, hh * DIL_HD:(hh + 1) * DIL_HD]
                if kind == 2:
                    store(kind, rows, hh, xh)
                    continue
                ms = jnp.mean(xh * xh, axis=-1, keepdims=True)
                y = xh * lax.rsqrt(ms + EPS) * g_ref[kind:kind + 1, :]
                y = y * cos_ref[rows, :] + pltpu.roll(y, DIL_HD // 2, 1) * sin_ref[rows, :]
                store(kind, rows, hh, y)

    def finish(rows, accs):
        epilogue(rows, accs)
        if dil > 1 and rows.stop % PERMUTE_ROWS == 0:
            for kind in range(3):
                permute(kind, rows.stop // PERMUTE_ROWS - 1)

    pending = None
    for rows in _row_subs(tm, DIL_ROW_SUB):
        h = h_ref[rows, :]
        accs = [_dot(h, w_ref[...]) for w_ref in (wq_ref, wk_ref, wv_ref)]
        if pending is not None:
            finish(*pending)
        pending = (rows, accs)
    finish(*pending)


def _dil_proj(h, w, group, dil, gains, cos, sin, batch):
    t, d = h.shape
    seq = t // batch
    tm = TOK_TILE
    nt = seq // tm
    tok = lambda b, i: (b * nt + i, 0)
    fixed = lambda b, i: (0, 0)
    out = jax.ShapeDtypeStruct((batch, dil, seq // dil, COL_TILE), BF16)
    n_groups = len(DIL_GROUPS)
    assert DIL_HD == LANES and PERMUTE_ROWS % (dil * BF16_ROWS) == 0 and tm % PERMUTE_ROWS == 0
    return pl.pallas_call(
        functools.partial(_dil_proj_kernel, dil=dil),
        grid=(batch, nt),
        in_specs=[pl.BlockSpec((tm, d), tok)]
        + [_w_cols(COL_TILE, HI_QD + kind * n_groups + group) for kind in range(3)]
        + [pl.BlockSpec(gains.shape, fixed),
           pl.BlockSpec((tm, DIL_HD), tok),
           pl.BlockSpec((tm, DIL_HD), tok)],
        out_specs=[pl.BlockSpec((1, dil, tm // dil, COL_TILE), lambda b, i: (b, 0, i, 0))] * 3,
        out_shape=[out] * 3,
        scratch_shapes=[] if dil == 1 else [
            pltpu.VMEM((3, COL_TILE // LANES, tm, LANES), F32),
            pltpu.VMEM((3, COL_TILE // LANES, ROW_STRIDE, tm // ROW_STRIDE, LANES), F32)],
        compiler_params=_params("parallel", "parallel"),
        name=f"dil_proj_d{dil}",
    )(h, w, w, w, gains, cos, sin)


def _dil_attn_kernel(q_ref, kp_ref, kc_ref, vp_ref, vc_ref, o_ref, lse_ref):
    n_qblk = q_ref.shape[1] // ATT_BLOCK
    row = lax.broadcasted_iota(jnp.int32, (ATT_BLOCK, 2 * ATT_BLOCK), 0)
    col = lax.broadcasted_iota(jnp.int32, (ATT_BLOCK, 2 * ATT_BLOCK), 1)
    band = (col >= row) & (col <= row + ATT_BLOCK)
    band_first = band & ((col >= ATT_BLOCK) | (pl.program_id(1) > 0))
    ones = jnp.ones((2 * ATT_BLOCK, DIL_HD), BF16)
    lane = lax.broadcasted_iota(jnp.int32, (ATT_BLOCK, LANES), 1)
    for sub, a in [(sub, a) for sub in range(q_ref.shape[0]) for a in range(n_qblk)]:
        rows = slice(a * ATT_BLOCK, (a + 1) * ATT_BLOCK)
        lse_all = jnp.zeros((ATT_BLOCK, LANES), F32)
        for hh in range(DIL_HEADS):
            sl = slice(hh * DIL_HD, (hh + 1) * DIL_HD)
            if a == 0:
                k_win = jnp.concatenate([kp_ref[sub, :, sl], kc_ref[sub, :ATT_BLOCK, sl]], axis=0)
                v_win = jnp.concatenate([vp_ref[sub, :, sl], vc_ref[sub, :ATT_BLOCK, sl]], axis=0)
            else:
                win = slice((a - 1) * ATT_BLOCK, (a + 1) * ATT_BLOCK)
                k_win = kc_ref[sub, win, sl]
                v_win = vc_ref[sub, win, sl]
            s = _dot_nt(q_ref[sub, rows, sl], k_win)
            s = jnp.where(band_first if a == 0 else band, s, NEG)
            m = jnp.max(s, axis=-1, keepdims=True)
            p = jnp.exp2(s - m).astype(BF16)
            ol = _dot(p, jnp.concatenate([v_win, ones], axis=1))
            l = ol[:, DIL_HD:]
            o_ref[sub, rows, sl] = (ol[:, :DIL_HD] / l).astype(o_ref.dtype)
            lse_all = jnp.where(lane == hh, m + jnp.log(l) * LOG2_E, lse_all)
        lse_ref[sub, rows, :] = lse_all


def _dil_attn(q, k, v, n_sub, dil):
    sub_len = q.size // COL_TILE // n_sub
    qb = min(sub_len, ATT_SUPER)
    ns = ATT_SUPER // qb
    n_qblk = qb // ATT_BLOCK
    q3, k3, v3 = (a.reshape(n_sub, sub_len, COL_TILE) for a in (q, k, v))
    blk = (ns, qb, COL_TILE)
    cur = lambda s, i: (s, i, 0)
    prev = lambda s, i: (s, jnp.maximum(i * n_qblk - 1, 0), 0)
    prev_blk = (ns, ATT_BLOCK, COL_TILE)
    o, lse = pl.pallas_call(
        _dil_attn_kernel,
        grid=(n_sub // ns, sub_len // qb),
        in_specs=[pl.BlockSpec(blk, cur), pl.BlockSpec(prev_blk, prev), pl.BlockSpec(blk, cur),
                  pl.BlockSpec(prev_blk, prev), pl.BlockSpec(blk, cur)],
        out_specs=[pl.BlockSpec(blk, cur), pl.BlockSpec((ns, qb, LANES), cur)],
        out_shape=[jax.ShapeDtypeStruct((n_sub, sub_len, COL_TILE), BF16),
                   jax.ShapeDtypeStruct((n_sub, sub_len, LANES), F32)],
        compiler_params=_params("parallel", "arbitrary"),
        name=f"dil_attn_d{dil}",
    )(q3, k3, k3, v3, v3)
    return o, lse


def _merge_kernel(x_ref, oa_ref, o0_ref, o1_ref, o2_ref, l0_ref, l1_ref, l2_ref,
                  sz_ref, sga_ref, sgd_ref, wga_ref, wdo_ref, wo_ref, out_ref, og_ref, lg_ref, hop_ref):
    tm = x_ref.shape[1]
    for g, ((_, dil), o_ref, l_ref) in enumerate(zip(DIL_GROUPS, (o0_ref, o1_ref, o2_ref),
                                                     (l0_ref, l1_ref, l2_ref))):
        s0 = min(dil, ROW_STRIDE)
        s1 = dil // s0
        planes = [(lg_ref.at[g], lambda r: l_ref[0, r])] + [
            (og_ref.at[g, hh], lambda r, hh=hh: o_ref[0, r, :, hh * DIL_HD:(hh + 1) * DIL_HD].astype(F32))
            for hh in range(DIL_HEADS)]
        for p, (dst_ref, src) in enumerate(planes):
            for r0 in range(s0):
                if s1 == 1:
                    dst_ref[pl.ds(r0, tm // s0, stride=s0), :] = src(r0)
                    continue
                for r1 in range(s1):
                    hop_ref[p, pl.ds(r1, tm // dil, stride=s1), :] = src(s0 * r1 + r0)
                dst_ref[pl.ds(r0, tm // s0, stride=s0), :] = hop_ref[p, 0:tm // s0, :]
    n_groups = len(DIL_GROUPS)

    def combine(rows):
        lses = [lg_ref[g, rows, :] for g in range(n_groups)]
        m = jnp.maximum(jnp.maximum(lses[0], lses[1]), lses[2])
        es = [jnp.exp2(l - m) for l in lses]
        den = es[0] + es[1] + es[2]
        ws = [e / den for e in es]
        heads = []
        for hh in range(DIL_HEADS):
            acc = ws[0][:, hh:hh + 1] * og_ref[0, hh, rows, :]
            for g in range(1, n_groups):
                acc = acc + ws[g][:, hh:hh + 1] * og_ref[g, hh, rows, :]
            heads.append(acc)
        return (jnp.concatenate(heads, axis=-1) * sz_ref[rows, :].astype(F32)).astype(BF16)

    def branches(rows, o_d):
        y_a = _dot(oa_ref[rows, :], wga_ref[...])
        y_d = _dot(o_d, wdo_ref[...])
        return y_a, y_d

    def gate(rows, y_a, y_d):
        return (sga_ref[rows, :].astype(F32) * y_a + sgd_ref[rows, :].astype(F32) * y_d).astype(BF16)

    def project(rows, y):
        out_ref[0, rows, :] = x_ref[0, rows, :] + _dot(y, wo_ref[...])

    for rows in _row_subs(tm):
        project(rows, gate(rows, *branches(rows, combine(rows))))


def _merge(x, o_a, o_ds, lses, gates, wga, wdo, wo):
    batch, seq, d = x.shape
    tm = MERGE_TOK_TILE
    nt = seq // tm
    tok = lambda b, i: (b * nt + i, 0)
    fixed = lambda b, i: (0, 0)
    dil_spec = lambda dil, w: pl.BlockSpec((1, dil, tm // dil, w), lambda b, i: (b, 0, i, 0))
    o_ds = [o.reshape(batch, dil, seq // dil, DIL_OUT) for o, (_, dil) in zip(o_ds, DIL_GROUPS)]
    lses = [l.reshape(batch, dil, seq // dil, LANES) for l, (_, dil) in zip(lses, DIL_GROUPS)]
    return pl.pallas_call(
        _merge_kernel,
        grid=(batch, nt),
        in_specs=[pl.BlockSpec((1, tm, d), lambda b, i: (b, i, 0)),
                  pl.BlockSpec((tm, GLA_V), tok)]
        + [dil_spec(dil, DIL_OUT) for _, dil in DIL_GROUPS]
        + [dil_spec(dil, LANES) for _, dil in DIL_GROUPS]
        + [pl.BlockSpec((tm, DIL_OUT), lambda b, i: (b * nt + i, GATE_ZD)),
           pl.BlockSpec((tm, d), lambda b, i: (b * nt + i, GATE_GA * COL_TILE // D_MODEL)),
           pl.BlockSpec((tm, d), lambda b, i: (b * nt + i, GATE_GD * COL_TILE // D_MODEL)),
           pl.BlockSpec(wga.shape, fixed),
           pl.BlockSpec(wdo.shape, fixed),
           pl.BlockSpec(wo.shape, fixed)],
        out_specs=pl.BlockSpec((1, tm, d), lambda b, i: (b, i, 0)),
        out_shape=jax.ShapeDtypeStruct(x.shape, x.dtype),
        scratch_shapes=[pltpu.VMEM((len(DIL_GROUPS), DIL_HEADS, tm, DIL_HD), F32),
                        pltpu.VMEM((len(DIL_GROUPS), tm, LANES), F32),
                        pltpu.VMEM((DIL_HEADS + 1, tm // ROW_STRIDE, LANES), F32)],
        compiler_params=_params("parallel", "parallel"),
        name="merge_out",
    )(x, o_a, *o_ds, *lses, gates, gates, gates, wga, wdo, wo)


def kernel(x, positions, norm_gain, w_in, gla_w_a2, gla_b_a, gla_out_gain, dil_q_gain, dil_k_gain,
           w_gla_out, w_dil_out, w_o):
    batch, seq, d = x.shape
    t = batch * seq
    half = DIL_HD // 2
    inv_freq = ROPE_THETA ** (-jnp.arange(half, dtype=F32) / half)
    freq = jnp.concatenate([inv_freq, inv_freq]).reshape(1, DIL_HD)
    pos = positions.astype(F32).reshape(t, 1)
    for layer in range(norm_gain.shape[0]):
        w = _w_cast(jnp.swapaxes(w_in[layer], 0, 1))
        wa2 = jnp.pad(gla_w_a2[layer], ((0, LANES - GLA_RANK), (0, 0))).astype(BF16)
        ba = gla_b_a[layer].reshape(1, GLA_QK)
        g_dqk = jnp.stack([dil_q_gain[layer] * (DIL_HD ** -0.5 * LOG2_E), dil_k_gain[layer]])

        o_a, gates, h, cos, sin = _nat_gla(x.reshape(t, d), norm_gain[layer], pos, freq, w,
                                           wa2, ba, gla_out_gain[layer].reshape(1, GLA_DV), seq)
        o_ds, lses = [], []
        for g, (win, dil) in enumerate(DIL_GROUPS):
            assert win // dil == ATT_BLOCK
            q_g, k_g, v_g = _dil_proj(h, w, g, dil, g_dqk, cos, sin, batch)
            o_g, lse_g = _dil_attn(q_g, k_g, v_g, batch * dil, dil)
            o_ds.append(o_g)
            lses.append(lse_g)
        x = _merge(x, o_a, o_ds, lses, gates,
                   w_gla_out[layer].astype(BF16), w_dil_out[layer].astype(BF16),
                   w_o[layer].astype(BF16))
    return x
```

```python
import functools
import itertools

import jax
import jax.numpy as jnp
from jax import lax
from jax.experimental import pallas as pl
from jax.experimental.pallas import tpu as pltpu

D_MODEL = 1024
EPS = 1e-6
ROPE_THETA = 10000.0
GLA_HEADS = 4
GLA_DK = 128
GLA_DV = 256
GLA_RANK = 16
GLA_TAU = 16.0
GLA_QK = GLA_HEADS * GLA_DK
GLA_V = GLA_HEADS * GLA_DV
DIL_GROUPS = ((128, 1), (512, 4), (2048, 16))
DIL_HEADS = 4
DIL_HD = 128
DIL_QK = len(DIL_GROUPS) * DIL_HEADS * DIL_HD
DIL_OUT = DIL_HEADS * DIL_HD
IN_SPLIT_SIZES = (GLA_QK, GLA_QK, GLA_V, GLA_V, GLA_RANK,
                  DIL_QK, DIL_QK, DIL_QK, DIL_OUT, D_MODEL, D_MODEL)

LANES = 128
BF16_ROWS = 16
ROW_STRIDE = 4
GLA_BLOCK = 128
GLA_MID = GLA_BLOCK // 2
ATT_BLOCK = 128
ATT_SUPER = 2048
TOK_TILE = 1024
DIL_ROW_SUB = 256
PERMUTE_ROWS = 512
GLA_TOK_TILE = 512
GLA_WIDE = 128
MERGE_TOK_TILE = 512
MERGE_ROW_SUB = 512
COL_TILE = DIL_HEADS * DIL_HD
VMEM_LIMIT_BYTES = 56 * 1024 * 1024

F32 = jnp.float32
BF16 = jnp.bfloat16
NEG = -1e30
LOG2_E = 1.4426950408889634

GATE_GA, GATE_GD, GATE_ZD = 0, 2, 4

(_QA, _KA, _VA, _RA, _ALR, _QD, _KD, _VD, _ZD, _GA, _GD) = (
    sum(IN_SPLIT_SIZES[:i]) for i in range(len(IN_SPLIT_SIZES)))


def _block_index(offset, width):
    assert offset % width == 0
    return offset // width


W_BLOCK = 1024
LO_PAD = -(-(_ALR + LANES) // W_BLOCK) * W_BLOCK
LO_QK, LO_V, LO_R = _block_index(_QA, 2 * GLA_QK), _block_index(_VA, GLA_V), _block_index(_RA, GLA_V)
LO_ALR = _block_index(_ALR, LANES)
HI_QD = _block_index(LO_PAD, COL_TILE)
HI_ZD = _block_index(LO_PAD + _ZD - _QD, DIL_OUT)
HI_GA, HI_GD = _block_index(LO_PAD + _GA - _QD, D_MODEL), _block_index(LO_PAD + _GD - _QD, D_MODEL)
assert _KA == _QA + GLA_QK and _KD - _QD == DIL_QK and _VD - _KD == DIL_QK


def _w_cols(width, index):
    return pl.BlockSpec((D_MODEL, width), lambda *_: (0, index))


def _params(*sem):
    return pltpu.CompilerParams(dimension_semantics=sem, vmem_limit_bytes=VMEM_LIMIT_BYTES)


def _dot(a, b):
    return jnp.dot(a, b, preferred_element_type=F32)


def _dot_nt(a, b):
    return lax.dot_general(a, b, (((1,), (1,)), ((), ())), preferred_element_type=F32)


def _dot_tn(a, b):
    return lax.dot_general(a, b, (((0,), (0,)), ((), ())), preferred_element_type=F32)


def _sigmoid_of_twice(half_x):
    return 0.5 * jnp.tanh(half_x) + 0.5


def _silu_of_twice(half_x):
    return half_x * jnp.tanh(half_x) + half_x


def _row_subs(n, sub):
    return [slice(r * sub, (r + 1) * sub) for r in range(n // sub)]


def _w_cast_kernel(wt_ref, o_ref, *, n_lo):
    j = pl.program_id(0)
    start = jnp.where(j < n_lo, j * W_BLOCK, _QD + (j - n_lo) * W_BLOCK)
    col = start + lax.broadcasted_iota(jnp.int32, (W_BLOCK, 1), 0)
    is_gate = ((col >= _RA) & (col < _ALR)) | (col >= _ZD)
    o_ref[...] = (wt_ref[...] * jnp.where(is_gate, 0.5, 1.0)).T.astype(o_ref.dtype)


def _w_cast(w_t):
    n, d = w_t.shape
    n_lo = _block_index(LO_PAD, W_BLOCK)
    n_hi = _block_index(n - _QD, W_BLOCK)
    assert _QD % BF16_ROWS == 0 and LO_PAD <= n

    def rows(j):
        start = jnp.where(j < n_lo, j * W_BLOCK, _QD + (j - n_lo) * W_BLOCK)
        return pl.multiple_of(start, BF16_ROWS), 0

    return pl.pallas_call(
        functools.partial(_w_cast_kernel, n_lo=n_lo),
        grid=(n_lo + n_hi,),
        in_specs=[pl.BlockSpec((pl.Element(W_BLOCK), pl.Element(d)), rows)],
        out_specs=pl.BlockSpec((d, W_BLOCK), lambda j: (0, j)),
        out_shape=jax.ShapeDtypeStruct((d, (n_lo + n_hi) * W_BLOCK), BF16),
        compiler_params=_params("parallel"),
        name="w_cast",
    )(w_t)


def _nat_gla_kernel(x_ref, ng_ref, pos_ref, freq_ref, wqk_ref, wv_ref, wr_ref, wga_ref, wgd_ref,
                    wzd_ref, walr_ref, wa2_ref, ba_ref, gain_ref,
                    oa_ref, gates_ref, h_ref, cos_ref, sin_ref, st_ref, *, tiles_per_seq):
    @pl.when(pl.program_id(0) % tiles_per_seq == 0)
    def _():
        st_ref[...] = jnp.zeros_like(st_ref)

    row = lax.broadcasted_iota(jnp.int32, (GLA_BLOCK, GLA_BLOCK), 0)
    col = lax.broadcasted_iota(jnp.int32, (GLA_BLOCK, GLA_BLOCK), 1)
    causal = col <= row
    tri = causal.astype(BF16)
    low = lax.broadcasted_iota(jnp.int32, (GLA_BLOCK // 2, DIL_HD), 1) < DIL_HD // 2

    gate_plan = ((wga_ref, 0, "sigmoid"), (wga_ref, 1, "sigmoid"), (wgd_ref, 0, "sigmoid"),
                 (wgd_ref, 1, "sigmoid"), (wzd_ref, 0, "silu"))

    def gate_tiles(rows, h, lo_j, hi_j):
        for j in range(lo_j, hi_j):
            w_ref, wj, act = gate_plan[j]
            acc = _dot(h, w_ref[:, wj * COL_TILE:(wj + 1) * COL_TILE])
            act_fn = _silu_of_twice if act == "silu" else _sigmoid_of_twice
            gates_ref[rows, j * COL_TILE:(j + 1) * COL_TILE] = act_fn(acc).astype(gates_ref.dtype)

    def recurrence(c, qk, hi, lo, v, r):
        rows = slice(c * GLA_BLOCK, (c + 1) * GLA_BLOCK)
        b = _dot(tri, hi) + _dot(tri, lo)
        b_mid = b[GLA_MID - 1:GLA_MID]
        b_last = b[GLA_BLOCK - 1:GLA_BLOCK]
        q = qk[:, :GLA_QK] * (GLA_DK ** -0.5)
        k = qk[:, GLA_QK:]
        q_in = (q * jnp.exp(b)).astype(BF16)
        q_mid = (q * jnp.exp(b - b_mid)).astype(BF16)
        k_mid = (k * jnp.exp(b_mid - b)).astype(BF16)
        k_end = (k * jnp.exp(b_last - b)).astype(BF16)
        dec = jnp.exp(b_last)
        yield
        heads = [(slice(hh * GLA_DK, (hh + 1) * GLA_DK), slice(hh * GLA_DV, (hh + 1) * GLA_DV))
                 for hh in range(GLA_HEADS)]
        attn = [_dot_nt(q_mid[:, ks], k_mid[:, ks]) for ks, _ in heads]
        kv_t = [_dot_tn(v[:, vs], k_end[:, ks]) for ks, vs in heads]
        yield
        outs = []
        for hh, (ks, vs) in enumerate(heads):
            st = st_ref[hh]
            a = jnp.where(causal, attn[hh], 0.0).astype(BF16)
            outs.append(_dot(a, v[:, vs]) + _dot_nt(q_in[:, ks], st.astype(BF16)))
            st_ref[hh] = st * dec[:, ks] + kv_t[hh]
        yield
        for (_, vs), o in zip(heads, outs):
            ms = jnp.mean(o * o, axis=-1, keepdims=True)
            o = o * lax.rsqrt(ms + EPS) * gain_ref[...] * r[:, vs]
            oa_ref[rows, vs] = o.astype(oa_ref.dtype)
        yield

    def advance(gen):
        if gen is not None:
            for _ in range(per):
                next(gen)

    def rope_tables(c):
        half = GLA_BLOCK // 2
        top = slice(c * GLA_BLOCK, c * GLA_BLOCK + half)
        bot = slice(c * GLA_BLOCK + half, (c + 1) * GLA_BLOCK)
        ang = jnp.where(low, pos_ref[top, :], pos_ref[bot, :]) * freq_ref[...]
        cos, sin = jnp.cos(ang), jnp.sin(ang)
        cos_x, sin_x = pltpu.roll(cos, DIL_HD // 2, 1), pltpu.roll(sin, DIL_HD // 2, 1)
        cos_ref[top, :] = jnp.where(low, cos, cos_x)
        cos_ref[bot, :] = jnp.where(low, cos_x, cos)
        sin_ref[top, :] = jnp.where(low, -sin, sin_x)
        sin_ref[bot, :] = jnp.where(low, -sin_x, sin)

    per = GLA_WIDE // GLA_BLOCK
    gen = late_gates = None
    for wb in range(x_ref.shape[0] // GLA_WIDE):
        rows = slice(wb * GLA_WIDE, (wb + 1) * GLA_WIDE)
        x = x_ref[rows, :]
        ms = jnp.mean(x * x, axis=-1, keepdims=True)
        h = (x * lax.rsqrt(ms + EPS) * ng_ref[...]).astype(BF16)
        h_ref[rows, :] = h
        alr = _dot(h, walr_ref[...]).astype(BF16)
        v = _dot(h, wv_ref[...]).astype(BF16)
        advance(gen)
        for c in range(wb * per, (wb + 1) * per):
            rope_tables(c)
        z = _dot(alr, wa2_ref[...]) + ba_ref[...]
        la = (jnp.minimum(z, 0.0) - jnp.log1p(jnp.exp(-jnp.abs(z)))) * (1.0 / GLA_TAU)
        hi = la.astype(BF16)
        lo = (la - hi.astype(F32)).astype(BF16)
        r = _silu_of_twice(_dot(h, wr_ref[...]))
        advance(gen)
        if late_gates is not None:
            gate_tiles(*late_gates, 0, 2)
        advance(gen)
        qk = _dot(h, wqk_ref[...])
        advance(gen)
        if late_gates is not None:
            gate_tiles(*late_gates, 2, len(gate_plan))
        late_gates = (rows, h)
        blocks = [slice(i * GLA_BLOCK, (i + 1) * GLA_BLOCK) for i in range(per)]
        gen = itertools.chain(*[recurrence(wb * per + i, qk[blk], hi[blk], lo[blk], v[blk], r[blk])
                                for i, blk in enumerate(blocks)])
    advance(gen)
    gate_tiles(*late_gates, 0, 2)
    advance(gen)
    gate_tiles(*late_gates, 2, len(gate_plan))
    for _ in gen:
        pass


def _nat_gla(x2, norm_gain, pos, freq, w, wa2, ba, gain, seq):
    t, d = x2.shape
    tm = GLA_TOK_TILE
    tok = lambda i: (i, 0)
    fixed = lambda i: (0, 0)
    small = (wa2, ba, gain)
    n_gates = 2 * D_MODEL + DIL_OUT
    table = jax.ShapeDtypeStruct((t, DIL_HD), F32)
    return pl.pallas_call(
        functools.partial(_nat_gla_kernel, tiles_per_seq=seq // tm),
        grid=(t // tm,),
        in_specs=[pl.BlockSpec((tm, d), tok), pl.BlockSpec((1, d), fixed),
                  pl.BlockSpec((tm, 1), tok), pl.BlockSpec((1, DIL_HD), fixed),
                  _w_cols(2 * GLA_QK, LO_QK), _w_cols(GLA_V, LO_V), _w_cols(GLA_V, LO_R),
                  _w_cols(D_MODEL, HI_GA), _w_cols(D_MODEL, HI_GD), _w_cols(DIL_OUT, HI_ZD),
                  _w_cols(LANES, LO_ALR)]
        + [pl.BlockSpec(w.shape, fixed) for w in small],
        out_specs=[pl.BlockSpec((tm, GLA_V), tok), pl.BlockSpec((tm, n_gates), tok),
                   pl.BlockSpec((tm, d), tok), pl.BlockSpec((tm, DIL_HD), tok),
                   pl.BlockSpec((tm, DIL_HD), tok)],
        out_shape=[jax.ShapeDtypeStruct((t, GLA_V), BF16), jax.ShapeDtypeStruct((t, n_gates), BF16),
                   jax.ShapeDtypeStruct((t, d), BF16), table, table],
        scratch_shapes=[pltpu.VMEM((GLA_HEADS, GLA_DV, GLA_DK), F32)],
        compiler_params=_params("arbitrary"),
        name="nat_gla",
    )(x2, norm_gain.reshape(1, d), pos, freq, *([w] * 7), *small)


def _dil_proj_kernel(h_ref, wq_ref, wk_ref, wv_ref, g_ref, cos_ref, sin_ref, q_ref, k_ref, v_ref,
                     *scratch, dil):
    n_planes = COL_TILE // LANES
    tm = h_ref.shape[0]
    outs = (q_ref, k_ref, v_ref)

    def store(kind, rows, plane, y):
        if dil == 1:
            outs[kind][0, 0, rows, plane * LANES:(plane + 1) * LANES] = y.astype(q_ref.dtype)
        else:
            scratch[0][kind, plane, rows, :] = y

    def permute(kind, part):
        s0 = min(dil, ROW_STRIDE)
        s1 = dil // s0
        t0 = part * PERMUTE_ROWS
        mid = slice(t0 // s0, (t0 + PERMUTE_ROWS) // s0)
        dst = slice(t0 // dil, (t0 + PERMUTE_ROWS) // dil)
        for plane in range(n_planes):
            cols = slice(plane * LANES, (plane + 1) * LANES)
            for r0 in range(s0):
                hop = scratch[0][kind, plane, pl.ds(t0 + r0, PERMUTE_ROWS // s0, stride=s0), :]
                if s1 == 1:
                    outs[kind][0, r0, dst, cols] = hop.astype(q_ref.dtype)
                    continue
                scratch[1][kind, plane, r0, mid, :] = hop
                for r1 in range(s1):
                    outs[kind][0, s0 * r1 + r0, dst, cols] = scratch[1][
                        kind, plane, r0, pl.ds(mid.start + r1, PERMUTE_ROWS // dil, stride=s1), :
                    ].astype(q_ref.dtype)

    def epilogue(rows, accs):
        for kind, acc in enumerate(accs):
            for hh in range(n_planes):
                xh = acc[:, hh * DIL_HD:(hh + 1) * DIL_HD]
                if kind == 2:
                    store(kind, rows, hh, xh)
                    continue
                ms = jnp.mean(xh * xh, axis=-1, keepdims=True)
                y = xh * lax.rsqrt(ms + EPS) * g_ref[kind:kind + 1, :]
                y = y * cos_ref[rows, :] + pltpu.roll(y, DIL_HD // 2, 1) * sin_ref[rows, :]
                store(kind, rows, hh, y)

    def finish(rows, accs):
        epilogue(rows, accs)
        if dil > 1 and rows.stop % PERMUTE_ROWS == 0:
            for kind in range(3):
                permute(kind, rows.stop // PERMUTE_ROWS - 1)

    pending = None
    for rows in _row_subs(tm, DIL_ROW_SUB):
        h = h_ref[rows, :]
        accs = [_dot(h, w_ref[...]) for w_ref in (wq_ref, wk_ref, wv_ref)]
        if pending is not None:
            finish(*pending)
        pending = (rows, accs)
    finish(*pending)


def _dil_proj(h, w, group, dil, gains, cos, sin, batch):
    t, d = h.shape
    seq = t // batch
    tm = TOK_TILE
    nt = seq // tm
    tok = lambda b, i: (b * nt + i, 0)
    fixed = lambda b, i: (0, 0)
    out = jax.ShapeDtypeStruct((batch, dil, seq // dil, COL_TILE), BF16)
    n_groups = len(DIL_GROUPS)
    assert DIL_HD == LANES and PERMUTE_ROWS % (dil * BF16_ROWS) == 0 and tm % PERMUTE_ROWS == 0
    return pl.pallas_call(
        functools.partial(_dil_proj_kernel, dil=dil),
        grid=(batch, nt),
        in_specs=[pl.BlockSpec((tm, d), tok)]
        + [_w_cols(COL_TILE, HI_QD + kind * n_groups + group) for kind in range(3)]
        + [pl.BlockSpec(gains.shape, fixed),
           pl.BlockSpec((tm, DIL_HD), tok),
           pl.BlockSpec((tm, DIL_HD), tok)],
        out_specs=[pl.BlockSpec((1, dil, tm // dil, COL_TILE), lambda b, i: (b, 0, i, 0))] * 3,
        out_shape=[out] * 3,
        scratch_shapes=[] if dil == 1 else [
            pltpu.VMEM((3, COL_TILE // LANES, tm, LANES), F32),
            pltpu.VMEM((3, COL_TILE // LANES, ROW_STRIDE, tm // ROW_STRIDE, LANES), F32)],
        compiler_params=_params("parallel", "parallel"),
        name=f"dil_proj_d{dil}",
    )(h, w, w, w, gains, cos, sin)


def _dil_attn_kernel(q_ref, kp_ref, kc_ref, vp_ref, vc_ref, o_ref, st_ref):
    n_qblk = q_ref.shape[1] // ATT_BLOCK
    row = lax.broadcasted_iota(jnp.int32, (ATT_BLOCK, 2 * ATT_BLOCK), 0)
    col = lax.broadcasted_iota(jnp.int32, (ATT_BLOCK, 2 * ATT_BLOCK), 1)
    band = (col >= row) & (col <= row + ATT_BLOCK)
    band_first = band & ((col >= ATT_BLOCK) | (pl.program_id(1) > 0))
    ones = jnp.ones((2 * ATT_BLOCK, DIL_HD), BF16)
    lane = lax.broadcasted_iota(jnp.int32, (ATT_BLOCK, LANES), 1)
    for sub, a in [(sub, a) for sub in range(q_ref.shape[0]) for a in range(n_qblk)]:
        rows = slice(a * ATT_BLOCK, (a + 1) * ATT_BLOCK)
        stats = jnp.zeros((ATT_BLOCK, LANES), F32)
        for hh in range(DIL_HEADS):
            sl = slice(hh * DIL_HD, (hh + 1) * DIL_HD)
            if a == 0:
                k_win = jnp.concatenate([kp_ref[sub, :, sl], kc_ref[sub, :ATT_BLOCK, sl]], axis=0)
                v_win = jnp.concatenate([vp_ref[sub, :, sl], vc_ref[sub, :ATT_BLOCK, sl]], axis=0)
            else:
                win = slice((a - 1) * ATT_BLOCK, (a + 1) * ATT_BLOCK)
                k_win = kc_ref[sub, win, sl]
                v_win = vc_ref[sub, win, sl]
            s = _dot_nt(q_ref[sub, rows, sl], k_win)
            s = jnp.where(band_first if a == 0 else band, s, NEG)
            m = jnp.max(s, axis=-1, keepdims=True)
            p = jnp.exp2(s - m).astype(BF16)
            ol = _dot(p, jnp.concatenate([v_win, ones], axis=1))
            o_ref[sub, rows, sl] = ol[:, :DIL_HD].astype(o_ref.dtype)
            stats = jnp.where(lane == hh, m, stats)
            stats = jnp.where(lane == DIL_HEADS + hh, ol[:, DIL_HD:], stats)
        st_ref[sub, rows, :] = stats


def _dil_attn(q, k, v, n_sub, dil):
    sub_len = q.size // COL_TILE // n_sub
    qb = min(sub_len, ATT_SUPER)
    ns = ATT_SUPER // qb
    n_qblk = qb // ATT_BLOCK
    q3, k3, v3 = (a.reshape(n_sub, sub_len, COL_TILE) for a in (q, k, v))
    blk = (ns, qb, COL_TILE)
    cur = lambda s, i: (s, i, 0)
    prev = lambda s, i: (s, jnp.maximum(i * n_qblk - 1, 0), 0)
    prev_blk = (ns, ATT_BLOCK, COL_TILE)
    o, lse = pl.pallas_call(
        _dil_attn_kernel,
        grid=(n_sub // ns, sub_len // qb),
        in_specs=[pl.BlockSpec(blk, cur), pl.BlockSpec(prev_blk, prev), pl.BlockSpec(blk, cur),
                  pl.BlockSpec(prev_blk, prev), pl.BlockSpec(blk, cur)],
        out_specs=[pl.BlockSpec(blk, cur), pl.BlockSpec((ns, qb, LANES), cur)],
        out_shape=[jax.ShapeDtypeStruct((n_sub, sub_len, COL_TILE), BF16),
                   jax.ShapeDtypeStruct((n_sub, sub_len, LANES), F32)],
        compiler_params=_params("parallel", "arbitrary"),
        name=f"dil_attn_d{dil}",
    )(q3, k3, k3, v3, v3)
    return o, lse


def _merge_kernel(x_ref, oa_ref, o0_ref, o1_ref, o2_ref, l0_ref, l1_ref, l2_ref,
                  sz_ref, sga_ref, sgd_ref, wga_ref, wdo_ref, wo_ref, out_ref, og_ref, lg_ref, hop_ref):
    tm = x_ref.shape[1]
    for g, ((_, dil), o_ref, l_ref) in enumerate(zip(DIL_GROUPS, (o0_ref, o1_ref, o2_ref),
                                                     (l0_ref, l1_ref, l2_ref))):
        s0 = min(dil, ROW_STRIDE)
        s1 = dil // s0
        planes = [(lg_ref.at[g], lambda r: l_ref[0, r])] + [
            (og_ref.at[g, hh], lambda r, hh=hh: o_ref[0, r, :, hh * DIL_HD:(hh + 1) * DIL_HD].astype(F32))
            for hh in range(DIL_HEADS)]
        for p, (dst_ref, src) in enumerate(planes):
            for r0 in range(s0):
                if s1 == 1:
                    dst_ref[pl.ds(r0, tm // s0, stride=s0), :] = src(r0)
                    continue
                for r1 in range(s1):
                    hop_ref[p, pl.ds(r1, tm // dil, stride=s1), :] = src(s0 * r1 + r0)
                dst_ref[pl.ds(r0, tm // s0, stride=s0), :] = hop_ref[p, 0:tm // s0, :]
    n_groups = len(DIL_GROUPS)

    def combine(rows):
        stats = [lg_ref[g, rows, :] for g in range(n_groups)]
        m = jnp.maximum(jnp.maximum(stats[0], stats[1]), stats[2])
        es = [jnp.exp2(st - m) for st in stats]
        dens = [pltpu.roll(st, LANES - DIL_HEADS, 1) for st in stats]
        den = es[0] * dens[0] + es[1] * dens[1] + es[2] * dens[2]
        ws = [e / den for e in es]
        heads = []
        for hh in range(DIL_HEADS):
            acc = ws[0][:, hh:hh + 1] * og_ref[0, hh, rows, :]
            for g in range(1, n_groups):
                acc = acc + ws[g][:, hh:hh + 1] * og_ref[g, hh, rows, :]
            heads.append(acc)
        return (jnp.concatenate(heads, axis=-1) * sz_ref[rows, :].astype(F32)).astype(BF16)

    def branches(rows, o_d):
        y_a = _dot(oa_ref[rows, :], wga_ref[...])
        y_d = _dot(o_d, wdo_ref[...])
        return y_a, y_d

    def gate(rows, y_a, y_d):
        return (sga_ref[rows, :].astype(F32) * y_a + sgd_ref[rows, :].astype(F32) * y_d).astype(BF16)

    def project(rows, y):
        out_ref[0, rows, :] = x_ref[0, rows, :] + _dot(y, wo_ref[...])

    for rows in _row_subs(tm, MERGE_ROW_SUB):
        project(rows, gate(rows, *branches(rows, combine(rows))))


def _merge(x, o_a, o_ds, lses, gates, wga, wdo, wo):
    batch, seq, d = x.shape
    tm = MERGE_TOK_TILE
    nt = seq // tm
    tok = lambda b, i: (b * nt + i, 0)
    fixed = lambda b, i: (0, 0)
    dil_spec = lambda dil, w: pl.BlockSpec((1, dil, tm // dil, w), lambda b, i: (b, 0, i, 0))
    o_ds = [o.reshape(batch, dil, seq // dil, DIL_OUT) for o, (_, dil) in zip(o_ds, DIL_GROUPS)]
    lses = [l.reshape(batch, dil, seq // dil, LANES) for l, (_, dil) in zip(lses, DIL_GROUPS)]
    return pl.pallas_call(
        _merge_kernel,
        grid=(batch, nt),
        in_specs=[pl.BlockSpec((1, tm, d), lambda b, i: (b, i, 0)),
                  pl.BlockSpec((tm, GLA_V), tok)]
        + [dil_spec(dil, DIL_OUT) for _, dil in DIL_GROUPS]
        + [dil_spec(dil, LANES) for _, dil in DIL_GROUPS]
        + [pl.BlockSpec((tm, DIL_OUT), lambda b, i: (b * nt + i, GATE_ZD)),
           pl.BlockSpec((tm, d), lambda b, i: (b * nt + i, GATE_GA * COL_TILE // D_MODEL)),
           pl.BlockSpec((tm, d), lambda b, i: (b * nt + i, GATE_GD * COL_TILE // D_MODEL)),
           pl.BlockSpec(wga.shape, fixed),
           pl.BlockSpec(wdo.shape, fixed),
           pl.BlockSpec(wo.shape, fixed)],
        out_specs=pl.BlockSpec((1, tm, d), lambda b, i: (b, i, 0)),
        out_shape=jax.ShapeDtypeStruct(x.shape, x.dtype),
        scratch_shapes=[pltpu.VMEM((len(DIL_GROUPS), DIL_HEADS, tm, DIL_HD), F32),
                        pltpu.VMEM((len(DIL_GROUPS), tm, LANES), F32),
                        pltpu.VMEM((DIL_HEADS + 1, tm // ROW_STRIDE, LANES), F32)],
        compiler_params=_params("parallel", "parallel"),
        name="merge_out",
    )(x, o_a, *o_ds, *lses, gates, gates, gates, wga, wdo, wo)


def kernel(x, positions, norm_gain, w_in, gla_w_a2, gla_b_a, gla_out_gain, dil_q_gain, dil_k_gain,
           w_gla_out, w_dil_out, w_o):
    batch, seq, d = x.shape
    t = batch * seq
    half = DIL_HD // 2
    inv_freq = ROPE_THETA ** (-jnp.arange(half, dtype=F32) / half)
    freq = jnp.concatenate([inv_freq, inv_freq]).reshape(1, DIL_HD)
    pos = positions.astype(F32).reshape(t, 1)
    for layer in range(norm_gain.shape[0]):
        w = _w_cast(jnp.swapaxes(w_in[layer], 0, 1))
        wa2 = jnp.pad(gla_w_a2[layer], ((0, LANES - GLA_RANK), (0, 0))).astype(BF16)
        ba = gla_b_a[layer].reshape(1, GLA_QK)
        g_dqk = jnp.stack([dil_q_gain[layer] * (DIL_HD ** -0.5 * LOG2_E), dil_k_gain[layer]])

        o_a, gates, h, cos, sin = _nat_gla(x.reshape(t, d), norm_gain[layer], pos, freq, w,
                                           wa2, ba, gla_out_gain[layer].reshape(1, GLA_DV), seq)
        o_ds, lses = [], []
        for g, (win, dil) in enumerate(DIL_GROUPS):
            assert win // dil == ATT_BLOCK
            q_g, k_g, v_g = _dil_proj(h, w, g, dil, g_dqk, cos, sin, batch)
            o_g, lse_g = _dil_attn(q_g, k_g, v_g, batch * dil, dil)
            o_ds.append(o_g)
            lses.append(lse_g)
        x = _merge(x, o_a, o_ds, lses, gates,
                   w_gla_out[layer].astype(BF16), w_dil_out[layer].astype(BF16),
                   w_o[layer].astype(BF16))
    return x
```

```python
import functools
import itertools

import jax
import jax.numpy as jnp
from jax import lax
from jax.experimental import pallas as pl
from jax.experimental.pallas import tpu as pltpu

D_MODEL = 1024
EPS = 1e-6
ROPE_THETA = 10000.0
GLA_HEADS = 4
GLA_DK = 128
GLA_DV = 256
GLA_RANK = 16
GLA_TAU = 16.0
GLA_QK = GLA_HEADS * GLA_DK
GLA_V = GLA_HEADS * GLA_DV
DIL_GROUPS = ((128, 1), (512, 4), (2048, 16))
DIL_HEADS = 4
DIL_HD = 128
DIL_QK = len(DIL_GROUPS) * DIL_HEADS * DIL_HD
DIL_OUT = DIL_HEADS * DIL_HD
IN_SPLIT_SIZES = (GLA_QK, GLA_QK, GLA_V, GLA_V, GLA_RANK,
                  DIL_QK, DIL_QK, DIL_QK, DIL_OUT, D_MODEL, D_MODEL)

LANES = 128
BF16_ROWS = 16
ROW_STRIDE = 4
GLA_BLOCK = 128
GLA_MID = GLA_BLOCK // 2
ATT_BLOCK = 128
ATT_SUPER = 2048
TOK_TILE = 1024
DIL_ROW_SUB = 256
PERMUTE_ROWS = 512
GLA_TOK_TILE = 512
GLA_WIDE = 128
MERGE_TOK_TILE = 512
MERGE_ROW_SUB = 512
COL_TILE = DIL_HEADS * DIL_HD
VMEM_LIMIT_BYTES = 56 * 1024 * 1024

F32 = jnp.float32
BF16 = jnp.bfloat16
NEG = -1e30
LOG2_E = 1.4426950408889634

GATE_GA, GATE_GD, GATE_ZD = 0, 2, 4

(_QA, _KA, _VA, _RA, _ALR, _QD, _KD, _VD, _ZD, _GA, _GD) = (
    sum(IN_SPLIT_SIZES[:i]) for i in range(len(IN_SPLIT_SIZES)))


def _block_index(offset, width):
    assert offset % width == 0
    return offset // width


W_BLOCK = 1024
LO_PAD = -(-(_ALR + LANES) // W_BLOCK) * W_BLOCK
LO_QK, LO_V, LO_R = _block_index(_QA, 2 * GLA_QK), _block_index(_VA, GLA_V), _block_index(_RA, GLA_V)
LO_ALR = _block_index(_ALR, LANES)
HI_QD = _block_index(LO_PAD, COL_TILE)
HI_ZD = _block_index(LO_PAD + _ZD - _QD, DIL_OUT)
HI_GA, HI_GD = _block_index(LO_PAD + _GA - _QD, D_MODEL), _block_index(LO_PAD + _GD - _QD, D_MODEL)
assert _KA == _QA + GLA_QK and _KD - _QD == DIL_QK and _VD - _KD == DIL_QK


def _w_cols(width, index):
    return pl.BlockSpec((D_MODEL, width), lambda *_: (0, index))


def _params(*sem):
    return pltpu.CompilerParams(dimension_semantics=sem, vmem_limit_bytes=VMEM_LIMIT_BYTES)


def _dot(a, b):
    return jnp.dot(a, b, preferred_element_type=F32)


def _dot_nt(a, b):
    return lax.dot_general(a, b, (((1,), (1,)), ((), ())), preferred_element_type=F32)


def _dot_tn(a, b):
    return lax.dot_general(a, b, (((0,), (0,)), ((), ())), preferred_element_type=F32)


def _sigmoid_of_twice(half_x):
    return 0.5 * jnp.tanh(half_x) + 0.5


def _silu_of_twice(half_x):
    return half_x * jnp.tanh(half_x) + half_x


def _row_subs(n, sub):
    return [slice(r * sub, (r + 1) * sub) for r in range(n // sub)]


def _w_cast_kernel(wt_ref, o_ref, *, n_lo):
    j = pl.program_id(0)
    start = jnp.where(j < n_lo, j * W_BLOCK, _QD + (j - n_lo) * W_BLOCK)
    col = start + lax.broadcasted_iota(jnp.int32, (W_BLOCK, 1), 0)
    is_gate = ((col >= _RA) & (col < _ALR)) | (col >= _ZD)
    o_ref[...] = (wt_ref[...] * jnp.where(is_gate, 0.5, 1.0)).T.astype(o_ref.dtype)


def _w_cast(w_t):
    n, d = w_t.shape
    n_lo = _block_index(LO_PAD, W_BLOCK)
    n_hi = _block_index(n - _QD, W_BLOCK)
    assert _QD % BF16_ROWS == 0 and LO_PAD <= n

    def rows(j):
        start = jnp.where(j < n_lo, j * W_BLOCK, _QD + (j - n_lo) * W_BLOCK)
        return pl.multiple_of(start, BF16_ROWS), 0

    return pl.pallas_call(
        functools.partial(_w_cast_kernel, n_lo=n_lo),
        grid=(n_lo + n_hi,),
        in_specs=[pl.BlockSpec((pl.Element(W_BLOCK), pl.Element(d)), rows)],
        out_specs=pl.BlockSpec((d, W_BLOCK), lambda j: (0, j)),
        out_shape=jax.ShapeDtypeStruct((d, (n_lo + n_hi) * W_BLOCK), BF16),
        compiler_params=_params("parallel"),
        name="w_cast",
    )(w_t)


def _nat_gla_kernel(x_ref, ng_ref, pos_ref, freq_ref, wqk_ref, wv_ref, wr_ref, wga_ref, wgd_ref,
                    wzd_ref, walr_ref, wa2_ref, ba_ref, gain_ref,
                    oa_ref, gates_ref, h_ref, cos_ref, sin_ref, st_ref, *, tiles_per_seq):
    @pl.when(pl.program_id(0) % tiles_per_seq == 0)
    def _():
        st_ref[...] = jnp.zeros_like(st_ref)

    row = lax.broadcasted_iota(jnp.int32, (GLA_BLOCK, GLA_BLOCK), 0)
    col = lax.broadcasted_iota(jnp.int32, (GLA_BLOCK, GLA_BLOCK), 1)
    causal = col <= row
    tri = causal.astype(BF16)
    tri2 = jnp.concatenate([tri, tri], axis=1)
    low = lax.broadcasted_iota(jnp.int32, (GLA_BLOCK // 2, DIL_HD), 1) < DIL_HD // 2

    gate_plan = ((wga_ref, 0, "sigmoid"), (wga_ref, 1, "sigmoid"), (wgd_ref, 0, "sigmoid"),
                 (wgd_ref, 1, "sigmoid"), (wzd_ref, 0, "silu"))

    def gate_tiles(rows, h, lo_j, hi_j):
        for j in range(lo_j, hi_j):
            w_ref, wj, act = gate_plan[j]
            acc = _dot(h, w_ref[:, wj * COL_TILE:(wj + 1) * COL_TILE])
            act_fn = _silu_of_twice if act == "silu" else _sigmoid_of_twice
            gates_ref[rows, j * COL_TILE:(j + 1) * COL_TILE] = act_fn(acc).astype(gates_ref.dtype)

    def recurrence(c, qk, hi, lo, v, r):
        rows = slice(c * GLA_BLOCK, (c + 1) * GLA_BLOCK)
        b = _dot(tri2, jnp.concatenate([hi, lo], axis=0))
        b_mid = b[GLA_MID - 1:GLA_MID]
        b_last = b[GLA_BLOCK - 1:GLA_BLOCK]
        q = qk[:, :GLA_QK] * (GLA_DK ** -0.5)
        k = qk[:, GLA_QK:]
        q_in = (q * jnp.exp(b)).astype(BF16)
        q_mid = (q * jnp.exp(b - b_mid)).astype(BF16)
        k_mid = (k * jnp.exp(b_mid - b)).astype(BF16)
        k_end = (k * jnp.exp(b_last - b)).astype(BF16)
        dec = jnp.exp(b_last)
        yield
        heads = [(slice(hh * GLA_DK, (hh + 1) * GLA_DK), slice(hh * GLA_DV, (hh + 1) * GLA_DV))
                 for hh in range(GLA_HEADS)]
        attn = [_dot_nt(q_mid[:, ks], k_mid[:, ks]) for ks, _ in heads]
        kv = [_dot_tn(k_end[:, ks], v[:, vs]) for ks, vs in heads]
        yield
        outs = []
        for hh, (ks, vs) in enumerate(heads):
            st = st_ref[hh]
            a = jnp.where(causal, attn[hh], 0.0).astype(BF16)
            outs.append(_dot(jnp.concatenate([a, q_in[:, ks]], axis=1),
                             jnp.concatenate([v[:, vs], st.astype(BF16)], axis=0)))
            d_col = jnp.broadcast_to(dec[:, ks], (GLA_DK, GLA_DK)).T
            st_ref[hh] = st * jnp.concatenate([d_col] * (GLA_DV // GLA_DK), axis=1) + kv[hh]
        yield
        for (_, vs), o in zip(heads, outs):
            ms = jnp.mean(o * o, axis=-1, keepdims=True)
            o = o * lax.rsqrt(ms + EPS) * gain_ref[...] * r[:, vs]
            oa_ref[rows, vs] = o.astype(oa_ref.dtype)
        yield

    def advance(gen):
        if gen is not None:
            for _ in range(per):
                next(gen)

    def rope_tables(c):
        half = GLA_BLOCK // 2
        top = slice(c * GLA_BLOCK, c * GLA_BLOCK + half)
        bot = slice(c * GLA_BLOCK + half, (c + 1) * GLA_BLOCK)
        ang = jnp.where(low, pos_ref[top, :], pos_ref[bot, :]) * freq_ref[...]
        cos, sin = jnp.cos(ang), jnp.sin(ang)
        cos_x, sin_x = pltpu.roll(cos, DIL_HD // 2, 1), pltpu.roll(sin, DIL_HD // 2, 1)
        cos_ref[top, :] = jnp.where(low, cos, cos_x)
        cos_ref[bot, :] = jnp.where(low, cos_x, cos)
        sin_ref[top, :] = jnp.where(low, -sin, sin_x)
        sin_ref[bot, :] = jnp.where(low, -sin_x, sin)

    per = GLA_WIDE // GLA_BLOCK
    gen = late_gates = None
    for wb in range(x_ref.shape[0] // GLA_WIDE):
        rows = slice(wb * GLA_WIDE, (wb + 1) * GLA_WIDE)
        x = x_ref[rows, :]
        ms = jnp.mean(x * x, axis=-1, keepdims=True)
        h = (x * lax.rsqrt(ms + EPS) * ng_ref[...]).astype(BF16)
        h_ref[rows, :] = h
        alr = _dot(h, walr_ref[...]).astype(BF16)
        v = _dot(h, wv_ref[...]).astype(BF16)
        advance(gen)
        for c in range(wb * per, (wb + 1) * per):
            rope_tables(c)
        z = _dot(alr, wa2_ref[...]) + ba_ref[...]
        la = (jnp.minimum(z, 0.0) - jnp.log1p(jnp.exp(-jnp.abs(z)))) * (1.0 / GLA_TAU)
        hi = la.astype(BF16)
        lo = (la - hi.astype(F32)).astype(BF16)
        r = _silu_of_twice(_dot(h, wr_ref[...]))
        advance(gen)
        if late_gates is not None:
            gate_tiles(*late_gates, 0, 2)
        advance(gen)
        qk = _dot(h, wqk_ref[...])
        advance(gen)
        if late_gates is not None:
            gate_tiles(*late_gates, 2, len(gate_plan))
        late_gates = (rows, h)
        blocks = [slice(i * GLA_BLOCK, (i + 1) * GLA_BLOCK) for i in range(per)]
        gen = itertools.chain(*[recurrence(wb * per + i, qk[blk], hi[blk], lo[blk], v[blk], r[blk])
                                for i, blk in enumerate(blocks)])
    advance(gen)
    gate_tiles(*late_gates, 0, 2)
    advance(gen)
    gate_tiles(*late_gates, 2, len(gate_plan))
    for _ in gen:
        pass


def _nat_gla(x2, norm_gain, pos, freq, w, wa2, ba, gain, seq):
    t, d = x2.shape
    tm = GLA_TOK_TILE
    tok = lambda i: (i, 0)
    fixed = lambda i: (0, 0)
    small = (wa2, ba, gain)
    n_gates = 2 * D_MODEL + DIL_OUT
    table = jax.ShapeDtypeStruct((t, DIL_HD), F32)
    return pl.pallas_call(
        functools.partial(_nat_gla_kernel, tiles_per_seq=seq // tm),
        grid=(t // tm,),
        in_specs=[pl.BlockSpec((tm, d), tok), pl.BlockSpec((1, d), fixed),
                  pl.BlockSpec((tm, 1), tok), pl.BlockSpec((1, DIL_HD), fixed),
                  _w_cols(2 * GLA_QK, LO_QK), _w_cols(GLA_V, LO_V), _w_cols(GLA_V, LO_R),
                  _w_cols(D_MODEL, HI_GA), _w_cols(D_MODEL, HI_GD), _w_cols(DIL_OUT, HI_ZD),
                  _w_cols(LANES, LO_ALR)]
        + [pl.BlockSpec(w.shape, fixed) for w in small],
        out_specs=[pl.BlockSpec((tm, GLA_V), tok), pl.BlockSpec((tm, n_gates), tok),
                   pl.BlockSpec((tm, d), tok), pl.BlockSpec((tm, DIL_HD), tok),
                   pl.BlockSpec((tm, DIL_HD), tok)],
        out_shape=[jax.ShapeDtypeStruct((t, GLA_V), BF16), jax.ShapeDtypeStruct((t, n_gates), BF16),
                   jax.ShapeDtypeStruct((t, d), BF16), table, table],
        scratch_shapes=[pltpu.VMEM((GLA_HEADS, GLA_DK, GLA_DV), F32)],
        compiler_params=_params("arbitrary"),
        name="nat_gla",
    )(x2, norm_gain.reshape(1, d), pos, freq, *([w] * 7), *small)


def _dil_proj_kernel(h_ref, wq_ref, wk_ref, wv_ref, g_ref, cos_ref, sin_ref, q_ref, k_ref, v_ref,
                     *scratch, dil):
    n_planes = COL_TILE // LANES
    tm = h_ref.shape[0]
    outs = (q_ref, k_ref, v_ref)

    def store(kind, rows, plane, y):
        if dil == 1:
            outs[kind][0, 0, rows, plane * LANES:(plane + 1) * LANES] = y.astype(q_ref.dtype)
        else:
            scratch[0][kind, plane, rows, :] = y

    def permute(kind, part):
        s0 = min(dil, ROW_STRIDE)
        s1 = dil // s0
        t0 = part * PERMUTE_ROWS
        mid = slice(t0 // s0, (t0 + PERMUTE_ROWS) // s0)
        dst = slice(t0 // dil, (t0 + PERMUTE_ROWS) // dil)
        for plane in range(n_planes):
            cols = slice(plane * LANES, (plane + 1) * LANES)
            for r0 in range(s0):
                hop = scratch[0][kind, plane, pl.ds(t0 + r0, PERMUTE_ROWS // s0, stride=s0), :]
                if s1 == 1:
                    outs[kind][0, r0, dst, cols] = hop.astype(q_ref.dtype)
                    continue
                scratch[1][kind, plane, r0, mid, :] = hop
                for r1 in range(s1):
                    outs[kind][0, s0 * r1 + r0, dst, cols] = scratch[1][
                        kind, plane, r0, pl.ds(mid.start + r1, PERMUTE_ROWS // dil, stride=s1), :
                    ].astype(q_ref.dtype)

    def epilogue(rows, accs):
        for kind, acc in enumerate(accs):
            for hh in range(n_planes):
                xh = acc[:, hh * DIL_HD:(hh + 1) * DIL_HD]
                if kind == 2:
                    store(kind, rows, hh, xh)
                    continue
                ms = jnp.mean(xh * xh, axis=-1, keepdims=True)
                y = xh * lax.rsqrt(ms + EPS) * g_ref[kind:kind + 1, :]
                y = y * cos_ref[rows, :] + pltpu.roll(y, DIL_HD // 2, 1) * sin_ref[rows, :]
                store(kind, rows, hh, y)

    def finish(rows, accs):
        epilogue(rows, accs)
        if dil > 1 and rows.stop % PERMUTE_ROWS == 0:
            for kind in range(3):
                permute(kind, rows.stop // PERMUTE_ROWS - 1)

    pending = None
    for rows in _row_subs(tm, DIL_ROW_SUB):
        h = h_ref[rows, :]
        accs = [_dot(h, w_ref[...]) for w_ref in (wq_ref, wk_ref, wv_ref)]
        if pending is not None:
            finish(*pending)
        pending = (rows, accs)
    finish(*pending)


def _dil_proj(h, w, group, dil, gains, cos, sin, batch):
    t, d = h.shape
    seq = t // batch
    tm = TOK_TILE
    nt = seq // tm
    tok = lambda b, i: (b * nt + i, 0)
    fixed = lambda b, i: (0, 0)
    out = jax.ShapeDtypeStruct((batch, dil, seq // dil, COL_TILE), BF16)
    n_groups = len(DIL_GROUPS)
    assert DIL_HD == LANES and PERMUTE_ROWS % (dil * BF16_ROWS) == 0 and tm % PERMUTE_ROWS == 0
    return pl.pallas_call(
        functools.partial(_dil_proj_kernel, dil=dil),
        grid=(batch, nt),
        in_specs=[pl.BlockSpec((tm, d), tok)]
        + [_w_cols(COL_TILE, HI_QD + kind * n_groups + group) for kind in range(3)]
        + [pl.BlockSpec(gains.shape, fixed),
           pl.BlockSpec((tm, DIL_HD), tok),
           pl.BlockSpec((tm, DIL_HD), tok)],
        out_specs=[pl.BlockSpec((1, dil, tm // dil, COL_TILE), lambda b, i: (b, 0, i, 0))] * 3,
        out_shape=[out] * 3,
        scratch_shapes=[] if dil == 1 else [
            pltpu.VMEM((3, COL_TILE // LANES, tm, LANES), F32),
            pltpu.VMEM((3, COL_TILE // LANES, ROW_STRIDE, tm // ROW_STRIDE, LANES), F32)],
        compiler_params=_params("parallel", "parallel"),
        name=f"dil_proj_d{dil}",
    )(h, w, w, w, gains, cos, sin)


def _dil_attn_kernel(q_ref, kp_ref, kc_ref, vp_ref, vc_ref, o_ref, lse_ref):
    n_qblk = q_ref.shape[1] // ATT_BLOCK
    row = lax.broadcasted_iota(jnp.int32, (ATT_BLOCK, 2 * ATT_BLOCK), 0)
    col = lax.broadcasted_iota(jnp.int32, (ATT_BLOCK, 2 * ATT_BLOCK), 1)
    band = (col >= row) & (col <= row + ATT_BLOCK)
    band_first = band & ((col >= ATT_BLOCK) | (pl.program_id(1) > 0))
    ones = jnp.ones((2 * ATT_BLOCK, DIL_HD), BF16)
    lane = lax.broadcasted_iota(jnp.int32, (ATT_BLOCK, LANES), 1)
    for sub, a in [(sub, a) for sub in range(q_ref.shape[0]) for a in range(n_qblk)]:
        rows = slice(a * ATT_BLOCK, (a + 1) * ATT_BLOCK)
        lse_all = jnp.zeros((ATT_BLOCK, LANES), F32)
        for hh in range(DIL_HEADS):
            sl = slice(hh * DIL_HD, (hh + 1) * DIL_HD)
            if a == 0:
                k_win = jnp.concatenate([kp_ref[sub, :, sl], kc_ref[sub, :ATT_BLOCK, sl]], axis=0)
                v_win = jnp.concatenate([vp_ref[sub, :, sl], vc_ref[sub, :ATT_BLOCK, sl]], axis=0)
            else:
                win = slice((a - 1) * ATT_BLOCK, (a + 1) * ATT_BLOCK)
                k_win = kc_ref[sub, win, sl]
                v_win = vc_ref[sub, win, sl]
            s = _dot_nt(q_ref[sub, rows, sl], k_win)
            s = jnp.where(band_first if a == 0 else band, s, NEG)
            m = jnp.max(s, axis=-1, keepdims=True)
            p = jnp.exp2(s - m).astype(BF16)
            ol = _dot(p, jnp.concatenate([v_win, ones], axis=1))
            l = ol[:, DIL_HD:]
            o_ref[sub, rows, sl] = (ol[:, :DIL_HD] / l).astype(o_ref.dtype)
            lse_all = jnp.where(lane == hh, m + jnp.log(l) * LOG2_E, lse_all)
        lse_ref[sub, rows, :] = lse_all


def _dil_attn(q, k, v, n_sub, dil):
    sub_len = q.size // COL_TILE // n_sub
    qb = min(sub_len, ATT_SUPER)
    ns = ATT_SUPER // qb
    n_qblk = qb // ATT_BLOCK
    q3, k3, v3 = (a.reshape(n_sub, sub_len, COL_TILE) for a in (q, k, v))
    blk = (ns, qb, COL_TILE)
    cur = lambda s, i: (s, i, 0)
    prev = lambda s, i: (s, jnp.maximum(i * n_qblk - 1, 0), 0)
    prev_blk = (ns, ATT_BLOCK, COL_TILE)
    o, lse = pl.pallas_call(
        _dil_attn_kernel,
        grid=(n_sub // ns, sub_len // qb),
        in_specs=[pl.BlockSpec(blk, cur), pl.BlockSpec(prev_blk, prev), pl.BlockSpec(blk, cur),
                  pl.BlockSpec(prev_blk, prev), pl.BlockSpec(blk, cur)],
        out_specs=[pl.BlockSpec(blk, cur), pl.BlockSpec((ns, qb, LANES), cur)],
        out_shape=[jax.ShapeDtypeStruct((n_sub, sub_len, COL_TILE), BF16),
                   jax.ShapeDtypeStruct((n_sub, sub_len, LANES), F32)],
        compiler_params=_params("parallel", "arbitrary"),
        name=f"dil_attn_d{dil}",
    )(q3, k3, k3, v3, v3)
    return o, lse


def _merge_kernel(x_ref, oa_ref, o0_ref, o1_ref, o2_ref, l0_ref, l1_ref, l2_ref,
                  sz_ref, sga_ref, sgd_ref, wga_ref, wdo_ref, wo_ref, out_ref, og_ref, lg_ref, hop_ref):
    tm = x_ref.shape[1]
    for g, ((_, dil), o_ref, l_ref) in enumerate(zip(DIL_GROUPS, (o0_ref, o1_ref, o2_ref),
                                                     (l0_ref, l1_ref, l2_ref))):
        s0 = min(dil, ROW_STRIDE)
        s1 = dil // s0
        planes = [(lg_ref.at[g], lambda r: l_ref[0, r])] + [
            (og_ref.at[g, hh], lambda r, hh=hh: o_ref[0, r, :, hh * DIL_HD:(hh + 1) * DIL_HD].astype(F32))
            for hh in range(DIL_HEADS)]
        for p, (dst_ref, src) in enumerate(planes):
            for r0 in range(s0):
                if s1 == 1:
                    dst_ref[pl.ds(r0, tm // s0, stride=s0), :] = src(r0)
                    continue
                for r1 in range(s1):
                    hop_ref[p, pl.ds(r1, tm // dil, stride=s1), :] = src(s0 * r1 + r0)
                dst_ref[pl.ds(r0, tm // s0, stride=s0), :] = hop_ref[p, 0:tm // s0, :]
    n_groups = len(DIL_GROUPS)

    def combine(rows):
        lses = [lg_ref[g, rows, :] for g in range(n_groups)]
        m = jnp.maximum(jnp.maximum(lses[0], lses[1]), lses[2])
        es = [jnp.exp2(l - m) for l in lses]
        den = es[0] + es[1] + es[2]
        ws = [e / den for e in es]
        heads = []
        for hh in range(DIL_HEADS):
            acc = ws[0][:, hh:hh + 1] * og_ref[0, hh, rows, :]
            for g in range(1, n_groups):
                acc = acc + ws[g][:, hh:hh + 1] * og_ref[g, hh, rows, :]
            heads.append(acc)
        return (jnp.concatenate(heads, axis=-1) * sz_ref[rows, :].astype(F32)).astype(BF16)

    def branches(rows, o_d):
        y_a = _dot(oa_ref[rows, :], wga_ref[...])
        y_d = _dot(o_d, wdo_ref[...])
        return y_a, y_d

    def gate(rows, y_a, y_d):
        return (sga_ref[rows, :].astype(F32) * y_a + sgd_ref[rows, :].astype(F32) * y_d).astype(BF16)

    def project(rows, y):
        out_ref[0, rows, :] = x_ref[0, rows, :] + _dot(y, wo_ref[...])

    for rows in _row_subs(tm, MERGE_ROW_SUB):
        project(rows, gate(rows, *branches(rows, combine(rows))))


def _merge(x, o_a, o_ds, lses, gates, wga, wdo, wo):
    batch, seq, d = x.shape
    tm = MERGE_TOK_TILE
    nt = seq // tm
    tok = lambda b, i: (b * nt + i, 0)
    fixed = lambda b, i: (0, 0)
    dil_spec = lambda dil, w: pl.BlockSpec((1, dil, tm // dil, w), lambda b, i: (b, 0, i, 0))
    o_ds = [o.reshape(batch, dil, seq // dil, DIL_OUT) for o, (_, dil) in zip(o_ds, DIL_GROUPS)]
    lses = [l.reshape(batch, dil, seq // dil, LANES) for l, (_, dil) in zip(lses, DIL_GROUPS)]
    return pl.pallas_call(
        _merge_kernel,
        grid=(batch, nt),
        in_specs=[pl.BlockSpec((1, tm, d), lambda b, i: (b, i, 0)),
                  pl.BlockSpec((tm, GLA_V), tok)]
        + [dil_spec(dil, DIL_OUT) for _, dil in DIL_GROUPS]
        + [dil_spec(dil, LANES) for _, dil in DIL_GROUPS]
        + [pl.BlockSpec((tm, DIL_OUT), lambda b, i: (b * nt + i, GATE_ZD)),
           pl.BlockSpec((tm, d), lambda b, i: (b * nt + i, GATE_GA * COL_TILE // D_MODEL)),
           pl.BlockSpec((tm, d), lambda b, i: (b * nt + i, GATE_GD * COL_TILE // D_MODEL)),
           pl.BlockSpec(wga.shape, fixed),
           pl.BlockSpec(wdo.shape, fixed),
           pl.BlockSpec(wo.shape, fixed)],
        out_specs=pl.BlockSpec((1, tm, d), lambda b, i: (b, i, 0)),
        out_shape=jax.ShapeDtypeStruct(x.shape, x.dtype),
        scratch_shapes=[pltpu.VMEM((len(DIL_GROUPS), DIL_HEADS, tm, DIL_HD), F32),
                        pltpu.VMEM((len(DIL_GROUPS), tm, LANES), F32),
                        pltpu.VMEM((DIL_HEADS + 1, tm // ROW_STRIDE, LANES), F32)],
        compiler_params=_params("parallel", "parallel"),
        name="merge_out",
    )(x, o_a, *o_ds, *lses, gates, gates, gates, wga, wdo, wo)


def kernel(x, positions, norm_gain, w_in, gla_w_a2, gla_b_a, gla_out_gain, dil_q_gain, dil_k_gain,
           w_gla_out, w_dil_out, w_o):
    batch, seq, d = x.shape
    t = batch * seq
    half = DIL_HD // 2
    inv_freq = ROPE_THETA ** (-jnp.arange(half, dtype=F32) / half)
    freq = jnp.concatenate([inv_freq, inv_freq]).reshape(1, DIL_HD)
    pos = positions.astype(F32).reshape(t, 1)
    for layer in range(norm_gain.shape[0]):
        w = _w_cast(jnp.swapaxes(w_in[layer], 0, 1))
        wa2 = jnp.pad(gla_w_a2[layer], ((0, LANES - GLA_RANK), (0, 0))).astype(BF16)
        ba = gla_b_a[layer].reshape(1, GLA_QK)
        g_dqk = jnp.stack([dil_q_gain[layer] * (DIL_HD ** -0.5 * LOG2_E), dil_k_gain[layer]])

        o_a, gates, h, cos, sin = _nat_gla(x.reshape(t, d), norm_gain[layer], pos, freq, w,
                                           wa2, ba, gla_out_gain[layer].reshape(1, GLA_DV), seq)
        o_ds, lses = [], []
        for g, (win, dil) in enumerate(DIL_GROUPS):
            assert win // dil == ATT_BLOCK
            q_g, k_g, v_g = _dil_proj(h, w, g, dil, g_dqk, cos, sin, batch)
            o_g, lse_g = _dil_attn(q_g, k_g, v_g, batch * dil, dil)
            o_ds.append(o_g)
            lses.append(lse_g)
        x = _merge(x, o_a, o_ds, lses, gates,
                   w_gla_out[layer].astype(BF16), w_dil_out[layer].astype(BF16),
                   w_o[layer].astype(BF16))
    return x
```

```python
import functools
import itertools

import jax
import jax.numpy as jnp
from jax import lax
from jax.experimental import pallas as pl
from jax.experimental.pallas import tpu as pltpu

D_MODEL = 1024
EPS = 1e-6
ROPE_THETA = 10000.0
GLA_HEADS = 4
GLA_DK = 128
GLA_DV = 256
GLA_RANK = 16
GLA_TAU = 16.0
GLA_QK = GLA_HEADS * GLA_DK
GLA_V = GLA_HEADS * GLA_DV
DIL_GROUPS = ((128, 1), (512, 4), (2048, 16))
DIL_HEADS = 4
DIL_HD = 128
DIL_QK = len(DIL_GROUPS) * DIL_HEADS * DIL_HD
DIL_OUT = DIL_HEADS * DIL_HD
IN_SPLIT_SIZES = (GLA_QK, GLA_QK, GLA_V, GLA_V, GLA_RANK,
                  DIL_QK, DIL_QK, DIL_QK, DIL_OUT, D_MODEL, D_MODEL)

LANES = 128
BF16_ROWS = 16
ROW_STRIDE = 4
GLA_BLOCK = 128
GLA_MID = GLA_BLOCK // 2
ATT_BLOCK = 128
ATT_SUPER = 2048
TOK_TILE = 1024
DIL_ROW_SUB = 256
PERMUTE_ROWS = 512
GLA_TOK_TILE = 512
GLA_WIDE = 128
MERGE_TOK_TILE = 512
MERGE_ROW_SUB = 512
COL_TILE = DIL_HEADS * DIL_HD
VMEM_LIMIT_BYTES = 56 * 1024 * 1024

F32 = jnp.float32
BF16 = jnp.bfloat16
NEG = -1e30
LOG2_E = 1.4426950408889634

GATE_GA, GATE_GD, GATE_ZD = 0, 2, 4

(_QA, _KA, _VA, _RA, _ALR, _QD, _KD, _VD, _ZD, _GA, _GD) = (
    sum(IN_SPLIT_SIZES[:i]) for i in range(len(IN_SPLIT_SIZES)))


def _block_index(offset, width):
    assert offset % width == 0
    return offset // width


W_BLOCK = 1024
LO_PAD = -(-(_ALR + LANES) // W_BLOCK) * W_BLOCK
LO_QK, LO_V, LO_R = _block_index(_QA, 2 * GLA_QK), _block_index(_VA, GLA_V), _block_index(_RA, GLA_V)
LO_ALR = _block_index(_ALR, LANES)
HI_QD = _block_index(LO_PAD, COL_TILE)
HI_ZD = _block_index(LO_PAD + _ZD - _QD, DIL_OUT)
HI_GA, HI_GD = _block_index(LO_PAD + _GA - _QD, D_MODEL), _block_index(LO_PAD + _GD - _QD, D_MODEL)
assert _KA == _QA + GLA_QK and _KD - _QD == DIL_QK and _VD - _KD == DIL_QK


def _w_cols(width, index):
    return pl.BlockSpec((D_MODEL, width), lambda *_: (0, index))


def _params(*sem):
    return pltpu.CompilerParams(dimension_semantics=sem, vmem_limit_bytes=VMEM_LIMIT_BYTES)


def _dot(a, b):
    return jnp.dot(a, b, preferred_element_type=F32)


def _dot_nt(a, b):
    return lax.dot_general(a, b, (((1,), (1,)), ((), ())), preferred_element_type=F32)


def _dot_tn(a, b):
    return lax.dot_general(a, b, (((0,), (0,)), ((), ())), preferred_element_type=F32)


def _sigmoid_of_twice(half_x):
    return 0.5 * jnp.tanh(half_x) + 0.5


def _silu_of_twice(half_x):
    return half_x * jnp.tanh(half_x) + half_x


def _row_subs(n, sub):
    return [slice(r * sub, (r + 1) * sub) for r in range(n // sub)]


def _w_cast_kernel(wt_ref, o_ref, *, n_lo):
    j = pl.program_id(0)
    start = jnp.where(j < n_lo, j * W_BLOCK, _QD + (j - n_lo) * W_BLOCK)
    col = start + lax.broadcasted_iota(jnp.int32, (W_BLOCK, 1), 0)
    is_gate = ((col >= _RA) & (col < _ALR)) | (col >= _ZD)
    o_ref[...] = (wt_ref[...] * jnp.where(is_gate, 0.5, 1.0)).T.astype(o_ref.dtype)


def _w_cast(w_t):
    n, d = w_t.shape
    n_lo = _block_index(LO_PAD, W_BLOCK)
    n_hi = _block_index(n - _QD, W_BLOCK)
    assert _QD % BF16_ROWS == 0 and LO_PAD <= n

    def rows(j):
        start = jnp.where(j < n_lo, j * W_BLOCK, _QD + (j - n_lo) * W_BLOCK)
        return pl.multiple_of(start, BF16_ROWS), 0

    return pl.pallas_call(
        functools.partial(_w_cast_kernel, n_lo=n_lo),
        grid=(n_lo + n_hi,),
        in_specs=[pl.BlockSpec((pl.Element(W_BLOCK), pl.Element(d)), rows)],
        out_specs=pl.BlockSpec((d, W_BLOCK), lambda j: (0, j)),
        out_shape=jax.ShapeDtypeStruct((d, (n_lo + n_hi) * W_BLOCK), BF16),
        compiler_params=_params("parallel"),
        name="w_cast",
    )(w_t)


def _nat_gla_kernel(x_ref, ng_ref, pos_ref, freq_ref, wqk_ref, wv_ref, wr_ref, wga_ref, wgd_ref,
                    wzd_ref, walr_ref, wa2_ref, ba_ref, gain_ref,
                    oa_ref, gates_ref, h_ref, cos_ref, sin_ref, st_ref, *, tiles_per_seq):
    @pl.when(pl.program_id(0) % tiles_per_seq == 0)
    def _():
        st_ref[...] = jnp.zeros_like(st_ref)

    row = lax.broadcasted_iota(jnp.int32, (GLA_BLOCK, GLA_BLOCK), 0)
    col = lax.broadcasted_iota(jnp.int32, (GLA_BLOCK, GLA_BLOCK), 1)
    causal = col <= row
    tri = causal.astype(BF16)
    tri2 = jnp.concatenate([tri, tri], axis=1)
    low = lax.broadcasted_iota(jnp.int32, (GLA_BLOCK // 2, DIL_HD), 1) < DIL_HD // 2

    gate_plan = ((wga_ref, 0, "sigmoid"), (wga_ref, 1, "sigmoid"), (wgd_ref, 0, "sigmoid"),
                 (wgd_ref, 1, "sigmoid"), (wzd_ref, 0, "silu"))

    def gate_tiles(rows, h, lo_j, hi_j):
        for j in range(lo_j, hi_j):
            w_ref, wj, act = gate_plan[j]
            acc = _dot(h, w_ref[:, wj * COL_TILE:(wj + 1) * COL_TILE])
            act_fn = _silu_of_twice if act == "silu" else _sigmoid_of_twice
            gates_ref[rows, j * COL_TILE:(j + 1) * COL_TILE] = act_fn(acc).astype(gates_ref.dtype)

    def recurrence(c, qk, hi, lo, v, r):
        rows = slice(c * GLA_BLOCK, (c + 1) * GLA_BLOCK)
        b = _dot(tri2, jnp.concatenate([hi, lo], axis=0))
        b_mid = b[GLA_MID - 1:GLA_MID]
        b_last = b[GLA_BLOCK - 1:GLA_BLOCK]
        q = qk[:, :GLA_QK] * (GLA_DK ** -0.5)
        k = qk[:, GLA_QK:]
        q_in = (q * jnp.exp(b)).astype(BF16)
        q_mid = (q * jnp.exp(b - b_mid)).astype(BF16)
        k_mid = (k * jnp.exp(b_mid - b)).astype(BF16)
        k_end = (k * jnp.exp(b_last - b)).astype(BF16)
        dec = jnp.exp(b_last)
        yield
        heads = [(slice(hh * GLA_DK, (hh + 1) * GLA_DK), slice(hh * GLA_DV, (hh + 1) * GLA_DV))
                 for hh in range(GLA_HEADS)]
        attn = [_dot_nt(q_mid[:, ks], k_mid[:, ks]) for ks, _ in heads]
        kv = [_dot_tn(k_end[:, ks], v[:, vs]) for ks, vs in heads]
        yield
        outs = []
        for hh, (ks, vs) in enumerate(heads):
            st = st_ref[hh]
            a = jnp.where(causal, attn[hh], 0.0).astype(BF16)
            outs.append(_dot(jnp.concatenate([a, q_in[:, ks]], axis=1),
                             jnp.concatenate([v[:, vs], st.astype(BF16)], axis=0)))
            d_col = jnp.broadcast_to(dec[:, ks], (GLA_DK, GLA_DK)).T
            st_ref[hh] = st * jnp.concatenate([d_col] * (GLA_DV // GLA_DK), axis=1) + kv[hh]
        yield
        for (_, vs), o in zip(heads, outs):
            ms = jnp.mean(o * o, axis=-1, keepdims=True)
            o = o * lax.rsqrt(ms + EPS) * gain_ref[...] * r[:, vs]
            oa_ref[rows, vs] = o.astype(oa_ref.dtype)
        yield

    def advance(gen):
        if gen is not None:
            for _ in range(per):
                next(gen)

    def rope_tables(c):
        half = GLA_BLOCK // 2
        top = slice(c * GLA_BLOCK, c * GLA_BLOCK + half)
        bot = slice(c * GLA_BLOCK + half, (c + 1) * GLA_BLOCK)
        ang = jnp.where(low, pos_ref[top, :], pos_ref[bot, :]) * freq_ref[...]
        cos, sin = jnp.cos(ang), jnp.sin(ang)
        cos_x, sin_x = pltpu.roll(cos, DIL_HD // 2, 1), pltpu.roll(sin, DIL_HD // 2, 1)
        cos_ref[top, :] = jnp.where(low, cos, cos_x)
        cos_ref[bot, :] = jnp.where(low, cos_x, cos)
        sin_ref[top, :] = jnp.where(low, -sin, sin_x)
        sin_ref[bot, :] = jnp.where(low, -sin_x, sin)

    per = GLA_WIDE // GLA_BLOCK
    gen = late_gates = None
    for wb in range(x_ref.shape[0] // GLA_WIDE):
        rows = slice(wb * GLA_WIDE, (wb + 1) * GLA_WIDE)
        x = x_ref[rows, :]
        ms = jnp.mean(x * x, axis=-1, keepdims=True)
        h = (x * lax.rsqrt(ms + EPS) * ng_ref[...]).astype(BF16)
        h_ref[rows, :] = h
        alr = _dot(h, walr_ref[...]).astype(BF16)
        v = _dot(h, wv_ref[...]).astype(BF16)
        advance(gen)
        for c in range(wb * per, (wb + 1) * per):
            rope_tables(c)
        z = _dot(alr, wa2_ref[...]) + ba_ref[...]
        la = (jnp.minimum(z, 0.0) - jnp.log(1.0 + jnp.exp(-jnp.abs(z)))) * (1.0 / GLA_TAU)
        hi = la.astype(BF16)
        lo = (la - hi.astype(F32)).astype(BF16)
        r = _silu_of_twice(_dot(h, wr_ref[...]))
        advance(gen)
        if late_gates is not None:
            gate_tiles(*late_gates, 0, 2)
        advance(gen)
        qk = _dot(h, wqk_ref[...])
        advance(gen)
        if late_gates is not None:
            gate_tiles(*late_gates, 2, len(gate_plan))
        late_gates = (rows, h)
        blocks = [slice(i * GLA_BLOCK, (i + 1) * GLA_BLOCK) for i in range(per)]
        gen = itertools.chain(*[recurrence(wb * per + i, qk[blk], hi[blk], lo[blk], v[blk], r[blk])
                                for i, blk in enumerate(blocks)])
    advance(gen)
    gate_tiles(*late_gates, 0, 2)
    advance(gen)
    gate_tiles(*late_gates, 2, len(gate_plan))
    for _ in gen:
        pass


def _nat_gla(x2, norm_gain, pos, freq, w, wa2, ba, gain, seq):
    t, d = x2.shape
    tm = GLA_TOK_TILE
    tok = lambda i: (i, 0)
    fixed = lambda i: (0, 0)
    small = (wa2, ba, gain)
    n_gates = 2 * D_MODEL + DIL_OUT
    table = jax.ShapeDtypeStruct((t, DIL_HD), F32)
    return pl.pallas_call(
        functools.partial(_nat_gla_kernel, tiles_per_seq=seq // tm),
        grid=(t // tm,),
        in_specs=[pl.BlockSpec((tm, d), tok), pl.BlockSpec((1, d), fixed),
                  pl.BlockSpec((tm, 1), tok), pl.BlockSpec((1, DIL_HD), fixed),
                  _w_cols(2 * GLA_QK, LO_QK), _w_cols(GLA_V, LO_V), _w_cols(GLA_V, LO_R),
                  _w_cols(D_MODEL, HI_GA), _w_cols(D_MODEL, HI_GD), _w_cols(DIL_OUT, HI_ZD),
                  _w_cols(LANES, LO_ALR)]
        + [pl.BlockSpec(w.shape, fixed) for w in small],
        out_specs=[pl.BlockSpec((tm, GLA_V), tok), pl.BlockSpec((tm, n_gates), tok),
                   pl.BlockSpec((tm, d), tok), pl.BlockSpec((tm, DIL_HD), tok),
                   pl.BlockSpec((tm, DIL_HD), tok)],
        out_shape=[jax.ShapeDtypeStruct((t, GLA_V), BF16), jax.ShapeDtypeStruct((t, n_gates), BF16),
                   jax.ShapeDtypeStruct((t, d), BF16), table, table],
        scratch_shapes=[pltpu.VMEM((GLA_HEADS, GLA_DK, GLA_DV), F32)],
        compiler_params=_params("arbitrary"),
        name="nat_gla",
    )(x2, norm_gain.reshape(1, d), pos, freq, *([w] * 7), *small)


def _dil_proj_kernel(h_ref, wq_ref, wk_ref, wv_ref, g_ref, cos_ref, sin_ref, q_ref, k_ref, v_ref,
                     *scratch, dil):
    n_planes = COL_TILE // LANES
    tm = h_ref.shape[0]
    outs = (q_ref, k_ref, v_ref)

    def store(kind, rows, plane, y):
        if dil == 1:
            outs[kind][0, 0, rows, plane * LANES:(plane + 1) * LANES] = y.astype(q_ref.dtype)
        else:
            scratch[0][kind, plane, rows, :] = y

    def permute(kind, part):
        s0 = min(dil, ROW_STRIDE)
        s1 = dil // s0
        t0 = part * PERMUTE_ROWS
        mid = slice(t0 // s0, (t0 + PERMUTE_ROWS) // s0)
        dst = slice(t0 // dil, (t0 + PERMUTE_ROWS) // dil)
        for plane in range(n_planes):
            cols = slice(plane * LANES, (plane + 1) * LANES)
            for r0 in range(s0):
                hop = scratch[0][kind, plane, pl.ds(t0 + r0, PERMUTE_ROWS // s0, stride=s0), :]
                if s1 == 1:
                    outs[kind][0, r0, dst, cols] = hop.astype(q_ref.dtype)
                    continue
                scratch[1][kind, plane, r0, mid, :] = hop
                for r1 in range(s1):
                    outs[kind][0, s0 * r1 + r0, dst, cols] = scratch[1][
                        kind, plane, r0, pl.ds(mid.start + r1, PERMUTE_ROWS // dil, stride=s1), :
                    ].astype(q_ref.dtype)

    def epilogue(rows, accs):
        for kind, acc in enumerate(accs):
            for hh in range(n_planes):
                xh = acc[:, hh * DIL_HD:(hh + 1) * DIL_HD]
                if kind == 2:
                    store(kind, rows, hh, xh)
                    continue
                ms = jnp.mean(xh * xh, axis=-1, keepdims=True)
                y = xh * lax.rsqrt(ms + EPS) * g_ref[kind:kind + 1, :]
                y = y * cos_ref[rows, :] + pltpu.roll(y, DIL_HD // 2, 1) * sin_ref[rows, :]
                store(kind, rows, hh, y)

    def finish(rows, accs):
        epilogue(rows, accs)
        if dil > 1 and rows.stop % PERMUTE_ROWS == 0:
            for kind in range(3):
                permute(kind, rows.stop // PERMUTE_ROWS - 1)

    pending = None
    for rows in _row_subs(tm, DIL_ROW_SUB):
        h = h_ref[rows, :]
        accs = [_dot(h, w_ref[...]) for w_ref in (wq_ref, wk_ref, wv_ref)]
        if pending is not None:
            finish(*pending)
        pending = (rows, accs)
    finish(*pending)


def _dil_proj(h, w, group, dil, gains, cos, sin, batch):
    t, d = h.shape
    seq = t // batch
    tm = TOK_TILE
    nt = seq // tm
    tok = lambda b, i: (b * nt + i, 0)
    fixed = lambda b, i: (0, 0)
    out = jax.ShapeDtypeStruct((batch, dil, seq // dil, COL_TILE), BF16)
    n_groups = len(DIL_GROUPS)
    assert DIL_HD == LANES and PERMUTE_ROWS % (dil * BF16_ROWS) == 0 and tm % PERMUTE_ROWS == 0
    return pl.pallas_call(
        functools.partial(_dil_proj_kernel, dil=dil),
        grid=(batch, nt),
        in_specs=[pl.BlockSpec((tm, d), tok)]
        + [_w_cols(COL_TILE, HI_QD + kind * n_groups + group) for kind in range(3)]
        + [pl.BlockSpec(gains.shape, fixed),
           pl.BlockSpec((tm, DIL_HD), tok),
           pl.BlockSpec((tm, DIL_HD), tok)],
        out_specs=[pl.BlockSpec((1, dil, tm // dil, COL_TILE), lambda b, i: (b, 0, i, 0))] * 3,
        out_shape=[out] * 3,
        scratch_shapes=[] if dil == 1 else [
            pltpu.VMEM((3, COL_TILE // LANES, tm, LANES), F32),
            pltpu.VMEM((3, COL_TILE // LANES, ROW_STRIDE, tm // ROW_STRIDE, LANES), F32)],
        compiler_params=_params("parallel", "parallel"),
        name=f"dil_proj_d{dil}",
    )(h, w, w, w, gains, cos, sin)


def _dil_attn_kernel(q_ref, kp_ref, kc_ref, vp_ref, vc_ref, o_ref, lse_ref):
    n_qblk = q_ref.shape[1] // ATT_BLOCK
    row = lax.broadcasted_iota(jnp.int32, (ATT_BLOCK, 2 * ATT_BLOCK), 0)
    col = lax.broadcasted_iota(jnp.int32, (ATT_BLOCK, 2 * ATT_BLOCK), 1)
    band = (col >= row) & (col <= row + ATT_BLOCK)
    band_first = band & ((col >= ATT_BLOCK) | (pl.program_id(1) > 0))
    ones = jnp.ones((2 * ATT_BLOCK, DIL_HD), BF16)
    lane = lax.broadcasted_iota(jnp.int32, (ATT_BLOCK, LANES), 1)
    for sub, a in [(sub, a) for sub in range(q_ref.shape[0]) for a in range(n_qblk)]:
        rows = slice(a * ATT_BLOCK, (a + 1) * ATT_BLOCK)
        lse_all = jnp.zeros((ATT_BLOCK, LANES), F32)
        for hh in range(DIL_HEADS):
            sl = slice(hh * DIL_HD, (hh + 1) * DIL_HD)
            if a == 0:
                k_win = jnp.concatenate([kp_ref[sub, :, sl], kc_ref[sub, :ATT_BLOCK, sl]], axis=0)
                v_win = jnp.concatenate([vp_ref[sub, :, sl], vc_ref[sub, :ATT_BLOCK, sl]], axis=0)
            else:
                win = slice((a - 1) * ATT_BLOCK, (a + 1) * ATT_BLOCK)
                k_win = kc_ref[sub, win, sl]
                v_win = vc_ref[sub, win, sl]
            s = _dot_nt(q_ref[sub, rows, sl], k_win)
            s = jnp.where(band_first if a == 0 else band, s, NEG)
            m = jnp.max(s, axis=-1, keepdims=True)
            p = jnp.exp2(s - m).astype(BF16)
            ol = _dot(p, jnp.concatenate([v_win, ones], axis=1))
            l = ol[:, DIL_HD:]
            o_ref[sub, rows, sl] = (ol[:, :DIL_HD] / l).astype(o_ref.dtype)
            lse_all = jnp.where(lane == hh, m + jnp.log(l) * LOG2_E, lse_all)
        lse_ref[sub, rows, :] = lse_all


def _dil_attn(q, k, v, n_sub, dil):
    sub_len = q.size // COL_TILE // n_sub
    qb = min(sub_len, ATT_SUPER)
    ns = ATT_SUPER // qb
    n_qblk = qb // ATT_BLOCK
    q3, k3, v3 = (a.reshape(n_sub, sub_len, COL_TILE) for a in (q, k, v))
    blk = (ns, qb, COL_TILE)
    cur = lambda s, i: (s, i, 0)
    prev = lambda s, i: (s, jnp.maximum(i * n_qblk - 1, 0), 0)
    prev_blk = (ns, ATT_BLOCK, COL_TILE)
    o, lse = pl.pallas_call(
        _dil_attn_kernel,
        grid=(n_sub // ns, sub_len // qb),
        in_specs=[pl.BlockSpec(blk, cur), pl.BlockSpec(prev_blk, prev), pl.BlockSpec(blk, cur),
                  pl.BlockSpec(prev_blk, prev), pl.BlockSpec(blk, cur)],
        out_specs=[pl.BlockSpec(blk, cur), pl.BlockSpec((ns, qb, LANES), cur)],
        out_shape=[jax.ShapeDtypeStruct((n_sub, sub_len, COL_TILE), BF16),
                   jax.ShapeDtypeStruct((n_sub, sub_len, LANES), F32)],
        compiler_params=_params("parallel", "arbitrary"),
        name=f"dil_attn_d{dil}",
    )(q3, k3, k3, v3, v3)
    return o, lse


def _merge_kernel(x_ref, oa_ref, o0_ref, o1_ref, o2_ref, l0_ref, l1_ref, l2_ref,
                  sz_ref, sga_ref, sgd_ref, wga_ref, wdo_ref, wo_ref, out_ref, og_ref, lg_ref, hop_ref):
    tm = x_ref.shape[1]
    for g, ((_, dil), o_ref, l_ref) in enumerate(zip(DIL_GROUPS, (o0_ref, o1_ref, o2_ref),
                                                     (l0_ref, l1_ref, l2_ref))):
        s0 = min(dil, ROW_STRIDE)
        s1 = dil // s0
        planes = [(lg_ref.at[g], lambda r: l_ref[0, r])] + [
            (og_ref.at[g, hh], lambda r, hh=hh: o_ref[0, r, :, hh * DIL_HD:(hh + 1) * DIL_HD].astype(F32))
            for hh in range(DIL_HEADS)]
        for p, (dst_ref, src) in enumerate(planes):
            for r0 in range(s0):
                if s1 == 1:
                    dst_ref[pl.ds(r0, tm // s0, stride=s0), :] = src(r0)
                    continue
                for r1 in range(s1):
                    hop_ref[p, pl.ds(r1, tm // dil, stride=s1), :] = src(s0 * r1 + r0)
                dst_ref[pl.ds(r0, tm // s0, stride=s0), :] = hop_ref[p, 0:tm // s0, :]
    n_groups = len(DIL_GROUPS)

    def combine(rows):
        lses = [lg_ref[g, rows, :] for g in range(n_groups)]
        m = jnp.maximum(jnp.maximum(lses[0], lses[1]), lses[2])
        es = [jnp.exp2(l - m) for l in lses]
        den = es[0] + es[1] + es[2]
        ws = [e / den for e in es]
        heads = []
        for hh in range(DIL_HEADS):
            acc = ws[0][:, hh:hh + 1] * og_ref[0, hh, rows, :]
            for g in range(1, n_groups):
                acc = acc + ws[g][:, hh:hh + 1] * og_ref[g, hh, rows, :]
            heads.append(acc)
        return (jnp.concatenate(heads, axis=-1) * sz_ref[rows, :].astype(F32)).astype(BF16)

    def branches(rows, o_d):
        y_a = _dot(oa_ref[rows, :], wga_ref[...])
        y_d = _dot(o_d, wdo_ref[...])
        return y_a, y_d

    def gate(rows, y_a, y_d):
        return (sga_ref[rows, :].astype(F32) * y_a + sgd_ref[rows, :].astype(F32) * y_d).astype(BF16)

    def project(rows, y):
        out_ref[0, rows, :] = x_ref[0, rows, :] + _dot(y, wo_ref[...])

    for rows in _row_subs(tm, MERGE_ROW_SUB):
        project(rows, gate(rows, *branches(rows, combine(rows))))


def _merge(x, o_a, o_ds, lses, gates, wga, wdo, wo):
    batch, seq, d = x.shape
    tm = MERGE_TOK_TILE
    nt = seq // tm
    tok = lambda b, i: (b * nt + i, 0)
    fixed = lambda b, i: (0, 0)
    dil_spec = lambda dil, w: pl.BlockSpec((1, dil, tm // dil, w), lambda b, i: (b, 0, i, 0))
    o_ds = [o.reshape(batch, dil, seq // dil, DIL_OUT) for o, (_, dil) in zip(o_ds, DIL_GROUPS)]
    lses = [l.reshape(batch, dil, seq // dil, LANES) for l, (_, dil) in zip(lses, DIL_GROUPS)]
    return pl.pallas_call(
        _merge_kernel,
        grid=(batch, nt),
        in_specs=[pl.BlockSpec((1, tm, d), lambda b, i: (b, i, 0)),
                  pl.BlockSpec((tm, GLA_V), tok)]
        + [dil_spec(dil, DIL_OUT) for _, dil in DIL_GROUPS]
        + [dil_spec(dil, LANES) for _, dil in DIL_GROUPS]
        + [pl.BlockSpec((tm, DIL_OUT), lambda b, i: (b * nt + i, GATE_ZD)),
           pl.BlockSpec((tm, d), lambda b, i: (b * nt + i, GATE_GA * COL_TILE // D_MODEL)),
           pl.BlockSpec((tm, d), lambda b, i: (b * nt + i, GATE_GD * COL_TILE // D_MODEL)),
           pl.BlockSpec(wga.shape, fixed),
           pl.BlockSpec(wdo.shape, fixed),
           pl.BlockSpec(wo.shape, fixed)],
        out_specs=pl.BlockSpec((1, tm, d), lambda b, i: (b, i, 0)),
        out_shape=jax.ShapeDtypeStruct(x.shape, x.dtype),
        scratch_shapes=[pltpu.VMEM((len(DIL_GROUPS), DIL_HEADS, tm, DIL_HD), F32),
                        pltpu.VMEM((len(DIL_GROUPS), tm, LANES), F32),
                        pltpu.VMEM((DIL_HEADS + 1, tm // ROW_STRIDE, LANES), F32)],
        compiler_params=_params("parallel", "parallel"),
        name="merge_out",
    )(x, o_a, *o_ds, *lses, gates, gates, gates, wga, wdo, wo)


def kernel(x, positions, norm_gain, w_in, gla_w_a2, gla_b_a, gla_out_gain, dil_q_gain, dil_k_gain,
           w_gla_out, w_dil_out, w_o):
    batch, seq, d = x.shape
    t = batch * seq
    half = DIL_HD // 2
    inv_freq = ROPE_THETA ** (-jnp.arange(half, dtype=F32) / half)
    freq = jnp.concatenate([inv_freq, inv_freq]).reshape(1, DIL_HD)
    pos = positions.astype(F32).reshape(t, 1)
    for layer in range(norm_gain.shape[0]):
        w = _w_cast(jnp.swapaxes(w_in[layer], 0, 1))
        wa2 = jnp.pad(gla_w_a2[layer], ((0, LANES - GLA_RANK), (0, 0))).astype(BF16)
        ba = gla_b_a[layer].reshape(1, GLA_QK)
        g_dqk = jnp.stack([dil_q_gain[layer] * (DIL_HD ** -0.5 * LOG2_E), dil_k_gain[layer]])

        o_a, gates, h, cos, sin = _nat_gla(x.reshape(t, d), norm_gain[layer], pos, freq, w,
                                           wa2, ba, gla_out_gain[layer].reshape(1, GLA_DV), seq)
        o_ds, lses = [], []
        for g, (win, dil) in enumerate(DIL_GROUPS):
            assert win // dil == ATT_BLOCK
            q_g, k_g, v_g = _dil_proj(h, w, g, dil, g_dqk, cos, sin, batch)
            o_g, lse_g = _dil_attn(q_g, k_g, v_g, batch * dil, dil)
            o_ds.append(o_g)
            lses.append(lse_g)
        x = _merge(x, o_a, o_ds, lses, gates,
                   w_gla_out[layer].astype(BF16), w_dil_out[layer].astype(BF16),
                   w_o[layer].astype(BF16))
    return x
```

```python
import functools
import itertools

import jax
import jax.numpy as jnp
from jax import lax
from jax.experimental import pallas as pl
from jax.experimental.pallas import tpu as pltpu

D_MODEL = 1024
EPS = 1e-6
ROPE_THETA = 10000.0
GLA_HEADS = 4
GLA_DK = 128
GLA_DV = 256
GLA_RANK = 16
GLA_TAU = 16.0
GLA_QK = GLA_HEADS * GLA_DK
GLA_V = GLA_HEADS * GLA_DV
DIL_GROUPS = ((128, 1), (512, 4), (2048, 16))
DIL_HEADS = 4
DIL_HD = 128
DIL_QK = len(DIL_GROUPS) * DIL_HEADS * DIL_HD
DIL_OUT = DIL_HEADS * DIL_HD
IN_SPLIT_SIZES = (GLA_QK, GLA_QK, GLA_V, GLA_V, GLA_RANK,
                  DIL_QK, DIL_QK, DIL_QK, DIL_OUT, D_MODEL, D_MODEL)

LANES = 128
BF16_ROWS = 16
ROW_STRIDE = 4
GLA_BLOCK = 128
GLA_MID = GLA_BLOCK // 2
ATT_BLOCK = 128
ATT_SUPER = 2048
TOK_TILE = 1024
DIL_ROW_SUB = 256
PERMUTE_ROWS = 512
GLA_TOK_TILE = 512
GLA_WIDE = 128
MERGE_TOK_TILE = 512
MERGE_ROW_SUB = 512
COL_TILE = DIL_HEADS * DIL_HD
VMEM_LIMIT_BYTES = 48 * 1024 * 1024

F32 = jnp.float32
BF16 = jnp.bfloat16
NEG = -1e30
LOG2_E = 1.4426950408889634

GATE_GA, GATE_GD, GATE_ZD = 0, 2, 4

(_QA, _KA, _VA, _RA, _ALR, _QD, _KD, _VD, _ZD, _GA, _GD) = (
    sum(IN_SPLIT_SIZES[:i]) for i in range(len(IN_SPLIT_SIZES)))


def _block_index(offset, width):
    assert offset % width == 0
    return offset // width


W_BLOCK = 1024
LO_PAD = -(-(_ALR + LANES) // W_BLOCK) * W_BLOCK
LO_QK, LO_V, LO_R = _block_index(_QA, 2 * GLA_QK), _block_index(_VA, GLA_V), _block_index(_RA, GLA_V)
LO_ALR = _block_index(_ALR, LANES)
HI_QD = _block_index(LO_PAD, COL_TILE)
HI_ZD = _block_index(LO_PAD + _ZD - _QD, DIL_OUT)
HI_GA, HI_GD = _block_index(LO_PAD + _GA - _QD, D_MODEL), _block_index(LO_PAD + _GD - _QD, D_MODEL)
assert _KA == _QA + GLA_QK and _KD - _QD == DIL_QK and _VD - _KD == DIL_QK


def _w_cols(width, index):
    return pl.BlockSpec((D_MODEL, width), lambda *_: (0, index))


def _params(*sem):
    return pltpu.CompilerParams(dimension_semantics=sem, vmem_limit_bytes=VMEM_LIMIT_BYTES)


def _dot(a, b):
    return jnp.dot(a, b, preferred_element_type=F32)


def _dot_nt(a, b):
    return lax.dot_general(a, b, (((1,), (1,)), ((), ())), preferred_element_type=F32)


def _dot_tn(a, b):
    return lax.dot_general(a, b, (((0,), (0,)), ((), ())), preferred_element_type=F32)


def _sigmoid_of_twice(half_x):
    return 0.5 * jnp.tanh(half_x) + 0.5


def _silu_of_twice(half_x):
    return half_x * jnp.tanh(half_x) + half_x


def _row_subs(n, sub):
    return [slice(r * sub, (r + 1) * sub) for r in range(n // sub)]


def _w_cast_kernel(wt_ref, o_ref, *, n_lo):
    j = pl.program_id(0)
    start = jnp.where(j < n_lo, j * W_BLOCK, _QD + (j - n_lo) * W_BLOCK)
    col = start + lax.broadcasted_iota(jnp.int32, (W_BLOCK, 1), 0)
    is_gate = ((col >= _RA) & (col < _ALR)) | (col >= _ZD)
    o_ref[...] = (wt_ref[...] * jnp.where(is_gate, 0.5, 1.0)).T.astype(o_ref.dtype)


def _w_cast(w_t):
    n, d = w_t.shape
    n_lo = _block_index(LO_PAD, W_BLOCK)
    n_hi = _block_index(n - _QD, W_BLOCK)
    assert _QD % BF16_ROWS == 0 and LO_PAD <= n

    def rows(j):
        start = jnp.where(j < n_lo, j * W_BLOCK, _QD + (j - n_lo) * W_BLOCK)
        return pl.multiple_of(start, BF16_ROWS), 0

    return pl.pallas_call(
        functools.partial(_w_cast_kernel, n_lo=n_lo),
        grid=(n_lo + n_hi,),
        in_specs=[pl.BlockSpec((pl.Element(W_BLOCK), pl.Element(d)), rows)],
        out_specs=pl.BlockSpec((d, W_BLOCK), lambda j: (0, j)),
        out_shape=jax.ShapeDtypeStruct((d, (n_lo + n_hi) * W_BLOCK), BF16),
        compiler_params=_params("parallel"),
        name="w_cast",
    )(w_t)


def _nat_gla_kernel(x_ref, ng_ref, pos_ref, freq_ref, wqk_ref, wv_ref, wr_ref, wga_ref, wgd_ref,
                    wzd_ref, walr_ref, wa2_ref, ba_ref, gain_ref,
                    oa_ref, gates_ref, h_ref, cos_ref, sin_ref, st_ref, *, tiles_per_seq):
    @pl.when(pl.program_id(0) % tiles_per_seq == 0)
    def _():
        st_ref[...] = jnp.zeros_like(st_ref)

    row = lax.broadcasted_iota(jnp.int32, (GLA_BLOCK, GLA_BLOCK), 0)
    col = lax.broadcasted_iota(jnp.int32, (GLA_BLOCK, GLA_BLOCK), 1)
    causal = col <= row
    tri = causal.astype(BF16)
    tri2 = jnp.concatenate([tri, tri], axis=1)
    low = lax.broadcasted_iota(jnp.int32, (GLA_BLOCK // 2, DIL_HD), 1) < DIL_HD // 2

    gate_plan = ((wga_ref, 0, "sigmoid"), (wga_ref, 1, "sigmoid"), (wgd_ref, 0, "sigmoid"),
                 (wgd_ref, 1, "sigmoid"), (wzd_ref, 0, "silu"))

    def gate_tiles(rows, h, lo_j, hi_j):
        for j in range(lo_j, hi_j):
            w_ref, wj, act = gate_plan[j]
            acc = _dot(h, w_ref[:, wj * COL_TILE:(wj + 1) * COL_TILE])
            act_fn = _silu_of_twice if act == "silu" else _sigmoid_of_twice
            gates_ref[rows, j * COL_TILE:(j + 1) * COL_TILE] = act_fn(acc).astype(gates_ref.dtype)

    def recurrence(c, qk, hi, lo, v, r):
        rows = slice(c * GLA_BLOCK, (c + 1) * GLA_BLOCK)
        b = _dot(tri2, jnp.concatenate([hi, lo], axis=0))
        b_mid = b[GLA_MID - 1:GLA_MID]
        b_last = b[GLA_BLOCK - 1:GLA_BLOCK]
        q = qk[:, :GLA_QK] * (GLA_DK ** -0.5)
        k = qk[:, GLA_QK:]
        q_in = (q * jnp.exp(b)).astype(BF16)
        q_mid = (q * jnp.exp(b - b_mid)).astype(BF16)
        k_mid = (k * jnp.exp(b_mid - b)).astype(BF16)
        k_end = (k * jnp.exp(b_last - b)).astype(BF16)
        dec = jnp.exp(b_last)
        yield
        heads = [(slice(hh * GLA_DK, (hh + 1) * GLA_DK), slice(hh * GLA_DV, (hh + 1) * GLA_DV))
                 for hh in range(GLA_HEADS)]
        attn = [_dot_nt(q_mid[:, ks], k_mid[:, ks]) for ks, _ in heads]
        kv = [_dot_tn(k_end[:, ks], v[:, vs]) for ks, vs in heads]
        yield
        outs = []
        for hh, (ks, vs) in enumerate(heads):
            st = st_ref[hh]
            a = jnp.where(causal, attn[hh], 0.0).astype(BF16)
            outs.append(_dot(jnp.concatenate([a, q_in[:, ks]], axis=1),
                             jnp.concatenate([v[:, vs], st.astype(BF16)], axis=0)))
            d_col = jnp.broadcast_to(dec[:, ks], (GLA_DK, GLA_DK)).T
            st_ref[hh] = st * jnp.concatenate([d_col] * (GLA_DV // GLA_DK), axis=1) + kv[hh]
        yield
        for (_, vs), o in zip(heads, outs):
            ms = jnp.mean(o * o, axis=-1, keepdims=True)
            o = o * lax.rsqrt(ms + EPS) * gain_ref[...] * r[:, vs]
            oa_ref[rows, vs] = o.astype(oa_ref.dtype)
        yield

    def advance(gen):
        if gen is not None:
            for _ in range(per):
                next(gen)

    def rope_tables(c):
        half = GLA_BLOCK // 2
        top = slice(c * GLA_BLOCK, c * GLA_BLOCK + half)
        bot = slice(c * GLA_BLOCK + half, (c + 1) * GLA_BLOCK)
        ang = jnp.where(low, pos_ref[top, :], pos_ref[bot, :]) * freq_ref[...]
        cos, sin = jnp.cos(ang), jnp.sin(ang)
        cos_x, sin_x = pltpu.roll(cos, DIL_HD // 2, 1), pltpu.roll(sin, DIL_HD // 2, 1)
        cos_ref[top, :] = jnp.where(low, cos, cos_x)
        cos_ref[bot, :] = jnp.where(low, cos_x, cos)
        sin_ref[top, :] = jnp.where(low, -sin, sin_x)
        sin_ref[bot, :] = jnp.where(low, -sin_x, sin)

    per = GLA_WIDE // GLA_BLOCK
    gen = late_gates = None
    for wb in range(x_ref.shape[0] // GLA_WIDE):
        rows = slice(wb * GLA_WIDE, (wb + 1) * GLA_WIDE)
        x = x_ref[rows, :]
        ms = jnp.mean(x * x, axis=-1, keepdims=True)
        h = (x * lax.rsqrt(ms + EPS) * ng_ref[...]).astype(BF16)
        h_ref[rows, :] = h
        alr = _dot(h, walr_ref[...]).astype(BF16)
        v = _dot(h, wv_ref[...]).astype(BF16)
        advance(gen)
        for c in range(wb * per, (wb + 1) * per):
            rope_tables(c)
        z = _dot(alr, wa2_ref[...]) + ba_ref[...]
        la = (jnp.minimum(z, 0.0) - jnp.log(1.0 + jnp.exp(-jnp.abs(z)))) * (1.0 / GLA_TAU)
        hi = la.astype(BF16)
        lo = (la - hi.astype(F32)).astype(BF16)
        r = _silu_of_twice(_dot(h, wr_ref[...]))
        advance(gen)
        if late_gates is not None:
            gate_tiles(*late_gates, 0, 2)
        advance(gen)
        qk = _dot(h, wqk_ref[...])
        advance(gen)
        if late_gates is not None:
            gate_tiles(*late_gates, 2, len(gate_plan))
        late_gates = (rows, h)
        blocks = [slice(i * GLA_BLOCK, (i + 1) * GLA_BLOCK) for i in range(per)]
        gen = itertools.chain(*[recurrence(wb * per + i, qk[blk], hi[blk], lo[blk], v[blk], r[blk])
                                for i, blk in enumerate(blocks)])
    advance(gen)
    gate_tiles(*late_gates, 0, 2)
    advance(gen)
    gate_tiles(*late_gates, 2, len(gate_plan))
    for _ in gen:
        pass


def _nat_gla(x2, norm_gain, pos, freq, w, wa2, ba, gain, seq):
    t, d = x2.shape
    tm = GLA_TOK_TILE
    tok = lambda i: (i, 0)
    fixed = lambda i: (0, 0)
    small = (wa2, ba, gain)
    n_gates = 2 * D_MODEL + DIL_OUT
    table = jax.ShapeDtypeStruct((t, DIL_HD), F32)
    return pl.pallas_call(
        functools.partial(_nat_gla_kernel, tiles_per_seq=seq // tm),
        grid=(t // tm,),
        in_specs=[pl.BlockSpec((tm, d), tok), pl.BlockSpec((1, d), fixed),
                  pl.BlockSpec((tm, 1), tok), pl.BlockSpec((1, DIL_HD), fixed),
                  _w_cols(2 * GLA_QK, LO_QK), _w_cols(GLA_V, LO_V), _w_cols(GLA_V, LO_R),
                  _w_cols(D_MODEL, HI_GA), _w_cols(D_MODEL, HI_GD), _w_cols(DIL_OUT, HI_ZD),
                  _w_cols(LANES, LO_ALR)]
        + [pl.BlockSpec(w.shape, fixed) for w in small],
        out_specs=[pl.BlockSpec((tm, GLA_V), tok), pl.BlockSpec((tm, n_gates), tok),
                   pl.BlockSpec((tm, d), tok), pl.BlockSpec((tm, DIL_HD), tok),
                   pl.BlockSpec((tm, DIL_HD), tok)],
        out_shape=[jax.ShapeDtypeStruct((t, GLA_V), BF16), jax.ShapeDtypeStruct((t, n_gates), BF16),
                   jax.ShapeDtypeStruct((t, d), BF16), table, table],
        scratch_shapes=[pltpu.VMEM((GLA_HEADS, GLA_DK, GLA_DV), F32)],
        compiler_params=_params("arbitrary"),
        name="nat_gla",
    )(x2, norm_gain.reshape(1, d), pos, freq, *([w] * 7), *small)


def _dil_proj_kernel(h_ref, wq_ref, wk_ref, wv_ref, g_ref, cos_ref, sin_ref, q_ref, k_ref, v_ref,
                     *scratch, dil):
    n_planes = COL_TILE // LANES
    tm = h_ref.shape[0]
    outs = (q_ref, k_ref, v_ref)

    def store(kind, rows, plane, y):
        if dil == 1:
            outs[kind][0, 0, rows, plane * LANES:(plane + 1) * LANES] = y.astype(q_ref.dtype)
        else:
            scratch[0][kind, plane, rows, :] = y

    def permute(kind, part):
        s0 = min(dil, ROW_STRIDE)
        s1 = dil // s0
        t0 = part * PERMUTE_ROWS
        mid = slice(t0 // s0, (t0 + PERMUTE_ROWS) // s0)
        dst = slice(t0 // dil, (t0 + PERMUTE_ROWS) // dil)
        for plane in range(n_planes):
            cols = slice(plane * LANES, (plane + 1) * LANES)
            for r0 in range(s0):
                hop = scratch[0][kind, plane, pl.ds(t0 + r0, PERMUTE_ROWS // s0, stride=s0), :]
                if s1 == 1:
                    outs[kind][0, r0, dst, cols] = hop.astype(q_ref.dtype)
                    continue
                scratch[1][kind, plane, r0, mid, :] = hop
                for r1 in range(s1):
                    outs[kind][0, s0 * r1 + r0, dst, cols] = scratch[1][
                        kind, plane, r0, pl.ds(mid.start + r1, PERMUTE_ROWS // dil, stride=s1), :
                    ].astype(q_ref.dtype)

    def epilogue(rows, accs):
        for kind, acc in enumerate(accs):
            for hh in range(n_planes):
                xh = acc[:, hh * DIL_HD:(hh + 1) * DIL_HD]
                if kind == 2:
                    store(kind, rows, hh, xh)
                    continue
                ms = jnp.mean(xh * xh, axis=-1, keepdims=True)
                y = xh * lax.rsqrt(ms + EPS) * g_ref[kind:kind + 1, :]
                y = y * cos_ref[rows, :] + pltpu.roll(y, DIL_HD // 2, 1) * sin_ref[rows, :]
                store(kind, rows, hh, y)

    def finish(rows, accs):
        epilogue(rows, accs)
        if dil > 1 and rows.stop % PERMUTE_ROWS == 0:
            for kind in range(3):
                permute(kind, rows.stop // PERMUTE_ROWS - 1)

    pending = None
    for rows in _row_subs(tm, DIL_ROW_SUB):
        h = h_ref[rows, :]
        accs = [_dot(h, w_ref[...]) for w_ref in (wq_ref, wk_ref, wv_ref)]
        if pending is not None:
            finish(*pending)
        pending = (rows, accs)
    finish(*pending)


def _dil_proj(h, w, group, dil, gains, cos, sin, batch):
    t, d = h.shape
    seq = t // batch
    tm = TOK_TILE
    nt = seq // tm
    tok = lambda b, i: (b * nt + i, 0)
    fixed = lambda b, i: (0, 0)
    out = jax.ShapeDtypeStruct((batch, dil, seq // dil, COL_TILE), BF16)
    n_groups = len(DIL_GROUPS)
    assert DIL_HD == LANES and PERMUTE_ROWS % (dil * BF16_ROWS) == 0 and tm % PERMUTE_ROWS == 0
    return pl.pallas_call(
        functools.partial(_dil_proj_kernel, dil=dil),
        grid=(batch, nt),
        in_specs=[pl.BlockSpec((tm, d), tok)]
        + [_w_cols(COL_TILE, HI_QD + kind * n_groups + group) for kind in range(3)]
        + [pl.BlockSpec(gains.shape, fixed),
           pl.BlockSpec((tm, DIL_HD), tok),
           pl.BlockSpec((tm, DIL_HD), tok)],
        out_specs=[pl.BlockSpec((1, dil, tm // dil, COL_TILE), lambda b, i: (b, 0, i, 0))] * 3,
        out_shape=[out] * 3,
        scratch_shapes=[] if dil == 1 else [
            pltpu.VMEM((3, COL_TILE // LANES, tm, LANES), F32),
            pltpu.VMEM((3, COL_TILE // LANES, ROW_STRIDE, tm // ROW_STRIDE, LANES), F32)],
        compiler_params=_params("parallel", "parallel"),
        name=f"dil_proj_d{dil}",
    )(h, w, w, w, gains, cos, sin)


def _dil_attn_kernel(q_ref, kp_ref, kc_ref, vp_ref, vc_ref, o_ref, lse_ref):
    n_qblk = q_ref.shape[1] // ATT_BLOCK
    row = lax.broadcasted_iota(jnp.int32, (ATT_BLOCK, 2 * ATT_BLOCK), 0)
    col = lax.broadcasted_iota(jnp.int32, (ATT_BLOCK, 2 * ATT_BLOCK), 1)
    band = (col >= row) & (col <= row + ATT_BLOCK)
    band_first = band & ((col >= ATT_BLOCK) | (pl.program_id(1) > 0))
    ones = jnp.ones((2 * ATT_BLOCK, DIL_HD), BF16)
    lane = lax.broadcasted_iota(jnp.int32, (ATT_BLOCK, LANES), 1)
    for sub, a in [(sub, a) for sub in range(q_ref.shape[0]) for a in range(n_qblk)]:
        rows = slice(a * ATT_BLOCK, (a + 1) * ATT_BLOCK)
        lse_all = jnp.zeros((ATT_BLOCK, LANES), F32)
        for hh in range(DIL_HEADS):
            sl = slice(hh * DIL_HD, (hh + 1) * DIL_HD)
            if a == 0:
                k_win = jnp.concatenate([kp_ref[sub, :, sl], kc_ref[sub, :ATT_BLOCK, sl]], axis=0)
                v_win = jnp.concatenate([vp_ref[sub, :, sl], vc_ref[sub, :ATT_BLOCK, sl]], axis=0)
            else:
                win = slice((a - 1) * ATT_BLOCK, (a + 1) * ATT_BLOCK)
                k_win = kc_ref[sub, win, sl]
                v_win = vc_ref[sub, win, sl]
            s = _dot_nt(q_ref[sub, rows, sl], k_win)
            s = jnp.where(band_first if a == 0 else band, s, NEG)
            m = jnp.max(s, axis=-1, keepdims=True)
            p = jnp.exp2(s - m).astype(BF16)
            ol = _dot(p, jnp.concatenate([v_win, ones], axis=1))
            l = ol[:, DIL_HD:]
            o_ref[sub, rows, sl] = (ol[:, :DIL_HD] / l).astype(o_ref.dtype)
            lse_all = jnp.where(lane == hh, m + jnp.log(l) * LOG2_E, lse_all)
        lse_ref[sub, rows, :] = lse_all


def _dil_attn(q, k, v, n_sub, dil):
    sub_len = q.size // COL_TILE // n_sub
    qb = min(sub_len, ATT_SUPER)
    ns = ATT_SUPER // qb
    n_qblk = qb // ATT_BLOCK
    q3, k3, v3 = (a.reshape(n_sub, sub_len, COL_TILE) for a in (q, k, v))
    blk = (ns, qb, COL_TILE)
    cur = lambda s, i: (s, i, 0)
    prev = lambda s, i: (s, jnp.maximum(i * n_qblk - 1, 0), 0)
    prev_blk = (ns, ATT_BLOCK, COL_TILE)
    o, lse = pl.pallas_call(
        _dil_attn_kernel,
        grid=(n_sub // ns, sub_len // qb),
        in_specs=[pl.BlockSpec(blk, cur), pl.BlockSpec(prev_blk, prev), pl.BlockSpec(blk, cur),
                  pl.BlockSpec(prev_blk, prev), pl.BlockSpec(blk, cur)],
        out_specs=[pl.BlockSpec(blk, cur), pl.BlockSpec((ns, qb, LANES), cur)],
        out_shape=[jax.ShapeDtypeStruct((n_sub, sub_len, COL_TILE), BF16),
                   jax.ShapeDtypeStruct((n_sub, sub_len, LANES), F32)],
        compiler_params=_params("parallel", "arbitrary"),
        name=f"dil_attn_d{dil}",
    )(q3, k3, k3, v3, v3)
    return o, lse


def _merge_kernel(x_ref, oa_ref, o0_ref, o1_ref, o2_ref, l0_ref, l1_ref, l2_ref,
                  sz_ref, sga_ref, sgd_ref, wga_ref, wdo_ref, wo_ref, out_ref, og_ref, lg_ref, hop_ref):
    tm = x_ref.shape[1]
    for g, ((_, dil), o_ref, l_ref) in enumerate(zip(DIL_GROUPS, (o0_ref, o1_ref, o2_ref),
                                                     (l0_ref, l1_ref, l2_ref))):
        s0 = min(dil, ROW_STRIDE)
        s1 = dil // s0
        planes = [(lg_ref.at[g], lambda r: l_ref[0, r])] + [
            (og_ref.at[g, hh], lambda r, hh=hh: o_ref[0, r, :, hh * DIL_HD:(hh + 1) * DIL_HD].astype(F32))
            for hh in range(DIL_HEADS)]
        for p, (dst_ref, src) in enumerate(planes):
            for r0 in range(s0):
                if s1 == 1:
                    dst_ref[pl.ds(r0, tm // s0, stride=s0), :] = src(r0)
                    continue
                for r1 in range(s1):
                    hop_ref[p, pl.ds(r1, tm // dil, stride=s1), :] = src(s0 * r1 + r0)
                dst_ref[pl.ds(r0, tm // s0, stride=s0), :] = hop_ref[p, 0:tm // s0, :]
    n_groups = len(DIL_GROUPS)

    def combine(rows):
        lses = [lg_ref[g, rows, :] for g in range(n_groups)]
        m = jnp.maximum(jnp.maximum(lses[0], lses[1]), lses[2])
        es = [jnp.exp2(l - m) for l in lses]
        den = es[0] + es[1] + es[2]
        ws = [e / den for e in es]
        heads = []
        for hh in range(DIL_HEADS):
            acc = ws[0][:, hh:hh + 1] * og_ref[0, hh, rows, :]
            for g in range(1, n_groups):
                acc = acc + ws[g][:, hh:hh + 1] * og_ref[g, hh, rows, :]
            heads.append(acc)
        return (jnp.concatenate(heads, axis=-1) * sz_ref[rows, :].astype(F32)).astype(BF16)

    def branches(rows, o_d):
        y_a = _dot(oa_ref[rows, :], wga_ref[...])
        y_d = _dot(o_d, wdo_ref[...])
        return y_a, y_d

    def gate(rows, y_a, y_d):
        return (sga_ref[rows, :].astype(F32) * y_a + sgd_ref[rows, :].astype(F32) * y_d).astype(BF16)

    def project(rows, y):
        out_ref[0, rows, :] = x_ref[0, rows, :] + _dot(y, wo_ref[...])

    for rows in _row_subs(tm, MERGE_ROW_SUB):
        project(rows, gate(rows, *branches(rows, combine(rows))))


def _merge(x, o_a, o_ds, lses, gates, wga, wdo, wo):
    batch, seq, d = x.shape
    tm = MERGE_TOK_TILE
    nt = seq // tm
    tok = lambda b, i: (b * nt + i, 0)
    fixed = lambda b, i: (0, 0)
    dil_spec = lambda dil, w: pl.BlockSpec((1, dil, tm // dil, w), lambda b, i: (b, 0, i, 0))
    o_ds = [o.reshape(batch, dil, seq // dil, DIL_OUT) for o, (_, dil) in zip(o_ds, DIL_GROUPS)]
    lses = [l.reshape(batch, dil, seq // dil, LANES) for l, (_, dil) in zip(lses, DIL_GROUPS)]
    return pl.pallas_call(
        _merge_kernel,
        grid=(batch, nt),
        in_specs=[pl.BlockSpec((1, tm, d), lambda b, i: (b, i, 0)),
                  pl.BlockSpec((tm, GLA_V), tok)]
        + [dil_spec(dil, DIL_OUT) for _, dil in DIL_GROUPS]
        + [dil_spec(dil, LANES) for _, dil in DIL_GROUPS]
        + [pl.BlockSpec((tm, DIL_OUT), lambda b, i: (b * nt + i, GATE_ZD)),
           pl.BlockSpec((tm, d), lambda b, i: (b * nt + i, GATE_GA * COL_TILE // D_MODEL)),
           pl.BlockSpec((tm, d), lambda b, i: (b * nt + i, GATE_GD * COL_TILE // D_MODEL)),
           pl.BlockSpec(wga.shape, fixed),
           pl.BlockSpec(wdo.shape, fixed),
           pl.BlockSpec(wo.shape, fixed)],
        out_specs=pl.BlockSpec((1, tm, d), lambda b, i: (b, i, 0)),
        out_shape=jax.ShapeDtypeStruct(x.shape, x.dtype),
        scratch_shapes=[pltpu.VMEM((len(DIL_GROUPS), DIL_HEADS, tm, DIL_HD), F32),
                        pltpu.VMEM((len(DIL_GROUPS), tm, LANES), F32),
                        pltpu.VMEM((DIL_HEADS + 1, tm // ROW_STRIDE, LANES), F32)],
        compiler_params=_params("parallel", "parallel"),
        name="merge_out",
    )(x, o_a, *o_ds, *lses, gates, gates, gates, wga, wdo, wo)


def kernel(x, positions, norm_gain, w_in, gla_w_a2, gla_b_a, gla_out_gain, dil_q_gain, dil_k_gain,
           w_gla_out, w_dil_out, w_o):
    batch, seq, d = x.shape
    t = batch * seq
    half = DIL_HD // 2
    inv_freq = ROPE_THETA ** (-jnp.arange(half, dtype=F32) / half)
    freq = jnp.concatenate([inv_freq, inv_freq]).reshape(1, DIL_HD)
    pos = positions.astype(F32).reshape(t, 1)
    for layer in range(norm_gain.shape[0]):
        w = _w_cast(jnp.swapaxes(w_in[layer], 0, 1))
        wa2 = jnp.pad(gla_w_a2[layer], ((0, LANES - GLA_RANK), (0, 0))).astype(BF16)
        ba = gla_b_a[layer].reshape(1, GLA_QK)
        g_dqk = jnp.stack([dil_q_gain[layer] * (DIL_HD ** -0.5 * LOG2_E), dil_k_gain[layer]])

        o_a, gates, h, cos, sin = _nat_gla(x.reshape(t, d), norm_gain[layer], pos, freq, w,
                                           wa2, ba, gla_out_gain[layer].reshape(1, GLA_DV), seq)
        o_ds, lses = [], []
        for g, (win, dil) in enumerate(DIL_GROUPS):
            assert win // dil == ATT_BLOCK
            q_g, k_g, v_g = _dil_proj(h, w, g, dil, g_dqk, cos, sin, batch)
            o_g, lse_g = _dil_attn(q_g, k_g, v_g, batch * dil, dil)
            o_ds.append(o_g)
            lses.append(lse_g)
        x = _merge(x, o_a, o_ds, lses, gates,
                   w_gla_out[layer].astype(BF16), w_dil_out[layer].astype(BF16),
                   w_o[layer].astype(BF16))
    return x
```

```python
import functools
import itertools

import jax
import jax.numpy as jnp
from jax import lax
from jax.experimental import pallas as pl
from jax.experimental.pallas import tpu as pltpu

D_MODEL = 1024
EPS = 1e-6
ROPE_THETA = 10000.0
GLA_HEADS = 4
GLA_DK = 128
GLA_DV = 256
GLA_RANK = 16
GLA_TAU = 16.0
GLA_QK = GLA_HEADS * GLA_DK
GLA_V = GLA_HEADS * GLA_DV
DIL_GROUPS = ((128, 1), (512, 4), (2048, 16))
DIL_HEADS = 4
DIL_HD = 128
DIL_QK = len(DIL_GROUPS) * DIL_HEADS * DIL_HD
DIL_OUT = DIL_HEADS * DIL_HD
IN_SPLIT_SIZES = (GLA_QK, GLA_QK, GLA_V, GLA_V, GLA_RANK,
                  DIL_QK, DIL_QK, DIL_QK, DIL_OUT, D_MODEL, D_MODEL)

LANES = 128
BF16_ROWS = 16
ROW_STRIDE = 4
GLA_BLOCK = 128
GLA_MID = GLA_BLOCK // 2
ATT_BLOCK = 128
ATT_SUPER = 2048
TOK_TILE = 1024
DIL_ROW_SUB = 256
PERMUTE_ROWS = 512
GLA_TOK_TILE = 512
GLA_WIDE = 128
MERGE_TOK_TILE = 512
MERGE_ROW_SUB = 512
COL_TILE = DIL_HEADS * DIL_HD
VMEM_LIMIT_BYTES = 40 * 1024 * 1024

F32 = jnp.float32
BF16 = jnp.bfloat16
NEG = -1e30
LOG2_E = 1.4426950408889634

GATE_GA, GATE_GD, GATE_ZD = 0, 2, 4

(_QA, _KA, _VA, _RA, _ALR, _QD, _KD, _VD, _ZD, _GA, _GD) = (
    sum(IN_SPLIT_SIZES[:i]) for i in range(len(IN_SPLIT_SIZES)))


def _block_index(offset, width):
    assert offset % width == 0
    return offset // width


W_BLOCK = 1024
LO_PAD = -(-(_ALR + LANES) // W_BLOCK) * W_BLOCK
LO_QK, LO_V, LO_R = _block_index(_QA, 2 * GLA_QK), _block_index(_VA, GLA_V), _block_index(_RA, GLA_V)
LO_ALR = _block_index(_ALR, LANES)
HI_QD = _block_index(LO_PAD, COL_TILE)
HI_ZD = _block_index(LO_PAD + _ZD - _QD, DIL_OUT)
HI_GA, HI_GD = _block_index(LO_PAD + _GA - _QD, D_MODEL), _block_index(LO_PAD + _GD - _QD, D_MODEL)
assert _KA == _QA + GLA_QK and _KD - _QD == DIL_QK and _VD - _KD == DIL_QK


def _w_cols(width, index):
    return pl.BlockSpec((D_MODEL, width), lambda *_: (0, index))


def _params(*sem):
    return pltpu.CompilerParams(dimension_semantics=sem, vmem_limit_bytes=VMEM_LIMIT_BYTES)


def _dot(a, b):
    return jnp.dot(a, b, preferred_element_type=F32)


def _dot_nt(a, b):
    return lax.dot_general(a, b, (((1,), (1,)), ((), ())), preferred_element_type=F32)


def _dot_tn(a, b):
    return lax.dot_general(a, b, (((0,), (0,)), ((), ())), preferred_element_type=F32)


def _sigmoid_of_twice(half_x):
    return 0.5 * jnp.tanh(half_x) + 0.5


def _silu_of_twice(half_x):
    return half_x * jnp.tanh(half_x) + half_x


def _row_subs(n, sub):
    return [slice(r * sub, (r + 1) * sub) for r in range(n // sub)]


def _w_cast_kernel(wt_ref, o_ref, *, n_lo):
    j = pl.program_id(0)
    start = jnp.where(j < n_lo, j * W_BLOCK, _QD + (j - n_lo) * W_BLOCK)
    col = start + lax.broadcasted_iota(jnp.int32, (W_BLOCK, 1), 0)
    is_gate = ((col >= _RA) & (col < _ALR)) | (col >= _ZD)
    o_ref[...] = (wt_ref[...] * jnp.where(is_gate, 0.5, 1.0)).T.astype(o_ref.dtype)


def _w_cast(w_t):
    n, d = w_t.shape
    n_lo = _block_index(LO_PAD, W_BLOCK)
    n_hi = _block_index(n - _QD, W_BLOCK)
    assert _QD % BF16_ROWS == 0 and LO_PAD <= n

    def rows(j):
        start = jnp.where(j < n_lo, j * W_BLOCK, _QD + (j - n_lo) * W_BLOCK)
        return pl.multiple_of(start, BF16_ROWS), 0

    return pl.pallas_call(
        functools.partial(_w_cast_kernel, n_lo=n_lo),
        grid=(n_lo + n_hi,),
        in_specs=[pl.BlockSpec((pl.Element(W_BLOCK), pl.Element(d)), rows)],
        out_specs=pl.BlockSpec((d, W_BLOCK), lambda j: (0, j)),
        out_shape=jax.ShapeDtypeStruct((d, (n_lo + n_hi) * W_BLOCK), BF16),
        compiler_params=_params("parallel"),
        name="w_cast",
    )(w_t)


def _nat_gla_kernel(x_ref, ng_ref, pos_ref, freq_ref, wqk_ref, wv_ref, wr_ref, wga_ref, wgd_ref,
                    wzd_ref, walr_ref, wa2_ref, ba_ref, gain_ref,
                    oa_ref, gates_ref, h_ref, cos_ref, sin_ref, st_ref, *, tiles_per_seq):
    @pl.when(pl.program_id(0) % tiles_per_seq == 0)
    def _():
        st_ref[...] = jnp.zeros_like(st_ref)

    row = lax.broadcasted_iota(jnp.int32, (GLA_BLOCK, GLA_BLOCK), 0)
    col = lax.broadcasted_iota(jnp.int32, (GLA_BLOCK, GLA_BLOCK), 1)
    causal = col <= row
    tri = causal.astype(BF16)
    tri2 = jnp.concatenate([tri, tri], axis=1)
    low = lax.broadcasted_iota(jnp.int32, (GLA_BLOCK // 2, DIL_HD), 1) < DIL_HD // 2

    gate_plan = ((wga_ref, 0, "sigmoid"), (wga_ref, 1, "sigmoid"), (wgd_ref, 0, "sigmoid"),
                 (wgd_ref, 1, "sigmoid"), (wzd_ref, 0, "silu"))

    def gate_tiles(rows, h, lo_j, hi_j):
        for j in range(lo_j, hi_j):
            w_ref, wj, act = gate_plan[j]
            acc = _dot(h, w_ref[:, wj * COL_TILE:(wj + 1) * COL_TILE])
            act_fn = _silu_of_twice if act == "silu" else _sigmoid_of_twice
            gates_ref[rows, j * COL_TILE:(j + 1) * COL_TILE] = act_fn(acc).astype(gates_ref.dtype)

    def recurrence(c, qk, hi, lo, v, r):
        rows = slice(c * GLA_BLOCK, (c + 1) * GLA_BLOCK)
        b = _dot(tri2, jnp.concatenate([hi, lo], axis=0))
        b_mid = b[GLA_MID - 1:GLA_MID]
        b_last = b[GLA_BLOCK - 1:GLA_BLOCK]
        q = qk[:, :GLA_QK] * (GLA_DK ** -0.5)
        k = qk[:, GLA_QK:]
        q_in = (q * jnp.exp(b)).astype(BF16)
        q_mid = (q * jnp.exp(b - b_mid)).astype(BF16)
        k_mid = (k * jnp.exp(b_mid - b)).astype(BF16)
        k_end = (k * jnp.exp(b_last - b)).astype(BF16)
        dec = jnp.exp(b_last)
        yield
        heads = [(slice(hh * GLA_DK, (hh + 1) * GLA_DK), slice(hh * GLA_DV, (hh + 1) * GLA_DV))
                 for hh in range(GLA_HEADS)]
        attn = [_dot_nt(q_mid[:, ks], k_mid[:, ks]) for ks, _ in heads]
        kv = [_dot_tn(k_end[:, ks], v[:, vs]) for ks, vs in heads]
        yield
        outs = []
        for hh, (ks, vs) in enumerate(heads):
            st = st_ref[hh]
            a = jnp.where(causal, attn[hh], 0.0).astype(BF16)
            outs.append(_dot(jnp.concatenate([a, q_in[:, ks]], axis=1),
                             jnp.concatenate([v[:, vs], st.astype(BF16)], axis=0)))
            d_col = jnp.broadcast_to(dec[:, ks], (GLA_DK, GLA_DK)).T
            st_ref[hh] = st * jnp.concatenate([d_col] * (GLA_DV // GLA_DK), axis=1) + kv[hh]
        yield
        for (_, vs), o in zip(heads, outs):
            ms = jnp.mean(o * o, axis=-1, keepdims=True)
            o = o * lax.rsqrt(ms + EPS) * gain_ref[...] * r[:, vs]
            oa_ref[rows, vs] = o.astype(oa_ref.dtype)
        yield

    def advance(gen):
        if gen is not None:
            for _ in range(per):
                next(gen)

    def rope_tables(c):
        half = GLA_BLOCK // 2
        top = slice(c * GLA_BLOCK, c * GLA_BLOCK + half)
        bot = slice(c * GLA_BLOCK + half, (c + 1) * GLA_BLOCK)
        ang = jnp.where(low, pos_ref[top, :], pos_ref[bot, :]) * freq_ref[...]
        cos, sin = jnp.cos(ang), jnp.sin(ang)
        cos_x, sin_x = pltpu.roll(cos, DIL_HD // 2, 1), pltpu.roll(sin, DIL_HD // 2, 1)
        cos_ref[top, :] = jnp.where(low, cos, cos_x)
        cos_ref[bot, :] = jnp.where(low, cos_x, cos)
        sin_ref[top, :] = jnp.where(low, -sin, sin_x)
        sin_ref[bot, :] = jnp.where(low, -sin_x, sin)

    per = GLA_WIDE // GLA_BLOCK
    gen = late_gates = None
    for wb in range(x_ref.shape[0] // GLA_WIDE):
        rows = slice(wb * GLA_WIDE, (wb + 1) * GLA_WIDE)
        x = x_ref[rows, :]
        ms = jnp.mean(x * x, axis=-1, keepdims=True)
        h = (x * lax.rsqrt(ms + EPS) * ng_ref[...]).astype(BF16)
        h_ref[rows, :] = h
        alr = _dot(h, walr_ref[...]).astype(BF16)
        v = _dot(h, wv_ref[...]).astype(BF16)
        advance(gen)
        for c in range(wb * per, (wb + 1) * per):
            rope_tables(c)
        z = _dot(alr, wa2_ref[...]) + ba_ref[...]
        la = (jnp.minimum(z, 0.0) - jnp.log(1.0 + jnp.exp(-jnp.abs(z)))) * (1.0 / GLA_TAU)
        hi = la.astype(BF16)
        lo = (la - hi.astype(F32)).astype(BF16)
        r = _silu_of_twice(_dot(h, wr_ref[...]))
        advance(gen)
        if late_gates is not None:
            gate_tiles(*late_gates, 0, 2)
        advance(gen)
        qk = _dot(h, wqk_ref[...])
        advance(gen)
        if late_gates is not None:
            gate_tiles(*late_gates, 2, len(gate_plan))
        late_gates = (rows, h)
        blocks = [slice(i * GLA_BLOCK, (i + 1) * GLA_BLOCK) for i in range(per)]
        gen = itertools.chain(*[recurrence(wb * per + i, qk[blk], hi[blk], lo[blk], v[blk], r[blk])
                                for i, blk in enumerate(blocks)])
    advance(gen)
    gate_tiles(*late_gates, 0, 2)
    advance(gen)
    gate_tiles(*late_gates, 2, len(gate_plan))
    for _ in gen:
        pass


def _nat_gla(x2, norm_gain, pos, freq, w, wa2, ba, gain, seq):
    t, d = x2.shape
    tm = GLA_TOK_TILE
    tok = lambda i: (i, 0)
    fixed = lambda i: (0, 0)
    small = (wa2, ba, gain)
    n_gates = 2 * D_MODEL + DIL_OUT
    table = jax.ShapeDtypeStruct((t, DIL_HD), F32)
    return pl.pallas_call(
        functools.partial(_nat_gla_kernel, tiles_per_seq=seq // tm),
        grid=(t // tm,),
        in_specs=[pl.BlockSpec((tm, d), tok), pl.BlockSpec((1, d), fixed),
                  pl.BlockSpec((tm, 1), tok), pl.BlockSpec((1, DIL_HD), fixed),
                  _w_cols(2 * GLA_QK, LO_QK), _w_cols(GLA_V, LO_V), _w_cols(GLA_V, LO_R),
                  _w_cols(D_MODEL, HI_GA), _w_cols(D_MODEL, HI_GD), _w_cols(DIL_OUT, HI_ZD),
                  _w_cols(LANES, LO_ALR)]
        + [pl.BlockSpec(w.shape, fixed) for w in small],
        out_specs=[pl.BlockSpec((tm, GLA_V), tok), pl.BlockSpec((tm, n_gates), tok),
                   pl.BlockSpec((tm, d), tok), pl.BlockSpec((tm, DIL_HD), tok),
                   pl.BlockSpec((tm, DIL_HD), tok)],
        out_shape=[jax.ShapeDtypeStruct((t, GLA_V), BF16), jax.ShapeDtypeStruct((t, n_gates), BF16),
                   jax.ShapeDtypeStruct((t, d), BF16), table, table],
        scratch_shapes=[pltpu.VMEM((GLA_HEADS, GLA_DK, GLA_DV), F32)],
        compiler_params=_params("arbitrary"),
        name="nat_gla",
    )(x2, norm_gain.reshape(1, d), pos, freq, *([w] * 7), *small)


def _dil_proj_kernel(h_ref, wq_ref, wk_ref, wv_ref, g_ref, cos_ref, sin_ref, q_ref, k_ref, v_ref,
                     *scratch, dil):
    n_planes = COL_TILE // LANES
    tm = h_ref.shape[0]
    outs = (q_ref, k_ref, v_ref)

    def store(kind, rows, plane, y):
        if dil == 1:
            outs[kind][0, 0, rows, plane * LANES:(plane + 1) * LANES] = y.astype(q_ref.dtype)
        else:
            scratch[0][kind, plane, rows, :] = y

    def permute(kind, part):
        s0 = min(dil, ROW_STRIDE)
        s1 = dil // s0
        t0 = part * PERMUTE_ROWS
        mid = slice(t0 // s0, (t0 + PERMUTE_ROWS) // s0)
        dst = slice(t0 // dil, (t0 + PERMUTE_ROWS) // dil)
        for plane in range(n_planes):
            cols = slice(plane * LANES, (plane + 1) * LANES)
            for r0 in range(s0):
                hop = scratch[0][kind, plane, pl.ds(t0 + r0, PERMUTE_ROWS // s0, stride=s0), :]
                if s1 == 1:
                    outs[kind][0, r0, dst, cols] = hop.astype(q_ref.dtype)
                    continue
                scratch[1][kind, plane, r0, mid, :] = hop
                for r1 in range(s1):
                    outs[kind][0, s0 * r1 + r0, dst, cols] = scratch[1][
                        kind, plane, r0, pl.ds(mid.start + r1, PERMUTE_ROWS // dil, stride=s1), :
                    ].astype(q_ref.dtype)

    def epilogue(rows, accs):
        for kind, acc in enumerate(accs):
            for hh in range(n_planes):
                xh = acc[:, hh * DIL_HD:(hh + 1) * DIL_HD]
                if kind == 2:
                    store(kind, rows, hh, xh)
                    continue
                ms = jnp.mean(xh * xh, axis=-1, keepdims=True)
                y = xh * lax.rsqrt(ms + EPS) * g_ref[kind:kind + 1, :]
                y = y * cos_ref[rows, :] + pltpu.roll(y, DIL_HD // 2, 1) * sin_ref[rows, :]
                store(kind, rows, hh, y)

    def finish(rows, accs):
        epilogue(rows, accs)
        if dil > 1 and rows.stop % PERMUTE_ROWS == 0:
            for kind in range(3):
                permute(kind, rows.stop // PERMUTE_ROWS - 1)

    pending = None
    for rows in _row_subs(tm, DIL_ROW_SUB):
        h = h_ref[rows, :]
        accs = [_dot(h, w_ref[...]) for w_ref in (wq_ref, wk_ref, wv_ref)]
        if pending is not None:
            finish(*pending)
        pending = (rows, accs)
    finish(*pending)


def _dil_proj(h, w, group, dil, gains, cos, sin, batch):
    t, d = h.shape
    seq = t // batch
    tm = TOK_TILE
    nt = seq // tm
    tok = lambda b, i: (b * nt + i, 0)
    fixed = lambda b, i: (0, 0)
    out = jax.ShapeDtypeStruct((batch, dil, seq // dil, COL_TILE), BF16)
    n_groups = len(DIL_GROUPS)
    assert DIL_HD == LANES and PERMUTE_ROWS % (dil * BF16_ROWS) == 0 and tm % PERMUTE_ROWS == 0
    return pl.pallas_call(
        functools.partial(_dil_proj_kernel, dil=dil),
        grid=(batch, nt),
        in_specs=[pl.BlockSpec((tm, d), tok)]
        + [_w_cols(COL_TILE, HI_QD + kind * n_groups + group) for kind in range(3)]
        + [pl.BlockSpec(gains.shape, fixed),
           pl.BlockSpec((tm, DIL_HD), tok),
           pl.BlockSpec((tm, DIL_HD), tok)],
        out_specs=[pl.BlockSpec((1, dil, tm // dil, COL_TILE), lambda b, i: (b, 0, i, 0))] * 3,
        out_shape=[out] * 3,
        scratch_shapes=[] if dil == 1 else [
            pltpu.VMEM((3, COL_TILE // LANES, tm, LANES), F32),
            pltpu.VMEM((3, COL_TILE // LANES, ROW_STRIDE, tm // ROW_STRIDE, LANES), F32)],
        compiler_params=_params("parallel", "parallel"),
        name=f"dil_proj_d{dil}",
    )(h, w, w, w, gains, cos, sin)


def _dil_attn_kernel(q_ref, kp_ref, kc_ref, vp_ref, vc_ref, o_ref, lse_ref):
    n_qblk = q_ref.shape[1] // ATT_BLOCK
    row = lax.broadcasted_iota(jnp.int32, (ATT_BLOCK, 2 * ATT_BLOCK), 0)
    col = lax.broadcasted_iota(jnp.int32, (ATT_BLOCK, 2 * ATT_BLOCK), 1)
    band = (col >= row) & (col <= row + ATT_BLOCK)
    band_first = band & ((col >= ATT_BLOCK) | (pl.program_id(1) > 0))
    ones = jnp.ones((2 * ATT_BLOCK, DIL_HD), BF16)
    lane = lax.broadcasted_iota(jnp.int32, (ATT_BLOCK, LANES), 1)
    for sub, a in [(sub, a) for sub in range(q_ref.shape[0]) for a in range(n_qblk)]:
        rows = slice(a * ATT_BLOCK, (a + 1) * ATT_BLOCK)
        lse_all = jnp.zeros((ATT_BLOCK, LANES), F32)
        for hh in range(DIL_HEADS):
            sl = slice(hh * DIL_HD, (hh + 1) * DIL_HD)
            if a == 0:
                k_win = jnp.concatenate([kp_ref[sub, :, sl], kc_ref[sub, :ATT_BLOCK, sl]], axis=0)
                v_win = jnp.concatenate([vp_ref[sub, :, sl], vc_ref[sub, :ATT_BLOCK, sl]], axis=0)
            else:
                win = slice((a - 1) * ATT_BLOCK, (a + 1) * ATT_BLOCK)
                k_win = kc_ref[sub, win, sl]
                v_win = vc_ref[sub, win, sl]
            s = _dot_nt(q_ref[sub, rows, sl], k_win)
            s = jnp.where(band_first if a == 0 else band, s, NEG)
            m = jnp.max(s, axis=-1, keepdims=True)
            p = jnp.exp2(s - m).astype(BF16)
            ol = _dot(p, jnp.concatenate([v_win, ones], axis=1))
            l = ol[:, DIL_HD:]
            o_ref[sub, rows, sl] = (ol[:, :DIL_HD] / l).astype(o_ref.dtype)
            lse_all = jnp.where(lane == hh, m + jnp.log(l) * LOG2_E, lse_all)
        lse_ref[sub, rows, :] = lse_all


def _dil_attn(q, k, v, n_sub, dil):
    sub_len = q.size // COL_TILE // n_sub
    qb = min(sub_len, ATT_SUPER)
    ns = ATT_SUPER // qb
    n_qblk = qb // ATT_BLOCK
    q3, k3, v3 = (a.reshape(n_sub, sub_len, COL_TILE) for a in (q, k, v))
    blk = (ns, qb, COL_TILE)
    cur = lambda s, i: (s, i, 0)
    prev = lambda s, i: (s, jnp.maximum(i * n_qblk - 1, 0), 0)
    prev_blk = (ns, ATT_BLOCK, COL_TILE)
    o, lse = pl.pallas_call(
        _dil_attn_kernel,
        grid=(n_sub // ns, sub_len // qb),
        in_specs=[pl.BlockSpec(blk, cur), pl.BlockSpec(prev_blk, prev), pl.BlockSpec(blk, cur),
                  pl.BlockSpec(prev_blk, prev), pl.BlockSpec(blk, cur)],
        out_specs=[pl.BlockSpec(blk, cur), pl.BlockSpec((ns, qb, LANES), cur)],
        out_shape=[jax.ShapeDtypeStruct((n_sub, sub_len, COL_TILE), BF16),
                   jax.ShapeDtypeStruct((n_sub, sub_len, LANES), F32)],
        compiler_params=_params("parallel", "arbitrary"),
        name=f"dil_attn_d{dil}",
    )(q3, k3, k3, v3, v3)
    return o, lse


def _merge_kernel(x_ref, oa_ref, o0_ref, o1_ref, o2_ref, l0_ref, l1_ref, l2_ref,
                  sz_ref, sga_ref, sgd_ref, wga_ref, wdo_ref, wo_ref, out_ref, og_ref, lg_ref, hop_ref):
    tm = x_ref.shape[1]
    for g, ((_, dil), o_ref, l_ref) in enumerate(zip(DIL_GROUPS, (o0_ref, o1_ref, o2_ref),
                                                     (l0_ref, l1_ref, l2_ref))):
        s0 = min(dil, ROW_STRIDE)
        s1 = dil // s0
        planes = [(lg_ref.at[g], lambda r: l_ref[0, r])] + [
            (og_ref.at[g, hh], lambda r, hh=hh: o_ref[0, r, :, hh * DIL_HD:(hh + 1) * DIL_HD].astype(F32))
            for hh in range(DIL_HEADS)]
        for p, (dst_ref, src) in enumerate(planes):
            for r0 in range(s0):
                if s1 == 1:
                    dst_ref[pl.ds(r0, tm // s0, stride=s0), :] = src(r0)
                    continue
                for r1 in range(s1):
                    hop_ref[p, pl.ds(r1, tm // dil, stride=s1), :] = src(s0 * r1 + r0)
                dst_ref[pl.ds(r0, tm // s0, stride=s0), :] = hop_ref[p, 0:tm // s0, :]
    n_groups = len(DIL_GROUPS)

    def combine(rows):
        lses = [lg_ref[g, rows, :] for g in range(n_groups)]
        m = jnp.maximum(jnp.maximum(lses[0], lses[1]), lses[2])
        es = [jnp.exp2(l - m) for l in lses]
        den = es[0] + es[1] + es[2]
        ws = [e / den for e in es]
        heads = []
        for hh in range(DIL_HEADS):
            acc = ws[0][:, hh:hh + 1] * og_ref[0, hh, rows, :]
            for g in range(1, n_groups):
                acc = acc + ws[g][:, hh:hh + 1] * og_ref[g, hh, rows, :]
            heads.append(acc)
        return (jnp.concatenate(heads, axis=-1) * sz_ref[rows, :].astype(F32)).astype(BF16)

    def branches(rows, o_d):
        y_a = _dot(oa_ref[rows, :], wga_ref[...])
        y_d = _dot(o_d, wdo_ref[...])
        return y_a, y_d

    def gate(rows, y_a, y_d):
        return (sga_ref[rows, :].astype(F32) * y_a + sgd_ref[rows, :].astype(F32) * y_d).astype(BF16)

    def project(rows, y):
        out_ref[0, rows, :] = x_ref[0, rows, :] + _dot(y, wo_ref[...])

    for rows in _row_subs(tm, MERGE_ROW_SUB):
        project(rows, gate(rows, *branches(rows, combine(rows))))


def _merge(x, o_a, o_ds, lses, gates, wga, wdo, wo):
    batch, seq, d = x.shape
    tm = MERGE_TOK_TILE
    nt = seq // tm
    tok = lambda b, i: (b * nt + i, 0)
    fixed = lambda b, i: (0, 0)
    dil_spec = lambda dil, w: pl.BlockSpec((1, dil, tm // dil, w), lambda b, i: (b, 0, i, 0))
    o_ds = [o.reshape(batch, dil, seq // dil, DIL_OUT) for o, (_, dil) in zip(o_ds, DIL_GROUPS)]
    lses = [l.reshape(batch, dil, seq // dil, LANES) for l, (_, dil) in zip(lses, DIL_GROUPS)]
    return pl.pallas_call(
        _merge_kernel,
        grid=(batch, nt),
        in_specs=[pl.BlockSpec((1, tm, d), lambda b, i: (b, i, 0)),
                  pl.BlockSpec((tm, GLA_V), tok)]
        + [dil_spec(dil, DIL_OUT) for _, dil in DIL_GROUPS]
        + [dil_spec(dil, LANES) for _, dil in DIL_GROUPS]
        + [pl.BlockSpec((tm, DIL_OUT), lambda b, i: (b * nt + i, GATE_ZD)),
           pl.BlockSpec((tm, d), lambda b, i: (b * nt + i, GATE_GA * COL_TILE // D_MODEL)),
           pl.BlockSpec((tm, d), lambda b, i: (b * nt + i, GATE_GD * COL_TILE // D_MODEL)),
           pl.BlockSpec(wga.shape, fixed),
           pl.BlockSpec(wdo.shape, fixed),
           pl.BlockSpec(wo.shape, fixed)],
        out_specs=pl.BlockSpec((1, tm, d), lambda b, i: (b, i, 0)),
        out_shape=jax.ShapeDtypeStruct(x.shape, x.dtype),
        scratch_shapes=[pltpu.VMEM((len(DIL_GROUPS), DIL_HEADS, tm, DIL_HD), F32),
                        pltpu.VMEM((len(DIL_GROUPS), tm, LANES), F32),
                        pltpu.VMEM((DIL_HEADS + 1, tm // ROW_STRIDE, LANES), F32)],
        compiler_params=_params("parallel", "parallel"),
        name="merge_out",
    )(x, o_a, *o_ds, *lses, gates, gates, gates, wga, wdo, wo)


def kernel(x, positions, norm_gain, w_in, gla_w_a2, gla_b_a, gla_out_gain, dil_q_gain, dil_k_gain,
           w_gla_out, w_dil_out, w_o):
    batch, seq, d = x.shape
    t = batch * seq
    half = DIL_HD // 2
    inv_freq = ROPE_THETA ** (-jnp.arange(half, dtype=F32) / half)
    freq = jnp.concatenate([inv_freq, inv_freq]).reshape(1, DIL_HD)
    pos = positions.astype(F32).reshape(t, 1)
    for layer in range(norm_gain.shape[0]):
        w = _w_cast(jnp.swapaxes(w_in[layer], 0, 1))
        wa2 = jnp.pad(gla_w_a2[layer], ((0, LANES - GLA_RANK), (0, 0))).astype(BF16)
        ba = gla_b_a[layer].reshape(1, GLA_QK)
        g_dqk = jnp.stack([dil_q_gain[layer] * (DIL_HD ** -0.5 * LOG2_E), dil_k_gain[layer]])

        o_a, gates, h, cos, sin = _nat_gla(x.reshape(t, d), norm_gain[layer], pos, freq, w,
                                           wa2, ba, gla_out_gain[layer].reshape(1, GLA_DV), seq)
        o_ds, lses = [], []
        for g, (win, dil) in enumerate(DIL_GROUPS):
            assert win // dil == ATT_BLOCK
            q_g, k_g, v_g = _dil_proj(h, w, g, dil, g_dqk, cos, sin, batch)
            o_g, lse_g = _dil_attn(q_g, k_g, v_g, batch * dil, dil)
            o_ds.append(o_g)
            lses.append(lse_g)
        x = _merge(x, o_a, o_ds, lses, gates,
                   w_gla_out[layer].astype(BF16), w_dil_out[layer].astype(BF16),
                   w_o[layer].astype(BF16))
    return x
```

```python
import functools
import itertools

import jax
import jax.numpy as jnp
from jax import lax
from jax.experimental import pallas as pl
from jax.experimental.pallas import tpu as pltpu

D_MODEL = 1024
EPS = 1e-6
ROPE_THETA = 10000.0
GLA_HEADS = 4
GLA_DK = 128
GLA_DV = 256
GLA_RANK = 16
GLA_TAU = 16.0
GLA_QK = GLA_HEADS * GLA_DK
GLA_V = GLA_HEADS * GLA_DV
DIL_GROUPS = ((128, 1), (512, 4), (2048, 16))
DIL_HEADS = 4
DIL_HD = 128
DIL_QK = len(DIL_GROUPS) * DIL_HEADS * DIL_HD
DIL_OUT = DIL_HEADS * DIL_HD
IN_SPLIT_SIZES = (GLA_QK, GLA_QK, GLA_V, GLA_V, GLA_RANK,
                  DIL_QK, DIL_QK, DIL_QK, DIL_OUT, D_MODEL, D_MODEL)

LANES = 128
BF16_ROWS = 16
ROW_STRIDE = 4
GLA_BLOCK = 128
GLA_MID = GLA_BLOCK // 2
ATT_BLOCK = 128
ATT_SUPER = 2048
TOK_TILE = 1024
DIL_ROW_SUB = 256
PERMUTE_ROWS = 512
GLA_TOK_TILE = 512
GLA_WIDE = 128
MERGE_TOK_TILE = 512
MERGE_ROW_SUB = 512
COL_TILE = DIL_HEADS * DIL_HD
VMEM_LIMIT_BYTES = 40 * 1024 * 1024
DIL_PROJ_VMEM_LIMIT_BYTES = 56 * 1024 * 1024

F32 = jnp.float32
BF16 = jnp.bfloat16
NEG = -1e30
LOG2_E = 1.4426950408889634

GATE_GA, GATE_GD, GATE_ZD = 0, 2, 4

(_QA, _KA, _VA, _RA, _ALR, _QD, _KD, _VD, _ZD, _GA, _GD) = (
    sum(IN_SPLIT_SIZES[:i]) for i in range(len(IN_SPLIT_SIZES)))


def _block_index(offset, width):
    assert offset % width == 0
    return offset // width


W_BLOCK = 1024
LO_PAD = -(-(_ALR + LANES) // W_BLOCK) * W_BLOCK
LO_QK, LO_V, LO_R = _block_index(_QA, 2 * GLA_QK), _block_index(_VA, GLA_V), _block_index(_RA, GLA_V)
LO_ALR = _block_index(_ALR, LANES)
HI_QD = _block_index(LO_PAD, COL_TILE)
HI_ZD = _block_index(LO_PAD + _ZD - _QD, DIL_OUT)
HI_GA, HI_GD = _block_index(LO_PAD + _GA - _QD, D_MODEL), _block_index(LO_PAD + _GD - _QD, D_MODEL)
assert _KA == _QA + GLA_QK and _KD - _QD == DIL_QK and _VD - _KD == DIL_QK


def _w_cols(width, index):
    return pl.BlockSpec((D_MODEL, width), lambda *_: (0, index))


def _params(*sem, vmem_limit=VMEM_LIMIT_BYTES):
    return pltpu.CompilerParams(dimension_semantics=sem, vmem_limit_bytes=vmem_limit)


def _dot(a, b):
    return jnp.dot(a, b, preferred_element_type=F32)


def _dot_nt(a, b):
    return lax.dot_general(a, b, (((1,), (1,)), ((), ())), preferred_element_type=F32)


def _dot_tn(a, b):
    return lax.dot_general(a, b, (((0,), (0,)), ((), ())), preferred_element_type=F32)


def _sigmoid_of_twice(half_x):
    return 0.5 * jnp.tanh(half_x) + 0.5


def _silu_of_twice(half_x):
    return half_x * jnp.tanh(half_x) + half_x


def _row_subs(n, sub):
    return [slice(r * sub, (r + 1) * sub) for r in range(n // sub)]


def _w_cast_kernel(wt_ref, o_ref, *, n_lo):
    j = pl.program_id(0)
    start = jnp.where(j < n_lo, j * W_BLOCK, _QD + (j - n_lo) * W_BLOCK)
    col = start + lax.broadcasted_iota(jnp.int32, (W_BLOCK, 1), 0)
    is_gate = ((col >= _RA) & (col < _ALR)) | (col >= _ZD)
    o_ref[...] = (wt_ref[...] * jnp.where(is_gate, 0.5, 1.0)).T.astype(o_ref.dtype)


def _w_cast(w_t):
    n, d = w_t.shape
    n_lo = _block_index(LO_PAD, W_BLOCK)
    n_hi = _block_index(n - _QD, W_BLOCK)
    assert _QD % BF16_ROWS == 0 and LO_PAD <= n

    def rows(j):
        start = jnp.where(j < n_lo, j * W_BLOCK, _QD + (j - n_lo) * W_BLOCK)
        return pl.multiple_of(start, BF16_ROWS), 0

    return pl.pallas_call(
        functools.partial(_w_cast_kernel, n_lo=n_lo),
        grid=(n_lo + n_hi,),
        in_specs=[pl.BlockSpec((pl.Element(W_BLOCK), pl.Element(d)), rows)],
        out_specs=pl.BlockSpec((d, W_BLOCK), lambda j: (0, j)),
        out_shape=jax.ShapeDtypeStruct((d, (n_lo + n_hi) * W_BLOCK), BF16),
        compiler_params=_params("parallel"),
        name="w_cast",
    )(w_t)


def _nat_gla_kernel(x_ref, ng_ref, pos_ref, freq_ref, wqk_ref, wv_ref, wr_ref, wga_ref, wgd_ref,
                    wzd_ref, walr_ref, wa2_ref, ba_ref, gain_ref,
                    oa_ref, gates_ref, h_ref, cos_ref, sin_ref, st_ref, *, tiles_per_seq):
    @pl.when(pl.program_id(0) % tiles_per_seq == 0)
    def _():
        st_ref[...] = jnp.zeros_like(st_ref)

    row = lax.broadcasted_iota(jnp.int32, (GLA_BLOCK, GLA_BLOCK), 0)
    col = lax.broadcasted_iota(jnp.int32, (GLA_BLOCK, GLA_BLOCK), 1)
    causal = col <= row
    tri = causal.astype(BF16)
    tri2 = jnp.concatenate([tri, tri], axis=1)
    low = lax.broadcasted_iota(jnp.int32, (GLA_BLOCK // 2, DIL_HD), 1) < DIL_HD // 2

    gate_plan = ((wga_ref, 0, "sigmoid"), (wga_ref, 1, "sigmoid"), (wgd_ref, 0, "sigmoid"),
                 (wgd_ref, 1, "sigmoid"), (wzd_ref, 0, "silu"))

    def gate_tiles(rows, h, lo_j, hi_j):
        for j in range(lo_j, hi_j):
            w_ref, wj, act = gate_plan[j]
            acc = _dot(h, w_ref[:, wj * COL_TILE:(wj + 1) * COL_TILE])
            act_fn = _silu_of_twice if act == "silu" else _sigmoid_of_twice
            gates_ref[rows, j * COL_TILE:(j + 1) * COL_TILE] = act_fn(acc).astype(gates_ref.dtype)

    def recurrence(c, qk, hi, lo, v, r):
        rows = slice(c * GLA_BLOCK, (c + 1) * GLA_BLOCK)
        b = _dot(tri2, jnp.concatenate([hi, lo], axis=0))
        b_mid = b[GLA_MID - 1:GLA_MID]
        b_last = b[GLA_BLOCK - 1:GLA_BLOCK]
        q = qk[:, :GLA_QK] * (GLA_DK ** -0.5)
        k = qk[:, GLA_QK:]
        q_in = (q * jnp.exp(b)).astype(BF16)
        q_mid = (q * jnp.exp(b - b_mid)).astype(BF16)
        k_mid = (k * jnp.exp(b_mid - b)).astype(BF16)
        k_end = (k * jnp.exp(b_last - b)).astype(BF16)
        dec = jnp.exp(b_last)
        yield
        heads = [(slice(hh * GLA_DK, (hh + 1) * GLA_DK), slice(hh * GLA_DV, (hh + 1) * GLA_DV))
                 for hh in range(GLA_HEADS)]
        attn = [_dot_nt(q_mid[:, ks], k_mid[:, ks]) for ks, _ in heads]
        kv = [_dot_tn(k_end[:, ks], v[:, vs]) for ks, vs in heads]
        yield
        outs = []
        for hh, (ks, vs) in enumerate(heads):
            st = st_ref[hh]
            a = jnp.where(causal, attn[hh], 0.0).astype(BF16)
            outs.append(_dot(jnp.concatenate([a, q_in[:, ks]], axis=1),
                             jnp.concatenate([v[:, vs], st.astype(BF16)], axis=0)))
            d_col = jnp.broadcast_to(dec[:, ks], (GLA_DK, GLA_DK)).T
            st_ref[hh] = st * jnp.concatenate([d_col] * (GLA_DV // GLA_DK), axis=1) + kv[hh]
        yield
        for (_, vs), o in zip(heads, outs):
            ms = jnp.mean(o * o, axis=-1, keepdims=True)
            o = o * lax.rsqrt(ms + EPS) * gain_ref[...] * r[:, vs]
            oa_ref[rows, vs] = o.astype(oa_ref.dtype)
        yield

    def advance(gen):
        if gen is not None:
            for _ in range(per):
                next(gen)

    def rope_tables(c):
        half = GLA_BLOCK // 2
        top = slice(c * GLA_BLOCK, c * GLA_BLOCK + half)
        bot = slice(c * GLA_BLOCK + half, (c + 1) * GLA_BLOCK)
        ang = jnp.where(low, pos_ref[top, :], pos_ref[bot, :]) * freq_ref[...]
        cos, sin = jnp.cos(ang), jnp.sin(ang)
        cos_x, sin_x = pltpu.roll(cos, DIL_HD // 2, 1), pltpu.roll(sin, DIL_HD // 2, 1)
        cos_ref[top, :] = jnp.where(low, cos, cos_x)
        cos_ref[bot, :] = jnp.where(low, cos_x, cos)
        sin_ref[top, :] = jnp.where(low, -sin, sin_x)
        sin_ref[bot, :] = jnp.where(low, -sin_x, sin)

    per = GLA_WIDE // GLA_BLOCK
    gen = late_gates = None
    for wb in range(x_ref.shape[0] // GLA_WIDE):
        rows = slice(wb * GLA_WIDE, (wb + 1) * GLA_WIDE)
        x = x_ref[rows, :]
        ms = jnp.mean(x * x, axis=-1, keepdims=True)
        h = (x * lax.rsqrt(ms + EPS) * ng_ref[...]).astype(BF16)
        h_ref[rows, :] = h
        alr = _dot(h, walr_ref[...]).astype(BF16)
        v = _dot(h, wv_ref[...]).astype(BF16)
        advance(gen)
        for c in range(wb * per, (wb + 1) * per):
            rope_tables(c)
        z = _dot(alr, wa2_ref[...]) + ba_ref[...]
        la = (jnp.minimum(z, 0.0) - jnp.log(1.0 + jnp.exp(-jnp.abs(z)))) * (1.0 / GLA_TAU)
        hi = la.astype(BF16)
        lo = (la - hi.astype(F32)).astype(BF16)
        r = _silu_of_twice(_dot(h, wr_ref[...]))
        advance(gen)
        if late_gates is not None:
            gate_tiles(*late_gates, 0, 2)
        advance(gen)
        qk = _dot(h, wqk_ref[...])
        advance(gen)
        if late_gates is not None:
            gate_tiles(*late_gates, 2, len(gate_plan))
        late_gates = (rows, h)
        blocks = [slice(i * GLA_BLOCK, (i + 1) * GLA_BLOCK) for i in range(per)]
        gen = itertools.chain(*[recurrence(wb * per + i, qk[blk], hi[blk], lo[blk], v[blk], r[blk])
                                for i, blk in enumerate(blocks)])
    advance(gen)
    gate_tiles(*late_gates, 0, 2)
    advance(gen)
    gate_tiles(*late_gates, 2, len(gate_plan))
    for _ in gen:
        pass


def _nat_gla(x2, norm_gain, pos, freq, w, wa2, ba, gain, seq):
    t, d = x2.shape
    tm = GLA_TOK_TILE
    tok = lambda i: (i, 0)
    fixed = lambda i: (0, 0)
    small = (wa2, ba, gain)
    n_gates = 2 * D_MODEL + DIL_OUT
    table = jax.ShapeDtypeStruct((t, DIL_HD), F32)
    return pl.pallas_call(
        functools.partial(_nat_gla_kernel, tiles_per_seq=seq // tm),
        grid=(t // tm,),
        in_specs=[pl.BlockSpec((tm, d), tok), pl.BlockSpec((1, d), fixed),
                  pl.BlockSpec((tm, 1), tok), pl.BlockSpec((1, DIL_HD), fixed),
                  _w_cols(2 * GLA_QK, LO_QK), _w_cols(GLA_V, LO_V), _w_cols(GLA_V, LO_R),
                  _w_cols(D_MODEL, HI_GA), _w_cols(D_MODEL, HI_GD), _w_cols(DIL_OUT, HI_ZD),
                  _w_cols(LANES, LO_ALR)]
        + [pl.BlockSpec(w.shape, fixed) for w in small],
        out_specs=[pl.BlockSpec((tm, GLA_V), tok), pl.BlockSpec((tm, n_gates), tok),
                   pl.BlockSpec((tm, d), tok), pl.BlockSpec((tm, DIL_HD), tok),
                   pl.BlockSpec((tm, DIL_HD), tok)],
        out_shape=[jax.ShapeDtypeStruct((t, GLA_V), BF16), jax.ShapeDtypeStruct((t, n_gates), BF16),
                   jax.ShapeDtypeStruct((t, d), BF16), table, table],
        scratch_shapes=[pltpu.VMEM((GLA_HEADS, GLA_DK, GLA_DV), F32)],
        compiler_params=_params("arbitrary"),
        name="nat_gla",
    )(x2, norm_gain.reshape(1, d), pos, freq, *([w] * 7), *small)


def _dil_proj_kernel(h_ref, wq_ref, wk_ref, wv_ref, g_ref, cos_ref, sin_ref, q_ref, k_ref, v_ref,
                     *scratch, dil):
    n_planes = COL_TILE // LANES
    tm = h_ref.shape[0]
    outs = (q_ref, k_ref, v_ref)

    def store(kind, rows, plane, y):
        if dil == 1:
            outs[kind][0, 0, rows, plane * LANES:(plane + 1) * LANES] = y.astype(q_ref.dtype)
        else:
            scratch[0][kind, plane, rows, :] = y

    def permute(kind, part):
        s0 = min(dil, ROW_STRIDE)
        s1 = dil // s0
        t0 = part * PERMUTE_ROWS
        mid = slice(t0 // s0, (t0 + PERMUTE_ROWS) // s0)
        dst = slice(t0 // dil, (t0 + PERMUTE_ROWS) // dil)
        for plane in range(n_planes):
            cols = slice(plane * LANES, (plane + 1) * LANES)
            for r0 in range(s0):
                hop = scratch[0][kind, plane, pl.ds(t0 + r0, PERMUTE_ROWS // s0, stride=s0), :]
                if s1 == 1:
                    outs[kind][0, r0, dst, cols] = hop.astype(q_ref.dtype)
                    continue
                scratch[1][kind, plane, r0, mid, :] = hop
                for r1 in range(s1):
                    outs[kind][0, s0 * r1 + r0, dst, cols] = scratch[1][
                        kind, plane, r0, pl.ds(mid.start + r1, PERMUTE_ROWS // dil, stride=s1), :
                    ].astype(q_ref.dtype)

    def epilogue(rows, accs):
        for kind, acc in enumerate(accs):
            for hh in range(n_planes):
                xh = acc[:, hh * DIL_HD:(hh + 1) * DIL_HD]
                if kind == 2:
                    store(kind, rows, hh, xh)
                    continue
                ms = jnp.mean(xh * xh, axis=-1, keepdims=True)
                y = xh * lax.rsqrt(ms + EPS) * g_ref[kind:kind + 1, :]
                y = y * cos_ref[rows, :] + pltpu.roll(y, DIL_HD // 2, 1) * sin_ref[rows, :]
                store(kind, rows, hh, y)

    def finish(rows, accs):
        epilogue(rows, accs)
        if dil > 1 and rows.stop % PERMUTE_ROWS == 0:
            for kind in range(3):
                permute(kind, rows.stop // PERMUTE_ROWS - 1)

    pending = None
    for rows in _row_subs(tm, DIL_ROW_SUB):
        h = h_ref[rows, :]
        accs = [_dot(h, w_ref[...]) for w_ref in (wq_ref, wk_ref, wv_ref)]
        if pending is not None:
            finish(*pending)
        pending = (rows, accs)
    finish(*pending)


def _dil_proj(h, w, group, dil, gains, cos, sin, batch):
    t, d = h.shape
    seq = t // batch
    tm = TOK_TILE
    nt = seq // tm
    tok = lambda b, i: (b * nt + i, 0)
    fixed = lambda b, i: (0, 0)
    out = jax.ShapeDtypeStruct((batch, dil, seq // dil, COL_TILE), BF16)
    n_groups = len(DIL_GROUPS)
    assert DIL_HD == LANES and PERMUTE_ROWS % (dil * BF16_ROWS) == 0 and tm % PERMUTE_ROWS == 0
    return pl.pallas_call(
        functools.partial(_dil_proj_kernel, dil=dil),
        grid=(batch, nt),
        in_specs=[pl.BlockSpec((tm, d), tok)]
        + [_w_cols(COL_TILE, HI_QD + kind * n_groups + group) for kind in range(3)]
        + [pl.BlockSpec(gains.shape, fixed),
           pl.BlockSpec((tm, DIL_HD), tok),
           pl.BlockSpec((tm, DIL_HD), tok)],
        out_specs=[pl.BlockSpec((1, dil, tm // dil, COL_TILE), lambda b, i: (b, 0, i, 0))] * 3,
        out_shape=[out] * 3,
        scratch_shapes=[] if dil == 1 else [
            pltpu.VMEM((3, COL_TILE // LANES, tm, LANES), F32),
            pltpu.VMEM((3, COL_TILE // LANES, ROW_STRIDE, tm // ROW_STRIDE, LANES), F32)],
        compiler_params=_params("parallel", "parallel", vmem_limit=DIL_PROJ_VMEM_LIMIT_BYTES),
        name=f"dil_proj_d{dil}",
    )(h, w, w, w, gains, cos, sin)


def _dil_attn_kernel(q_ref, kp_ref, kc_ref, vp_ref, vc_ref, o_ref, lse_ref):
    n_qblk = q_ref.shape[1] // ATT_BLOCK
    row = lax.broadcasted_iota(jnp.int32, (ATT_BLOCK, 2 * ATT_BLOCK), 0)
    col = lax.broadcasted_iota(jnp.int32, (ATT_BLOCK, 2 * ATT_BLOCK), 1)
    band = (col >= row) & (col <= row + ATT_BLOCK)
    band_first = band & ((col >= ATT_BLOCK) | (pl.program_id(1) > 0))
    ones = jnp.ones((2 * ATT_BLOCK, DIL_HD), BF16)
    lane = lax.broadcasted_iota(jnp.int32, (ATT_BLOCK, LANES), 1)
    for sub, a in [(sub, a) for sub in range(q_ref.shape[0]) for a in range(n_qblk)]:
        rows = slice(a * ATT_BLOCK, (a + 1) * ATT_BLOCK)
        lse_all = jnp.zeros((ATT_BLOCK, LANES), F32)
        for hh in range(DIL_HEADS):
            sl = slice(hh * DIL_HD, (hh + 1) * DIL_HD)
            if a == 0:
                k_win = jnp.concatenate([kp_ref[sub, :, sl], kc_ref[sub, :ATT_BLOCK, sl]], axis=0)
                v_win = jnp.concatenate([vp_ref[sub, :, sl], vc_ref[sub, :ATT_BLOCK, sl]], axis=0)
            else:
                win = slice((a - 1) * ATT_BLOCK, (a + 1) * ATT_BLOCK)
                k_win = kc_ref[sub, win, sl]
                v_win = vc_ref[sub, win, sl]
            s = _dot_nt(q_ref[sub, rows, sl], k_win)
            s = jnp.where(band_first if a == 0 else band, s, NEG)
            m = jnp.max(s, axis=-1, keepdims=True)
            p = jnp.exp2(s - m).astype(BF16)
            ol = _dot(p, jnp.concatenate([v_win, ones], axis=1))
            l = ol[:, DIL_HD:]
            o_ref[sub, rows, sl] = (ol[:, :DIL_HD] / l).astype(o_ref.dtype)
            lse_all = jnp.where(lane == hh, m + jnp.log(l) * LOG2_E, lse_all)
        lse_ref[sub, rows, :] = lse_all


def _dil_attn(q, k, v, n_sub, dil):
    sub_len = q.size // COL_TILE // n_sub
    qb = min(sub_len, ATT_SUPER)
    ns = ATT_SUPER // qb
    n_qblk = qb // ATT_BLOCK
    q3, k3, v3 = (a.reshape(n_sub, sub_len, COL_TILE) for a in (q, k, v))
    blk = (ns, qb, COL_TILE)
    cur = lambda s, i: (s, i, 0)
    prev = lambda s, i: (s, jnp.maximum(i * n_qblk - 1, 0), 0)
    prev_blk = (ns, ATT_BLOCK, COL_TILE)
    o, lse = pl.pallas_call(
        _dil_attn_kernel,
        grid=(n_sub // ns, sub_len // qb),
        in_specs=[pl.BlockSpec(blk, cur), pl.BlockSpec(prev_blk, prev), pl.BlockSpec(blk, cur),
                  pl.BlockSpec(prev_blk, prev), pl.BlockSpec(blk, cur)],
        out_specs=[pl.BlockSpec(blk, cur), pl.BlockSpec((ns, qb, LANES), cur)],
        out_shape=[jax.ShapeDtypeStruct((n_sub, sub_len, COL_TILE), BF16),
                   jax.ShapeDtypeStruct((n_sub, sub_len, LANES), F32)],
        compiler_params=_params("parallel", "arbitrary"),
        name=f"dil_attn_d{dil}",
    )(q3, k3, k3, v3, v3)
    return o, lse


def _merge_kernel(x_ref, oa_ref, o0_ref, o1_ref, o2_ref, l0_ref, l1_ref, l2_ref,
                  sz_ref, sga_ref, sgd_ref, wga_ref, wdo_ref, wo_ref, out_ref, og_ref, lg_ref, hop_ref):
    tm = x_ref.shape[1]
    for g, ((_, dil), o_ref, l_ref) in enumerate(zip(DIL_GROUPS, (o0_ref, o1_ref, o2_ref),
                                                     (l0_ref, l1_ref, l2_ref))):
        s0 = min(dil, ROW_STRIDE)
        s1 = dil // s0
        planes = [(lg_ref.at[g], lambda r: l_ref[0, r])] + [
            (og_ref.at[g, hh], lambda r, hh=hh: o_ref[0, r, :, hh * DIL_HD:(hh + 1) * DIL_HD].astype(F32))
            for hh in range(DIL_HEADS)]
        for p, (dst_ref, src) in enumerate(planes):
            for r0 in range(s0):
                if s1 == 1:
                    dst_ref[pl.ds(r0, tm // s0, stride=s0), :] = src(r0)
                    continue
                for r1 in range(s1):
                    hop_ref[p, pl.ds(r1, tm // dil, stride=s1), :] = src(s0 * r1 + r0)
                dst_ref[pl.ds(r0, tm // s0, stride=s0), :] = hop_ref[p, 0:tm // s0, :]
    n_groups = len(DIL_GROUPS)

    def combine(rows):
        lses = [lg_ref[g, rows, :] for g in range(n_groups)]
        m = jnp.maximum(jnp.maximum(lses[0], lses[1]), lses[2])
        es = [jnp.exp2(l - m) for l in lses]
        den = es[0] + es[1] + es[2]
        ws = [e / den for e in es]
        heads = []
        for hh in range(DIL_HEADS):
            acc = ws[0][:, hh:hh + 1] * og_ref[0, hh, rows, :]
            for g in range(1, n_groups):
                acc = acc + ws[g][:, hh:hh + 1] * og_ref[g, hh, rows, :]
            heads.append(acc)
        return (jnp.concatenate(heads, axis=-1) * sz_ref[rows, :].astype(F32)).astype(BF16)

    def branches(rows, o_d):
        y_a = _dot(oa_ref[rows, :], wga_ref[...])
        y_d = _dot(o_d, wdo_ref[...])
        return y_a, y_d

    def gate(rows, y_a, y_d):
        return (sga_ref[rows, :].astype(F32) * y_a + sgd_ref[rows, :].astype(F32) * y_d).astype(BF16)

    def project(rows, y):
        out_ref[0, rows, :] = x_ref[0, rows, :] + _dot(y, wo_ref[...])

    for rows in _row_subs(tm, MERGE_ROW_SUB):
        project(rows, gate(rows, *branches(rows, combine(rows))))


def _merge(x, o_a, o_ds, lses, gates, wga, wdo, wo):
    batch, seq, d = x.shape
    tm = MERGE_TOK_TILE
    nt = seq // tm
    tok = lambda b, i: (b * nt + i, 0)
    fixed = lambda b, i: (0, 0)
    dil_spec = lambda dil, w: pl.BlockSpec((1, dil, tm // dil, w), lambda b, i: (b, 0, i, 0))
    o_ds = [o.reshape(batch, dil, seq // dil, DIL_OUT) for o, (_, dil) in zip(o_ds, DIL_GROUPS)]
    lses = [l.reshape(batch, dil, seq // dil, LANES) for l, (_, dil) in zip(lses, DIL_GROUPS)]
    return pl.pallas_call(
        _merge_kernel,
        grid=(batch, nt),
        in_specs=[pl.BlockSpec((1, tm, d), lambda b, i: (b, i, 0)),
                  pl.BlockSpec((tm, GLA_V), tok)]
        + [dil_spec(dil, DIL_OUT) for _, dil in DIL_GROUPS]
        + [dil_spec(dil, LANES) for _, dil in DIL_GROUPS]
        + [pl.BlockSpec((tm, DIL_OUT), lambda b, i: (b * nt + i, GATE_ZD)),
           pl.BlockSpec((tm, d), lambda b, i: (b * nt + i, GATE_GA * COL_TILE // D_MODEL)),
           pl.BlockSpec((tm, d), lambda b, i: (b * nt + i, GATE_GD * COL_TILE // D_MODEL)),
           pl.BlockSpec(wga.shape, fixed),
           pl.BlockSpec(wdo.shape, fixed),
           pl.BlockSpec(wo.shape, fixed)],
        out_specs=pl.BlockSpec((1, tm, d), lambda b, i: (b, i, 0)),
        out_shape=jax.ShapeDtypeStruct(x.shape, x.dtype),
        scratch_shapes=[pltpu.VMEM((len(DIL_GROUPS), DIL_HEADS, tm, DIL_HD), F32),
                        pltpu.VMEM((len(DIL_GROUPS), tm, LANES), F32),
                        pltpu.VMEM((DIL_HEADS + 1, tm // ROW_STRIDE, LANES), F32)],
        compiler_params=_params("parallel", "parallel"),
        name="merge_out",
    )(x, o_a, *o_ds, *lses, gates, gates, gates, wga, wdo, wo)


def kernel(x, positions, norm_gain, w_in, gla_w_a2, gla_b_a, gla_out_gain, dil_q_gain, dil_k_gain,
           w_gla_out, w_dil_out, w_o):
    batch, seq, d = x.shape
    t = batch * seq
    half = DIL_HD // 2
    inv_freq = ROPE_THETA ** (-jnp.arange(half, dtype=F32) / half)
    freq = jnp.concatenate([inv_freq, inv_freq]).reshape(1, DIL_HD)
    pos = positions.astype(F32).reshape(t, 1)
    for layer in range(norm_gain.shape[0]):
        w = _w_cast(jnp.swapaxes(w_in[layer], 0, 1))
        wa2 = jnp.pad(gla_w_a2[layer], ((0, LANES - GLA_RANK), (0, 0))).astype(BF16)
        ba = gla_b_a[layer].reshape(1, GLA_QK)
        g_dqk = jnp.stack([dil_q_gain[layer] * (DIL_HD ** -0.5 * LOG2_E), dil_k_gain[layer]])

        o_a, gates, h, cos, sin = _nat_gla(x.reshape(t, d), norm_gain[layer], pos, freq, w,
                                           wa2, ba, gla_out_gain[layer].reshape(1, GLA_DV), seq)
        o_ds, lses = [], []
        for g, (win, dil) in enumerate(DIL_GROUPS):
            assert win // dil == ATT_BLOCK
            q_g, k_g, v_g = _dil_proj(h, w, g, dil, g_dqk, cos, sin, batch)
            o_g, lse_g = _dil_attn(q_g, k_g, v_g, batch * dil, dil)
            o_ds.append(o_g)
            lses.append(lse_g)
        x = _merge(x, o_a, o_ds, lses, gates,
                   w_gla_out[layer].astype(BF16), w_dil_out[layer].astype(BF16),
                   w_o[layer].astype(BF16))
    return x
```

```python
import functools
import itertools

import jax
import jax.numpy as jnp
from jax import lax
from jax.experimental import pallas as pl
from jax.experimental.pallas import tpu as pltpu

D_MODEL = 1024
EPS = 1e-6
ROPE_THETA = 10000.0
GLA_HEADS = 4
GLA_DK = 128
GLA_DV = 256
GLA_RANK = 16
GLA_TAU = 16.0
GLA_QK = GLA_HEADS * GLA_DK
GLA_V = GLA_HEADS * GLA_DV
DIL_GROUPS = ((128, 1), (512, 4), (2048, 16))
DIL_HEADS = 4
DIL_HD = 128
DIL_QK = len(DIL_GROUPS) * DIL_HEADS * DIL_HD
DIL_OUT = DIL_HEADS * DIL_HD
IN_SPLIT_SIZES = (GLA_QK, GLA_QK, GLA_V, GLA_V, GLA_RANK,
                  DIL_QK, DIL_QK, DIL_QK, DIL_OUT, D_MODEL, D_MODEL)

LANES = 128
BF16_ROWS = 16
ROW_STRIDE = 4
GLA_BLOCK = 128
GLA_MID = GLA_BLOCK // 2
ATT_BLOCK = 128
ATT_SUPER = 2048
TOK_TILE = 1024
DIL_ROW_SUB = 256
PERMUTE_ROWS = 512
GLA_TOK_TILE = 512
GLA_WIDE = 128
MERGE_TOK_TILE = 512
MERGE_ROW_SUB = 512
COL_TILE = DIL_HEADS * DIL_HD
VMEM_LIMIT_BYTES = 40 * 1024 * 1024

F32 = jnp.float32
BF16 = jnp.bfloat16
NEG = -1e30
LOG2_E = 1.4426950408889634

GATE_GA, GATE_GD, GATE_ZD = 0, 2, 4

(_QA, _KA, _VA, _RA, _ALR, _QD, _KD, _VD, _ZD, _GA, _GD) = (
    sum(IN_SPLIT_SIZES[:i]) for i in range(len(IN_SPLIT_SIZES)))


def _block_index(offset, width):
    assert offset % width == 0
    return offset // width


W_BLOCK = 1024
LO_PAD = -(-(_ALR + LANES) // W_BLOCK) * W_BLOCK
LO_QK, LO_V, LO_R = _block_index(_QA, 2 * GLA_QK), _block_index(_VA, GLA_V), _block_index(_RA, GLA_V)
LO_ALR = _block_index(_ALR, LANES)
HI_QD = _block_index(LO_PAD, COL_TILE)
HI_ZD = _block_index(LO_PAD + _ZD - _QD, DIL_OUT)
HI_GA, HI_GD = _block_index(LO_PAD + _GA - _QD, D_MODEL), _block_index(LO_PAD + _GD - _QD, D_MODEL)
assert _KA == _QA + GLA_QK and _KD - _QD == DIL_QK and _VD - _KD == DIL_QK


def _w_cols(width, index):
    return pl.BlockSpec((D_MODEL, width), lambda *_: (0, index))


def _params(*sem):
    return pltpu.CompilerParams(dimension_semantics=sem, vmem_limit_bytes=VMEM_LIMIT_BYTES)


def _dot(a, b):
    return jnp.dot(a, b, preferred_element_type=F32)


def _dot_nt(a, b):
    return lax.dot_general(a, b, (((1,), (1,)), ((), ())), preferred_element_type=F32)


def _dot_tn(a, b):
    return lax.dot_general(a, b, (((0,), (0,)), ((), ())), preferred_element_type=F32)


def _sigmoid_of_twice(half_x):
    return 0.5 * jnp.tanh(half_x) + 0.5


def _silu_of_twice(half_x):
    return half_x * jnp.tanh(half_x) + half_x


def _row_subs(n, sub):
    return [slice(r * sub, (r + 1) * sub) for r in range(n // sub)]


def _w_cast_kernel(wt_ref, o_ref, *, n_lo):
    j = pl.program_id(0)
    start = jnp.where(j < n_lo, j * W_BLOCK, _QD + (j - n_lo) * W_BLOCK)
    col = start + lax.broadcasted_iota(jnp.int32, (W_BLOCK, 1), 0)
    is_gate = ((col >= _RA) & (col < _ALR)) | (col >= _ZD)
    o_ref[...] = (wt_ref[...] * jnp.where(is_gate, 0.5, 1.0)).T.astype(o_ref.dtype)


def _w_cast(w_t):
    n, d = w_t.shape
    n_lo = _block_index(LO_PAD, W_BLOCK)
    n_hi = _block_index(n - _QD, W_BLOCK)
    assert _QD % BF16_ROWS == 0 and LO_PAD <= n

    def rows(j):
        start = jnp.where(j < n_lo, j * W_BLOCK, _QD + (j - n_lo) * W_BLOCK)
        return pl.multiple_of(start, BF16_ROWS), 0

    return pl.pallas_call(
        functools.partial(_w_cast_kernel, n_lo=n_lo),
        grid=(n_lo + n_hi,),
        in_specs=[pl.BlockSpec((pl.Element(W_BLOCK), pl.Element(d)), rows)],
        out_specs=pl.BlockSpec((d, W_BLOCK), lambda j: (0, j)),
        out_shape=jax.ShapeDtypeStruct((d, (n_lo + n_hi) * W_BLOCK), BF16),
        compiler_params=_params("parallel"),
        name="w_cast",
    )(w_t)


def _nat_gla_kernel(x_ref, ng_ref, pos_ref, freq_ref, wqk_ref, wv_ref, wr_ref, wga_ref, wgd_ref,
                    wzd_ref, walr_ref, wa2_ref, ba_ref, gain_ref,
                    oa_ref, gates_ref, h_ref, cos_ref, sin_ref, st_ref, *, tiles_per_seq):
    @pl.when(pl.program_id(0) % tiles_per_seq == 0)
    def _():
        st_ref[...] = jnp.zeros_like(st_ref)

    row = lax.broadcasted_iota(jnp.int32, (GLA_BLOCK, GLA_BLOCK), 0)
    col = lax.broadcasted_iota(jnp.int32, (GLA_BLOCK, GLA_BLOCK), 1)
    causal = col <= row
    tri = causal.astype(BF16)
    tri2 = jnp.concatenate([tri, tri], axis=1)
    low = lax.broadcasted_iota(jnp.int32, (GLA_BLOCK // 2, DIL_HD), 1) < DIL_HD // 2

    gate_plan = ((wga_ref, 0, "sigmoid"), (wga_ref, 1, "sigmoid"), (wgd_ref, 0, "sigmoid"),
                 (wgd_ref, 1, "sigmoid"), (wzd_ref, 0, "silu"))

    def gate_tiles(rows, h, lo_j, hi_j):
        for j in range(lo_j, hi_j):
            w_ref, wj, act = gate_plan[j]
            acc = _dot(h, w_ref[:, wj * COL_TILE:(wj + 1) * COL_TILE])
            act_fn = _silu_of_twice if act == "silu" else _sigmoid_of_twice
            gates_ref[rows, j * COL_TILE:(j + 1) * COL_TILE] = act_fn(acc).astype(gates_ref.dtype)

    def recurrence(c, qk, hi, lo, v, r):
        rows = slice(c * GLA_BLOCK, (c + 1) * GLA_BLOCK)
        b = _dot(tri2, jnp.concatenate([hi, lo], axis=0))
        b_mid = b[GLA_MID - 1:GLA_MID]
        b_last = b[GLA_BLOCK - 1:GLA_BLOCK]
        q = qk[:, :GLA_QK] * (GLA_DK ** -0.5)
        k = qk[:, GLA_QK:]
        q_in = (q * jnp.exp(b)).astype(BF16)
        q_mid = (q * jnp.exp(b - b_mid)).astype(BF16)
        k_mid = (k * jnp.exp(b_mid - b)).astype(BF16)
        k_end = (k * jnp.exp(b_last - b)).astype(BF16)
        dec = jnp.exp(b_last)
        yield
        heads = [(slice(hh * GLA_DK, (hh + 1) * GLA_DK), slice(hh * GLA_DV, (hh + 1) * GLA_DV))
                 for hh in range(GLA_HEADS)]
        attn = [_dot_nt(q_mid[:, ks], k_mid[:, ks]) for ks, _ in heads]
        kv = [_dot_tn(k_end[:, ks], v[:, vs]) for ks, vs in heads]
        yield
        outs = []
        for hh, (ks, vs) in enumerate(heads):
            st = st_ref[hh]
            a = jnp.where(causal, attn[hh], 0.0).astype(BF16)
            outs.append(_dot(jnp.concatenate([a, q_in[:, ks]], axis=1),
                             jnp.concatenate([v[:, vs], st.astype(BF16)], axis=0)))
            d_col = jnp.broadcast_to(dec[:, ks], (GLA_DK, GLA_DK)).T
            st_ref[hh] = st * jnp.concatenate([d_col] * (GLA_DV // GLA_DK), axis=1) + kv[hh]
        yield
        for (_, vs), o in zip(heads, outs):
            ms = jnp.mean(o * o, axis=-1, keepdims=True)
            o = o * lax.rsqrt(ms + EPS) * gain_ref[...] * r[:, vs]
            oa_ref[rows, vs] = o.astype(oa_ref.dtype)
        yield

    def advance(gen):
        if gen is not None:
            for _ in range(per):
                next(gen)

    def rope_tables(c):
        half = GLA_BLOCK // 2
        top = slice(c * GLA_BLOCK, c * GLA_BLOCK + half)
        bot = slice(c * GLA_BLOCK + half, (c + 1) * GLA_BLOCK)
        pos = jnp.broadcast_to(pos_ref[c], (GLA_BLOCK, GLA_BLOCK)).T
        ang = jnp.where(low, pos[:half], pos[half:]) * freq_ref[...]
        cos, sin = jnp.cos(ang), jnp.sin(ang)
        cos_x, sin_x = pltpu.roll(cos, DIL_HD // 2, 1), pltpu.roll(sin, DIL_HD // 2, 1)
        cos_ref[top, :] = jnp.where(low, cos, cos_x)
        cos_ref[bot, :] = jnp.where(low, cos_x, cos)
        sin_ref[top, :] = jnp.where(low, -sin, sin_x)
        sin_ref[bot, :] = jnp.where(low, -sin_x, sin)

    per = GLA_WIDE // GLA_BLOCK
    gen = late_gates = None
    for wb in range(x_ref.shape[0] // GLA_WIDE):
        rows = slice(wb * GLA_WIDE, (wb + 1) * GLA_WIDE)
        x = x_ref[rows, :]
        ms = jnp.mean(x * x, axis=-1, keepdims=True)
        h = (x * lax.rsqrt(ms + EPS) * ng_ref[...]).astype(BF16)
        h_ref[rows, :] = h
        alr = _dot(h, walr_ref[...]).astype(BF16)
        v = _dot(h, wv_ref[...]).astype(BF16)
        advance(gen)
        z = _dot(alr, wa2_ref[...]) + ba_ref[...]
        la = (jnp.minimum(z, 0.0) - jnp.log(1.0 + jnp.exp(-jnp.abs(z)))) * (1.0 / GLA_TAU)
        hi = la.astype(BF16)
        lo = (la - hi.astype(F32)).astype(BF16)
        r = _silu_of_twice(_dot(h, wr_ref[...]))
        advance(gen)
        if late_gates is not None:
            gate_tiles(*late_gates, 0, 2)
        advance(gen)
        qk = _dot(h, wqk_ref[...])
        for c in range(wb * per, (wb + 1) * per):
            rope_tables(c)
        advance(gen)
        if late_gates is not None:
            gate_tiles(*late_gates, 2, len(gate_plan))
        late_gates = (rows, h)
        blocks = [slice(i * GLA_BLOCK, (i + 1) * GLA_BLOCK) for i in range(per)]
        gen = itertools.chain(*[recurrence(wb * per + i, qk[blk], hi[blk], lo[blk], v[blk], r[blk])
                                for i, blk in enumerate(blocks)])
    advance(gen)
    gate_tiles(*late_gates, 0, 2)
    advance(gen)
    gate_tiles(*late_gates, 2, len(gate_plan))
    for _ in gen:
        pass


def _nat_gla(x2, norm_gain, pos, freq, w, wa2, ba, gain, seq):
    t, d = x2.shape
    tm = GLA_TOK_TILE
    tok = lambda i: (i, 0)
    fixed = lambda i: (0, 0)
    small = (wa2, ba, gain)
    n_gates = 2 * D_MODEL + DIL_OUT
    table = jax.ShapeDtypeStruct((t, DIL_HD), F32)
    return pl.pallas_call(
        functools.partial(_nat_gla_kernel, tiles_per_seq=seq // tm),
        grid=(t // tm,),
        in_specs=[pl.BlockSpec((tm, d), tok), pl.BlockSpec((1, d), fixed),
                  pl.BlockSpec((tm // GLA_BLOCK, 1, GLA_BLOCK), lambda i: (i, 0, 0)),
                  pl.BlockSpec((1, DIL_HD), fixed),
                  _w_cols(2 * GLA_QK, LO_QK), _w_cols(GLA_V, LO_V), _w_cols(GLA_V, LO_R),
                  _w_cols(D_MODEL, HI_GA), _w_cols(D_MODEL, HI_GD), _w_cols(DIL_OUT, HI_ZD),
                  _w_cols(LANES, LO_ALR)]
        + [pl.BlockSpec(w.shape, fixed) for w in small],
        out_specs=[pl.BlockSpec((tm, GLA_V), tok), pl.BlockSpec((tm, n_gates), tok),
                   pl.BlockSpec((tm, d), tok), pl.BlockSpec((tm, DIL_HD), tok),
                   pl.BlockSpec((tm, DIL_HD), tok)],
        out_shape=[jax.ShapeDtypeStruct((t, GLA_V), BF16), jax.ShapeDtypeStruct((t, n_gates), BF16),
                   jax.ShapeDtypeStruct((t, d), BF16), table, table],
        scratch_shapes=[pltpu.VMEM((GLA_HEADS, GLA_DK, GLA_DV), F32)],
        compiler_params=_params("arbitrary"),
        name="nat_gla",
    )(x2, norm_gain.reshape(1, d), pos, freq, *([w] * 7), *small)


def _dil_proj_kernel(h_ref, wq_ref, wk_ref, wv_ref, g_ref, cos_ref, sin_ref, q_ref, k_ref, v_ref,
                     *scratch, dil):
    n_planes = COL_TILE // LANES
    tm = h_ref.shape[0]
    outs = (q_ref, k_ref, v_ref)

    def store(kind, rows, plane, y):
        if dil == 1:
            outs[kind][0, 0, rows, plane * LANES:(plane + 1) * LANES] = y.astype(q_ref.dtype)
        else:
            scratch[0][kind, plane, rows, :] = y

    def permute(kind, part):
        s0 = min(dil, ROW_STRIDE)
        s1 = dil // s0
        t0 = part * PERMUTE_ROWS
        mid = slice(t0 // s0, (t0 + PERMUTE_ROWS) // s0)
        dst = slice(t0 // dil, (t0 + PERMUTE_ROWS) // dil)
        for plane in range(n_planes):
            cols = slice(plane * LANES, (plane + 1) * LANES)
            for r0 in range(s0):
                hop = scratch[0][kind, plane, pl.ds(t0 + r0, PERMUTE_ROWS // s0, stride=s0), :]
                if s1 == 1:
                    outs[kind][0, r0, dst, cols] = hop.astype(q_ref.dtype)
                    continue
                scratch[1][kind, plane, r0, mid, :] = hop
                for r1 in range(s1):
                    outs[kind][0, s0 * r1 + r0, dst, cols] = scratch[1][
                        kind, plane, r0, pl.ds(mid.start + r1, PERMUTE_ROWS // dil, stride=s1), :
                    ].astype(q_ref.dtype)

    def epilogue(rows, accs):
        for kind, acc in enumerate(accs):
            for hh in range(n_planes):
                xh = acc[:, hh * DIL_HD:(hh + 1) * DIL_HD]
                if kind == 2:
                    store(kind, rows, hh, xh)
                    continue
                ms = jnp.mean(xh * xh, axis=-1, keepdims=True)
                y = xh * lax.rsqrt(ms + EPS) * g_ref[kind:kind + 1, :]
                y = y * cos_ref[rows, :] + pltpu.roll(y, DIL_HD // 2, 1) * sin_ref[rows, :]
                store(kind, rows, hh, y)

    def finish(rows, accs):
        epilogue(rows, accs)
        if dil > 1 and rows.stop % PERMUTE_ROWS == 0:
            for kind in range(3):
                permute(kind, rows.stop // PERMUTE_ROWS - 1)

    pending = None
    for rows in _row_subs(tm, DIL_ROW_SUB):
        h = h_ref[rows, :]
        accs = [_dot(h, w_ref[...]) for w_ref in (wq_ref, wk_ref, wv_ref)]
        if pending is not None:
            finish(*pending)
        pending = (rows, accs)
    finish(*pending)


def _dil_proj(h, w, group, dil, gains, cos, sin, batch):
    t, d = h.shape
    seq = t // batch
    tm = TOK_TILE
    nt = seq // tm
    tok = lambda b, i: (b * nt + i, 0)
    fixed = lambda b, i: (0, 0)
    out = jax.ShapeDtypeStruct((batch, dil, seq // dil, COL_TILE), BF16)
    n_groups = len(DIL_GROUPS)
    assert DIL_HD == LANES and PERMUTE_ROWS % (dil * BF16_ROWS) == 0 and tm % PERMUTE_ROWS == 0
    return pl.pallas_call(
        functools.partial(_dil_proj_kernel, dil=dil),
        grid=(batch, nt),
        in_specs=[pl.BlockSpec((tm, d), tok)]
        + [_w_cols(COL_TILE, HI_QD + kind * n_groups + group) for kind in range(3)]
        + [pl.BlockSpec(gains.shape, fixed),
           pl.BlockSpec((tm, DIL_HD), tok),
           pl.BlockSpec((tm, DIL_HD), tok)],
        out_specs=[pl.BlockSpec((1, dil, tm // dil, COL_TILE), lambda b, i: (b, 0, i, 0))] * 3,
        out_shape=[out] * 3,
        scratch_shapes=[] if dil == 1 else [
            pltpu.VMEM((3, COL_TILE // LANES, tm, LANES), F32),
            pltpu.VMEM((3, COL_TILE // LANES, ROW_STRIDE, tm // ROW_STRIDE, LANES), F32)],
        compiler_params=_params("parallel", "parallel"),
        name=f"dil_proj_d{dil}",
    )(h, w, w, w, gains, cos, sin)


def _dil_attn_kernel(q_ref, kp_ref, kc_ref, vp_ref, vc_ref, o_ref, lse_ref):
    n_qblk = q_ref.shape[1] // ATT_BLOCK
    row = lax.broadcasted_iota(jnp.int32, (ATT_BLOCK, 2 * ATT_BLOCK), 0)
    col = lax.broadcasted_iota(jnp.int32, (ATT_BLOCK, 2 * ATT_BLOCK), 1)
    band = (col >= row) & (col <= row + ATT_BLOCK)
    band_first = band & ((col >= ATT_BLOCK) | (pl.program_id(1) > 0))
    ones = jnp.ones((2 * ATT_BLOCK, DIL_HD), BF16)
    lane = lax.broadcasted_iota(jnp.int32, (ATT_BLOCK, LANES), 1)
    for sub, a in [(sub, a) for sub in range(q_ref.shape[0]) for a in range(n_qblk)]:
        rows = slice(a * ATT_BLOCK, (a + 1) * ATT_BLOCK)
        lse_all = jnp.zeros((ATT_BLOCK, LANES), F32)
        for hh in range(DIL_HEADS):
            sl = slice(hh * DIL_HD, (hh + 1) * DIL_HD)
            if a == 0:
                k_win = jnp.concatenate([kp_ref[sub, :, sl], kc_ref[sub, :ATT_BLOCK, sl]], axis=0)
                v_win = jnp.concatenate([vp_ref[sub, :, sl], vc_ref[sub, :ATT_BLOCK, sl]], axis=0)
            else:
                win = slice((a - 1) * ATT_BLOCK, (a + 1) * ATT_BLOCK)
                k_win = kc_ref[sub, win, sl]
                v_win = vc_ref[sub, win, sl]
            s = _dot_nt(q_ref[sub, rows, sl], k_win)
            s = jnp.where(band_first if a == 0 else band, s, NEG)
            m = jnp.max(s, axis=-1, keepdims=True)
            p = jnp.exp2(s - m).astype(BF16)
            ol = _dot(p, jnp.concatenate([v_win, ones], axis=1))
            l = ol[:, DIL_HD:]
            o_ref[sub, rows, sl] = (ol[:, :DIL_HD] / l).astype(o_ref.dtype)
            lse_all = jnp.where(lane == hh, m + jnp.log(l) * LOG2_E, lse_all)
        lse_ref[sub, rows, :] = lse_all


def _dil_attn(q, k, v, n_sub, dil):
    sub_len = q.size // COL_TILE // n_sub
    qb = min(sub_len, ATT_SUPER)
    ns = ATT_SUPER // qb
    n_qblk = qb // ATT_BLOCK
    q3, k3, v3 = (a.reshape(n_sub, sub_len, COL_TILE) for a in (q, k, v))
    blk = (ns, qb, COL_TILE)
    cur = lambda s, i: (s, i, 0)
    prev = lambda s, i: (s, jnp.maximum(i * n_qblk - 1, 0), 0)
    prev_blk = (ns, ATT_BLOCK, COL_TILE)
    o, lse = pl.pallas_call(
        _dil_attn_kernel,
        grid=(n_sub // ns, sub_len // qb),
        in_specs=[pl.BlockSpec(blk, cur), pl.BlockSpec(prev_blk, prev), pl.BlockSpec(blk, cur),
                  pl.BlockSpec(prev_blk, prev), pl.BlockSpec(blk, cur)],
        out_specs=[pl.BlockSpec(blk, cur), pl.BlockSpec((ns, qb, LANES), cur)],
        out_shape=[jax.ShapeDtypeStruct((n_sub, sub_len, COL_TILE), BF16),
                   jax.ShapeDtypeStruct((n_sub, sub_len, LANES), F32)],
        compiler_params=_params("parallel", "arbitrary"),
        name=f"dil_attn_d{dil}",
    )(q3, k3, k3, v3, v3)
    return o, lse


def _merge_kernel(x_ref, oa_ref, o0_ref, o1_ref, o2_ref, l0_ref, l1_ref, l2_ref,
                  sz_ref, sga_ref, sgd_ref, wga_ref, wdo_ref, wo_ref, out_ref, og_ref, lg_ref, hop_ref):
    tm = x_ref.shape[1]
    for g, ((_, dil), o_ref, l_ref) in enumerate(zip(DIL_GROUPS, (o0_ref, o1_ref, o2_ref),
                                                     (l0_ref, l1_ref, l2_ref))):
        s0 = min(dil, ROW_STRIDE)
        s1 = dil // s0
        planes = [(lg_ref.at[g], lambda r: l_ref[0, r])] + [
            (og_ref.at[g, hh], lambda r, hh=hh: o_ref[0, r, :, hh * DIL_HD:(hh + 1) * DIL_HD].astype(F32))
            for hh in range(DIL_HEADS)]
        for p, (dst_ref, src) in enumerate(planes):
            for r0 in range(s0):
                if s1 == 1:
                    dst_ref[pl.ds(r0, tm // s0, stride=s0), :] = src(r0)
                    continue
                for r1 in range(s1):
                    hop_ref[p, pl.ds(r1, tm // dil, stride=s1), :] = src(s0 * r1 + r0)
                dst_ref[pl.ds(r0, tm // s0, stride=s0), :] = hop_ref[p, 0:tm // s0, :]
    n_groups = len(DIL_GROUPS)

    def combine(rows):
        lses = [lg_ref[g, rows, :] for g in range(n_groups)]
        m = jnp.maximum(jnp.maximum(lses[0], lses[1]), lses[2])
        es = [jnp.exp2(l - m) for l in lses]
        den = es[0] + es[1] + es[2]
        ws = [e / den for e in es]
        heads = []
        for hh in range(DIL_HEADS):
            acc = ws[0][:, hh:hh + 1] * og_ref[0, hh, rows, :]
            for g in range(1, n_groups):
                acc = acc + ws[g][:, hh:hh + 1] * og_ref[g, hh, rows, :]
            heads.append(acc)
        return (jnp.concatenate(heads, axis=-1) * sz_ref[rows, :].astype(F32)).astype(BF16)

    def branches(rows, o_d):
        y_a = _dot(oa_ref[rows, :], wga_ref[...])
        y_d = _dot(o_d, wdo_ref[...])
        return y_a, y_d

    def gate(rows, y_a, y_d):
        return (sga_ref[rows, :].astype(F32) * y_a + sgd_ref[rows, :].astype(F32) * y_d).astype(BF16)

    def project(rows, y):
        out_ref[0, rows, :] = x_ref[0, rows, :] + _dot(y, wo_ref[...])

    for rows in _row_subs(tm, MERGE_ROW_SUB):
        project(rows, gate(rows, *branches(rows, combine(rows))))


def _merge(x, o_a, o_ds, lses, gates, wga, wdo, wo):
    batch, seq, d = x.shape
    tm = MERGE_TOK_TILE
    nt = seq // tm
    tok = lambda b, i: (b * nt + i, 0)
    fixed = lambda b, i: (0, 0)
    dil_spec = lambda dil, w: pl.BlockSpec((1, dil, tm // dil, w), lambda b, i: (b, 0, i, 0))
    o_ds = [o.reshape(batch, dil, seq // dil, DIL_OUT) for o, (_, dil) in zip(o_ds, DIL_GROUPS)]
    lses = [l.reshape(batch, dil, seq // dil, LANES) for l, (_, dil) in zip(lses, DIL_GROUPS)]
    return pl.pallas_call(
        _merge_kernel,
        grid=(batch, nt),
        in_specs=[pl.BlockSpec((1, tm, d), lambda b, i: (b, i, 0)),
                  pl.BlockSpec((tm, GLA_V), tok)]
        + [dil_spec(dil, DIL_OUT) for _, dil in DIL_GROUPS]
        + [dil_spec(dil, LANES) for _, dil in DIL_GROUPS]
        + [pl.BlockSpec((tm, DIL_OUT), lambda b, i: (b * nt + i, GATE_ZD)),
           pl.BlockSpec((tm, d), lambda b, i: (b * nt + i, GATE_GA * COL_TILE // D_MODEL)),
           pl.BlockSpec((tm, d), lambda b, i: (b * nt + i, GATE_GD * COL_TILE // D_MODEL)),
           pl.BlockSpec(wga.shape, fixed),
           pl.BlockSpec(wdo.shape, fixed),
           pl.BlockSpec(wo.shape, fixed)],
        out_specs=pl.BlockSpec((1, tm, d), lambda b, i: (b, i, 0)),
        out_shape=jax.ShapeDtypeStruct(x.shape, x.dtype),
        scratch_shapes=[pltpu.VMEM((len(DIL_GROUPS), DIL_HEADS, tm, DIL_HD), F32),
                        pltpu.VMEM((len(DIL_GROUPS), tm, LANES), F32),
                        pltpu.VMEM((DIL_HEADS + 1, tm // ROW_STRIDE, LANES), F32)],
        compiler_params=_params("parallel", "parallel"),
        name="merge_out",
    )(x, o_a, *o_ds, *lses, gates, gates, gates, wga, wdo, wo)


def kernel(x, positions, norm_gain, w_in, gla_w_a2, gla_b_a, gla_out_gain, dil_q_gain, dil_k_gain,
           w_gla_out, w_dil_out, w_o):
    batch, seq, d = x.shape
    t = batch * seq
    half = DIL_HD // 2
    inv_freq = ROPE_THETA ** (-jnp.arange(half, dtype=F32) / half)
    freq = jnp.concatenate([inv_freq, inv_freq]).reshape(1, DIL_HD)
    pos = positions.astype(F32).reshape(t // GLA_BLOCK, 1, GLA_BLOCK)
    for layer in range(norm_gain.shape[0]):
        w = _w_cast(jnp.swapaxes(w_in[layer], 0, 1))
        wa2 = jnp.pad(gla_w_a2[layer], ((0, LANES - GLA_RANK), (0, 0))).astype(BF16)
        ba = gla_b_a[layer].reshape(1, GLA_QK)
        g_dqk = jnp.stack([dil_q_gain[layer] * (DIL_HD ** -0.5 * LOG2_E), dil_k_gain[layer]])

        o_a, gates, h, cos, sin = _nat_gla(x.reshape(t, d), norm_gain[layer], pos, freq, w,
                                           wa2, ba, gla_out_gain[layer].reshape(1, GLA_DV), seq)
        o_ds, lses = [], []
        for g, (win, dil) in enumerate(DIL_GROUPS):
            assert win // dil == ATT_BLOCK
            q_g, k_g, v_g = _dil_proj(h, w, g, dil, g_dqk, cos, sin, batch)
            o_g, lse_g = _dil_attn(q_g, k_g, v_g, batch * dil, dil)
            o_ds.append(o_g)
            lses.append(lse_g)
        x = _merge(x, o_a, o_ds, lses, gates,
                   w_gla_out[layer].astype(BF16), w_dil_out[layer].astype(BF16),
                   w_o[layer].astype(BF16))
    return x
```

```python
import functools
import itertools

import jax
import jax.numpy as jnp
from jax import lax
from jax.experimental import pallas as pl
from jax.experimental.pallas import tpu as pltpu

D_MODEL = 1024
EPS = 1e-6
ROPE_THETA = 10000.0
GLA_HEADS = 4
GLA_DK = 128
GLA_DV = 256
GLA_RANK = 16
GLA_TAU = 16.0
GLA_QK = GLA_HEADS * GLA_DK
GLA_V = GLA_HEADS * GLA_DV
DIL_GROUPS = ((128, 1), (512, 4), (2048, 16))
DIL_HEADS = 4
DIL_HD = 128
DIL_QK = len(DIL_GROUPS) * DIL_HEADS * DIL_HD
DIL_OUT = DIL_HEADS * DIL_HD
IN_SPLIT_SIZES = (GLA_QK, GLA_QK, GLA_V, GLA_V, GLA_RANK,
                  DIL_QK, DIL_QK, DIL_QK, DIL_OUT, D_MODEL, D_MODEL)

LANES = 128
BF16_ROWS = 16
ROW_STRIDE = 4
GLA_BLOCK = 128
GLA_MID = GLA_BLOCK // 2
ATT_BLOCK = 128
ATT_SUPER = 2048
TOK_TILE = 1024
DIL_ROW_SUB = 256
PERMUTE_ROWS = 512
GLA_TOK_TILE = 512
GLA_WIDE = 128
MERGE_TOK_TILE = 512
MERGE_ROW_SUB = 512
COL_TILE = DIL_HEADS * DIL_HD
VMEM_LIMIT_BYTES = 40 * 1024 * 1024

F32 = jnp.float32
BF16 = jnp.bfloat16
NEG = -1e30
LOG2_E = 1.4426950408889634

GATE_GA, GATE_GD, GATE_ZD = 0, 2, 4

(_QA, _KA, _VA, _RA, _ALR, _QD, _KD, _VD, _ZD, _GA, _GD) = (
    sum(IN_SPLIT_SIZES[:i]) for i in range(len(IN_SPLIT_SIZES)))


def _block_index(offset, width):
    assert offset % width == 0
    return offset // width


W_BLOCK = 1024
LO_PAD = -(-(_ALR + LANES) // W_BLOCK) * W_BLOCK
LO_QK, LO_V, LO_R = _block_index(_QA, 2 * GLA_QK), _block_index(_VA, GLA_V), _block_index(_RA, GLA_V)
LO_ALR = _block_index(_ALR, LANES)
HI_QD = _block_index(LO_PAD, COL_TILE)
HI_ZD = _block_index(LO_PAD + _ZD - _QD, DIL_OUT)
HI_GA, HI_GD = _block_index(LO_PAD + _GA - _QD, D_MODEL), _block_index(LO_PAD + _GD - _QD, D_MODEL)
assert _KA == _QA + GLA_QK and _KD - _QD == DIL_QK and _VD - _KD == DIL_QK


def _w_cols(width, index):
    return pl.BlockSpec((D_MODEL, width), lambda *_: (0, index))


def _params(*sem):
    return pltpu.CompilerParams(dimension_semantics=sem, vmem_limit_bytes=VMEM_LIMIT_BYTES)


def _dot(a, b):
    return jnp.dot(a, b, preferred_element_type=F32)


def _dot_nt(a, b):
    return lax.dot_general(a, b, (((1,), (1,)), ((), ())), preferred_element_type=F32)


def _dot_tn(a, b):
    return lax.dot_general(a, b, (((0,), (0,)), ((), ())), preferred_element_type=F32)


def _sigmoid_of_twice(half_x):
    return 0.5 * jnp.tanh(half_x) + 0.5


def _silu_of_twice(half_x):
    return half_x * jnp.tanh(half_x) + half_x


def _row_subs(n, sub):
    return [slice(r * sub, (r + 1) * sub) for r in range(n // sub)]


def _w_cast_kernel(wt_ref, o_ref, *, n_lo):
    j = pl.program_id(0)
    start = jnp.where(j < n_lo, j * W_BLOCK, _QD + (j - n_lo) * W_BLOCK)
    col = start + lax.broadcasted_iota(jnp.int32, (W_BLOCK, 1), 0)
    is_gate = ((col >= _RA) & (col < _ALR)) | (col >= _ZD)
    o_ref[...] = (wt_ref[...] * jnp.where(is_gate, 0.5, 1.0)).T.astype(o_ref.dtype)


def _w_cast(w_t):
    n, d = w_t.shape
    n_lo = _block_index(LO_PAD, W_BLOCK)
    n_hi = _block_index(n - _QD, W_BLOCK)
    assert _QD % BF16_ROWS == 0 and LO_PAD <= n

    def rows(j):
        start = jnp.where(j < n_lo, j * W_BLOCK, _QD + (j - n_lo) * W_BLOCK)
        return pl.multiple_of(start, BF16_ROWS), 0

    return pl.pallas_call(
        functools.partial(_w_cast_kernel, n_lo=n_lo),
        grid=(n_lo + n_hi,),
        in_specs=[pl.BlockSpec((pl.Element(W_BLOCK), pl.Element(d)), rows)],
        out_specs=pl.BlockSpec((d, W_BLOCK), lambda j: (0, j)),
        out_shape=jax.ShapeDtypeStruct((d, (n_lo + n_hi) * W_BLOCK), BF16),
        compiler_params=_params("parallel"),
        name="w_cast",
    )(w_t)


def _nat_gla_kernel(x_ref, ng_ref, pos_ref, freq_ref, wqk_ref, wv_ref, wr_ref, wga_ref, wgd_ref,
                    wzd_ref, walr_ref, wa2_ref, ba_ref, gain_ref,
                    oa_ref, gates_ref, h_ref, cos_ref, sin_ref, st_ref, *, tiles_per_seq):
    @pl.when(pl.program_id(0) % tiles_per_seq == 0)
    def _():
        st_ref[...] = jnp.zeros_like(st_ref)

    row = lax.broadcasted_iota(jnp.int32, (GLA_BLOCK, GLA_BLOCK), 0)
    col = lax.broadcasted_iota(jnp.int32, (GLA_BLOCK, GLA_BLOCK), 1)
    causal = col <= row
    tri = causal.astype(BF16)
    tri2 = jnp.concatenate([tri, tri], axis=1)
    low = lax.broadcasted_iota(jnp.int32, (GLA_BLOCK // 2, DIL_HD), 1) < DIL_HD // 2

    gate_plan = ((wga_ref, 0, "sigmoid"), (wga_ref, 1, "sigmoid"), (wgd_ref, 0, "sigmoid"),
                 (wgd_ref, 1, "sigmoid"), (wzd_ref, 0, "silu"))

    def gate_tiles(rows, h, lo_j, hi_j):
        for j in range(lo_j, hi_j):
            w_ref, wj, act = gate_plan[j]
            acc = _dot(h, w_ref[:, wj * COL_TILE:(wj + 1) * COL_TILE])
            act_fn = _silu_of_twice if act == "silu" else _sigmoid_of_twice
            gates_ref[rows, j * COL_TILE:(j + 1) * COL_TILE] = act_fn(acc).astype(gates_ref.dtype)

    def recurrence(c, qk, hi, lo, v, r):
        rows = slice(c * GLA_BLOCK, (c + 1) * GLA_BLOCK)
        b = _dot(tri2, jnp.concatenate([hi, lo], axis=0))
        b_mid = b[GLA_MID - 1:GLA_MID]
        b_last = b[GLA_BLOCK - 1:GLA_BLOCK]
        q = qk[:, :GLA_QK] * (GLA_DK ** -0.5)
        k = qk[:, GLA_QK:]
        q_in = (q * jnp.exp(b)).astype(BF16)
        q_mid = (q * jnp.exp(b - b_mid)).astype(BF16)
        k_mid = (k * jnp.exp(b_mid - b)).astype(BF16)
        k_end = (k * jnp.exp(b_last - b)).astype(BF16)
        dec = jnp.exp(b_last)
        yield
        heads = [(slice(hh * GLA_DK, (hh + 1) * GLA_DK), slice(hh * GLA_DV, (hh + 1) * GLA_DV))
                 for hh in range(GLA_HEADS)]
        attn = [_dot_nt(q_mid[:, ks], k_mid[:, ks]) for ks, _ in heads]
        kv = [_dot_tn(k_end[:, ks], v[:, vs]) for ks, vs in heads]
        yield
        outs = []
        for hh, (ks, vs) in enumerate(heads):
            st = st_ref[hh]
            a = jnp.where(causal, attn[hh], 0.0).astype(BF16)
            outs.append(_dot(jnp.concatenate([a, q_in[:, ks]], axis=1),
                             jnp.concatenate([v[:, vs], st.astype(BF16)], axis=0)))
            d_col = jnp.broadcast_to(dec[:, ks], (GLA_DK, GLA_DK)).T
            st_ref[hh] = st * jnp.concatenate([d_col] * (GLA_DV // GLA_DK), axis=1) + kv[hh]
        yield
        for (_, vs), o in zip(heads, outs):
            ms = jnp.mean(o * o, axis=-1, keepdims=True)
            o = o * lax.rsqrt(ms + EPS) * gain_ref[...] * r[:, vs]
            oa_ref[rows, vs] = o.astype(oa_ref.dtype)
        yield

    def advance(gen):
        if gen is not None:
            for _ in range(per):
                next(gen)

    def rope_tables(c):
        half = GLA_BLOCK // 2
        top = slice(c * GLA_BLOCK, c * GLA_BLOCK + half)
        bot = slice(c * GLA_BLOCK + half, (c + 1) * GLA_BLOCK)
        pos = jnp.broadcast_to(pos_ref[c], (GLA_BLOCK, GLA_BLOCK)).T
        ang = jnp.where(low, pos[:half], pos[half:]) * freq_ref[...]
        cos, sin = jnp.cos(ang), jnp.sin(ang)
        cos_x, sin_x = pltpu.roll(cos, DIL_HD // 2, 1), pltpu.roll(sin, DIL_HD // 2, 1)
        cos_ref[top, :] = jnp.where(low, cos, cos_x)
        cos_ref[bot, :] = jnp.where(low, cos_x, cos)
        sin_ref[top, :] = jnp.where(low, -sin, sin_x)
        sin_ref[bot, :] = jnp.where(low, -sin_x, sin)

    per = GLA_WIDE // GLA_BLOCK
    gen = late_gates = None
    for wb in range(x_ref.shape[0] // GLA_WIDE):
        rows = slice(wb * GLA_WIDE, (wb + 1) * GLA_WIDE)
        x = x_ref[rows, :]
        ms = jnp.mean(x * x, axis=-1, keepdims=True)
        h = (x * lax.rsqrt(ms + EPS) * ng_ref[...]).astype(BF16)
        h_ref[rows, :] = h
        alr = _dot(h, walr_ref[...]).astype(BF16)
        v = _dot(h, wv_ref[...]).astype(BF16)
        advance(gen)
        z = _dot(alr, wa2_ref[...]) + ba_ref[...]
        la = (jnp.minimum(z, 0.0) - jnp.log(1.0 + jnp.exp(-jnp.abs(z)))) * (1.0 / GLA_TAU)
        hi = la.astype(BF16)
        lo = (la - hi.astype(F32)).astype(BF16)
        r = _silu_of_twice(_dot(h, wr_ref[...]))
        advance(gen)
        if late_gates is not None:
            gate_tiles(*late_gates, 0, 2)
        advance(gen)
        qk = _dot(h, wqk_ref[...])
        for c in range(wb * per, (wb + 1) * per):
            rope_tables(c)
        advance(gen)
        if late_gates is not None:
            gate_tiles(*late_gates, 2, len(gate_plan))
        late_gates = (rows, h)
        blocks = [slice(i * GLA_BLOCK, (i + 1) * GLA_BLOCK) for i in range(per)]
        gen = itertools.chain(*[recurrence(wb * per + i, qk[blk], hi[blk], lo[blk], v[blk], r[blk])
                                for i, blk in enumerate(blocks)])
    advance(gen)
    gate_tiles(*late_gates, 0, 2)
    advance(gen)
    gate_tiles(*late_gates, 2, len(gate_plan))
    for _ in gen:
        pass


def _nat_gla(x2, norm_gain, pos, freq, w, wa2, ba, gain, seq):
    t, d = x2.shape
    tm = GLA_TOK_TILE
    tok = lambda i: (i, 0)
    fixed = lambda i: (0, 0)
    small = (wa2, ba, gain)
    n_gates = 2 * D_MODEL + DIL_OUT
    table = jax.ShapeDtypeStruct((t, DIL_HD), F32)
    return pl.pallas_call(
        functools.partial(_nat_gla_kernel, tiles_per_seq=seq // tm),
        grid=(t // tm,),
        in_specs=[pl.BlockSpec((tm, d), tok), pl.BlockSpec((1, d), fixed),
                  pl.BlockSpec((tm // GLA_BLOCK, 1, GLA_BLOCK), lambda i: (i, 0, 0)),
                  pl.BlockSpec((1, DIL_HD), fixed),
                  _w_cols(2 * GLA_QK, LO_QK), _w_cols(GLA_V, LO_V), _w_cols(GLA_V, LO_R),
                  _w_cols(D_MODEL, HI_GA), _w_cols(D_MODEL, HI_GD), _w_cols(DIL_OUT, HI_ZD),
                  _w_cols(LANES, LO_ALR)]
        + [pl.BlockSpec(w.shape, fixed) for w in small],
        out_specs=[pl.BlockSpec((tm, GLA_V), tok), pl.BlockSpec((tm, n_gates), tok),
                   pl.BlockSpec((tm, d), tok), pl.BlockSpec((tm, DIL_HD), tok),
                   pl.BlockSpec((tm, DIL_HD), tok)],
        out_shape=[jax.ShapeDtypeStruct((t, GLA_V), BF16), jax.ShapeDtypeStruct((t, n_gates), BF16),
                   jax.ShapeDtypeStruct((t, d), BF16), table, table],
        scratch_shapes=[pltpu.VMEM((GLA_HEADS, GLA_DK, GLA_DV), F32)],
        compiler_params=_params("arbitrary"),
        name="nat_gla",
    )(x2, norm_gain.reshape(1, d), pos, freq, *([w] * 7), *small)


def _dil_proj_kernel(h_ref, wq_ref, wk_ref, wv_ref, g_ref, cos_ref, sin_ref, q_ref, k_ref, v_ref,
                     *scratch, dil):
    n_planes = COL_TILE // LANES
    tm = h_ref.shape[0]
    outs = (q_ref, k_ref, v_ref)

    def store(kind, rows, plane, y):
        if dil == 1:
            outs[kind][0, 0, rows, plane * LANES:(plane + 1) * LANES] = y.astype(q_ref.dtype)
        elif dil <= ROW_STRIDE:
            scratch[0][kind, plane, rows, :] = y
        else:
            for n in range(rows.start // dil, rows.stop // dil):
                src = slice(n * dil - rows.start, (n + 1) * dil - rows.start)
                scratch[0][kind, plane, pl.ds(n * (dil + 1), dil), :] = y[src]

    def permute(kind, part):
        s0 = min(dil, ROW_STRIDE)
        s1 = dil // s0
        t0 = part * PERMUTE_ROWS
        mid = slice(t0 // s0, (t0 + PERMUTE_ROWS) // s0)
        dst = slice(t0 // dil, (t0 + PERMUTE_ROWS) // dil)
        for plane in range(n_planes):
            cols = slice(plane * LANES, (plane + 1) * LANES)
            if dil > ROW_STRIDE:
                for r in range(dil):
                    outs[kind][0, r, dst, cols] = scratch[0][
                        kind, plane, pl.ds(dst.start * (dil + 1) + r, PERMUTE_ROWS // dil, stride=dil + 1), :
                    ].astype(q_ref.dtype)
                continue
            for r0 in range(s0):
                hop = scratch[0][kind, plane, pl.ds(t0 + r0, PERMUTE_ROWS // s0, stride=s0), :]
                if s1 == 1:
                    outs[kind][0, r0, dst, cols] = hop.astype(q_ref.dtype)
                    continue
                scratch[1][kind, plane, r0, mid, :] = hop
                for r1 in range(s1):
                    outs[kind][0, s0 * r1 + r0, dst, cols] = scratch[1][
                        kind, plane, r0, pl.ds(mid.start + r1, PERMUTE_ROWS // dil, stride=s1), :
                    ].astype(q_ref.dtype)

    def epilogue(rows, accs):
        for kind, acc in enumerate(accs):
            for hh in range(n_planes):
                xh = acc[:, hh * DIL_HD:(hh + 1) * DIL_HD]
                if kind == 2:
                    store(kind, rows, hh, xh)
                    continue
                ms = jnp.mean(xh * xh, axis=-1, keepdims=True)
                y = xh * lax.rsqrt(ms + EPS) * g_ref[kind:kind + 1, :]
                y = y * cos_ref[rows, :] + pltpu.roll(y, DIL_HD // 2, 1) * sin_ref[rows, :]
                store(kind, rows, hh, y)

    def finish(rows, accs):
        epilogue(rows, accs)
        if dil > 1 and rows.stop % PERMUTE_ROWS == 0:
            for kind in range(3):
                permute(kind, rows.stop // PERMUTE_ROWS - 1)

    pending = None
    for rows in _row_subs(tm, DIL_ROW_SUB):
        h = h_ref[rows, :]
        accs = [_dot(h, w_ref[...]) for w_ref in (wq_ref, wk_ref, wv_ref)]
        if pending is not None:
            finish(*pending)
        pending = (rows, accs)
    finish(*pending)


def _dil_proj(h, w, group, dil, gains, cos, sin, batch):
    t, d = h.shape
    seq = t // batch
    tm = TOK_TILE
    nt = seq // tm
    tok = lambda b, i: (b * nt + i, 0)
    fixed = lambda b, i: (0, 0)
    out = jax.ShapeDtypeStruct((batch, dil, seq // dil, COL_TILE), BF16)
    n_groups = len(DIL_GROUPS)
    assert DIL_HD == LANES and PERMUTE_ROWS % (dil * BF16_ROWS) == 0 and tm % PERMUTE_ROWS == 0
    return pl.pallas_call(
        functools.partial(_dil_proj_kernel, dil=dil),
        grid=(batch, nt),
        in_specs=[pl.BlockSpec((tm, d), tok)]
        + [_w_cols(COL_TILE, HI_QD + kind * n_groups + group) for kind in range(3)]
        + [pl.BlockSpec(gains.shape, fixed),
           pl.BlockSpec((tm, DIL_HD), tok),
           pl.BlockSpec((tm, DIL_HD), tok)],
        out_specs=[pl.BlockSpec((1, dil, tm // dil, COL_TILE), lambda b, i: (b, 0, i, 0))] * 3,
        out_shape=[out] * 3,
        scratch_shapes=[] if dil == 1 else [
            pltpu.VMEM((3, COL_TILE // LANES, tm + tm // dil, LANES), F32),
            pltpu.VMEM((3, COL_TILE // LANES, ROW_STRIDE, tm // ROW_STRIDE, LANES), F32)],
        compiler_params=_params("parallel", "parallel"),
        name=f"dil_proj_d{dil}",
    )(h, w, w, w, gains, cos, sin)


def _dil_attn_kernel(q_ref, kp_ref, kc_ref, vp_ref, vc_ref, o_ref, lse_ref):
    n_qblk = q_ref.shape[1] // ATT_BLOCK
    row = lax.broadcasted_iota(jnp.int32, (ATT_BLOCK, 2 * ATT_BLOCK), 0)
    col = lax.broadcasted_iota(jnp.int32, (ATT_BLOCK, 2 * ATT_BLOCK), 1)
    band = (col >= row) & (col <= row + ATT_BLOCK)
    band_first = band & ((col >= ATT_BLOCK) | (pl.program_id(1) > 0))
    ones = jnp.ones((2 * ATT_BLOCK, DIL_HD), BF16)
    lane = lax.broadcasted_iota(jnp.int32, (ATT_BLOCK, LANES), 1)
    for sub, a in [(sub, a) for sub in range(q_ref.shape[0]) for a in range(n_qblk)]:
        rows = slice(a * ATT_BLOCK, (a + 1) * ATT_BLOCK)
        lse_all = jnp.zeros((ATT_BLOCK, LANES), F32)
        for hh in range(DIL_HEADS):
            sl = slice(hh * DIL_HD, (hh + 1) * DIL_HD)
            if a == 0:
                k_win = jnp.concatenate([kp_ref[sub, :, sl], kc_ref[sub, :ATT_BLOCK, sl]], axis=0)
                v_win = jnp.concatenate([vp_ref[sub, :, sl], vc_ref[sub, :ATT_BLOCK, sl]], axis=0)
            else:
                win = slice((a - 1) * ATT_BLOCK, (a + 1) * ATT_BLOCK)
                k_win = kc_ref[sub, win, sl]
                v_win = vc_ref[sub, win, sl]
            s = _dot_nt(q_ref[sub, rows, sl], k_win)
            s = jnp.where(band_first if a == 0 else band, s, NEG)
            m = jnp.max(s, axis=-1, keepdims=True)
            p = jnp.exp2(s - m).astype(BF16)
            ol = _dot(p, jnp.concatenate([v_win, ones], axis=1))
            l = ol[:, DIL_HD:]
            o_ref[sub, rows, sl] = (ol[:, :DIL_HD] / l).astype(o_ref.dtype)
            lse_all = jnp.where(lane == hh, m + jnp.log(l) * LOG2_E, lse_all)
        lse_ref[sub, rows, :] = lse_all


def _dil_attn(q, k, v, n_sub, dil):
    sub_len = q.size // COL_TILE // n_sub
    qb = min(sub_len, ATT_SUPER)
    ns = ATT_SUPER // qb
    n_qblk = qb // ATT_BLOCK
    q3, k3, v3 = (a.reshape(n_sub, sub_len, COL_TILE) for a in (q, k, v))
    blk = (ns, qb, COL_TILE)
    cur = lambda s, i: (s, i, 0)
    prev = lambda s, i: (s, jnp.maximum(i * n_qblk - 1, 0), 0)
    prev_blk = (ns, ATT_BLOCK, COL_TILE)
    o, lse = pl.pallas_call(
        _dil_attn_kernel,
        grid=(n_sub // ns, sub_len // qb),
        in_specs=[pl.BlockSpec(blk, cur), pl.BlockSpec(prev_blk, prev), pl.BlockSpec(blk, cur),
                  pl.BlockSpec(prev_blk, prev), pl.BlockSpec(blk, cur)],
        out_specs=[pl.BlockSpec(blk, cur), pl.BlockSpec((ns, qb, LANES), cur)],
        out_shape=[jax.ShapeDtypeStruct((n_sub, sub_len, COL_TILE), BF16),
                   jax.ShapeDtypeStruct((n_sub, sub_len, LANES), F32)],
        compiler_params=_params("parallel", "arbitrary"),
        name=f"dil_attn_d{dil}",
    )(q3, k3, k3, v3, v3)
    return o, lse


def _merge_kernel(x_ref, oa_ref, o0_ref, o1_ref, o2_ref, l0_ref, l1_ref, l2_ref,
                  sz_ref, sga_ref, sgd_ref, wga_ref, wdo_ref, wo_ref, out_ref, og_ref, lg_ref, hop_ref):
    tm = x_ref.shape[1]
    for g, ((_, dil), o_ref, l_ref) in enumerate(zip(DIL_GROUPS, (o0_ref, o1_ref, o2_ref),
                                                     (l0_ref, l1_ref, l2_ref))):
        s0 = min(dil, ROW_STRIDE)
        s1 = dil // s0
        planes = [(lg_ref.at[g], lambda r: l_ref[0, r])] + [
            (og_ref.at[g, hh], lambda r, hh=hh: o_ref[0, r, :, hh * DIL_HD:(hh + 1) * DIL_HD].astype(F32))
            for hh in range(DIL_HEADS)]
        for p, (dst_ref, src) in enumerate(planes):
            for r0 in range(s0):
                if s1 == 1:
                    dst_ref[pl.ds(r0, tm // s0, stride=s0), :] = src(r0)
                    continue
                for r1 in range(s1):
                    hop_ref[p, pl.ds(r1, tm // dil, stride=s1), :] = src(s0 * r1 + r0)
                dst_ref[pl.ds(r0, tm // s0, stride=s0), :] = hop_ref[p, 0:tm // s0, :]
    n_groups = len(DIL_GROUPS)

    def combine(rows):
        lses = [lg_ref[g, rows, :] for g in range(n_groups)]
        m = jnp.maximum(jnp.maximum(lses[0], lses[1]), lses[2])
        es = [jnp.exp2(l - m) for l in lses]
        den = es[0] + es[1] + es[2]
        ws = [e / den for e in es]
        heads = []
        for hh in range(DIL_HEADS):
            acc = ws[0][:, hh:hh + 1] * og_ref[0, hh, rows, :]
            for g in range(1, n_groups):
                acc = acc + ws[g][:, hh:hh + 1] * og_ref[g, hh, rows, :]
            heads.append(acc)
        return (jnp.concatenate(heads, axis=-1) * sz_ref[rows, :].astype(F32)).astype(BF16)

    def branches(rows, o_d):
        y_a = _dot(oa_ref[rows, :], wga_ref[...])
        y_d = _dot(o_d, wdo_ref[...])
        return y_a, y_d

    def gate(rows, y_a, y_d):
        return (sga_ref[rows, :].astype(F32) * y_a + sgd_ref[rows, :].astype(F32) * y_d).astype(BF16)

    def project(rows, y):
        out_ref[0, rows, :] = x_ref[0, rows, :] + _dot(y, wo_ref[...])

    for rows in _row_subs(tm, MERGE_ROW_SUB):
        project(rows, gate(rows, *branches(rows, combine(rows))))


def _merge(x, o_a, o_ds, lses, gates, wga, wdo, wo):
    batch, seq, d = x.shape
    tm = MERGE_TOK_TILE
    nt = seq // tm
    tok = lambda b, i: (b * nt + i, 0)
    fixed = lambda b, i: (0, 0)
    dil_spec = lambda dil, w: pl.BlockSpec((1, dil, tm // dil, w), lambda b, i: (b, 0, i, 0))
    o_ds = [o.reshape(batch, dil, seq // dil, DIL_OUT) for o, (_, dil) in zip(o_ds, DIL_GROUPS)]
    lses = [l.reshape(batch, dil, seq // dil, LANES) for l, (_, dil) in zip(lses, DIL_GROUPS)]
    return pl.pallas_call(
        _merge_kernel,
        grid=(batch, nt),
        in_specs=[pl.BlockSpec((1, tm, d), lambda b, i: (b, i, 0)),
                  pl.BlockSpec((tm, GLA_V), tok)]
        + [dil_spec(dil, DIL_OUT) for _, dil in DIL_GROUPS]
        + [dil_spec(dil, LANES) for _, dil in DIL_GROUPS]
        + [pl.BlockSpec((tm, DIL_OUT), lambda b, i: (b * nt + i, GATE_ZD)),
           pl.BlockSpec((tm, d), lambda b, i: (b * nt + i, GATE_GA * COL_TILE // D_MODEL)),
           pl.BlockSpec((tm, d), lambda b, i: (b * nt + i, GATE_GD * COL_TILE // D_MODEL)),
           pl.BlockSpec(wga.shape, fixed),
           pl.BlockSpec(wdo.shape, fixed),
           pl.BlockSpec(wo.shape, fixed)],
        out_specs=pl.BlockSpec((1, tm, d), lambda b, i: (b, i, 0)),
        out_shape=jax.ShapeDtypeStruct(x.shape, x.dtype),
        scratch_shapes=[pltpu.VMEM((len(DIL_GROUPS), DIL_HEADS, tm, DIL_HD), F32),
                        pltpu.VMEM((len(DIL_GROUPS), tm, LANES), F32),
                        pltpu.VMEM((DIL_HEADS + 1, tm // ROW_STRIDE, LANES), F32)],
        compiler_params=_params("parallel", "parallel"),
        name="merge_out",
    )(x, o_a, *o_ds, *lses, gates, gates, gates, wga, wdo, wo)


def kernel(x, positions, norm_gain, w_in, gla_w_a2, gla_b_a, gla_out_gain, dil_q_gain, dil_k_gain,
           w_gla_out, w_dil_out, w_o):
    batch, seq, d = x.shape
    t = batch * seq
    half = DIL_HD // 2
    inv_freq = ROPE_THETA ** (-jnp.arange(half, dtype=F32) / half)
    freq = jnp.concatenate([inv_freq, inv_freq]).reshape(1, DIL_HD)
    pos = positions.astype(F32).reshape(t // GLA_BLOCK, 1, GLA_BLOCK)
    for layer in range(norm_gain.shape[0]):
        w = _w_cast(jnp.swapaxes(w_in[layer], 0, 1))
        wa2 = jnp.pad(gla_w_a2[layer], ((0, LANES - GLA_RANK), (0, 0))).astype(BF16)
        ba = gla_b_a[layer].reshape(1, GLA_QK)
        g_dqk = jnp.stack([dil_q_gain[layer] * (DIL_HD ** -0.5 * LOG2_E), dil_k_gain[layer]])

        o_a, gates, h, cos, sin = _nat_gla(x.reshape(t, d), norm_gain[layer], pos, freq, w,
                                           wa2, ba, gla_out_gain[layer].reshape(1, GLA_DV), seq)
        o_ds, lses = [], []
        for g, (win, dil) in enumerate(DIL_GROUPS):
            assert win // dil == ATT_BLOCK
            q_g, k_g, v_g = _dil_proj(h, w, g, dil, g_dqk, cos, sin, batch)
            o_g, lse_g = _dil_attn(q_g, k_g, v_g, batch * dil, dil)
            o_ds.append(o_g)
            lses.append(lse_g)
        x = _merge(x, o_a, o_ds, lses, gates,
                   w_gla_out[layer].astype(BF16), w_dil_out[layer].astype(BF16),
                   w_o[layer].astype(BF16))
    return x
```
